```python
import math
import jax, jax.numpy as jnp
from jax import lax
import numpy as np

D_MODEL = 1024
BATCH = 4
SEQ = 8192
DEPTH = 1

D_PLE = 256
D_MIX = D_MODEL
S5_WIDTH = D_MIX // 2
S5_GROUP = 16
S5_GROUPS = S5_WIDTH // S5_GROUP
S5_STATE = 64
RWKV_WIDTH = D_MIX - S5_WIDTH
RWKV_HEAD = 64
RWKV_HEADS = RWKV_WIDTH // RWKV_HEAD
D_IN = S5_WIDTH + 3 * RWKV_WIDTH
DECAY_LORA = 64
ICLR_LORA = 64
GATE_LORA = 128
RWKV_GN_EPS = 64e-5
N_EXPERTS = 32
TOP_K = 4
D_FF = D_MODEL
SWIGLU_LIMIT = 7.0
SWIGLU_ALPHA = 1.702
MOE_BLOCK = 256
RMS_EPS = 1e-6
LAMBDA_RE_MAX = -1e-4

kernel_name = 'hybrid_s5_rwkv7_moe_ple_layer'


def rms_norm(x, g):
    xf = x.astype(jnp.float32)
    y = xf * lax.rsqrt(jnp.mean(xf * xf, axis=-1, keepdims=True) + RMS_EPS)
    return (y * g.astype(jnp.float32)).astype(x.dtype)


def token_shift(t):
    return jnp.pad(t, ((0, 0), (1, 0), (0, 0)))[:, :-1]


def s5_mixer(u, lam_re, lam_im, log_dt, b_re, b_im, c_re, c_im, d_skip, w_glu):
    f32 = jnp.float32
    bsz, seq, _ = u.shape
    uf = u.astype(f32)
    ug = uf.reshape(bsz, seq, S5_GROUPS, S5_GROUP)
    lre = jnp.minimum(lam_re.astype(f32), LAMBDA_RE_MAX)
    lim = lam_im.astype(f32)
    dt = jnp.exp(log_dt.astype(f32))[:, None]
    mag = jnp.exp(lre * dt)
    lb_re = mag * jnp.cos(lim * dt)
    lb_im = mag * jnp.sin(lim * dt)
    den = lre * lre + lim * lim
    z_re = lb_re - 1.0
    coef_re = ((z_re * lre + lb_im * lim) / den)[..., None]
    coef_im = ((lb_im * lre - z_re * lim) / den)[..., None]
    b_re = b_re.astype(f32)
    b_im = b_im.astype(f32)
    bb_re = coef_re * b_re - coef_im * b_im
    bb_im = coef_re * b_im + coef_im * b_re
    bu_re = jnp.einsum('bsgc,gpc->bsgp', ug, bb_re)
    bu_im = jnp.einsum('bsgc,gpc->bsgp', ug, bb_im)
    a_re = jnp.broadcast_to(lb_re, bu_re.shape)
    a_im = jnp.broadcast_to(lb_im, bu_im.shape)

    def combine(e1, e2):
        a1r, a1i, b1r, b1i = e1
        a2r, a2i, b2r, b2i = e2
        return (a2r * a1r - a2i * a1i,
                a2r * a1i + a2i * a1r,
                a2r * b1r - a2i * b1i + b2r,
                a2r * b1i + a2i * b1r + b2i)

    _, _, xs_re, xs_im = lax.associative_scan(combine, (a_re, a_im, bu_re, bu_im), axis=1)
    y = (jnp.einsum('bsgp,gcp->bsgc', xs_re, c_re.astype(f32))
         - jnp.einsum('bsgp,gcp->bsgc', xs_im, c_im.astype(f32)))
    y = y.reshape(bsz, seq, S5_WIDTH) + d_skip.astype(f32) * uf
    y = jax.nn.gelu(y)
    y = y * jax.nn.sigmoid(y @ w_glu.astype(f32))
    return y.astype(u.dtype)


def rwkv7_step(state, inp):
    r_t, w_t, k_t, v_t, neg_kk_t, kka_t = inp
    sa = jnp.einsum('bhvk,bhk->bhv', state, neg_kk_t)
    state = (state * w_t[:, :, None, :]
             + sa[..., None] * kka_t[:, :, None, :]
             + v_t[..., None] * k_t[:, :, None, :])
    return state, jnp.einsum('bhvk,bhk->bhv', state, r_t)


def rwkv7_mixer(h, r, k, v, mu_rkv, mu_wag, w0, w1, w2, a0, a1, a2, g1, g2,
                k_k, k_a, r_k, ln_w, ln_b):
    f32 = jnp.float32
    bsz, seq, _ = h.shape
    hf = h.astype(f32)
    dh = token_shift(hf) - hf
    xw = hf + dh * mu_wag[0]
    xa = hf + dh * mu_wag[1]
    xg = hf + dh * mu_wag[2]
    r = r.astype(f32)
    k = k.astype(f32)
    v = v.astype(f32)
    r = r + (token_shift(r) - r) * mu_rkv[0]
    k = k + (token_shift(k) - k) * mu_rkv[1]
    v = v + (token_shift(v) - v) * mu_rkv[2]
    w_log = -jax.nn.softplus(-(w0 + jnp.tanh(xw @ w1) @ w2)) - 0.5
    decay = jnp.exp(-jnp.exp(w_log))
    a = jax.nn.sigmoid(a0 + (xa @ a1) @ a2)
    g = jax.nn.sigmoid(xg @ g1) @ g2

    def heads(t):
        return t.reshape(bsz, seq, RWKV_HEADS, RWKV_HEAD)

    kk = heads(k * k_k)
    kk = kk / jnp.maximum(jnp.sqrt(jnp.sum(kk * kk, axis=-1, keepdims=True)), 1e-12)
    k = k * (1.0 + (a - 1.0) * k_a)
    rh, kh, vh, wh, ah = heads(r), heads(k), heads(v), heads(decay), heads(a)

    def seq_first(t):
        return jnp.moveaxis(t, 1, 0)

    state0 = jnp.zeros((bsz, RWKV_HEADS, RWKV_HEAD, RWKV_HEAD), f32)
    _, ys = lax.scan(rwkv7_step, state0,
                     (seq_first(rh), seq_first(wh), seq_first(kh), seq_first(vh),
                      seq_first(-kk), seq_first(kk * ah)))
    y = jnp.moveaxis(ys, 0, 1)
    mean = jnp.mean(y, axis=-1, keepdims=True)
    var = jnp.mean(jnp.square(y - mean), axis=-1, keepdims=True)
    yn = ((y - mean) * lax.rsqrt(var + RWKV_GN_EPS)).reshape(bsz, seq, RWKV_WIDTH)
    yn = yn * ln_w + ln_b
    bonus = jnp.sum(rh * kh * r_k, axis=-1, keepdims=True) * vh
    out = (yn + bonus.reshape(bsz, seq, RWKV_WIDTH)) * g
    return out.astype(h.dtype)


def moe_ffn(h, router_w, router_b, w_gate, b_gate, w_up, b_up, w_down, b_down):
    f32 = jnp.float32
    bsz, seq, dm = h.shape
    n_tok = bsz * seq
    hf = h.reshape(n_tok, dm)
    logits = (hf @ router_w).astype(f32) + router_b.astype(f32)
    top_v, top_i = lax.top_k(logits, TOP_K)
    gates = jax.nn.softmax(top_v, axis=-1)
    m = n_tok * TOP_K
    e_flat = top_i.reshape(m)
    tok_flat = jnp.repeat(jnp.arange(n_tok, dtype=jnp.int32), TOP_K)
    g_flat = gates.reshape(m)
    order = jnp.argsort(e_flat)
    e_s = e_flat[order]
    tok_s = tok_flat[order]
    g_s = g_flat[order]
    counts = jnp.bincount(e_flat, length=N_EXPERTS)
    starts = jnp.cumsum(counts) - counts
    padded = ((counts + MOE_BLOCK - 1) // MOE_BLOCK) * MOE_BLOCK
    pends = jnp.cumsum(padded)
    pstarts = pends - padded
    dest = pstarts[e_s] + (jnp.arange(m) - starts[e_s])
    n_blocks = (m + MOE_BLOCK - 1) // MOE_BLOCK + N_EXPERTS
    m_pad = n_blocks * MOE_BLOCK
    tok_pad = jnp.zeros((m_pad,), jnp.int32).at[dest].set(tok_s)
    g_pad = jnp.zeros((m_pad,), f32).at[dest].set(g_s)
    block_e = jnp.minimum(
        jnp.searchsorted(pends, jnp.arange(n_blocks) * MOE_BLOCK, side='right'),
        N_EXPERTS - 1)

    def run_block(args):
        e, toks, gw = args
        xb = hf[toks]
        gt = jnp.minimum(xb @ w_gate[e] + b_gate[e], SWIGLU_LIMIT)
        up = jnp.clip(xb @ w_up[e] + b_up[e], -SWIGLU_LIMIT, SWIGLU_LIMIT)
        act = (up + 1.0) * gt * jax.nn.sigmoid(SWIGLU_ALPHA * gt)
        yb = act @ w_down[e] + b_down[e]
        return yb.astype(f32) * gw[:, None]

    ys = lax.map(run_block, (block_e, tok_pad.reshape(n_blocks, MOE_BLOCK),
                             g_pad.reshape(n_blocks, MOE_BLOCK)))
    out = jnp.zeros((n_tok, dm), f32).at[tok_pad].add(ys.reshape(m_pad, dm))
    return out.reshape(bsz, seq, dm).astype(h.dtype)


def setup_inputs(seed: int = 0) -> dict:
    key = jax.random.key(seed)
    ks = iter(jax.random.split(key, 64))
    f32 = jnp.float32
    L, D = DEPTH, D_MODEL

    def nrm(shape, scale):
        return jax.random.normal(next(ks), shape, f32) * scale

    def unif(shape, lo, hi):
        return jax.random.uniform(next(ks), shape, f32, lo, hi)

    lam_im = jnp.broadcast_to(jnp.pi * jnp.arange(S5_STATE, dtype=f32), (L, S5_GROUPS, S5_STATE))
    return {
        'x': nrm((BATCH, SEQ, D), 1.0),
        'p': nrm((DEPTH, BATCH, SEQ, D_PLE), 1.0),
        'norm_mix_g': 1.0 + nrm((L, D), 0.02),
        'w_in': nrm((L, D, D_IN), D ** -0.5),
        's5_lam_re': -0.5 + nrm((L, S5_GROUPS, S5_STATE), 0.01),
        's5_lam_im': lam_im,
        's5_log_dt': unif((L, S5_GROUPS), math.log(1e-3), math.log(1e-1)),
        's5_b_re': nrm((L, S5_GROUPS, S5_STATE, S5_GROUP), (2 * S5_GROUP) ** -0.5),
        's5_b_im': nrm((L, S5_GROUPS, S5_STATE, S5_GROUP), (2 * S5_GROUP) ** -0.5),
        's5_c_re': nrm((L, S5_GROUPS, S5_GROUP, S5_STATE), (2 * S5_STATE) ** -0.5),
        's5_c_im': nrm((L, S5_GROUPS, S5_GROUP, S5_STATE), (2 * S5_STATE) ** -0.5),
        's5_d': nrm((L, S5_WIDTH), 1.0),
        's5_w_glu': nrm((L, S5_WIDTH, S5_WIDTH), S5_WIDTH ** -0.5),
        'rwkv_mu_rkv': unif((L, 3, RWKV_WIDTH), 0.2, 0.8),
        'rwkv_mu_wag': unif((L, 3, D), 0.2, 0.8),
        'rwkv_w0': unif((L, RWKV_WIDTH), -6.0, -1.0),
        'rwkv_w1': nrm((L, D, DECAY_LORA), D ** -0.5),
        'rwkv_w2': nrm((L, DECAY_LORA, RWKV_WIDTH), 0.5 * DECAY_LORA ** -0.5),
        'rwkv_a0': nrm((L, RWKV_WIDTH), 0.1),
        'rwkv_a1': nrm((L, D, ICLR_LORA), D ** -0.5),
        'rwkv_a2': nrm((L, ICLR_LORA, RWKV_WIDTH), 0.5 * ICLR_LORA ** -0.5),
        'rwkv_g1': nrm((L, D, GATE_LORA), D ** -0.5),
        'rwkv_g2': nrm((L, GATE_LORA, RWKV_WIDTH), GATE_LORA ** -0.5),
        'rwkv_k_k': 0.85 + nrm((L, RWKV_WIDTH), 0.02),
        'rwkv_k_a': 1.0 + nrm((L, RWKV_WIDTH), 0.02),
        'rwkv_r_k': nrm((L, RWKV_HEADS, RWKV_HEAD), 0.1),
        'rwkv_ln_w': 1.0 + nrm((L, RWKV_WIDTH), 0.02),
        'rwkv_ln_b': nrm((L, RWKV_WIDTH), 0.02),
        'w_out': nrm((L, D_MIX, D), D_MIX ** -0.5),
        'norm_moe_g': 1.0 + nrm((L, D), 0.02),
        'router_w': nrm((L, D, N_EXPERTS), D ** -0.5),
        'router_b': nrm((L, N_EXPERTS), 0.01),
        'exp_w_gate': nrm((L, N_EXPERTS, D, D_FF), D ** -0.5),
        'exp_b_gate': nrm((L, N_EXPERTS, D_FF), 0.01),
        'exp_w_up': nrm((L, N_EXPERTS, D, D_FF), D ** -0.5),
        'exp_b_up': nrm((L, N_EXPERTS, D_FF), 0.01),
        'exp_w_down': nrm((L, N_EXPERTS, D_FF, D), D_FF ** -0.5),
        'exp_b_down': nrm((L, N_EXPERTS, D), 0.01),
        'norm_ple_g': 1.0 + nrm((L, D), 0.02),
        'ple_w_proj': nrm((L, D_PLE, D), D_PLE ** -0.5),
        'ple_w_gate': nrm((L, D, D), D ** -0.5),
        'final_norm_g': 1.0 + nrm((D,), 0.02),
    }


def reference(x, p, norm_mix_g, w_in, s5_lam_re, s5_lam_im, s5_log_dt, s5_b_re, s5_b_im,
              s5_c_re, s5_c_im, s5_d, s5_w_glu, rwkv_mu_rkv, rwkv_mu_wag, rwkv_w0, rwkv_w1,
              rwkv_w2, rwkv_a0, rwkv_a1, rwkv_a2, rwkv_g1, rwkv_g2, rwkv_k_k, rwkv_k_a,
              rwkv_r_k, rwkv_ln_w, rwkv_ln_b, w_out, norm_moe_g, router_w, router_b,
              exp_w_gate, exp_b_gate, exp_w_up, exp_b_up, exp_w_down, exp_b_down,
              norm_ple_g, ple_w_proj, ple_w_gate, final_norm_g):
    r0 = S5_WIDTH
    k0 = r0 + RWKV_WIDTH
    v0 = k0 + RWKV_WIDTH
    for i in range(DEPTH):
        h = rms_norm(x, norm_mix_g[i])
        proj = h @ w_in[i]
        y_s5 = s5_mixer(proj[..., :r0], s5_lam_re[i], s5_lam_im[i], s5_log_dt[i],
                        s5_b_re[i], s5_b_im[i], s5_c_re[i], s5_c_im[i], s5_d[i], s5_w_glu[i])
        y_rwkv = rwkv7_mixer(h, proj[..., r0:k0], proj[..., k0:v0], proj[..., v0:],
                             rwkv_mu_rkv[i], rwkv_mu_wag[i], rwkv_w0[i], rwkv_w1[i], rwkv_w2[i],
                             rwkv_a0[i], rwkv_a1[i], rwkv_a2[i], rwkv_g1[i], rwkv_g2[i],
                             rwkv_k_k[i], rwkv_k_a[i], rwkv_r_k[i], rwkv_ln_w[i], rwkv_ln_b[i])
        x = x + jnp.concatenate([y_s5, y_rwkv], axis=-1) @ w_out[i]
        x = x + moe_ffn(rms_norm(x, norm_moe_g[i]), router_w[i], router_b[i],
                        exp_w_gate[i], exp_b_gate[i], exp_w_up[i], exp_b_up[i],
                        exp_w_down[i], exp_b_down[i])
        gate = jax.nn.sigmoid(rms_norm(x, norm_ple_g[i]) @ ple_w_gate[i])
        x = x + (p[i] @ ple_w_proj[i]) * gate
    return rms_norm(x, final_norm_g)
```

```python
import functools

import jax
import jax.numpy as jnp
from jax import lax
from jax.experimental import pallas as pl
from jax.experimental.pallas import tpu as pltpu

F32 = jnp.float32
BF16 = jnp.bfloat16

S5_GROUP = 16
S5_STATE = 64
RWKV_HEAD = 64
TOP_K = 4
RMS_EPS = 1e-6
RWKV_GN_EPS = 64e-5
LAMBDA_RE_MAX = -1e-4
SWIGLU_LIMIT = 7.0
SWIGLU_ALPHA = 1.702

LANES = 128
SUBLANES = 8
VMEM_LIMIT = 56 * 1024 * 1024

S5_CHUNK = 64
RWKV_CHUNK = 64
RWKV_SUB = 16
TOKEN_TILE = 256
EXPERT_TILE = 512
NEG_BIG = -1e30


def _dot(a, b):
    return jnp.dot(a, b, preferred_element_type=F32)


def _dot_nt(a, b):
    return lax.dot_general(a, b, (((1,), (1,)), ((), ())), preferred_element_type=F32)


def _dot_tn(a, b):
    return lax.dot_general(a, b, (((0,), (0,)), ((), ())), preferred_element_type=F32)


def _split_dot(a, b_bf16):
    hi = a.astype(BF16)
    lo = (a - hi.astype(F32)).astype(BF16)
    return _dot(hi, b_bf16) + _dot(lo, b_bf16)


def _rms(t, gain):
    return t * lax.rsqrt(jnp.mean(t * t, axis=-1, keepdims=True) + RMS_EPS) * gain


def _sigmoid(t):
    return 1.0 / (1.0 + jnp.exp(-t))


def _params(*sem):
    return pltpu.CompilerParams(dimension_semantics=sem, vmem_limit_bytes=VMEM_LIMIT)


def _full(shape):
    return pl.BlockSpec(shape, lambda *_: (0,) * len(shape))


def _mix_in_kernel(seq, x_ref, xp_ref, gain_ref, win_ref, w1_ref, a1_ref, g1_ref, muwag_ref,
                   murkv_ref, vec_ref, w2_ref, a2_ref, g2_ref, ones_ref,
                   u_o, r_o, lw_o, k_o, v_o, an_o, bb_o, g_o, bonus_o):
    tm = x_ref.shape[0]
    width = u_o.shape[1]
    gain = gain_ref[...]
    h = _rms(x_ref[...], gain)
    keep = jnp.where((pl.program_id(0) * tm) % seq == 0, 0.0, 1.0)
    hp = _rms(xp_ref[...], gain) * keep
    h_ext = jnp.concatenate([hp, h], axis=0)
    hs = pltpu.roll(h_ext, 1, 0)[SUBLANES:]
    proj = _dot(h_ext.astype(BF16), win_ref[...])
    rkv = proj[:, width:]
    rkv_cur = rkv[SUBLANES:]
    rkv_prev = pltpu.roll(rkv, 1, 0)[SUBLANES:]
    u_o[...] = proj[SUBLANES:, :width]

    dh = hs - h
    xw = (h + dh * muwag_ref[0:1, :]).astype(BF16)
    xa = (h + dh * muwag_ref[1:2, :]).astype(BF16)
    xg = (h + dh * muwag_ref[2:3, :]).astype(BF16)
    lw_hidden = jnp.tanh(_dot(xw, w1_ref[...])).astype(BF16)
    la_hidden = _dot(xa, a1_ref[...]).astype(BF16)
    lg_hidden = _sigmoid(_dot(xg, g1_ref[...])).astype(BF16)

    w0 = vec_ref[0:1, :]
    a0 = vec_ref[1:2, :]
    k_k = vec_ref[2:3, :]
    k_a = vec_ref[3:4, :]
    r_k = vec_ref[4:5, :]
    z = w0 + _dot(lw_hidden, w2_ref[...])
    nz = -z
    softplus = jnp.maximum(nz, 0.0) + jnp.log(1.0 + jnp.exp(-jnp.abs(nz)))
    lw_o[...] = -jnp.exp(-softplus - 0.5)
    a = _sigmoid(a0 + _dot(la_hidden, a2_ref[...]))
    g_o[...] = _dot(lg_hidden, g2_ref[...])

    r_cur = rkv_cur[:, :width]
    k_cur = rkv_cur[:, width:2 * width]
    v_cur = rkv_cur[:, 2 * width:]
    r = r_cur + (rkv_prev[:, :width] - r_cur) * murkv_ref[0:1, :]
    k = k_cur + (rkv_prev[:, width:2 * width] - k_cur) * murkv_ref[1:2, :]
    v = v_cur + (rkv_prev[:, 2 * width:] - v_cur) * murkv_ref[2:3, :]

    ones = ones_ref[...]
    kk = k * k_k
    norm = jnp.sqrt(_split_dot(kk * kk, ones))
    kkn = kk / jnp.maximum(norm, 1e-12)
    k2 = k * (1.0 + (a - 1.0) * k_a)
    r_o[...] = r
    k_o[...] = k2
    v_o[...] = v
    an_o[...] = -kkn
    bb_o[...] = kkn * a
    bonus_o[...] = _split_dot(r * k2 * r_k, ones) * v


def _mix_in(x2d, seq, gain, w_in, w1, a1, g1, mu_wag, mu_rkv, vecs, w2, a2, g2, head_ones):
    n, d = x2d.shape
    width = w_in.shape[1] // 4
    tm = TOKEN_TILE
    prev_blocks = tm // SUBLANES
    row = lambda i: (i, 0)
    out = jax.ShapeDtypeStruct((n, width), F32)
    return pl.pallas_call(
        functools.partial(_mix_in_kernel, seq),
        grid=(n // tm,),
        in_specs=[
            pl.BlockSpec((tm, d), row),
            pl.BlockSpec((SUBLANES, d), lambda i: (jnp.maximum(i * prev_blocks - 1, 0), 0)),
            _full(gain.shape), _full(w_in.shape), _full(w1.shape), _full(a1.shape), _full(g1.shape),
            _full(mu_wag.shape), _full(mu_rkv.shape), _full(vecs.shape), _full(w2.shape),
            _full(a2.shape), _full(g2.shape), _full(head_ones.shape),
        ],
        out_specs=[pl.BlockSpec((tm, width), row)] * 9,
        out_shape=[out] * 9,
        compiler_params=_params("parallel"),
        name="mix_in",
    )(x2d, x2d, gain, w_in, w1, a1, g1, mu_wag, mu_rkv, vecs, w2, a2, g2, head_ones)


def _s5_tables(lam_re, lam_im, log_dt, b_re, b_im, c_re, c_im, n_chunks):
    t = S5_CHUNK
    hi = lax.Precision.HIGHEST
    lre = jnp.minimum(lam_re.astype(F32), LAMBDA_RE_MAX)
    lim = lam_im.astype(F32)
    dt = jnp.exp(log_dt.astype(F32))[:, None]
    mag = jnp.exp(lre * dt)
    lb_re = mag * jnp.cos(lim * dt)
    lb_im = mag * jnp.sin(lim * dt)
    den = lre * lre + lim * lim
    z_re = lb_re - 1.0
    coef_re = ((z_re * lre + lb_im * lim) / den)[..., None]
    coef_im = ((lb_im * lre - z_re * lim) / den)[..., None]
    b_re = b_re.astype(F32)
    b_im = b_im.astype(F32)
    bb_re = coef_re * b_re - coef_im * b_im
    bb_im = coef_re * b_im + coef_im * b_re
    c_re = c_re.astype(F32)
    c_im = c_im.astype(F32)

    def power(e):
        e = e.astype(F32)[:, None, None]
        m = jnp.exp(e * (lre * dt))
        ang = e * (lim * dt)
        return m * jnp.cos(ang), m * jnp.sin(ang)

    p_re, p_im = power(jnp.arange(t + 1))
    ab_re = p_re[:t, :, :, None] * bb_re - p_im[:t, :, :, None] * bb_im
    ab_im = p_re[:t, :, :, None] * bb_im + p_im[:t, :, :, None] * bb_re
    kern = (jnp.einsum('gcp,tgpd->gtcd', c_re, ab_re, precision=hi)
            - jnp.einsum('gcp,tgpd->gtcd', c_im, ab_im, precision=hi))
    s_idx = jnp.arange(t)[:, None]
    t_idx = jnp.arange(t)[None, :]
    lag = t_idx - s_idx
    toep = kern[:, jnp.maximum(lag, 0)]
    toep = jnp.where((lag >= 0)[None, :, :, None, None], toep, 0.0)
    g = toep.shape[0]
    c = S5_GROUP
    conv = toep.transpose(0, 4, 1, 3, 2).reshape(g, c * t, c * t)
    st_re = ab_re[::-1].transpose(1, 3, 0, 2).reshape(g, c * t, S5_STATE)
    st_im = ab_im[::-1].transpose(1, 3, 0, 2).reshape(g, c * t, S5_STATE)
    conv_full = jnp.concatenate([conv, st_re, st_im], axis=2).astype(BF16)
    q_re, q_im = p_re[1:], p_im[1:]
    ca_re = c_re[:, :, None, :] * q_re.transpose(1, 0, 2)[:, None] - c_im[:, :, None, :] * q_im.transpose(1, 0, 2)[:, None]
    ca_im = c_re[:, :, None, :] * q_im.transpose(1, 0, 2)[:, None] + c_im[:, :, None, :] * q_re.transpose(1, 0, 2)[:, None]
    cross = jnp.concatenate([ca_re.transpose(0, 3, 1, 2).reshape(g, S5_STATE, c * t),
                             -ca_im.transpose(0, 3, 1, 2).reshape(g, S5_STATE, c * t)], axis=1).astype(BF16)
    n_steps = max(1, (n_chunks - 1).bit_length())
    s_re, s_im = power(t * (2 ** jnp.arange(n_steps)))
    tab = jnp.stack([jnp.concatenate([s_re, s_re], axis=-1),
                     jnp.concatenate([-s_im, s_im], axis=-1)], axis=1)
    tab = tab.transpose(2, 0, 1, 3).reshape(g, 2 * n_steps, 2 * S5_STATE)
    return conv_full, cross, tab, n_steps


def _s5_conv_kernel(n_chunks, n_steps, u_ref, conv_ref, cross_ref, tab_ref, y_ref):
    ct = cross_ref.shape[2]
    z = _dot(u_ref[0], conv_ref[0])
    y = z[:, :ct]
    x = z[:, ct:]
    rows = x.shape[0]
    chunk = lax.broadcasted_iota(jnp.int32, x.shape, 0) % n_chunks
    half = x.shape[1] // 2
    for j in range(n_steps):
        sh = 1 << j
        xs = jnp.where(chunk >= sh, pltpu.roll(x, sh, 0), 0.0)
        x = x + tab_ref[0, 2 * j:2 * j + 1, :] * xs + tab_ref[0, 2 * j + 1:2 * j + 2, :] * pltpu.roll(xs, half, 1)
    x_in = jnp.where(chunk >= 1, pltpu.roll(x, 1, 0), 0.0)
    del rows
    y_ref[0] = y + _dot(x_in.astype(BF16), cross_ref[0])


def _s5_conv(u2d, batch, seq, tables):
    conv_full, cross, tab, n_steps = tables
    n, width = u2d.shape
    g = width // S5_GROUP
    t = S5_CHUNK
    n_chunks = seq // t
    ct = S5_GROUP * t
    ut = u2d.astype(BF16).reshape(batch, n_chunks, t, g, S5_GROUP).transpose(3, 0, 1, 4, 2)
    ut = ut.reshape(g, batch * n_chunks, ct)
    rows = batch * n_chunks
    y = pl.pallas_call(
        functools.partial(_s5_conv_kernel, n_chunks, n_steps),
        grid=(g,),
        in_specs=[
            pl.BlockSpec((1, rows, ct), lambda i: (i, 0, 0)),
            pl.BlockSpec((1,) + conv_full.shape[1:], lambda i: (i, 0, 0)),
            pl.BlockSpec((1,) + cross.shape[1:], lambda i: (i, 0, 0)),
            pl.BlockSpec((1,) + tab.shape[1:], lambda i: (i, 0, 0)),
        ],
        out_specs=pl.BlockSpec((1, rows, ct), lambda i: (i, 0, 0)),
        out_shape=jax.ShapeDtypeStruct((g, rows, ct), F32),
        compiler_params=_params("parallel"),
        name="s5_conv",
    )(ut, conv_full, cross, tab)
    return y.reshape(g, batch, n_chunks, S5_GROUP, t).transpose(1, 2, 4, 0, 3).reshape(n, width)


def _rwkv_chunk_kernel(r_ref, lw_ref, k_ref, v_ref, an_ref, bb_ref, rhat_o, yhat_o, g_o, h_o):
    t = r_ref.shape[0]
    heads = r_ref.shape[1] // RWKV_HEAD
    row = lax.broadcasted_iota(jnp.int32, (t, t), 0)
    col = lax.broadcasted_iota(jnp.int32, (t, t), 1)
    incl = row >= col
    strict = row > col
    same_blk = (row // RWKV_SUB) == (col // RWKV_SUB)
    eye = jnp.where(row == col, 1.0, 0.0).astype(F32)
    tril = jnp.where(incl, 1.0, 0.0).astype(BF16)

    lw = lw_ref[...]
    p1 = lw.astype(BF16)
    rem = lw - p1.astype(F32)
    p2 = rem.astype(BF16)
    p3 = (rem - p2.astype(F32)).astype(BF16)
    cs = _dot(tril, p1) + _dot(tril, p2) + _dot(tril, p3)
    cs_end = cs[t - 1:t, :]
    p_inc = jnp.exp(cs)
    p_inv = jnp.exp(-cs)
    p_prev = jnp.exp(cs - lw)
    p_end = jnp.exp(cs_end - cs)
    p_tot = jnp.exp(cs_end)

    a_t = an_ref[...] * p_prev
    r_t = r_ref[...] * p_inc
    b_t = bb_ref[...] * p_inv
    k_t = k_ref[...] * p_inv
    b_h = bb_ref[...] * p_end
    k_h = k_ref[...] * p_end
    v = v_ref[...]

    def mm(x, y):
        return _dot(x.astype(BF16), y.astype(BF16))

    rhat, yhat = [], []
    for hd in range(heads):
        sl = slice(hd * RWKV_HEAD, (hd + 1) * RWKV_HEAD)
        a_hd, r_hd, v_hd = a_t[:, sl], r_t[:, sl], v[:, sl]
        ar = jnp.concatenate([a_hd, r_hd], axis=0).astype(BF16)
        bk = jnp.concatenate([b_t[:, sl], k_t[:, sl]], axis=0).astype(BF16)
        prod = _dot_nt(ar, bk)
        l_ab = jnp.where(strict, prod[:t, :t], 0.0)
        l_ak = jnp.where(strict, prod[:t, t:], 0.0)
        l_rb = jnp.where(incl, prod[t:, :t], 0.0)
        l_rk = jnp.where(incl, prod[t:, t:], 0.0)
        l_d = jnp.where(same_blk, l_ab, 0.0)
        l_o = l_ab - l_d
        l2 = mm(l_d, l_d)
        l4 = mm(l2, l2)
        l8 = mm(l4, l4)
        d_inv = mm(mm(mm(eye + l_d, eye + l2), eye + l4), eye + l8)
        n1 = mm(d_inv, l_o)
        n2 = mm(n1, n1)
        m_inv = mm(mm(eye + n1, eye + n2), d_inv)

        x1 = mm(l_ak, v_hd)
        wu = mm(m_inv, jnp.concatenate([a_hd, x1], axis=1))
        ry = mm(l_rb, wu)
        rhat.append(r_hd + ry[:, :RWKV_HEAD])
        yhat.append(ry[:, RWKV_HEAD:] + mm(l_rk, v_hd))
        w_mat = wu[:, :RWKV_HEAD]
        u_v = wu[:, RWKV_HEAD:]
        erow = lax.broadcasted_iota(jnp.int32, (RWKV_HEAD, RWKV_HEAD), 0)
        ecol = lax.broadcasted_iota(jnp.int32, (RWKV_HEAD, RWKV_HEAD), 1)
        diag = jnp.where(erow == ecol, p_tot[:, sl], 0.0)
        g_o[0, hd] = diag + _dot_tn(w_mat.astype(BF16), b_h[:, sl].astype(BF16))
        uv = jnp.concatenate([u_v, v_hd], axis=0).astype(BF16)
        bkh = jnp.concatenate([b_h[:, sl], k_h[:, sl]], axis=0).astype(BF16)
        h_o[0, hd] = _dot_tn(uv, bkh)
    rhat_o[...] = jnp.concatenate(rhat, axis=1)
    yhat_o[...] = jnp.concatenate(yhat, axis=1)


def _rwkv_chunks(r, lw, k, v, an, bb):
    n, width = r.shape
    t = RWKV_CHUNK
    heads = width // RWKV_HEAD
    nck = n // t
    tok = pl.BlockSpec((t, width), lambda i: (i, 0))
    mat = pl.BlockSpec((1, heads, RWKV_HEAD, RWKV_HEAD), lambda i: (i, 0, 0, 0))
    return pl.pallas_call(
        _rwkv_chunk_kernel,
        grid=(nck,),
        in_specs=[tok] * 6,
        out_specs=[tok, tok, mat, mat],
        out_shape=[jax.ShapeDtypeStruct((n, width), F32)] * 2
        + [jax.ShapeDtypeStruct((nck, heads, RWKV_HEAD, RWKV_HEAD), F32)] * 2,
        compiler_params=_params("parallel"),
        name="rwkv_chunk",
    )(r, lw, k, v, an, bb)


def _rwkv_scan_kernel(rhat_ref, yhat_ref, g_ref, h_ref, gate_ref, bonus_ref, lnw_ref, lnb_ref,
                      ones_ref, y_ref, state):
    heads = state.shape[0]

    @pl.when(pl.program_id(1) == 0)
    def _():
        state[...] = jnp.zeros_like(state)

    ys = []
    for hd in range(heads):
        sl = slice(hd * RWKV_HEAD, (hd + 1) * RWKV_HEAD)
        s = state[hd]
        s_hi = s.astype(BF16)
        s_lo = (s - s_hi.astype(F32)).astype(BF16)
        rh = rhat_ref[:, sl].astype(BF16)
        ys.append(yhat_ref[:, sl] + _dot_nt(rh, s_hi) + _dot_nt(rh, s_lo))
        gm = g_ref[0, hd].astype(BF16)
        state[hd] = _dot(s_hi, gm) + _dot(s_lo, gm) + h_ref[0, hd]
    y = jnp.concatenate(ys, axis=1)
    ones = ones_ref[...]
    inv = 1.0 / RWKV_HEAD
    mean = _split_dot(y, ones) * inv
    cen = y - mean
    var = _split_dot(cen * cen, ones) * inv
    yn = cen * lax.rsqrt(var + RWKV_GN_EPS) * lnw_ref[...] + lnb_ref[...]
    y_ref[...] = (yn + bonus_ref[...]) * gate_ref[...]


def _rwkv_scan(rhat, yhat, g_mat, h_mat, gate, bonus, ln_w, ln_b, head_ones, batch, seq):
    n, width = rhat.shape
    t = RWKV_CHUNK
    heads = width // RWKV_HEAD
    nc = seq // t
    tok = pl.BlockSpec((t, width), lambda b, c: (b * nc + c, 0))
    mat = pl.BlockSpec((1, heads, RWKV_HEAD, RWKV_HEAD), lambda b, c: (b * nc + c, 0, 0, 0))
    return pl.pallas_call(
        _rwkv_scan_kernel,
        grid=(batch, nc),
        in_specs=[tok, tok, mat, mat, tok, tok, _full(ln_w.shape), _full(ln_b.shape),
                  _full(head_ones.shape)],
        out_specs=tok,
        out_shape=jax.ShapeDtypeStruct((n, width), F32),
        scratch_shapes=[pltpu.VMEM((heads, RWKV_HEAD, RWKV_HEAD), F32)],
        compiler_params=_params("arbitrary", "arbitrary"),
        name="rwkv_scan",
    )(rhat, yhat, g_mat, h_mat, gate, bonus, ln_w, ln_b, head_ones)


def _post_mix_kernel(n_experts, x_ref, yc_ref, u_ref, yr_ref, d_ref, wglu_ref, wout_ref, gain_ref,
                     rw_hi_ref, rw_lo_ref, rb_ref, x1_o, hm_o, idx_o, gate_o, rank_o, cnt_o, running):
    tm = x_ref.shape[0]
    width = yc_ref.shape[1]

    @pl.when(pl.program_id(0) == 0)
    def _():
        running[...] = jnp.zeros_like(running)

    y = yc_ref[...] + d_ref[...] * u_ref[...]
    y = 0.5 * y * (1.0 + jnp.tanh(0.7978845608028654 * (y + 0.044715 * (y * y * y))))
    y = y * _sigmoid(_dot(y.astype(BF16), wglu_ref[...]))
    x1 = (x_ref[...] + _dot(y.astype(BF16), wout_ref[:width, :])
          + _dot(yr_ref[...].astype(BF16), wout_ref[width:, :]))
    x1_o[...] = x1
    hm = _rms(x1, gain_ref[...])
    hm_o[...] = hm
    h_hi = hm.astype(BF16)
    h_lo = (hm - h_hi.astype(F32)).astype(BF16)
    logits = (_dot(h_hi, rw_hi_ref[...]) + _dot(h_lo, rw_hi_ref[...]) + _dot(h_hi, rw_lo_ref[...])
              + rb_ref[...])

    lane = lax.broadcasted_iota(jnp.int32, logits.shape, 1)
    lanef = lane.astype(F32)
    sel = jnp.zeros(logits.shape, F32)
    idx_cols, val_cols = [], []
    work = logits
    for _ in range(TOP_K):
        m = jnp.max(work, axis=-1, keepdims=True)
        pick = jnp.min(jnp.where(work == m, lanef, float(LANES)), axis=-1, keepdims=True)
        hit = lanef == pick
        sel = jnp.where(hit, 1.0, sel)
        work = jnp.where(hit, -jnp.inf, work)
        idx_cols.append(pick)
        val_cols.append(m)
    exps = [jnp.exp(vv - val_cols[0]) for vv in val_cols]
    denom = exps[0] + exps[1] + exps[2] + exps[3]

    row = lax.broadcasted_iota(jnp.int32, (tm, tm), 0)
    col = lax.broadcasted_iota(jnp.int32, (tm, tm), 1)
    before = jnp.where(row > col, 1.0, 0.0).astype(BF16)
    excl = _dot(before, sel.astype(BF16)) + running[...]
    idx_out = jnp.zeros(logits.shape, F32)
    gate_out = jnp.zeros(logits.shape, F32)
    rank_out = jnp.zeros(logits.shape, F32)
    for j in range(TOP_K):
        rk = jnp.sum(jnp.where(lanef == idx_cols[j], excl, 0.0), axis=-1, keepdims=True)
        idx_out = jnp.where(lane == j, idx_cols[j], idx_out)
        gate_out = jnp.where(lane == j, exps[j] / denom, gate_out)
        rank_out = jnp.where(lane == j, rk, rank_out)
    idx_o[...] = idx_out.astype(jnp.int32)
    gate_o[...] = gate_out
    rank_o[...] = rank_out.astype(jnp.int32)
    running[...] = running[...] + jnp.sum(sel, axis=0, keepdims=True)
    cnt_o[...] = jnp.broadcast_to(running[...], cnt_o.shape).astype(jnp.int32)
    del n_experts


def _post_mix(x2d, yconv, u, y_rwkv, s5_d, w_glu, w_out, gain, rw_hi, rw_lo, rb, n_experts):
    n, d = x2d.shape
    width = yconv.shape[1]
    tm = TOKEN_TILE
    row = lambda i: (i, 0)
    tok_d = pl.BlockSpec((tm, d), row)
    tok_w = pl.BlockSpec((tm, width), row)
    tok_l = pl.BlockSpec((tm, LANES), row)
    return pl.pallas_call(
        functools.partial(_post_mix_kernel, n_experts),
        grid=(n // tm,),
        in_specs=[tok_d, tok_w, tok_w, tok_w, _full(s5_d.shape), _full(w_glu.shape),
                  _full(w_out.shape), _full(gain.shape), _full(rw_hi.shape), _full(rw_lo.shape),
                  _full(rb.shape)],
        out_specs=[tok_d, tok_d, tok_l, tok_l, tok_l, _full((SUBLANES, LANES))],
        out_shape=[jax.ShapeDtypeStruct((n, d), F32), jax.ShapeDtypeStruct((n, d), F32),
                   jax.ShapeDtypeStruct((n, LANES), jnp.int32), jax.ShapeDtypeStruct((n, LANES), F32),
                   jax.ShapeDtypeStruct((n, LANES), jnp.int32),
                   jax.ShapeDtypeStruct((SUBLANES, LANES), jnp.int32)],
        scratch_shapes=[pltpu.VMEM((1, LANES), F32)],
        compiler_params=_params("arbitrary"),
        name="post_mix",
    )(x2d, yconv, u, y_rwkv, s5_d, w_glu, w_out, gain, rw_hi, rw_lo, rb)


def _row_copy(src_ref, src_row, dst_ref, dst_row, sem):
    return pltpu.make_async_copy(src_ref.at[pl.ds(src_row, 1)], dst_ref.at[pl.ds(dst_row, 1)], sem)


def _dispatch_kernel(slot_ref, hm_ref, xs_in_ref, xs_ref, sem):
    del xs_in_ref
    tm = hm_ref.shape[0]

    def start(r, c):
        for j in range(TOP_K):
            _row_copy(hm_ref, r, xs_ref, slot_ref[r * TOP_K + j], sem).start()
        return c

    lax.fori_loop(0, tm, start, 0)

    def wait(r, c):
        for j in range(TOP_K):
            _row_copy(hm_ref, 0, xs_ref, 0, sem).wait()
        return c

    lax.fori_loop(0, tm, wait, 0)


def _dispatch(hm, slots_flat, m_pad):
    n, d = hm.shape
    tm = TOKEN_TILE
    xs0 = jnp.zeros((m_pad, d), hm.dtype)
    return pl.pallas_call(
        _dispatch_kernel,
        grid=(n // tm,),
        in_specs=[pl.BlockSpec((tm * TOP_K,), lambda i: (i,), memory_space=pltpu.SMEM),
                  pl.BlockSpec((tm, d), lambda i: (i, 0)),
                  pl.BlockSpec(memory_space=pl.ANY)],
        out_specs=pl.BlockSpec(memory_space=pl.ANY),
        out_shape=jax.ShapeDtypeStruct((m_pad, d), hm.dtype),
        scratch_shapes=[pltpu.SemaphoreType.DMA(())],
        input_output_aliases={2: 0},
        compiler_params=_params("arbitrary"),
        name="dispatch",
    )(slots_flat, hm, xs0)


def _experts_kernel(be_ref, nu_ref, xs_ref, wg_ref, bg_ref, wu_ref, bu_ref, wd_ref, bd_ref, ys_ref):
    del be_ref

    @pl.when(pl.program_id(0) < nu_ref[0])
    def _():
        xb = xs_ref[...].astype(BF16)
        gt = jnp.minimum(_dot(xb, wg_ref[0]) + bg_ref[0], SWIGLU_LIMIT)
        up = jnp.clip(_dot(xb, wu_ref[0]) + bu_ref[0], -SWIGLU_LIMIT, SWIGLU_LIMIT)
        act = (up + 1.0) * gt * _sigmoid(SWIGLU_ALPHA * gt)
        ys_ref[...] = _dot(act.astype(BF16), wd_ref[0]) + bd_ref[0]

    @pl.when(pl.program_id(0) >= nu_ref[0])
    def _():
        ys_ref[...] = jnp.zeros_like(ys_ref)


def _experts(xs, block_e, n_used, wg, bg, wu, bu, wd, bd):
    m_pad, d = xs.shape
    tmb = EXPERT_TILE
    dff = wg.shape[2]
    wmap = lambda i, be, nu: (be[i], 0, 0)
    grid_spec = pltpu.PrefetchScalarGridSpec(
        num_scalar_prefetch=2,
        grid=(m_pad // tmb,),
        in_specs=[pl.BlockSpec((tmb, d), lambda i, be, nu: (i, 0)),
                  pl.BlockSpec((1, d, dff), wmap), pl.BlockSpec((1, 1, dff), wmap),
                  pl.BlockSpec((1, d, dff), wmap), pl.BlockSpec((1, 1, dff), wmap),
                  pl.BlockSpec((1, dff, d), wmap), pl.BlockSpec((1, 1, d), wmap)],
        out_specs=pl.BlockSpec((tmb, d), lambda i, be, nu: (i, 0)),
    )
    return pl.pallas_call(
        _experts_kernel,
        grid_spec=grid_spec,
        out_shape=jax.ShapeDtypeStruct((m_pad, d), F32),
        compiler_params=_params("arbitrary"),
        name="experts",
    )(block_e, n_used, xs, wg, bg, wu, bu, wd, bd)


def _final_kernel(slot_ref, x1_ref, gate_ref, p_ref, ys_ref, gple_ref, wproj_ref, wgate_ref,
                  gfin_ref, out_ref, buf, sem):
    tm = x1_ref.shape[0]

    def start(r, c):
        for j in range(TOP_K):
            _row_copy(ys_ref, slot_ref[r * TOP_K + j], buf.at[j], r, sem).start()
        return c

    lax.fori_loop(0, tm, start, 0)

    def wait(r, c):
        for j in range(TOP_K):
            _row_copy(ys_ref, 0, buf.at[j], 0, sem).wait()
        return c

    lax.fori_loop(0, tm, wait, 0)

    x2 = x1_ref[...]
    gates = gate_ref[...]
    for j in range(TOP_K):
        x2 = x2 + gates[:, j:j + 1] * buf[j]
    gate = _sigmoid(_dot(_rms(x2, gple_ref[...]).astype(BF16), wgate_ref[...]))
    x3 = x2 + _dot(p_ref[...].astype(BF16), wproj_ref[...]) * gate
    out_ref[...] = _rms(x3, gfin_ref[...])


def _final(x1, gates, slots_flat, p2d, ys, g_ple, w_proj, w_gate, g_fin):
    n, d = x1.shape
    tm = TOKEN_TILE
    row = lambda i: (i, 0)
    return pl.pallas_call(
        _final_kernel,
        grid=(n // tm,),
        in_specs=[pl.BlockSpec((tm * TOP_K,), lambda i: (i,), memory_space=pltpu.SMEM),
                  pl.BlockSpec((tm, d), row), pl.BlockSpec((tm, LANES), row),
                  pl.BlockSpec((tm, p2d.shape[1]), row), pl.BlockSpec(memory_space=pl.ANY),
                  _full(g_ple.shape), _full(w_proj.shape), _full(w_gate.shape), _full(g_fin.shape)],
        out_specs=pl.BlockSpec((tm, d), row),
        out_shape=jax.ShapeDtypeStruct((n, d), F32),
        scratch_shapes=[pltpu.VMEM((TOP_K, tm, d), F32), pltpu.SemaphoreType.DMA(())],
        compiler_params=_params("arbitrary"),
        name="final",
    )(slots_flat, x1, gates, p2d, ys, g_ple, w_proj, w_gate, g_fin)


def _head_ones(width):
    hd = jnp.arange(width) // RWKV_HEAD
    return (hd[:, None] == hd[None, :]).astype(BF16)


def _layer(x2d, p2d, batch, seq, norm_mix_g, w_in, s5_lam_re, s5_lam_im, s5_log_dt, s5_b_re, s5_b_im,
           s5_c_re, s5_c_im, s5_d, s5_w_glu, mu_rkv, mu_wag, w0, w1, w2, a0, a1, a2, g1, g2, k_k, k_a,
           r_k, ln_w, ln_b, w_out, norm_moe_g, router_w, router_b, wg, bg, wu, bu, wd, bd,
           norm_ple_g, ple_w_proj, ple_w_gate):
    n, d = x2d.shape
    width = w_in.shape[1] // 4
    n_experts = router_w.shape[1]
    row2 = lambda t: t.reshape(1, -1).astype(F32)
    head_ones = _head_ones(width)
    vecs = jnp.zeros((SUBLANES, width), F32)
    vecs = vecs.at[0].set(w0).at[1].set(a0).at[2].set(k_k).at[3].set(k_a).at[4].set(r_k.reshape(-1))

    u, r, lw, k, v, an, bb, gate, bonus = _mix_in(
        x2d, seq, row2(norm_mix_g), w_in.astype(BF16), w1.astype(BF16), a1.astype(BF16),
        g1.astype(BF16), mu_wag.astype(F32), mu_rkv.astype(F32), vecs, w2.astype(BF16),
        a2.astype(BF16), g2.astype(BF16), head_ones)

    tables = _s5_tables(s5_lam_re, s5_lam_im, s5_log_dt, s5_b_re, s5_b_im, s5_c_re, s5_c_im,
                        seq // S5_CHUNK)
    yconv = _s5_conv(u, batch, seq, tables)

    rhat, yhat, g_mat, h_mat = _rwkv_chunks(r, lw, k, v, an, bb)
    y_rwkv = _rwkv_scan(rhat, yhat, g_mat, h_mat, gate, bonus, row2(ln_w), row2(ln_b), head_ones,
                        batch, seq)

    rw = jnp.zeros((d, LANES), F32).at[:, :n_experts].set(router_w.astype(F32))
    rw_hi = rw.astype(BF16)
    rw_lo = (rw - rw_hi.astype(F32)).astype(BF16)
    rb = jnp.full((1, LANES), NEG_BIG, F32).at[0, :n_experts].set(router_b.astype(F32))
    x1, hm, idx, gates, rank, counts = _post_mix(
        x2d, yconv, u, y_rwkv, row2(s5_d), s5_w_glu.astype(BF16), w_out.astype(BF16),
        row2(norm_moe_g), rw_hi, rw_lo, rb, n_experts)

    tmb = EXPERT_TILE
    m = n * TOP_K
    n_blocks = m // tmb + n_experts
    cnt = counts[0, :n_experts]
    padded = ((cnt + tmb - 1) // tmb) * tmb
    pends = jnp.cumsum(padded)
    pstarts = pends - padded
    slots = (pstarts[idx[:, :TOP_K]] + rank[:, :TOP_K]).astype(jnp.int32).reshape(m)
    block_start = jnp.arange(n_blocks, dtype=jnp.int32) * tmb
    block_e = jnp.minimum(jnp.sum((pends[None, :] <= block_start[:, None]).astype(jnp.int32), axis=1),
                          n_experts - 1).astype(jnp.int32)
    n_used = (pends[-1] // tmb).astype(jnp.int32).reshape(1)

    xs = _dispatch(hm, slots, n_blocks * tmb)
    ys = _experts(xs, block_e, n_used, wg.astype(BF16), bg.reshape(n_experts, 1, -1).astype(F32),
                  wu.astype(BF16), bu.reshape(n_experts, 1, -1).astype(F32), wd.astype(BF16),
                  bd.reshape(n_experts, 1, -1).astype(F32))
    return x1, gates, slots, ys, p2d, row2(norm_ple_g), ple_w_proj.astype(BF16), ple_w_gate.astype(BF16)


def kernel(x, p, norm_mix_g, w_in, s5_lam_re, s5_lam_im, s5_log_dt, s5_b_re, s5_b_im, s5_c_re, s5_c_im, s5_d, s5_w_glu, rwkv_mu_rkv, rwkv_mu_wag, rwkv_w0, rwkv_w1, rwkv_w2, rwkv_a0, rwkv_a1, rwkv_a2, rwkv_g1, rwkv_g2, rwkv_k_k, rwkv_k_a, rwkv_r_k, rwkv_ln_w, rwkv_ln_b, w_out, norm_moe_g, router_w, router_b, exp_w_gate, exp_b_gate, exp_w_up, exp_b_up, exp_w_down, exp_b_down, norm_ple_g, ple_w_proj, ple_w_gate, final_norm_g):
    batch, seq, d = x.shape
    assert w_in.shape[0] == 1, "the final kernel fuses the last RMSNorm: single-layer stacks only"
    i = 0
    x1, gates, slots, ys, p2d, g_ple, w_proj, w_gate = _layer(
        x.reshape(batch * seq, d), p[i].reshape(batch * seq, -1), batch, seq, norm_mix_g[i], w_in[i],
        s5_lam_re[i], s5_lam_im[i], s5_log_dt[i], s5_b_re[i], s5_b_im[i], s5_c_re[i], s5_c_im[i],
        s5_d[i], s5_w_glu[i], rwkv_mu_rkv[i], rwkv_mu_wag[i], rwkv_w0[i], rwkv_w1[i], rwkv_w2[i],
        rwkv_a0[i], rwkv_a1[i], rwkv_a2[i], rwkv_g1[i], rwkv_g2[i], rwkv_k_k[i], rwkv_k_a[i],
        rwkv_r_k[i], rwkv_ln_w[i], rwkv_ln_b[i], w_out[i], norm_moe_g[i], router_w[i],
        router_b[i], exp_w_gate[i], exp_b_gate[i], exp_w_up[i], exp_b_up[i], exp_w_down[i],
        exp_b_down[i], norm_ple_g[i], ple_w_proj[i], ple_w_gate[i])
    out = _final(x1, gates, slots, p2d, ys, g_ple, w_proj, w_gate,
                 final_norm_g.reshape(1, -1).astype(F32))
    return out.reshape(batch, seq, d)
```

```python
import functools

import jax
import jax.numpy as jnp
from jax import lax
from jax.experimental import pallas as pl
from jax.experimental.pallas import tpu as pltpu

F32 = jnp.float32
BF16 = jnp.bfloat16

S5_GROUP = 16
S5_STATE = 64
RWKV_HEAD = 64
TOP_K = 4
RMS_EPS = 1e-6
RWKV_GN_EPS = 64e-5
LAMBDA_RE_MAX = -1e-4
SWIGLU_LIMIT = 7.0
SWIGLU_ALPHA = 1.702

LANES = 128
SUBLANES = 8
VMEM_LIMIT = 56 * 1024 * 1024

S5_CHUNK = 64
RWKV_CHUNK = 64
RWKV_SUB = 16
TOKEN_TILE = 256
EXPERT_TILE = 512
NEG_BIG = -1e30


def _dot(a, b):
    return jnp.dot(a, b, preferred_element_type=F32)


def _dot_nt(a, b):
    return lax.dot_general(a, b, (((1,), (1,)), ((), ())), preferred_element_type=F32)


def _dot_tn(a, b):
    return lax.dot_general(a, b, (((0,), (0,)), ((), ())), preferred_element_type=F32)


def _split_dot(a, b_bf16):
    hi = a.astype(BF16)
    lo = (a - hi.astype(F32)).astype(BF16)
    return _dot(hi, b_bf16) + _dot(lo, b_bf16)


def _rms(t, gain):
    return t * lax.rsqrt(jnp.mean(t * t, axis=-1, keepdims=True) + RMS_EPS) * gain


def _sigmoid(t):
    return 1.0 / (1.0 + jnp.exp(-t))


def _params(*sem):
    return pltpu.CompilerParams(dimension_semantics=sem, vmem_limit_bytes=VMEM_LIMIT)


def _full(shape):
    return pl.BlockSpec(shape, lambda *_: (0,) * len(shape))


def _mix_in_kernel(seq, x_ref, xp_ref, gain_ref, win_ref, w1_ref, a1_ref, g1_ref, muwag_ref,
                   murkv_ref, vec_ref, w2_ref, a2_ref, g2_ref, ones_ref,
                   u_o, r_o, lw_o, k_o, v_o, an_o, bb_o, g_o, bonus_o):
    tm = x_ref.shape[0]
    width = u_o.shape[1]
    gain = gain_ref[...]
    h = _rms(x_ref[...], gain)
    keep = jnp.where((pl.program_id(0) * tm) % seq == 0, 0.0, 1.0)
    hp = _rms(xp_ref[...], gain) * keep
    h_ext = jnp.concatenate([hp, h], axis=0)
    hs = pltpu.roll(h_ext, 1, 0)[SUBLANES:]
    proj = _dot(h_ext.astype(BF16), win_ref[...])
    rkv = proj[:, width:]
    rkv_cur = rkv[SUBLANES:]
    rkv_prev = pltpu.roll(rkv, 1, 0)[SUBLANES:]
    u_o[...] = proj[SUBLANES:, :width]

    dh = hs - h
    xw = (h + dh * muwag_ref[0:1, :]).astype(BF16)
    xa = (h + dh * muwag_ref[1:2, :]).astype(BF16)
    xg = (h + dh * muwag_ref[2:3, :]).astype(BF16)
    lw_hidden = jnp.tanh(_dot(xw, w1_ref[...])).astype(BF16)
    la_hidden = _dot(xa, a1_ref[...]).astype(BF16)
    lg_hidden = _sigmoid(_dot(xg, g1_ref[...])).astype(BF16)

    w0 = vec_ref[0:1, :]
    a0 = vec_ref[1:2, :]
    k_k = vec_ref[2:3, :]
    k_a = vec_ref[3:4, :]
    r_k = vec_ref[4:5, :]
    z = w0 + _dot(lw_hidden, w2_ref[...])
    nz = -z
    softplus = jnp.maximum(nz, 0.0) + jnp.log(1.0 + jnp.exp(-jnp.abs(nz)))
    lw_o[...] = -jnp.exp(-softplus - 0.5)
    a = _sigmoid(a0 + _dot(la_hidden, a2_ref[...]))
    g_o[...] = _dot(lg_hidden, g2_ref[...])

    r_cur = rkv_cur[:, :width]
    k_cur = rkv_cur[:, width:2 * width]
    v_cur = rkv_cur[:, 2 * width:]
    r = r_cur + (rkv_prev[:, :width] - r_cur) * murkv_ref[0:1, :]
    k = k_cur + (rkv_prev[:, width:2 * width] - k_cur) * murkv_ref[1:2, :]
    v = v_cur + (rkv_prev[:, 2 * width:] - v_cur) * murkv_ref[2:3, :]

    ones = ones_ref[...]
    kk = k * k_k
    norm = jnp.sqrt(_split_dot(kk * kk, ones))
    kkn = kk / jnp.maximum(norm, 1e-12)
    k2 = k * (1.0 + (a - 1.0) * k_a)
    r_o[...] = r
    k_o[...] = k2
    v_o[...] = v
    an_o[...] = -kkn
    bb_o[...] = kkn * a
    bonus_o[...] = _split_dot(r * k2 * r_k, ones) * v


def _mix_in(x2d, seq, gain, w_in, w1, a1, g1, mu_wag, mu_rkv, vecs, w2, a2, g2, head_ones):
    n, d = x2d.shape
    width = w_in.shape[1] // 4
    tm = TOKEN_TILE
    prev_blocks = tm // SUBLANES
    row = lambda i: (i, 0)
    out = jax.ShapeDtypeStruct((n, width), F32)
    return pl.pallas_call(
        functools.partial(_mix_in_kernel, seq),
        grid=(n // tm,),
        in_specs=[
            pl.BlockSpec((tm, d), row),
            pl.BlockSpec((SUBLANES, d), lambda i: (jnp.maximum(i * prev_blocks - 1, 0), 0)),
            _full(gain.shape), _full(w_in.shape), _full(w1.shape), _full(a1.shape), _full(g1.shape),
            _full(mu_wag.shape), _full(mu_rkv.shape), _full(vecs.shape), _full(w2.shape),
            _full(a2.shape), _full(g2.shape), _full(head_ones.shape),
        ],
        out_specs=[pl.BlockSpec((tm, width), row)] * 9,
        out_shape=[out] * 9,
        compiler_params=_params("parallel"),
        name="mix_in",
    )(x2d, x2d, gain, w_in, w1, a1, g1, mu_wag, mu_rkv, vecs, w2, a2, g2, head_ones)


def _s5_tables(lam_re, lam_im, log_dt, b_re, b_im, c_re, c_im, n_chunks):
    t = S5_CHUNK
    hi = lax.Precision.HIGHEST
    lre = jnp.minimum(lam_re.astype(F32), LAMBDA_RE_MAX)
    lim = lam_im.astype(F32)
    dt = jnp.exp(log_dt.astype(F32))[:, None]
    mag = jnp.exp(lre * dt)
    lb_re = mag * jnp.cos(lim * dt)
    lb_im = mag * jnp.sin(lim * dt)
    den = lre * lre + lim * lim
    z_re = lb_re - 1.0
    coef_re = ((z_re * lre + lb_im * lim) / den)[..., None]
    coef_im = ((lb_im * lre - z_re * lim) / den)[..., None]
    b_re = b_re.astype(F32)
    b_im = b_im.astype(F32)
    bb_re = coef_re * b_re - coef_im * b_im
    bb_im = coef_re * b_im + coef_im * b_re
    c_re = c_re.astype(F32)
    c_im = c_im.astype(F32)

    def power(e):
        e = e.astype(F32)[:, None, None]
        m = jnp.exp(e * (lre * dt))
        ang = e * (lim * dt)
        return m * jnp.cos(ang), m * jnp.sin(ang)

    p_re, p_im = power(jnp.arange(t + 1))
    ab_re = p_re[:t, :, :, None] * bb_re - p_im[:t, :, :, None] * bb_im
    ab_im = p_re[:t, :, :, None] * bb_im + p_im[:t, :, :, None] * bb_re
    kern = (jnp.einsum('gcp,tgpd->gtcd', c_re, ab_re, precision=hi)
            - jnp.einsum('gcp,tgpd->gtcd', c_im, ab_im, precision=hi))
    s_idx = jnp.arange(t)[:, None]
    t_idx = jnp.arange(t)[None, :]
    lag = t_idx - s_idx
    toep = kern[:, jnp.maximum(lag, 0)]
    toep = jnp.where((lag >= 0)[None, :, :, None, None], toep, 0.0)
    g = toep.shape[0]
    c = S5_GROUP
    conv = toep.transpose(0, 4, 1, 3, 2).reshape(g, c * t, c * t)
    st_re = ab_re[::-1].transpose(1, 3, 0, 2).reshape(g, c * t, S5_STATE)
    st_im = ab_im[::-1].transpose(1, 3, 0, 2).reshape(g, c * t, S5_STATE)
    conv_full = jnp.concatenate([conv, st_re, st_im], axis=2).astype(BF16)
    q_re, q_im = p_re[1:], p_im[1:]
    ca_re = c_re[:, :, None, :] * q_re.transpose(1, 0, 2)[:, None] - c_im[:, :, None, :] * q_im.transpose(1, 0, 2)[:, None]
    ca_im = c_re[:, :, None, :] * q_im.transpose(1, 0, 2)[:, None] + c_im[:, :, None, :] * q_re.transpose(1, 0, 2)[:, None]
    cross = jnp.concatenate([ca_re.transpose(0, 3, 1, 2).reshape(g, S5_STATE, c * t),
                             -ca_im.transpose(0, 3, 1, 2).reshape(g, S5_STATE, c * t)], axis=1).astype(BF16)
    n_steps = max(1, (n_chunks - 1).bit_length())
    s_re, s_im = power(t * (2 ** jnp.arange(n_steps)))
    tab = jnp.stack([jnp.concatenate([s_re, s_re], axis=-1),
                     jnp.concatenate([-s_im, s_im], axis=-1)], axis=1)
    tab = tab.transpose(2, 0, 1, 3).reshape(g, 2 * n_steps, 2 * S5_STATE)
    return conv_full, cross, tab, n_steps


def _s5_conv_kernel(n_chunks, n_steps, u_ref, conv_ref, cross_ref, tab_ref, y_ref):
    ct = cross_ref.shape[2]
    z = _dot(u_ref[0], conv_ref[0])
    y = z[:, :ct]
    x = z[:, ct:]
    rows = x.shape[0]
    chunk = lax.broadcasted_iota(jnp.int32, x.shape, 0) % n_chunks
    half = x.shape[1] // 2
    for j in range(n_steps):
        sh = 1 << j
        xs = jnp.where(chunk >= sh, pltpu.roll(x, sh, 0), 0.0)
        x = x + tab_ref[0, 2 * j:2 * j + 1, :] * xs + tab_ref[0, 2 * j + 1:2 * j + 2, :] * pltpu.roll(xs, half, 1)
    x_in = jnp.where(chunk >= 1, pltpu.roll(x, 1, 0), 0.0)
    del rows
    y_ref[0] = y + _dot(x_in.astype(BF16), cross_ref[0])


def _s5_conv(u2d, batch, seq, tables):
    conv_full, cross, tab, n_steps = tables
    n, width = u2d.shape
    g = width // S5_GROUP
    t = S5_CHUNK
    n_chunks = seq // t
    ct = S5_GROUP * t
    ut = u2d.astype(BF16).reshape(batch, n_chunks, t, g, S5_GROUP).transpose(3, 0, 1, 4, 2)
    ut = ut.reshape(g, batch * n_chunks, ct)
    rows = batch * n_chunks
    y = pl.pallas_call(
        functools.partial(_s5_conv_kernel, n_chunks, n_steps),
        grid=(g,),
        in_specs=[
            pl.BlockSpec((1, rows, ct), lambda i: (i, 0, 0)),
            pl.BlockSpec((1,) + conv_full.shape[1:], lambda i: (i, 0, 0)),
            pl.BlockSpec((1,) + cross.shape[1:], lambda i: (i, 0, 0)),
            pl.BlockSpec((1,) + tab.shape[1:], lambda i: (i, 0, 0)),
        ],
        out_specs=pl.BlockSpec((1, rows, ct), lambda i: (i, 0, 0)),
        out_shape=jax.ShapeDtypeStruct((g, rows, ct), F32),
        compiler_params=_params("parallel"),
        name="s5_conv",
    )(ut, conv_full, cross, tab)
    return y.reshape(g, batch, n_chunks, S5_GROUP, t).transpose(1, 2, 4, 0, 3).reshape(n, width)


def _pair_blockdiag(y, left):
    return jnp.concatenate([jnp.where(left, y, 0.0), jnp.where(left, 0.0, y)], axis=0).astype(BF16)


def _rwkv_chunk_kernel(r_ref, lw_ref, k_ref, v_ref, an_ref, bb_ref, rhat_o, yhat_o, g_o, h_o):
    t = RWKV_CHUNK
    rows, width = r_ref.shape
    n_chunks = rows // t
    pairs = width // LANES
    row = lax.broadcasted_iota(jnp.int32, (t, LANES), 0)
    lane = lax.broadcasted_iota(jnp.int32, (t, LANES), 1)
    col = jnp.bitwise_and(lane, RWKV_HEAD - 1)
    left = lane < RWKV_HEAD
    incl = row >= col
    strict = row > col
    same_blk = (row // RWKV_SUB) == (col // RWKV_SUB)
    eye = jnp.where(row == col, 1.0, 0.0).astype(F32)
    brow = lax.broadcasted_iota(jnp.int32, (LANES, LANES), 0)
    bcol = lax.broadcasted_iota(jnp.int32, (LANES, LANES), 1)
    same_head = (brow // RWKV_HEAD) == (bcol // RWKV_HEAD)
    eye_full = brow == bcol
    crow = lax.broadcasted_iota(jnp.int32, (rows, rows), 0)
    ccol = lax.broadcasted_iota(jnp.int32, (rows, rows), 1)
    tril = jnp.where(crow >= ccol, jnp.where((crow // t) == (ccol // t), 1.0, 0.0), 0.0).astype(BF16)

    lw = lw_ref[...]
    p1 = lw.astype(BF16)
    rem = lw - p1.astype(F32)
    p2 = rem.astype(BF16)
    p3 = (rem - p2.astype(F32)).astype(BF16)
    cs = _dot(tril, p1) + _dot(tril, p2) + _dot(tril, p3)
    a_t = an_ref[...] * jnp.exp(cs - lw)
    r_t = r_ref[...] * jnp.exp(cs)
    p_inv = jnp.exp(-cs)
    b_t = bb_ref[...] * p_inv
    k_t = k_ref[...] * p_inv

    def bd(y):
        return _pair_blockdiag(y, left)

    def pmm(x, y_bd):
        return _dot(x.astype(BF16), y_bd)

    units = [(c, j) for c in range(n_chunks) for j in range(pairs)]

    def tile(arr, c, j):
        return arr[c * t:(c + 1) * t, j * LANES:(j + 1) * LANES]

    a2 = [tile(a_t, c, j) for c, j in units]
    r2 = [tile(r_t, c, j) for c, j in units]
    v2 = [tile(v_ref[...], c, j) for c, j in units]
    ar = [jnp.concatenate([a, r], axis=0).astype(BF16) for a, r in zip(a2, r2)]
    prod_b = [_dot_nt(x, bd(tile(b_t, c, j))) for x, (c, j) in zip(ar, units)]
    prod_k = [_dot_nt(x, bd(tile(k_t, c, j))) for x, (c, j) in zip(ar, units)]
    l_ab = [jnp.where(strict, p[:t], 0.0) for p in prod_b]
    l_rb = [jnp.where(incl, p[t:], 0.0) for p in prod_b]
    l_ak = [jnp.where(strict, p[:t], 0.0) for p in prod_k]
    l_rk = [jnp.where(incl, p[t:], 0.0) for p in prod_k]
    l_d = [jnp.where(same_blk, m, 0.0) for m in l_ab]
    l_o = [m - d for m, d in zip(l_ab, l_d)]
    l2 = [pmm(m, bd(m)) for m in l_d]
    q1 = [pmm(eye + m, bd(eye + s)) for m, s in zip(l_d, l2)]
    l4 = [pmm(m, bd(m)) for m in l2]
    q2 = [pmm(q, bd(eye + s)) for q, s in zip(q1, l4)]
    l8 = [pmm(m, bd(m)) for m in l4]
    d_inv = [pmm(q, bd(eye + s)) for q, s in zip(q2, l8)]
    x1 = [pmm(m, bd(v)) for m, v in zip(l_ak, v2)]

    def bd2(z1, z2):
        return jnp.concatenate([bd(z1), bd(z2)], axis=1)

    dz = [pmm(d, bd2(a, x)) for d, a, x in zip(d_inv, a2, x1)]
    n1 = [pmm(d, bd(o)) for d, o in zip(d_inv, l_o)]
    n2 = [pmm(m, bd(m)) for m in n1]
    t1 = [z + pmm(m, bd2(z[:, :LANES], z[:, LANES:])) for z, m in zip(dz, n2)]
    wu = [z + pmm(m, bd2(z[:, :LANES], z[:, LANES:])) for z, m in zip(t1, n1)]
    ry = [pmm(m, bd2(z[:, :LANES], z[:, LANES:])) for m, z in zip(l_rb, wu)]
    rk_v = [pmm(m, bd(v)) for m, v in zip(l_rk, v2)]

    for i, (c, j) in enumerate(units):
        rs = slice(c * t, (c + 1) * t)
        ls = slice(j * LANES, (j + 1) * LANES)
        rhat_o[rs, ls] = r2[i] + ry[i][:, :LANES]
        yhat_o[rs, ls] = ry[i][:, LANES:] + rk_v[i]
        cs_c = cs[rs, ls]
        cs_end = cs_c[t - 1:t, :]
        p_end = jnp.exp(cs_end - cs_c)
        b_h = (bb_ref[rs, ls] * p_end).astype(BF16)
        k_h = (k_ref[rs, ls] * p_end).astype(BF16)
        g_full = _dot_tn(wu[i][:, :LANES].astype(BF16), b_h)
        g_bd = jnp.where(same_head, g_full, 0.0) + jnp.where(eye_full, jnp.exp(cs_end), 0.0)
        g_o[c, j] = g_bd.astype(BF16)
        uv = jnp.concatenate([wu[i][:, LANES:], v2[i]], axis=0).astype(BF16)
        h_full = _dot_tn(uv, jnp.concatenate([b_h, k_h], axis=0))
        h_o[c, j] = jnp.where(left, h_full[:RWKV_HEAD], h_full[RWKV_HEAD:])


RWKV_CHUNKS_PER_STEP = 2


def _rwkv_chunks(r, lw, k, v, an, bb):
    n, width = r.shape
    t = RWKV_CHUNK
    cps = RWKV_CHUNKS_PER_STEP
    pairs = width // LANES
    nck = n // t
    tok = pl.BlockSpec((cps * t, width), lambda i: (i, 0))
    return pl.pallas_call(
        _rwkv_chunk_kernel,
        grid=(nck // cps,),
        in_specs=[tok] * 6,
        out_specs=[tok, tok,
                   pl.BlockSpec((cps, pairs, LANES, LANES), lambda i: (i, 0, 0, 0)),
                   pl.BlockSpec((cps, pairs, RWKV_HEAD, LANES), lambda i: (i, 0, 0, 0))],
        out_shape=[jax.ShapeDtypeStruct((n, width), F32)] * 2
        + [jax.ShapeDtypeStruct((nck, pairs, LANES, LANES), BF16),
           jax.ShapeDtypeStruct((nck, pairs, RWKV_HEAD, LANES), F32)],
        compiler_params=_params("parallel"),
        name="rwkv_chunk",
    )(r, lw, k, v, an, bb)


def _rwkv_scan_kernel(rhat_ref, yhat_ref, g_ref, h_ref, gate_ref, bonus_ref, lnw_ref, lnb_ref,
                      ones_ref, y_ref, state):
    t = RWKV_CHUNK
    batch, pairs = state.shape[0], state.shape[1]
    n_chunks = rhat_ref.shape[1] // t
    lane = lax.broadcasted_iota(jnp.int32, (RWKV_HEAD, LANES), 1)
    left = lane < RWKV_HEAD

    @pl.when(pl.program_id(0) == 0)
    def _():
        state[...] = jnp.zeros_like(state)

    s = [[state[b, j] for j in range(pairs)] for b in range(batch)]
    y_rows = []
    for b in range(batch):
        chunk_rows = []
        for c in range(n_chunks):
            rs = slice(c * t, (c + 1) * t)
            tiles = []
            for j in range(pairs):
                ls = slice(j * LANES, (j + 1) * LANES)
                s_bd = _pair_blockdiag(s[b][j], left)
                tiles.append(yhat_ref[b, rs, ls] + _dot_nt(rhat_ref[b, rs, ls].astype(BF16), s_bd))
                s[b][j] = _dot(s[b][j].astype(BF16), g_ref[b, c, j]) + h_ref[b, c, j]
            chunk_rows.append(jnp.concatenate(tiles, axis=1))
        y_rows.append(jnp.concatenate(chunk_rows, axis=0))
    for b in range(batch):
        for j in range(pairs):
            state[b, j] = s[b][j]

    ones = ones_ref[...]
    inv = 1.0 / RWKV_HEAD
    for b in range(batch):
        y = y_rows[b]
        mean = _split_dot(y, ones) * inv
        var = _split_dot(y * y, ones) * inv - mean * mean
        yn = (y - mean) * lax.rsqrt(var + RWKV_GN_EPS) * lnw_ref[...] + lnb_ref[...]
        y_ref[b] = (yn + bonus_ref[b]) * gate_ref[b]


def _rwkv_scan(rhat, yhat, g_mat, h_mat, gate, bonus, ln_w, ln_b, head_ones, batch, seq):
    n, width = rhat.shape
    t = RWKV_CHUNK
    cps = RWKV_CHUNKS_PER_STEP
    pairs = width // LANES
    nc = seq // t
    tok3 = lambda a: a.reshape(batch, seq, width)
    tok = pl.BlockSpec((batch, cps * t, width), lambda i: (0, i, 0))
    out = pl.pallas_call(
        _rwkv_scan_kernel,
        grid=(nc // cps,),
        in_specs=[tok, tok,
                  pl.BlockSpec((batch, cps, pairs, LANES, LANES), lambda i: (0, i, 0, 0, 0)),
                  pl.BlockSpec((batch, cps, pairs, RWKV_HEAD, LANES), lambda i: (0, i, 0, 0, 0)),
                  tok, tok, _full(ln_w.shape), _full(ln_b.shape), _full(head_ones.shape)],
        out_specs=tok,
        out_shape=jax.ShapeDtypeStruct((batch, seq, width), F32),
        scratch_shapes=[pltpu.VMEM((batch, pairs, RWKV_HEAD, LANES), F32)],
        compiler_params=_params("arbitrary"),
        name="rwkv_scan",
    )(tok3(rhat), tok3(yhat), g_mat.reshape(batch, nc, pairs, LANES, LANES),
      h_mat.reshape(batch, nc, pairs, RWKV_HEAD, LANES), tok3(gate), tok3(bonus), ln_w, ln_b, head_ones)
    return out.reshape(n, width)


def _post_mix_kernel(n_experts, x_ref, yc_ref, u_ref, yr_ref, d_ref, wglu_ref, wout_ref, gain_ref,
                     rw_hi_ref, rw_lo_ref, rb_ref, x1_o, hm_o, idx_o, gate_o, rank_o, cnt_o, running):
    tm = x_ref.shape[0]
    width = yc_ref.shape[1]

    @pl.when(pl.program_id(0) == 0)
    def _():
        running[...] = jnp.zeros_like(running)

    y = yc_ref[...] + d_ref[...] * u_ref[...]
    y = 0.5 * y * (1.0 + jnp.tanh(0.7978845608028654 * (y + 0.044715 * (y * y * y))))
    y = y * _sigmoid(_dot(y.astype(BF16), wglu_ref[...]))
    x1 = (x_ref[...] + _dot(y.astype(BF16), wout_ref[:width, :])
          + _dot(yr_ref[...].astype(BF16), wout_ref[width:, :]))
    x1_o[...] = x1
    hm = _rms(x1, gain_ref[...])
    hm_o[...] = hm
    h_hi = hm.astype(BF16)
    h_lo = (hm - h_hi.astype(F32)).astype(BF16)
    logits = (_dot(h_hi, rw_hi_ref[...]) + _dot(h_lo, rw_hi_ref[...]) + _dot(h_hi, rw_lo_ref[...])
              + rb_ref[...])

    lane = lax.broadcasted_iota(jnp.int32, logits.shape, 1)
    lanef = lane.astype(F32)
    sel = jnp.zeros(logits.shape, F32)
    idx_cols, val_cols = [], []
    work = logits
    for _ in range(TOP_K):
        m = jnp.max(work, axis=-1, keepdims=True)
        pick = jnp.min(jnp.where(work == m, lanef, float(LANES)), axis=-1, keepdims=True)
        hit = lanef == pick
        sel = jnp.where(hit, 1.0, sel)
        work = jnp.where(hit, -jnp.inf, work)
        idx_cols.append(pick)
        val_cols.append(m)
    exps = [jnp.exp(vv - val_cols[0]) for vv in val_cols]
    denom = exps[0] + exps[1] + exps[2] + exps[3]

    row = lax.broadcasted_iota(jnp.int32, (tm, tm), 0)
    col = lax.broadcasted_iota(jnp.int32, (tm, tm), 1)
    before = jnp.where(row > col, 1.0, 0.0).astype(BF16)
    excl = _dot(before, sel.astype(BF16)) + running[...]
    idx_out = jnp.zeros(logits.shape, F32)
    gate_out = jnp.zeros(logits.shape, F32)
    rank_out = jnp.zeros(logits.shape, F32)
    for j in range(TOP_K):
        rk = jnp.sum(jnp.where(lanef == idx_cols[j], excl, 0.0), axis=-1, keepdims=True)
        idx_out = jnp.where(lane == j, idx_cols[j], idx_out)
        gate_out = jnp.where(lane == j, exps[j] / denom, gate_out)
        rank_out = jnp.where(lane == j, rk, rank_out)
    idx_o[...] = idx_out.astype(jnp.int32)
    gate_o[...] = gate_out
    rank_o[...] = rank_out.astype(jnp.int32)
    running[...] = running[...] + jnp.sum(sel, axis=0, keepdims=True)
    cnt_o[...] = jnp.broadcast_to(running[...], cnt_o.shape).astype(jnp.int32)
    del n_experts


def _post_mix(x2d, yconv, u, y_rwkv, s5_d, w_glu, w_out, gain, rw_hi, rw_lo, rb, n_experts):
    n, d = x2d.shape
    width = yconv.shape[1]
    tm = TOKEN_TILE
    row = lambda i: (i, 0)
    tok_d = pl.BlockSpec((tm, d), row)
    tok_w = pl.BlockSpec((tm, width), row)
    tok_l = pl.BlockSpec((tm, LANES), row)
    return pl.pallas_call(
        functools.partial(_post_mix_kernel, n_experts),
        grid=(n // tm,),
        in_specs=[tok_d, tok_w, tok_w, tok_w, _full(s5_d.shape), _full(w_glu.shape),
                  _full(w_out.shape), _full(gain.shape), _full(rw_hi.shape), _full(rw_lo.shape),
                  _full(rb.shape)],
        out_specs=[tok_d, tok_d, tok_l, tok_l, tok_l, _full((SUBLANES, LANES))],
        out_shape=[jax.ShapeDtypeStruct((n, d), F32), jax.ShapeDtypeStruct((n, d), F32),
                   jax.ShapeDtypeStruct((n, LANES), jnp.int32), jax.ShapeDtypeStruct((n, LANES), F32),
                   jax.ShapeDtypeStruct((n, LANES), jnp.int32),
                   jax.ShapeDtypeStruct((SUBLANES, LANES), jnp.int32)],
        scratch_shapes=[pltpu.VMEM((1, LANES), F32)],
        compiler_params=_params("arbitrary"),
        name="post_mix",
    )(x2d, yconv, u, y_rwkv, s5_d, w_glu, w_out, gain, rw_hi, rw_lo, rb)


def _row_copy(src_ref, src_row, dst_ref, dst_row, sem):
    return pltpu.make_async_copy(src_ref.at[pl.ds(src_row, 1)], dst_ref.at[pl.ds(dst_row, 1)], sem)


def _dispatch_kernel(slot_ref, hm_ref, xs_in_ref, xs_ref, sem):
    del xs_in_ref
    tm = hm_ref.shape[0]

    def start(r, c):
        for j in range(TOP_K):
            _row_copy(hm_ref, r, xs_ref, slot_ref[r * TOP_K + j], sem).start()
        return c

    lax.fori_loop(0, tm, start, 0)

    def wait(r, c):
        for j in range(TOP_K):
            _row_copy(hm_ref, 0, xs_ref, 0, sem).wait()
        return c

    lax.fori_loop(0, tm, wait, 0)


def _dispatch(hm, slots_flat, m_pad):
    n, d = hm.shape
    tm = TOKEN_TILE
    xs0 = jnp.zeros((m_pad, d), hm.dtype)
    return pl.pallas_call(
        _dispatch_kernel,
        grid=(n // tm,),
        in_specs=[pl.BlockSpec((tm * TOP_K,), lambda i: (i,), memory_space=pltpu.SMEM),
                  pl.BlockSpec((tm, d), lambda i: (i, 0)),
                  pl.BlockSpec(memory_space=pl.ANY)],
        out_specs=pl.BlockSpec(memory_space=pl.ANY),
        out_shape=jax.ShapeDtypeStruct((m_pad, d), hm.dtype),
        scratch_shapes=[pltpu.SemaphoreType.DMA(())],
        input_output_aliases={2: 0},
        compiler_params=_params("arbitrary"),
        name="dispatch",
    )(slots_flat, hm, xs0)


def _experts_kernel(be_ref, nu_ref, xs_ref, wg_ref, bg_ref, wu_ref, bu_ref, wd_ref, bd_ref, ys_ref):
    del be_ref

    @pl.when(pl.program_id(0) < nu_ref[0])
    def _():
        xb = xs_ref[...].astype(BF16)
        gt = jnp.minimum(_dot(xb, wg_ref[0]) + bg_ref[0], SWIGLU_LIMIT)
        up = jnp.clip(_dot(xb, wu_ref[0]) + bu_ref[0], -SWIGLU_LIMIT, SWIGLU_LIMIT)
        act = (up + 1.0) * gt * _sigmoid(SWIGLU_ALPHA * gt)
        ys_ref[...] = _dot(act.astype(BF16), wd_ref[0]) + bd_ref[0]

    @pl.when(pl.program_id(0) >= nu_ref[0])
    def _():
        ys_ref[...] = jnp.zeros_like(ys_ref)


def _experts(xs, block_e, n_used, wg, bg, wu, bu, wd, bd):
    m_pad, d = xs.shape
    tmb = EXPERT_TILE
    dff = wg.shape[2]
    wmap = lambda i, be, nu: (be[i], 0, 0)
    grid_spec = pltpu.PrefetchScalarGridSpec(
        num_scalar_prefetch=2,
        grid=(m_pad // tmb,),
        in_specs=[pl.BlockSpec((tmb, d), lambda i, be, nu: (i, 0)),
                  pl.BlockSpec((1, d, dff), wmap), pl.BlockSpec((1, 1, dff), wmap),
                  pl.BlockSpec((1, d, dff), wmap), pl.BlockSpec((1, 1, dff), wmap),
                  pl.BlockSpec((1, dff, d), wmap), pl.BlockSpec((1, 1, d), wmap)],
        out_specs=pl.BlockSpec((tmb, d), lambda i, be, nu: (i, 0)),
    )
    return pl.pallas_call(
        _experts_kernel,
        grid_spec=grid_spec,
        out_shape=jax.ShapeDtypeStruct((m_pad, d), F32),
        compiler_params=_params("arbitrary"),
        name="experts",
    )(block_e, n_used, xs, wg, bg, wu, bu, wd, bd)


def _final_kernel(slot_ref, x1_ref, gate_ref, p_ref, ys_ref, gple_ref, wproj_ref, wgate_ref,
                  gfin_ref, out_ref, buf, sem):
    tm = x1_ref.shape[0]

    def start(r, c):
        for j in range(TOP_K):
            _row_copy(ys_ref, slot_ref[r * TOP_K + j], buf.at[j], r, sem).start()
        return c

    lax.fori_loop(0, tm, start, 0)

    def wait(r, c):
        for j in range(TOP_K):
            _row_copy(ys_ref, 0, buf.at[j], 0, sem).wait()
        return c

    lax.fori_loop(0, tm, wait, 0)

    x2 = x1_ref[...]
    gates = gate_ref[...]
    for j in range(TOP_K):
        x2 = x2 + gates[:, j:j + 1] * buf[j]
    gate = _sigmoid(_dot(_rms(x2, gple_ref[...]).astype(BF16), wgate_ref[...]))
    x3 = x2 + _dot(p_ref[...].astype(BF16), wproj_ref[...]) * gate
    out_ref[...] = _rms(x3, gfin_ref[...])


def _final(x1, gates, slots_flat, p2d, ys, g_ple, w_proj, w_gate, g_fin):
    n, d = x1.shape
    tm = TOKEN_TILE
    row = lambda i: (i, 0)
    return pl.pallas_call(
        _final_kernel,
        grid=(n // tm,),
        in_specs=[pl.BlockSpec((tm * TOP_K,), lambda i: (i,), memory_space=pltpu.SMEM),
                  pl.BlockSpec((tm, d), row), pl.BlockSpec((tm, LANES), row),
                  pl.BlockSpec((tm, p2d.shape[1]), row), pl.BlockSpec(memory_space=pl.ANY),
                  _full(g_ple.shape), _full(w_proj.shape), _full(w_gate.shape), _full(g_fin.shape)],
        out_specs=pl.BlockSpec((tm, d), row),
        out_shape=jax.ShapeDtypeStruct((n, d), F32),
        scratch_shapes=[pltpu.VMEM((TOP_K, tm, d), F32), pltpu.SemaphoreType.DMA(())],
        compiler_params=_params("arbitrary"),
        name="final",
    )(slots_flat, x1, gates, p2d, ys, g_ple, w_proj, w_gate, g_fin)


def _head_ones(width):
    hd = jnp.arange(width) // RWKV_HEAD
    return (hd[:, None] == hd[None, :]).astype(BF16)


def _layer(x2d, p2d, batch, seq, norm_mix_g, w_in, s5_lam_re, s5_lam_im, s5_log_dt, s5_b_re, s5_b_im,
           s5_c_re, s5_c_im, s5_d, s5_w_glu, mu_rkv, mu_wag, w0, w1, w2, a0, a1, a2, g1, g2, k_k, k_a,
           r_k, ln_w, ln_b, w_out, norm_moe_g, router_w, router_b, wg, bg, wu, bu, wd, bd,
           norm_ple_g, ple_w_proj, ple_w_gate):
    n, d = x2d.shape
    width = w_in.shape[1] // 4
    n_experts = router_w.shape[1]
    row2 = lambda t: t.reshape(1, -1).astype(F32)
    head_ones = _head_ones(width)
    vecs = jnp.zeros((SUBLANES, width), F32)
    vecs = vecs.at[0].set(w0).at[1].set(a0).at[2].set(k_k).at[3].set(k_a).at[4].set(r_k.reshape(-1))

    u, r, lw, k, v, an, bb, gate, bonus = _mix_in(
        x2d, seq, row2(norm_mix_g), w_in.astype(BF16), w1.astype(BF16), a1.astype(BF16),
        g1.astype(BF16), mu_wag.astype(F32), mu_rkv.astype(F32), vecs, w2.astype(BF16),
        a2.astype(BF16), g2.astype(BF16), head_ones)

    tables = _s5_tables(s5_lam_re, s5_lam_im, s5_log_dt, s5_b_re, s5_b_im, s5_c_re, s5_c_im,
                        seq // S5_CHUNK)
    yconv = _s5_conv(u, batch, seq, tables)

    rhat, yhat, g_mat, h_mat = _rwkv_chunks(r, lw, k, v, an, bb)
    y_rwkv = _rwkv_scan(rhat, yhat, g_mat, h_mat, gate, bonus, row2(ln_w), row2(ln_b), head_ones,
                        batch, seq)

    rw = jnp.zeros((d, LANES), F32).at[:, :n_experts].set(router_w.astype(F32))
    rw_hi = rw.astype(BF16)
    rw_lo = (rw - rw_hi.astype(F32)).astype(BF16)
    rb = jnp.full((1, LANES), NEG_BIG, F32).at[0, :n_experts].set(router_b.astype(F32))
    x1, hm, idx, gates, rank, counts = _post_mix(
        x2d, yconv, u, y_rwkv, row2(s5_d), s5_w_glu.astype(BF16), w_out.astype(BF16),
        row2(norm_moe_g), rw_hi, rw_lo, rb, n_experts)

    tmb = EXPERT_TILE
    m = n * TOP_K
    n_blocks = m // tmb + n_experts
    cnt = counts[0, :n_experts]
    padded = ((cnt + tmb - 1) // tmb) * tmb
    pends = jnp.cumsum(padded)
    pstarts = pends - padded
    slots = (pstarts[idx[:, :TOP_K]] + rank[:, :TOP_K]).astype(jnp.int32).reshape(m)
    block_start = jnp.arange(n_blocks, dtype=jnp.int32) * tmb
    block_e = jnp.minimum(jnp.sum((pends[None, :] <= block_start[:, None]).astype(jnp.int32), axis=1),
                          n_experts - 1).astype(jnp.int32)
    n_used = (pends[-1] // tmb).astype(jnp.int32).reshape(1)

    xs = _dispatch(hm, slots, n_blocks * tmb)
    ys = _experts(xs, block_e, n_used, wg.astype(BF16), bg.reshape(n_experts, 1, -1).astype(F32),
                  wu.astype(BF16), bu.reshape(n_experts, 1, -1).astype(F32), wd.astype(BF16),
                  bd.reshape(n_experts, 1, -1).astype(F32))
    return x1, gates, slots, ys, p2d, row2(norm_ple_g), ple_w_proj.astype(BF16), ple_w_gate.astype(BF16)


def kernel(x, p, norm_mix_g, w_in, s5_lam_re, s5_lam_im, s5_log_dt, s5_b_re, s5_b_im, s5_c_re, s5_c_im, s5_d, s5_w_glu, rwkv_mu_rkv, rwkv_mu_wag, rwkv_w0, rwkv_w1, rwkv_w2, rwkv_a0, rwkv_a1, rwkv_a2, rwkv_g1, rwkv_g2, rwkv_k_k, rwkv_k_a, rwkv_r_k, rwkv_ln_w, rwkv_ln_b, w_out, norm_moe_g, router_w, router_b, exp_w_gate, exp_b_gate, exp_w_up, exp_b_up, exp_w_down, exp_b_down, norm_ple_g, ple_w_proj, ple_w_gate, final_norm_g):
    batch, seq, d = x.shape
    assert w_in.shape[0] == 1, "the final kernel fuses the last RMSNorm: single-layer stacks only"
    i = 0
    x1, gates, slots, ys, p2d, g_ple, w_proj, w_gate = _layer(
        x.reshape(batch * seq, d), p[i].reshape(batch * seq, -1), batch, seq, norm_mix_g[i], w_in[i],
        s5_lam_re[i], s5_lam_im[i], s5_log_dt[i], s5_b_re[i], s5_b_im[i], s5_c_re[i], s5_c_im[i],
        s5_d[i], s5_w_glu[i], rwkv_mu_rkv[i], rwkv_mu_wag[i], rwkv_w0[i], rwkv_w1[i], rwkv_w2[i],
        rwkv_a0[i], rwkv_a1[i], rwkv_a2[i], rwkv_g1[i], rwkv_g2[i], rwkv_k_k[i], rwkv_k_a[i],
        rwkv_r_k[i], rwkv_ln_w[i], rwkv_ln_b[i], w_out[i], norm_moe_g[i], router_w[i],
        router_b[i], exp_w_gate[i], exp_b_gate[i], exp_w_up[i], exp_b_up[i], exp_w_down[i],
        exp_b_down[i], norm_ple_g[i], ple_w_proj[i], ple_w_gate[i])
    out = _final(x1, gates, slots, p2d, ys, g_ple, w_proj, w_gate,
                 final_norm_g.reshape(1, -1).astype(F32))
    return out.reshape(batch, seq, d)
```

```python
import functools

import jax
import jax.numpy as jnp
from jax import lax
from jax.experimental import pallas as pl
from jax.experimental.pallas import tpu as pltpu

F32 = jnp.float32
BF16 = jnp.bfloat16

S5_GROUP = 16
S5_STATE = 64
RWKV_HEAD = 64
DECAY_LORA = 64
ICLR_LORA = 64
TOP_K = 4
RMS_EPS = 1e-6
RWKV_GN_EPS = 64e-5
LAMBDA_RE_MAX = -1e-4
SWIGLU_LIMIT = 7.0
SWIGLU_ALPHA = 1.702

LANES = 128
SUBLANES = 8
VMEM_LIMIT = 56 * 1024 * 1024

S5_CHUNK = 64
RWKV_CHUNK = 64
RWKV_SUB = 16
TOKEN_TILE = 256
EXPERT_TILE = 512
MOE_TILE = 512
RUN_ALIGN = 8
NEG_BIG = -1e30


def _dot(a, b):
    return jnp.dot(a, b, preferred_element_type=F32)


def _dot_nt(a, b):
    return lax.dot_general(a, b, (((1,), (1,)), ((), ())), preferred_element_type=F32)


def _dot_tn(a, b):
    return lax.dot_general(a, b, (((0,), (0,)), ((), ())), preferred_element_type=F32)


def _rms(t, gain):
    return t * lax.rsqrt(jnp.mean(t * t, axis=-1, keepdims=True) + RMS_EPS) * gain


def _sigmoid(t):
    return 1.0 / (1.0 + jnp.exp(-t))


def _params(*sem):
    return pltpu.CompilerParams(dimension_semantics=sem, vmem_limit_bytes=VMEM_LIMIT)


def _full(shape):
    return pl.BlockSpec(shape, lambda *_: (0,) * len(shape))


def _mix_in_kernel(seq, x_ref, xp_ref, gain_ref, win_ref, murkv_ref, vec_ref, w2_ref, ones_ref,
                   u_o, r_o, lw_o, k_o, v_o, an_o, bb_o, g_o, bonus_o):
    tm = x_ref.shape[0]
    width = u_o.shape[1]
    lora = w2_ref.shape[0]
    gain = gain_ref[...]
    h = _rms(x_ref[...], gain)
    keep = jnp.where((pl.program_id(0) * tm) % seq == 0, 0.0, 1.0)
    hp = _rms(xp_ref[...], gain) * keep
    h_ext = jnp.concatenate([hp, h], axis=0).astype(BF16)
    proj = _dot(h_ext, win_ref[...])
    shifted = pltpu.roll(proj[:, width:4 * width + lora], 1, 0)[SUBLANES:]
    cur = proj[SUBLANES:]
    u_o[...] = cur[:, :width]

    l1 = cur[:, 4 * width + lora:] + shifted[:, 3 * width:]
    lane = lax.broadcasted_iota(jnp.int32, l1.shape, 1)
    hidden = jnp.where(lane < DECAY_LORA, jnp.tanh(l1),
                       jnp.where(lane < DECAY_LORA + ICLR_LORA, l1, _sigmoid(l1))).astype(BF16)
    z3 = _dot(hidden, w2_ref[...])

    w0 = vec_ref[0:1, :]
    a0 = vec_ref[1:2, :]
    k_k = vec_ref[2:3, :]
    k_a = vec_ref[3:4, :]
    r_k = vec_ref[4:5, :]
    nz = -(w0 + z3[:, :width])
    softplus = jnp.maximum(nz, 0.0) + jnp.log(1.0 + jnp.exp(-jnp.abs(nz)))
    lw_o[...] = -jnp.exp(-softplus - 0.5)
    a = _sigmoid(a0 + z3[:, width:2 * width])
    g_o[...] = z3[:, 2 * width:]

    r_cur = cur[:, width:2 * width]
    k_cur = cur[:, 2 * width:3 * width]
    v_cur = cur[:, 3 * width:4 * width]
    r = r_cur + (shifted[:, :width] - r_cur) * murkv_ref[0:1, :]
    k = k_cur + (shifted[:, width:2 * width] - k_cur) * murkv_ref[1:2, :]
    v = v_cur + (shifted[:, 2 * width:3 * width] - v_cur) * murkv_ref[2:3, :]

    ones = ones_ref[...]
    kk = k * k_k
    norm = jnp.sqrt(_dot((kk * kk).astype(BF16), ones))
    kkn = kk / jnp.maximum(norm, 1e-12)
    k2 = k * (1.0 + (a - 1.0) * k_a)
    r_o[...] = r
    k_o[...] = k2
    v_o[...] = v
    an_o[...] = -kkn
    bb_o[...] = kkn * a
    bonus_o[...] = _dot((r * k2 * r_k).astype(BF16), ones) * v


def _mix_in(x2d, seq, gain, w_ext, mu_rkv, vecs, w2_cat, head_ones):
    n, d = x2d.shape
    width = w2_cat.shape[1] // 3
    tm = TOKEN_TILE
    prev_blocks = tm // SUBLANES
    row = lambda i: (i, 0)
    out = jax.ShapeDtypeStruct((n, width), F32)
    return pl.pallas_call(
        functools.partial(_mix_in_kernel, seq),
        grid=(n // tm,),
        in_specs=[
            pl.BlockSpec((tm, d), row),
            pl.BlockSpec((SUBLANES, d), lambda i: (jnp.maximum(i * prev_blocks - 1, 0), 0)),
            _full(gain.shape), _full(w_ext.shape), _full(mu_rkv.shape), _full(vecs.shape),
            _full(w2_cat.shape), _full(head_ones.shape),
        ],
        out_specs=[pl.BlockSpec((tm, width), row)] * 9,
        out_shape=[out] * 9,
        compiler_params=_params("parallel"),
        name="mix_in",
    )(x2d, x2d, gain, w_ext, mu_rkv, vecs, w2_cat, head_ones)


def _mix_in_weights(w_in, w1, a1, g1, mu_wag, w2, a2, g2):
    l1 = jnp.concatenate([w1, a1, g1], axis=1).astype(F32)
    mu = jnp.concatenate([jnp.broadcast_to(mu_wag[j][:, None], (w.shape[0], w.shape[1]))
                          for j, w in enumerate((w1, a1, g1))], axis=1).astype(F32)
    w_ext = jnp.concatenate([w_in.astype(F32), mu * l1, (1.0 - mu) * l1], axis=1).astype(BF16)
    width = w2.shape[1]
    z = lambda rows: jnp.zeros((rows, width), F32)
    w2_cat = jnp.concatenate([
        jnp.concatenate([w2.astype(F32), z(w2.shape[0]), z(w2.shape[0])], axis=1),
        jnp.concatenate([z(a2.shape[0]), a2.astype(F32), z(a2.shape[0])], axis=1),
        jnp.concatenate([z(g2.shape[0]), z(g2.shape[0]), g2.astype(F32)], axis=1)], axis=0).astype(BF16)
    return w_ext, w2_cat


def _s5_tables(lam_re, lam_im, log_dt, b_re, b_im, c_re, c_im, n_chunks):
    t = S5_CHUNK
    hi = lax.Precision.HIGHEST
    lre = jnp.minimum(lam_re.astype(F32), LAMBDA_RE_MAX)
    lim = lam_im.astype(F32)
    dt = jnp.exp(log_dt.astype(F32))[:, None]
    mag = jnp.exp(lre * dt)
    lb_re = mag * jnp.cos(lim * dt)
    lb_im = mag * jnp.sin(lim * dt)
    den = lre * lre + lim * lim
    z_re = lb_re - 1.0
    coef_re = ((z_re * lre + lb_im * lim) / den)[..., None]
    coef_im = ((lb_im * lre - z_re * lim) / den)[..., None]
    b_re = b_re.astype(F32)
    b_im = b_im.astype(F32)
    bb_re = coef_re * b_re - coef_im * b_im
    bb_im = coef_re * b_im + coef_im * b_re
    c_re = c_re.astype(F32)
    c_im = c_im.astype(F32)

    def power(e):
        e = e.astype(F32)[:, None, None]
        m = jnp.exp(e * (lre * dt))
        ang = e * (lim * dt)
        return m * jnp.cos(ang), m * jnp.sin(ang)

    p_re, p_im = power(jnp.arange(t + 1))
    ab_re = p_re[:t, :, :, None] * bb_re - p_im[:t, :, :, None] * bb_im
    ab_im = p_re[:t, :, :, None] * bb_im + p_im[:t, :, :, None] * bb_re
    kern = (jnp.einsum('gcp,tgpd->gtcd', c_re, ab_re, precision=hi)
            - jnp.einsum('gcp,tgpd->gtcd', c_im, ab_im, precision=hi))
    g = kern.shape[0]
    c = S5_GROUP
    kvec = kern.transpose(0, 3, 2, 1).reshape(g, c, c // 2, 2 * t)
    st_re = ab_re[::-1].transpose(1, 3, 0, 2).reshape(g, c * t, S5_STATE)
    st_im = ab_im[::-1].transpose(1, 3, 0, 2).reshape(g, c * t, S5_STATE)
    state_map = jnp.concatenate([st_re, st_im], axis=2).astype(BF16)
    q_re, q_im = p_re[1:], p_im[1:]
    ca_re = c_re[:, :, None, :] * q_re.transpose(1, 0, 2)[:, None] - c_im[:, :, None, :] * q_im.transpose(1, 0, 2)[:, None]
    ca_im = c_re[:, :, None, :] * q_im.transpose(1, 0, 2)[:, None] + c_im[:, :, None, :] * q_re.transpose(1, 0, 2)[:, None]
    cross = jnp.concatenate([ca_re.transpose(0, 3, 1, 2).reshape(g, S5_STATE, c * t),
                             -ca_im.transpose(0, 3, 1, 2).reshape(g, S5_STATE, c * t)], axis=1).astype(BF16)
    n_steps = max(1, (n_chunks - 1).bit_length())
    s_re, s_im = power(t * (2 ** jnp.arange(n_steps)))
    tab = jnp.stack([jnp.concatenate([s_re, s_re], axis=-1),
                     jnp.concatenate([-s_im, s_im], axis=-1)], axis=1)
    tab = tab.transpose(2, 0, 1, 3).reshape(g, 2 * n_steps, 2 * S5_STATE)
    return kvec, state_map, cross, tab, n_steps


def _s5_conv_kernel(n_chunks, n_steps, u_ref, kvec_ref, st_ref, cross_ref, tab_ref, y_ref, conv):
    t = S5_CHUNK
    row = lax.broadcasted_iota(jnp.int32, (t, LANES), 0)
    lane = lax.broadcasted_iota(jnp.int32, (t, LANES), 1)
    causal = jnp.bitwise_and(lane, t - 1) >= row
    for cin in range(S5_GROUP):
        for cp in range(S5_GROUP // 2):
            base = jnp.broadcast_to(kvec_ref[0, cin, cp:cp + 1, :], (t, LANES))
            blk = pltpu.roll(base, 0, 1, stride=1, stride_axis=0)
            conv[cin * t:(cin + 1) * t, cp * LANES:(cp + 1) * LANES] = jnp.where(causal, blk, 0.0).astype(BF16)
    u = u_ref[0]
    y = _dot(u, conv[...])
    x = _dot(u, st_ref[0])
    chunk = lax.broadcasted_iota(jnp.int32, x.shape, 0) % n_chunks
    half = x.shape[1] // 2
    for j in range(n_steps):
        sh = 1 << j
        xs = jnp.where(chunk >= sh, pltpu.roll(x, sh, 0), 0.0)
        x = x + tab_ref[0, 2 * j:2 * j + 1, :] * xs + tab_ref[0, 2 * j + 1:2 * j + 2, :] * pltpu.roll(xs, half, 1)
    x_in = jnp.where(chunk >= 1, pltpu.roll(x, 1, 0), 0.0)
    y_ref[0] = y + _dot(x_in.astype(BF16), cross_ref[0])


def _s5_conv(u2d, batch, seq, tables):
    kvec, state_map, cross, tab, n_steps = tables
    n, width = u2d.shape
    g = width // S5_GROUP
    t = S5_CHUNK
    n_chunks = seq // t
    ct = S5_GROUP * t
    ut = u2d.astype(BF16).reshape(batch, n_chunks, t, g, S5_GROUP).transpose(3, 0, 1, 4, 2)
    ut = ut.reshape(g, batch * n_chunks, ct)
    rows = batch * n_chunks
    y = pl.pallas_call(
        functools.partial(_s5_conv_kernel, n_chunks, n_steps),
        grid=(g,),
        in_specs=[
            pl.BlockSpec((1, rows, ct), lambda i: (i, 0, 0)),
            pl.BlockSpec((1,) + kvec.shape[1:], lambda i: (i, 0, 0, 0)),
            pl.BlockSpec((1,) + state_map.shape[1:], lambda i: (i, 0, 0)),
            pl.BlockSpec((1,) + cross.shape[1:], lambda i: (i, 0, 0)),
            pl.BlockSpec((1,) + tab.shape[1:], lambda i: (i, 0, 0)),
        ],
        out_specs=pl.BlockSpec((1, rows, ct), lambda i: (i, 0, 0)),
        out_shape=jax.ShapeDtypeStruct((g, rows, ct), F32),
        scratch_shapes=[pltpu.VMEM((ct, ct), BF16)],
        compiler_params=_params("parallel"),
        name="s5_conv",
    )(ut, kvec, state_map, cross, tab)
    return y.reshape(g, batch, n_chunks, S5_GROUP, t).transpose(1, 2, 4, 0, 3).reshape(n, width)


def _pair_blockdiag(y, left):
    return jnp.concatenate([jnp.where(left, y, 0.0), jnp.where(left, 0.0, y)], axis=0).astype(BF16)


def _rwkv_chunk_kernel(r_ref, lw_ref, k_ref, v_ref, an_ref, bb_ref, rhat_o, yhat_o, g_o, h_o):
    t = RWKV_CHUNK
    rows, width = r_ref.shape
    n_chunks = rows // t
    pairs = width // LANES
    row = lax.broadcasted_iota(jnp.int32, (t, LANES), 0)
    lane = lax.broadcasted_iota(jnp.int32, (t, LANES), 1)
    col = jnp.bitwise_and(lane, RWKV_HEAD - 1)
    left = lane < RWKV_HEAD
    incl = row >= col
    strict = row > col
    same_blk = (row // RWKV_SUB) == (col // RWKV_SUB)
    eye = jnp.where(row == col, 1.0, 0.0).astype(F32)
    brow = lax.broadcasted_iota(jnp.int32, (LANES, LANES), 0)
    bcol = lax.broadcasted_iota(jnp.int32, (LANES, LANES), 1)
    same_head = (brow // RWKV_HEAD) == (bcol // RWKV_HEAD)
    eye_full = brow == bcol
    crow = lax.broadcasted_iota(jnp.int32, (rows, rows), 0)
    ccol = lax.broadcasted_iota(jnp.int32, (rows, rows), 1)
    tril = jnp.where(crow >= ccol, jnp.where((crow // t) == (ccol // t), 1.0, 0.0), 0.0).astype(BF16)

    lw = lw_ref[...]
    p1 = lw.astype(BF16)
    rem = lw - p1.astype(F32)
    p2 = rem.astype(BF16)
    p3 = (rem - p2.astype(F32)).astype(BF16)
    cs = _dot(tril, p1) + _dot(tril, p2) + _dot(tril, p3)
    a_t = an_ref[...] * jnp.exp(cs - lw)
    r_t = r_ref[...] * jnp.exp(cs)
    p_inv = jnp.exp(-cs)
    b_t = bb_ref[...] * p_inv
    k_t = k_ref[...] * p_inv

    def bd(y):
        return _pair_blockdiag(y, left)

    def pmm(x, y_bd):
        return _dot(x.astype(BF16), y_bd)

    units = [(c, j) for c in range(n_chunks) for j in range(pairs)]

    def tile(arr, c, j):
        return arr[c * t:(c + 1) * t, j * LANES:(j + 1) * LANES]

    a2 = [tile(a_t, c, j) for c, j in units]
    r2 = [tile(r_t, c, j) for c, j in units]
    v2 = [tile(v_ref[...], c, j) for c, j in units]
    ar = [jnp.concatenate([a, r], axis=0).astype(BF16) for a, r in zip(a2, r2)]
    prod_b = [_dot_nt(x, bd(tile(b_t, c, j))) for x, (c, j) in zip(ar, units)]
    prod_k = [_dot_nt(x, bd(tile(k_t, c, j))) for x, (c, j) in zip(ar, units)]
    l_ab = [jnp.where(strict, p[:t], 0.0) for p in prod_b]
    l_rb = [jnp.where(incl, p[t:], 0.0) for p in prod_b]
    l_ak = [jnp.where(strict, p[:t], 0.0) for p in prod_k]
    l_rk = [jnp.where(incl, p[t:], 0.0) for p in prod_k]
    l_d = [jnp.where(same_blk, m, 0.0) for m in l_ab]
    l_o = [m - d for m, d in zip(l_ab, l_d)]
    l2 = [pmm(m, bd(m)) for m in l_d]
    q1 = [pmm(eye + m, bd(eye + s)) for m, s in zip(l_d, l2)]
    l4 = [pmm(m, bd(m)) for m in l2]
    q2 = [pmm(q, bd(eye + s)) for q, s in zip(q1, l4)]
    l8 = [pmm(m, bd(m)) for m in l4]
    d_inv = [pmm(q, bd(eye + s)) for q, s in zip(q2, l8)]
    x1 = [pmm(m, bd(v)) for m, v in zip(l_ak, v2)]

    def bd2(z1, z2):
        return jnp.concatenate([bd(z1), bd(z2)], axis=1)

    dz = [pmm(d, bd2(a, x)) for d, a, x in zip(d_inv, a2, x1)]
    n1 = [pmm(d, bd(o)) for d, o in zip(d_inv, l_o)]
    n2 = [pmm(m, bd(m)) for m in n1]
    t1 = [z + pmm(m, bd2(z[:, :LANES], z[:, LANES:])) for z, m in zip(dz, n2)]
    wu = [z + pmm(m, bd2(z[:, :LANES], z[:, LANES:])) for z, m in zip(t1, n1)]
    ry = [pmm(m, bd2(z[:, :LANES], z[:, LANES:])) for m, z in zip(l_rb, wu)]
    rk_v = [pmm(m, bd(v)) for m, v in zip(l_rk, v2)]

    for i, (c, j) in enumerate(units):
        rs = slice(c * t, (c + 1) * t)
        ls = slice(j * LANES, (j + 1) * LANES)
        rhat_o[rs, ls] = r2[i] + ry[i][:, :LANES]
        yhat_o[rs, ls] = ry[i][:, LANES:] + rk_v[i]
        cs_c = cs[rs, ls]
        cs_end = cs_c[t - 1:t, :]
        p_end = jnp.exp(cs_end - cs_c)
        b_h = (bb_ref[rs, ls] * p_end).astype(BF16)
        k_h = (k_ref[rs, ls] * p_end).astype(BF16)
        g_full = _dot_tn(wu[i][:, :LANES].astype(BF16), b_h)
        g_bd = jnp.where(same_head, g_full, 0.0) + jnp.where(eye_full, jnp.exp(cs_end), 0.0)
        g_o[c, j] = g_bd.astype(BF16)
        uv = jnp.concatenate([wu[i][:, LANES:], v2[i]], axis=0).astype(BF16)
        h_full = _dot_tn(uv, jnp.concatenate([b_h, k_h], axis=0))
        h_o[c, j] = jnp.where(left, h_full[:RWKV_HEAD], h_full[RWKV_HEAD:])


RWKV_CHUNKS_PER_STEP = 2


def _rwkv_chunks(r, lw, k, v, an, bb):
    n, width = r.shape
    t = RWKV_CHUNK
    cps = RWKV_CHUNKS_PER_STEP
    pairs = width // LANES
    nck = n // t
    tok = pl.BlockSpec((cps * t, width), lambda i: (i, 0))
    return pl.pallas_call(
        _rwkv_chunk_kernel,
        grid=(nck // cps,),
        in_specs=[tok] * 6,
        out_specs=[tok, tok,
                   pl.BlockSpec((cps, pairs, LANES, LANES), lambda i: (i, 0, 0, 0)),
                   pl.BlockSpec((cps, pairs, RWKV_HEAD, LANES), lambda i: (i, 0, 0, 0))],
        out_shape=[jax.ShapeDtypeStruct((n, width), F32)] * 2
        + [jax.ShapeDtypeStruct((nck, pairs, LANES, LANES), BF16),
           jax.ShapeDtypeStruct((nck, pairs, RWKV_HEAD, LANES), F32)],
        compiler_params=_params("parallel"),
        name="rwkv_chunk",
    )(r, lw, k, v, an, bb)


def _rwkv_scan_kernel(rhat_ref, yhat_ref, g_ref, h_ref, gate_ref, bonus_ref, lnw_ref, lnb_ref,
                      ones_ref, y_ref, state):
    t = RWKV_CHUNK
    batch, pairs = state.shape[0], state.shape[1]
    n_chunks = rhat_ref.shape[1] // t
    lane = lax.broadcasted_iota(jnp.int32, (RWKV_HEAD, LANES), 1)
    left = lane < RWKV_HEAD

    @pl.when(pl.program_id(0) == 0)
    def _():
        state[...] = jnp.zeros_like(state)

    s = [[state[b, j] for j in range(pairs)] for b in range(batch)]
    y_rows = []
    for b in range(batch):
        chunk_rows = []
        for c in range(n_chunks):
            rs = slice(c * t, (c + 1) * t)
            tiles = []
            for j in range(pairs):
                ls = slice(j * LANES, (j + 1) * LANES)
                s_bd = _pair_blockdiag(s[b][j], left)
                tiles.append(yhat_ref[b, rs, ls] + _dot_nt(rhat_ref[b, rs, ls].astype(BF16), s_bd))
                s[b][j] = _dot(s[b][j].astype(BF16), g_ref[b, c, j]) + h_ref[b, c, j]
            chunk_rows.append(jnp.concatenate(tiles, axis=1))
        y_rows.append(jnp.concatenate(chunk_rows, axis=0))
    for b in range(batch):
        for j in range(pairs):
            state[b, j] = s[b][j]

    ones = ones_ref[...]
    inv = 1.0 / RWKV_HEAD
    for b in range(batch):
        y = y_rows[b]
        cen = y - _dot(y.astype(BF16), ones) * inv
        var = _dot((cen * cen).astype(BF16), ones) * inv
        yn = cen * lax.rsqrt(var + RWKV_GN_EPS) * lnw_ref[...] + lnb_ref[...]
        y_ref[b] = (yn + bonus_ref[b]) * gate_ref[b]


def _rwkv_scan(rhat, yhat, g_mat, h_mat, gate, bonus, ln_w, ln_b, head_ones, batch, seq):
    n, width = rhat.shape
    t = RWKV_CHUNK
    cps = RWKV_CHUNKS_PER_STEP
    pairs = width // LANES
    nc = seq // t
    tok3 = lambda a: a.reshape(batch, seq, width)
    tok = pl.BlockSpec((batch, cps * t, width), lambda i: (0, i, 0))
    out = pl.pallas_call(
        _rwkv_scan_kernel,
        grid=(nc // cps,),
        in_specs=[tok, tok,
                  pl.BlockSpec((batch, cps, pairs, LANES, LANES), lambda i: (0, i, 0, 0, 0)),
                  pl.BlockSpec((batch, cps, pairs, RWKV_HEAD, LANES), lambda i: (0, i, 0, 0, 0)),
                  tok, tok, _full(ln_w.shape), _full(ln_b.shape), _full(head_ones.shape)],
        out_specs=tok,
        out_shape=jax.ShapeDtypeStruct((batch, seq, width), F32),
        scratch_shapes=[pltpu.VMEM((batch, pairs, RWKV_HEAD, LANES), F32)],
        compiler_params=_params("arbitrary"),
        name="rwkv_scan",
    )(tok3(rhat), tok3(yhat), g_mat.reshape(batch, nc, pairs, LANES, LANES),
      h_mat.reshape(batch, nc, pairs, RWKV_HEAD, LANES), tok3(gate), tok3(bonus), ln_w, ln_b, head_ones)
    return out.reshape(n, width)


def _post_mix_kernel(n_experts, x_ref, yc_ref, u_ref, yr_ref, d_ref, wglu_ref, wout_ref, gain_ref,
                     rw_ref, rb_ref, x1_o, hm_o, gate_o, lpos_o, meta_o, tot_o, running):
    tm = x_ref.shape[0]
    width = yc_ref.shape[1]

    @pl.when(pl.program_id(0) == 0)
    def _():
        running[...] = jnp.zeros_like(running)

    y = yc_ref[...] + d_ref[...] * u_ref[...]
    y = 0.5 * y * (1.0 + jnp.tanh(0.7978845608028654 * (y + 0.044715 * (y * y * y))))
    y = y * _sigmoid(_dot(y.astype(BF16), wglu_ref[...]))
    x1 = (x_ref[...] + _dot(y.astype(BF16), wout_ref[:width, :])
          + _dot(yr_ref[...].astype(BF16), wout_ref[width:, :]))
    x1_o[...] = x1
    hm = _rms(x1, gain_ref[...])
    hm_o[...] = hm.astype(BF16)
    logits = _dot(hm.astype(BF16), rw_ref[...]) + rb_ref[...]

    lane = lax.broadcasted_iota(jnp.int32, logits.shape, 1)
    lanef = lane.astype(F32)
    sel = jnp.zeros(logits.shape, F32)
    idx_cols, val_cols = [], []
    work = logits
    for _ in range(TOP_K):
        m = jnp.max(work, axis=-1, keepdims=True)
        pick = jnp.min(jnp.where(work == m, lanef, float(LANES)), axis=-1, keepdims=True)
        hit = lanef == pick
        sel = jnp.where(hit, 1.0, sel)
        work = jnp.where(hit, -jnp.inf, work)
        idx_cols.append(pick)
        val_cols.append(m)
    exps = [jnp.exp(vv - val_cols[0]) for vv in val_cols]
    denom = exps[0] + exps[1] + exps[2] + exps[3]

    row = lax.broadcasted_iota(jnp.int32, (tm, tm), 0)
    col = lax.broadcasted_iota(jnp.int32, (tm, tm), 1)
    before = jnp.where(row > col, 1.0, 0.0).astype(BF16)
    local = _dot(before, sel.astype(BF16))
    cnt = jnp.sum(sel, axis=0, keepdims=True)
    cnt_al = jnp.floor((cnt + (RUN_ALIGN - 1)) * (1.0 / RUN_ALIGN)) * RUN_ALIGN
    erow = lax.broadcasted_iota(jnp.int32, (LANES, LANES), 0)
    ecol = lax.broadcasted_iota(jnp.int32, (LANES, LANES), 1)
    upper = jnp.where(erow < ecol, 1.0, 0.0).astype(BF16)
    toff = _dot(jnp.broadcast_to(cnt_al, (SUBLANES, LANES)).astype(BF16), upper)[0:1]
    tbase = running[...]
    gate_out = jnp.zeros(logits.shape, F32)
    lpos_out = jnp.zeros(logits.shape, F32)
    for j in range(TOP_K):
        lp = jnp.sum(jnp.where(lanef == idx_cols[j], local + toff, 0.0), axis=-1, keepdims=True)
        gate_out = jnp.where(lane == j, exps[j] / denom, gate_out)
        lpos_out = jnp.where(lane == j, lp, lpos_out)
    gate_o[...] = gate_out
    lpos_o[...] = lpos_out.astype(jnp.int32)
    srow = lax.broadcasted_iota(jnp.int32, (SUBLANES, LANES), 0)
    meta = jnp.where(srow == 0, cnt, jnp.where(srow == 1, tbase, jnp.where(srow == 2, toff, 0.0)))
    meta_o[...] = meta.astype(jnp.int32)
    running[...] = tbase + cnt_al
    tot_o[...] = jnp.broadcast_to(running[...], tot_o.shape).astype(jnp.int32)
    del n_experts


def _post_mix(x2d, yconv, u, y_rwkv, s5_d, w_glu, w_out, gain, rw, rb, n_experts):
    n, d = x2d.shape
    width = yconv.shape[1]
    tm = MOE_TILE
    row = lambda i: (i, 0)
    tok_d = pl.BlockSpec((tm, d), row)
    tok_w = pl.BlockSpec((tm, width), row)
    tok_l = pl.BlockSpec((tm, LANES), row)
    return pl.pallas_call(
        functools.partial(_post_mix_kernel, n_experts),
        grid=(n // tm,),
        in_specs=[tok_d, tok_w, tok_w, tok_w, _full(s5_d.shape), _full(w_glu.shape),
                  _full(w_out.shape), _full(gain.shape), _full(rw.shape), _full(rb.shape)],
        out_specs=[tok_d, tok_d, tok_l, tok_l,
                   pl.BlockSpec((SUBLANES, LANES), row), _full((SUBLANES, LANES))],
        out_shape=[jax.ShapeDtypeStruct((n, d), F32), jax.ShapeDtypeStruct((n, d), BF16),
                   jax.ShapeDtypeStruct((n, LANES), F32), jax.ShapeDtypeStruct((n, LANES), jnp.int32),
                   jax.ShapeDtypeStruct((n // tm * SUBLANES, LANES), jnp.int32),
                   jax.ShapeDtypeStruct((SUBLANES, LANES), jnp.int32)],
        scratch_shapes=[pltpu.VMEM((1, LANES), F32)],
        compiler_params=_params("arbitrary"),
        name="post_mix",
    )(x2d, yconv, u, y_rwkv, s5_d, w_glu, w_out, gain, rw, rb)


def _sorted_rows(tm, n_experts):
    return -(-(tm * TOP_K + n_experts * (RUN_ALIGN - 1)) // LANES) * LANES


def _run_copies(n_experts, pstart_ref, meta_ref, make_copy, wait):
    for e in range(n_experts):
        groups = (meta_ref[0, e] + (RUN_ALIGN - 1)) // RUN_ALIGN
        seg = pstart_ref[e] + meta_ref[1, e]
        loc = meta_ref[2, e]

        def body(g, c, seg=seg, loc=loc):
            cp = make_copy(pl.multiple_of(loc + g * RUN_ALIGN, RUN_ALIGN),
                           pl.multiple_of(seg + g * RUN_ALIGN, RUN_ALIGN))
            if wait:
                cp.wait()
            else:
                cp.start(priority=e % 2)
            return c

        lax.fori_loop(0, groups, body, 0)


def _dispatch_kernel(n_experts, pstart_ref, meta_ref, lpos_ref, hm_ref, xs_in_ref, xs_ref, srt, sem):
    del xs_in_ref
    tm = hm_ref.shape[0]
    rows = srt.shape[0]
    pos_t = jnp.transpose(lpos_ref[...].astype(F32))
    rid = lax.broadcasted_iota(jnp.int32, (rows, tm), 0).astype(F32)
    perm = jnp.zeros((rows, tm), F32)
    for j in range(TOP_K):
        perm = perm + jnp.where(rid == pos_t[j:j + 1, :], 1.0, 0.0)
    srt[...] = _dot(perm.astype(BF16), hm_ref[...])

    def make_copy(loc, dst):
        return pltpu.make_async_copy(srt.at[pl.ds(loc, RUN_ALIGN)], xs_ref.at[pl.ds(dst, RUN_ALIGN)], sem)

    _run_copies(n_experts, pstart_ref, meta_ref, make_copy, wait=False)
    _run_copies(n_experts, pstart_ref, meta_ref, make_copy, wait=True)


def _dispatch(hm, lpos, meta, pstarts, m_pad, n_experts):
    n, dh = hm.shape
    tm = MOE_TILE
    rows = _sorted_rows(tm, n_experts)
    xs0 = jnp.zeros((m_pad, dh), F32)
    grid_spec = pltpu.PrefetchScalarGridSpec(
        num_scalar_prefetch=1,
        grid=(n // tm,),
        in_specs=[pl.BlockSpec((SUBLANES, LANES), lambda i, ps: (i, 0), memory_space=pltpu.SMEM),
                  pl.BlockSpec((tm, LANES), lambda i, ps: (i, 0)),
                  pl.BlockSpec((tm, dh), lambda i, ps: (i, 0)),
                  pl.BlockSpec(memory_space=pl.ANY)],
        out_specs=pl.BlockSpec(memory_space=pl.ANY),
        scratch_shapes=[pltpu.VMEM((rows, dh), F32), pltpu.SemaphoreType.DMA(())],
    )
    return pl.pallas_call(
        functools.partial(_dispatch_kernel, n_experts),
        grid_spec=grid_spec,
        out_shape=jax.ShapeDtypeStruct((m_pad, dh), F32),
        input_output_aliases={4: 0},
        compiler_params=_params("arbitrary"),
        name="dispatch",
    )(pstarts, meta, lpos, hm, xs0)


def _experts_kernel(be_ref, nu_ref, xs_ref, wg_ref, bg_ref, wu_ref, bu_ref, wd_ref, bd_ref, ys_ref,
                    wg_s, wu_s, wd_s):
    i = pl.program_id(0)
    changed = jnp.logical_or(i == 0, be_ref[i] != be_ref[jnp.maximum(i - 1, 0)])

    @pl.when(changed)
    def _():
        wg_s[...] = wg_ref[0].astype(BF16)
        wu_s[...] = wu_ref[0].astype(BF16)
        wd_s[...] = wd_ref[0].astype(BF16)

    @pl.when(i < nu_ref[0])
    def _():
        xb = xs_ref[...].astype(BF16)
        gt = jnp.minimum(_dot(xb, wg_s[...]) + bg_ref[0], SWIGLU_LIMIT)
        up = jnp.clip(_dot(xb, wu_s[...]) + bu_ref[0], -SWIGLU_LIMIT, SWIGLU_LIMIT)
        act = (up + 1.0) * gt * _sigmoid(SWIGLU_ALPHA * gt)
        ys_ref[...] = _dot(act.astype(BF16), wd_s[...]) + bd_ref[0]

    @pl.when(i >= nu_ref[0])
    def _():
        ys_ref[...] = jnp.zeros_like(ys_ref)


def _experts(xs, block_e, n_used, wg, bg, wu, bu, wd, bd):
    m_pad, dh = xs.shape
    tmb = EXPERT_TILE
    d, dff = wg.shape[1], wg.shape[2]
    wmap = lambda i, be, nu: (be[i], 0, 0)
    grid_spec = pltpu.PrefetchScalarGridSpec(
        num_scalar_prefetch=2,
        grid=(m_pad // tmb,),
        in_specs=[pl.BlockSpec((tmb, dh), lambda i, be, nu: (i, 0)),
                  pl.BlockSpec((1, d, dff), wmap), pl.BlockSpec((1, 1, dff), wmap),
                  pl.BlockSpec((1, d, dff), wmap), pl.BlockSpec((1, 1, dff), wmap),
                  pl.BlockSpec((1, dff, d), wmap), pl.BlockSpec((1, 1, d), wmap)],
        out_specs=pl.BlockSpec((tmb, dh), lambda i, be, nu: (i, 0)),
        scratch_shapes=[pltpu.VMEM((d, dff), BF16), pltpu.VMEM((d, dff), BF16), pltpu.VMEM((dff, d), BF16)],
    )
    return pl.pallas_call(
        _experts_kernel,
        grid_spec=grid_spec,
        out_shape=jax.ShapeDtypeStruct((m_pad, dh), F32),
        compiler_params=_params("arbitrary"),
        name="experts",
    )(block_e, n_used, xs, wg, bg, wu, bu, wd, bd)


def _final_kernel(n_experts, pstart_ref, meta_ref, lpos_ref, x1_ref, gate_ref, p_ref, ys_ref, gple_ref,
                  wproj_ref, wgate_ref, gfin_ref, out_ref, buf, sem):
    tm = x1_ref.shape[0]
    rows = buf.shape[0]

    @pl.when(pl.program_id(0) == 0)
    def _():
        buf[...] = jnp.zeros_like(buf)

    def make_copy(loc, src):
        return pltpu.make_async_copy(ys_ref.at[pl.ds(src, RUN_ALIGN)], buf.at[pl.ds(loc, RUN_ALIGN)], sem)

    _run_copies(n_experts, pstart_ref, meta_ref, make_copy, wait=False)

    cid = lax.broadcasted_iota(jnp.int32, (tm, rows), 1)
    lpos = lpos_ref[...]
    gates = gate_ref[...]
    comb = jnp.zeros((tm, rows), F32)
    for j in range(TOP_K):
        comb = comb + jnp.where(cid == lpos[:, j:j + 1], gates[:, j:j + 1], 0.0)

    _run_copies(n_experts, pstart_ref, meta_ref, make_copy, wait=True)
    x2 = x1_ref[...] + _dot(comb.astype(BF16), buf[...].astype(BF16))
    gate = _sigmoid(_dot(_rms(x2, gple_ref[...]).astype(BF16), wgate_ref[...]))
    x3 = x2 + _dot(p_ref[...].astype(BF16), wproj_ref[...]) * gate
    out_ref[...] = _rms(x3, gfin_ref[...])


def _final(x1, gates, lpos, meta, pstarts, p2d, ys, g_ple, w_proj, w_gate, g_fin, n_experts):
    n, d = x1.shape
    tm = MOE_TILE
    rows = _sorted_rows(tm, n_experts)
    row = lambda i, ps: (i, 0)
    const = lambda shape: pl.BlockSpec(shape, lambda i, ps: (0,) * len(shape))
    grid_spec = pltpu.PrefetchScalarGridSpec(
        num_scalar_prefetch=1,
        grid=(n // tm,),
        in_specs=[pl.BlockSpec((SUBLANES, LANES), row, memory_space=pltpu.SMEM),
                  pl.BlockSpec((tm, LANES), row), pl.BlockSpec((tm, d), row), pl.BlockSpec((tm, LANES), row),
                  pl.BlockSpec((tm, p2d.shape[1]), row), pl.BlockSpec(memory_space=pl.ANY),
                  const(g_ple.shape), const(w_proj.shape), const(w_gate.shape), const(g_fin.shape)],
        out_specs=pl.BlockSpec((tm, d), row),
        scratch_shapes=[pltpu.VMEM((rows, ys.shape[1]), ys.dtype), pltpu.SemaphoreType.DMA(())],
    )
    return pl.pallas_call(
        functools.partial(_final_kernel, n_experts),
        grid_spec=grid_spec,
        out_shape=jax.ShapeDtypeStruct((n, d), F32),
        compiler_params=_params("arbitrary"),
        name="final",
    )(pstarts, meta, lpos, x1, gates, p2d, ys, g_ple, w_proj, w_gate, g_fin)


def _head_ones(width):
    hd = jnp.arange(width) // RWKV_HEAD
    return (hd[:, None] == hd[None, :]).astype(BF16)


def _layer(x2d, p2d, batch, seq, norm_mix_g, w_in, s5_lam_re, s5_lam_im, s5_log_dt, s5_b_re, s5_b_im,
           s5_c_re, s5_c_im, s5_d, s5_w_glu, mu_rkv, mu_wag, w0, w1, w2, a0, a1, a2, g1, g2, k_k, k_a,
           r_k, ln_w, ln_b, w_out, norm_moe_g, router_w, router_b, wg, bg, wu, bu, wd, bd,
           norm_ple_g, ple_w_proj, ple_w_gate):
    n, d = x2d.shape
    width = w_in.shape[1] // 4
    n_experts = router_w.shape[1]
    row2 = lambda t: t.reshape(1, -1).astype(F32)
    head_ones = _head_ones(width)
    vecs = jnp.zeros((SUBLANES, width), F32)
    vecs = vecs.at[0].set(w0).at[1].set(a0).at[2].set(k_k).at[3].set(k_a).at[4].set(r_k.reshape(-1))

    w_ext, w2_cat = _mix_in_weights(w_in, w1, a1, g1, mu_wag, w2, a2, g2)
    u, r, lw, k, v, an, bb, gate, bonus = _mix_in(
        x2d, seq, row2(norm_mix_g), w_ext, mu_rkv.astype(F32), vecs, w2_cat, head_ones)

    tables = _s5_tables(s5_lam_re, s5_lam_im, s5_log_dt, s5_b_re, s5_b_im, s5_c_re, s5_c_im,
                        seq // S5_CHUNK)
    yconv = _s5_conv(u, batch, seq, tables)

    rhat, yhat, g_mat, h_mat = _rwkv_chunks(r, lw, k, v, an, bb)
    y_rwkv = _rwkv_scan(rhat, yhat, g_mat, h_mat, gate, bonus, row2(ln_w), row2(ln_b), head_ones,
                        batch, seq)

    rw = jnp.zeros((d, LANES), BF16).at[:, :n_experts].set(router_w.astype(BF16))
    rb = jnp.full((1, LANES), NEG_BIG, F32).at[0, :n_experts].set(router_b.astype(F32))
    x1, hm, gates, lpos, meta, totals = _post_mix(
        x2d, yconv, u, y_rwkv, row2(s5_d), s5_w_glu.astype(BF16), w_out.astype(BF16),
        row2(norm_moe_g), rw, rb, n_experts)

    tmb = EXPERT_TILE
    n_tiles = n // MOE_TILE
    max_rows = n * TOP_K + n_tiles * n_experts * (RUN_ALIGN - 1)
    n_blocks = -(-max_rows // tmb) + n_experts
    seg = totals[0, :n_experts]
    padded = ((seg + tmb - 1) // tmb) * tmb
    pends = jnp.cumsum(padded)
    pstarts = (pends - padded).astype(jnp.int32)
    block_start = jnp.arange(n_blocks, dtype=jnp.int32) * tmb
    block_e = jnp.minimum(jnp.sum((pends[None, :] <= block_start[:, None]).astype(jnp.int32), axis=1),
                          n_experts - 1).astype(jnp.int32)
    n_used = (pends[-1] // tmb).astype(jnp.int32).reshape(1)

    xs = _dispatch(hm, lpos, meta, pstarts, n_blocks * tmb, n_experts)
    ys = _experts(xs, block_e, n_used, wg, bg.reshape(n_experts, 1, -1).astype(F32),
                  wu, bu.reshape(n_experts, 1, -1).astype(F32), wd,
                  bd.reshape(n_experts, 1, -1).astype(F32))
    return (x1, gates, lpos, meta, pstarts, ys, p2d, row2(norm_ple_g), ple_w_proj.astype(BF16),
            ple_w_gate.astype(BF16), n_experts)


def kernel(x, p, norm_mix_g, w_in, s5_lam_re, s5_lam_im, s5_log_dt, s5_b_re, s5_b_im, s5_c_re, s5_c_im, s5_d, s5_w_glu, rwkv_mu_rkv, rwkv_mu_wag, rwkv_w0, rwkv_w1, rwkv_w2, rwkv_a0, rwkv_a1, rwkv_a2, rwkv_g1, rwkv_g2, rwkv_k_k, rwkv_k_a, rwkv_r_k, rwkv_ln_w, rwkv_ln_b, w_out, norm_moe_g, router_w, router_b, exp_w_gate, exp_b_gate, exp_w_up, exp_b_up, exp_w_down, exp_b_down, norm_ple_g, ple_w_proj, ple_w_gate, final_norm_g):
    batch, seq, d = x.shape
    assert w_in.shape[0] == 1, "the final kernel fuses the last RMSNorm: single-layer stacks only"
    i = 0
    x1, gates, lpos, meta, pstarts, ys, p2d, g_ple, w_proj, w_gate, n_experts = _layer(
        x.reshape(batch * seq, d), p[i].reshape(batch * seq, -1), batch, seq, norm_mix_g[i], w_in[i],
        s5_lam_re[i], s5_lam_im[i], s5_log_dt[i], s5_b_re[i], s5_b_im[i], s5_c_re[i], s5_c_im[i],
        s5_d[i], s5_w_glu[i], rwkv_mu_rkv[i], rwkv_mu_wag[i], rwkv_w0[i], rwkv_w1[i], rwkv_w2[i],
        rwkv_a0[i], rwkv_a1[i], rwkv_a2[i], rwkv_g1[i], rwkv_g2[i], rwkv_k_k[i], rwkv_k_a[i],
        rwkv_r_k[i], rwkv_ln_w[i], rwkv_ln_b[i], w_out[i], norm_moe_g[i], router_w[i],
        router_b[i], exp_w_gate[i], exp_b_gate[i], exp_w_up[i], exp_b_up[i], exp_w_down[i],
        exp_b_down[i], norm_ple_g[i], ple_w_proj[i], ple_w_gate[i])
    out = _final(x1, gates, lpos, meta, pstarts, p2d, ys, g_ple, w_proj, w_gate,
                 final_norm_g.reshape(1, -1).astype(F32), n_experts)
    return out.reshape(batch, seq, d)
```

```python
import functools

import jax
import jax.numpy as jnp
from jax import lax
from jax.experimental import pallas as pl
from jax.experimental.pallas import tpu as pltpu

F32 = jnp.float32
BF16 = jnp.bfloat16

S5_GROUP = 16
S5_STATE = 64
RWKV_HEAD = 64
DECAY_LORA = 64
ICLR_LORA = 64
TOP_K = 4
RMS_EPS = 1e-6
RWKV_GN_EPS = 64e-5
LAMBDA_RE_MAX = -1e-4
SWIGLU_LIMIT = 7.0
SWIGLU_ALPHA = 1.702

LANES = 128
SUBLANES = 8
VMEM_LIMIT = 56 * 1024 * 1024

S5_CHUNK = 64
RWKV_CHUNK = 64
RWKV_SUB = 16
TOKEN_TILE = 256
EXPERT_TILE = 512
MOE_TILE = 512
RUN_ALIGN = 8
NEG_BIG = -1e30


def _dot(a, b):
    return jnp.dot(a, b, preferred_element_type=F32)


def _dot_nt(a, b):
    return lax.dot_general(a, b, (((1,), (1,)), ((), ())), preferred_element_type=F32)


def _dot_tn(a, b):
    return lax.dot_general(a, b, (((0,), (0,)), ((), ())), preferred_element_type=F32)


def _rms(t, gain):
    return t * lax.rsqrt(jnp.mean(t * t, axis=-1, keepdims=True) + RMS_EPS) * gain


def _sigmoid(t):
    return 1.0 / (1.0 + jnp.exp(-t))


def _params(*sem):
    return pltpu.CompilerParams(dimension_semantics=sem, vmem_limit_bytes=VMEM_LIMIT)


def _full(shape):
    return pl.BlockSpec(shape, lambda *_: (0,) * len(shape))


def _mix_in_kernel(seq, x_ref, xp_ref, gain_ref, win_ref, murkv_ref, vec_ref, w2_ref, ones_ref,
                   u_o, r_o, lw_o, k_o, v_o, an_o, bb_o, g_o, bonus_o):
    tm = x_ref.shape[0]
    width = u_o.shape[1]
    lora = w2_ref.shape[0]
    gain = gain_ref[...]
    h = _rms(x_ref[...], gain)
    keep = jnp.where((pl.program_id(0) * tm) % seq == 0, 0.0, 1.0)
    hp = _rms(xp_ref[...], gain) * keep
    h_ext = jnp.concatenate([hp, h], axis=0).astype(BF16)
    proj = _dot(h_ext, win_ref[...])
    shifted = pltpu.roll(proj[:, width:4 * width + lora], 1, 0)[SUBLANES:]
    cur = proj[SUBLANES:]
    u_o[...] = cur[:, :width]

    l1 = cur[:, 4 * width + lora:] + shifted[:, 3 * width:]
    lane = lax.broadcasted_iota(jnp.int32, l1.shape, 1)
    hidden = jnp.where(lane < DECAY_LORA, jnp.tanh(l1),
                       jnp.where(lane < DECAY_LORA + ICLR_LORA, l1, _sigmoid(l1))).astype(BF16)
    z3 = _dot(hidden, w2_ref[...])

    w0 = vec_ref[0:1, :]
    a0 = vec_ref[1:2, :]
    k_k = vec_ref[2:3, :]
    k_a = vec_ref[3:4, :]
    r_k = vec_ref[4:5, :]
    nz = -(w0 + z3[:, :width])
    softplus = jnp.maximum(nz, 0.0) + jnp.log(1.0 + jnp.exp(-jnp.abs(nz)))
    lw_o[...] = -jnp.exp(-softplus - 0.5)
    a = _sigmoid(a0 + z3[:, width:2 * width])
    g_o[...] = z3[:, 2 * width:]

    r_cur = cur[:, width:2 * width]
    k_cur = cur[:, 2 * width:3 * width]
    v_cur = cur[:, 3 * width:4 * width]
    r = r_cur + (shifted[:, :width] - r_cur) * murkv_ref[0:1, :]
    k = k_cur + (shifted[:, width:2 * width] - k_cur) * murkv_ref[1:2, :]
    v = v_cur + (shifted[:, 2 * width:3 * width] - v_cur) * murkv_ref[2:3, :]

    ones = ones_ref[...]
    kk = k * k_k
    norm = jnp.sqrt(_dot((kk * kk).astype(BF16), ones))
    kkn = kk / jnp.maximum(norm, 1e-12)
    k2 = k * (1.0 + (a - 1.0) * k_a)
    r_o[...] = r
    k_o[...] = k2
    v_o[...] = v
    an_o[...] = -kkn
    bb_o[...] = kkn * a
    bonus_o[...] = _dot((r * k2 * r_k).astype(BF16), ones) * v


def _mix_in(x2d, seq, gain, w_ext, mu_rkv, vecs, w2_cat, head_ones):
    n, d = x2d.shape
    width = w2_cat.shape[1] // 3
    tm = TOKEN_TILE
    prev_blocks = tm // SUBLANES
    row = lambda i: (i, 0)
    out = jax.ShapeDtypeStruct((n, width), F32)
    return pl.pallas_call(
        functools.partial(_mix_in_kernel, seq),
        grid=(n // tm,),
        in_specs=[
            pl.BlockSpec((tm, d), row),
            pl.BlockSpec((SUBLANES, d), lambda i: (jnp.maximum(i * prev_blocks - 1, 0), 0)),
            _full(gain.shape), _full(w_ext.shape), _full(mu_rkv.shape), _full(vecs.shape),
            _full(w2_cat.shape), _full(head_ones.shape),
        ],
        out_specs=[pl.BlockSpec((tm, width), row)] * 9,
        out_shape=[out] * 9,
        compiler_params=_params("parallel"),
        name="mix_in",
    )(x2d, x2d, gain, w_ext, mu_rkv, vecs, w2_cat, head_ones)


def _mix_in_weights(w_in, w1, a1, g1, mu_wag, w2, a2, g2):
    l1 = jnp.concatenate([w1, a1, g1], axis=1).astype(F32)
    mu = jnp.concatenate([jnp.broadcast_to(mu_wag[j][:, None], (w.shape[0], w.shape[1]))
                          for j, w in enumerate((w1, a1, g1))], axis=1).astype(F32)
    w_ext = jnp.concatenate([w_in.astype(F32), mu * l1, (1.0 - mu) * l1], axis=1).astype(BF16)
    width = w2.shape[1]
    z = lambda rows: jnp.zeros((rows, width), F32)
    w2_cat = jnp.concatenate([
        jnp.concatenate([w2.astype(F32), z(w2.shape[0]), z(w2.shape[0])], axis=1),
        jnp.concatenate([z(a2.shape[0]), a2.astype(F32), z(a2.shape[0])], axis=1),
        jnp.concatenate([z(g2.shape[0]), z(g2.shape[0]), g2.astype(F32)], axis=1)], axis=0).astype(BF16)
    return w_ext, w2_cat


def _s5_tables(lam_re, lam_im, log_dt, b_re, b_im, c_re, c_im, n_chunks):
    t = S5_CHUNK
    hi = lax.Precision.HIGHEST
    lre = jnp.minimum(lam_re.astype(F32), LAMBDA_RE_MAX)
    lim = lam_im.astype(F32)
    dt = jnp.exp(log_dt.astype(F32))[:, None]
    mag = jnp.exp(lre * dt)
    lb_re = mag * jnp.cos(lim * dt)
    lb_im = mag * jnp.sin(lim * dt)
    den = lre * lre + lim * lim
    z_re = lb_re - 1.0
    coef_re = ((z_re * lre + lb_im * lim) / den)[..., None]
    coef_im = ((lb_im * lre - z_re * lim) / den)[..., None]
    b_re = b_re.astype(F32)
    b_im = b_im.astype(F32)
    bb_re = coef_re * b_re - coef_im * b_im
    bb_im = coef_re * b_im + coef_im * b_re
    c_re = c_re.astype(F32)
    c_im = c_im.astype(F32)

    def power(e):
        e = e.astype(F32)[:, None, None]
        m = jnp.exp(e * (lre * dt))
        ang = e * (lim * dt)
        return m * jnp.cos(ang), m * jnp.sin(ang)

    p_re, p_im = power(jnp.arange(t + 1))
    ab_re = p_re[:t, :, :, None] * bb_re - p_im[:t, :, :, None] * bb_im
    ab_im = p_re[:t, :, :, None] * bb_im + p_im[:t, :, :, None] * bb_re
    kern = (jnp.einsum('gcp,tgpd->gtcd', c_re, ab_re, precision=hi)
            - jnp.einsum('gcp,tgpd->gtcd', c_im, ab_im, precision=hi))
    g = kern.shape[0]
    c = S5_GROUP
    kvec = kern.transpose(0, 3, 2, 1).reshape(g, c, c // 2, 2 * t)
    st_re = ab_re[::-1].transpose(1, 3, 0, 2).reshape(g, c * t, S5_STATE)
    st_im = ab_im[::-1].transpose(1, 3, 0, 2).reshape(g, c * t, S5_STATE)
    state_map = jnp.concatenate([st_re, st_im], axis=2).astype(BF16)
    q_re, q_im = p_re[1:], p_im[1:]
    ca_re = c_re[:, :, None, :] * q_re.transpose(1, 0, 2)[:, None] - c_im[:, :, None, :] * q_im.transpose(1, 0, 2)[:, None]
    ca_im = c_re[:, :, None, :] * q_im.transpose(1, 0, 2)[:, None] + c_im[:, :, None, :] * q_re.transpose(1, 0, 2)[:, None]
    cross = jnp.concatenate([ca_re.transpose(0, 3, 1, 2).reshape(g, S5_STATE, c * t),
                             -ca_im.transpose(0, 3, 1, 2).reshape(g, S5_STATE, c * t)], axis=1).astype(BF16)
    n_steps = max(1, (n_chunks - 1).bit_length())
    s_re, s_im = power(t * (2 ** jnp.arange(n_steps)))
    tab = jnp.stack([jnp.concatenate([s_re, s_re], axis=-1),
                     jnp.concatenate([-s_im, s_im], axis=-1)], axis=1)
    tab = tab.transpose(2, 0, 1, 3).reshape(g, 2 * n_steps, 2 * S5_STATE)
    return kvec, state_map, cross, tab, n_steps


def _s5_conv_kernel(n_chunks, n_steps, u_ref, kvec_ref, st_ref, cross_ref, tab_ref, y_ref, conv):
    t = S5_CHUNK
    row = lax.broadcasted_iota(jnp.int32, (t, LANES), 0)
    lane = lax.broadcasted_iota(jnp.int32, (t, LANES), 1)
    causal = jnp.bitwise_and(lane, t - 1) >= row
    for cin in range(S5_GROUP):
        for cp in range(S5_GROUP // 2):
            base = jnp.broadcast_to(kvec_ref[0, cin, cp:cp + 1, :], (t, LANES))
            blk = pltpu.roll(base, 0, 1, stride=1, stride_axis=0)
            conv[cin * t:(cin + 1) * t, cp * LANES:(cp + 1) * LANES] = jnp.where(causal, blk, 0.0).astype(BF16)
    u = u_ref[0]
    y = _dot(u, conv[...])
    x = _dot(u, st_ref[0])
    chunk = lax.broadcasted_iota(jnp.int32, x.shape, 0) % n_chunks
    half = x.shape[1] // 2
    for j in range(n_steps):
        sh = 1 << j
        xs = jnp.where(chunk >= sh, pltpu.roll(x, sh, 0), 0.0)
        x = x + tab_ref[0, 2 * j:2 * j + 1, :] * xs + tab_ref[0, 2 * j + 1:2 * j + 2, :] * pltpu.roll(xs, half, 1)
    x_in = jnp.where(chunk >= 1, pltpu.roll(x, 1, 0), 0.0)
    y_ref[0] = y + _dot(x_in.astype(BF16), cross_ref[0])


def _s5_conv(u2d, batch, seq, tables):
    kvec, state_map, cross, tab, n_steps = tables
    n, width = u2d.shape
    g = width // S5_GROUP
    t = S5_CHUNK
    n_chunks = seq // t
    ct = S5_GROUP * t
    ut = u2d.astype(BF16).reshape(batch, n_chunks, t, g, S5_GROUP).transpose(3, 0, 1, 4, 2)
    ut = ut.reshape(g, batch * n_chunks, ct)
    rows = batch * n_chunks
    y = pl.pallas_call(
        functools.partial(_s5_conv_kernel, n_chunks, n_steps),
        grid=(g,),
        in_specs=[
            pl.BlockSpec((1, rows, ct), lambda i: (i, 0, 0)),
            pl.BlockSpec((1,) + kvec.shape[1:], lambda i: (i, 0, 0, 0)),
            pl.BlockSpec((1,) + state_map.shape[1:], lambda i: (i, 0, 0)),
            pl.BlockSpec((1,) + cross.shape[1:], lambda i: (i, 0, 0)),
            pl.BlockSpec((1,) + tab.shape[1:], lambda i: (i, 0, 0)),
        ],
        out_specs=pl.BlockSpec((1, rows, ct), lambda i: (i, 0, 0)),
        out_shape=jax.ShapeDtypeStruct((g, rows, ct), F32),
        scratch_shapes=[pltpu.VMEM((ct, ct), BF16)],
        compiler_params=_params("parallel"),
        name="s5_conv",
    )(ut, kvec, state_map, cross, tab)
    return y.reshape(g, batch, n_chunks, S5_GROUP, t).transpose(1, 2, 4, 0, 3).reshape(n, width)


def _pair_blockdiag(y, left):
    return jnp.concatenate([jnp.where(left, y, 0.0), jnp.where(left, 0.0, y)], axis=0).astype(BF16)


def _rwkv_chunk_kernel(r_ref, lw_ref, k_ref, v_ref, an_ref, bb_ref, rhat_o, yhat_o, g_o, h_o):
    t = RWKV_CHUNK
    rows, width = r_ref.shape
    n_chunks = rows // t
    pairs = width // LANES
    row = lax.broadcasted_iota(jnp.int32, (t, LANES), 0)
    lane = lax.broadcasted_iota(jnp.int32, (t, LANES), 1)
    col = jnp.bitwise_and(lane, RWKV_HEAD - 1)
    left = lane < RWKV_HEAD
    incl = row >= col
    strict = row > col
    same_blk = (row // RWKV_SUB) == (col // RWKV_SUB)
    eye = jnp.where(row == col, 1.0, 0.0).astype(F32)
    brow = lax.broadcasted_iota(jnp.int32, (LANES, LANES), 0)
    bcol = lax.broadcasted_iota(jnp.int32, (LANES, LANES), 1)
    same_head = (brow // RWKV_HEAD) == (bcol // RWKV_HEAD)
    eye_full = brow == bcol
    crow = lax.broadcasted_iota(jnp.int32, (rows, rows), 0)
    ccol = lax.broadcasted_iota(jnp.int32, (rows, rows), 1)
    tril = jnp.where(crow >= ccol, jnp.where((crow // t) == (ccol // t), 1.0, 0.0), 0.0).astype(BF16)

    lw = lw_ref[...]
    p1 = lw.astype(BF16)
    rem = lw - p1.astype(F32)
    p2 = rem.astype(BF16)
    p3 = (rem - p2.astype(F32)).astype(BF16)
    cs = _dot(tril, p1) + _dot(tril, p2) + _dot(tril, p3)
    a_t = an_ref[...] * jnp.exp(cs - lw)
    r_t = r_ref[...] * jnp.exp(cs)
    p_inv = jnp.exp(-cs)
    b_t = bb_ref[...] * p_inv
    k_t = k_ref[...] * p_inv

    def bd(y):
        return _pair_blockdiag(y, left)

    def pmm(x, y_bd):
        return _dot(x.astype(BF16), y_bd)

    units = [(c, j) for c in range(n_chunks) for j in range(pairs)]

    def tile(arr, c, j):
        return arr[c * t:(c + 1) * t, j * LANES:(j + 1) * LANES]

    a2 = [tile(a_t, c, j) for c, j in units]
    r2 = [tile(r_t, c, j) for c, j in units]
    v2 = [tile(v_ref[...], c, j) for c, j in units]
    ar = [jnp.concatenate([a, r], axis=0).astype(BF16) for a, r in zip(a2, r2)]
    prod = [_dot_nt(x, jnp.concatenate([bd(tile(b_t, c, j)), bd(tile(k_t, c, j))], axis=0))
            for x, (c, j) in zip(ar, units)]
    l_ab = [jnp.where(strict, p[:t, :LANES], 0.0) for p in prod]
    l_rb = [jnp.where(incl, p[t:, :LANES], 0.0) for p in prod]
    l_ak = [jnp.where(strict, p[:t, LANES:], 0.0) for p in prod]
    l_rk = [jnp.where(incl, p[t:, LANES:], 0.0) for p in prod]
    l_d = [jnp.where(same_blk, m, 0.0) for m in l_ab]
    l_o = [m - d for m, d in zip(l_ab, l_d)]
    l2 = [pmm(m, bd(m)) for m in l_d]
    q1 = [pmm(eye + m, bd(eye + s)) for m, s in zip(l_d, l2)]
    l4 = [pmm(m, bd(m)) for m in l2]
    q2 = [pmm(q, bd(eye + s)) for q, s in zip(q1, l4)]
    l8 = [pmm(m, bd(m)) for m in l4]
    d_inv = [pmm(q, bd(eye + s)) for q, s in zip(q2, l8)]
    akv = [pmm(jnp.concatenate([m, n], axis=0), bd(v)) for m, n, v in zip(l_ak, l_rk, v2)]
    x1 = [z[:t] for z in akv]

    def bd2(z1, z2):
        return jnp.concatenate([bd(z1), bd(z2)], axis=1)

    dz = [pmm(d, bd2(a, x)) for d, a, x in zip(d_inv, a2, x1)]
    n1 = [pmm(d, bd(o)) for d, o in zip(d_inv, l_o)]
    n2 = [pmm(m, bd(m)) for m in n1]
    t1 = [z + pmm(m, bd2(z[:, :LANES], z[:, LANES:])) for z, m in zip(dz, n2)]
    wu = [z + pmm(m, bd2(z[:, :LANES], z[:, LANES:])) for z, m in zip(t1, n1)]
    ry = [pmm(m, bd2(z[:, :LANES], z[:, LANES:])) for m, z in zip(l_rb, wu)]
    rk_v = [z[t:] for z in akv]

    for i, (c, j) in enumerate(units):
        rs = slice(c * t, (c + 1) * t)
        ls = slice(j * LANES, (j + 1) * LANES)
        rhat_o[rs, ls] = r2[i] + ry[i][:, :LANES]
        yhat_o[rs, ls] = ry[i][:, LANES:] + rk_v[i]
        cs_c = cs[rs, ls]
        cs_end = cs_c[t - 1:t, :]
        p_end = jnp.exp(cs_end - cs_c)
        b_h = (bb_ref[rs, ls] * p_end).astype(BF16)
        k_h = (k_ref[rs, ls] * p_end).astype(BF16)
        g_full = _dot_tn(wu[i][:, :LANES].astype(BF16), b_h)
        g_bd = jnp.where(same_head, g_full, 0.0) + jnp.where(eye_full, jnp.exp(cs_end), 0.0)
        g_o[c, j] = g_bd.astype(BF16)
        uv = jnp.concatenate([wu[i][:, LANES:], v2[i]], axis=0).astype(BF16)
        h_full = _dot_tn(uv, jnp.concatenate([b_h, k_h], axis=0))
        h_o[c, j] = jnp.where(left, h_full[:RWKV_HEAD], h_full[RWKV_HEAD:])


RWKV_CHUNKS_PER_STEP = 2


def _rwkv_chunks(r, lw, k, v, an, bb):
    n, width = r.shape
    t = RWKV_CHUNK
    cps = RWKV_CHUNKS_PER_STEP
    pairs = width // LANES
    nck = n // t
    tok = pl.BlockSpec((cps * t, width), lambda i: (i, 0))
    return pl.pallas_call(
        _rwkv_chunk_kernel,
        grid=(nck // cps,),
        in_specs=[tok] * 6,
        out_specs=[tok, tok,
                   pl.BlockSpec((cps, pairs, LANES, LANES), lambda i: (i, 0, 0, 0)),
                   pl.BlockSpec((cps, pairs, RWKV_HEAD, LANES), lambda i: (i, 0, 0, 0))],
        out_shape=[jax.ShapeDtypeStruct((n, width), F32)] * 2
        + [jax.ShapeDtypeStruct((nck, pairs, LANES, LANES), BF16),
           jax.ShapeDtypeStruct((nck, pairs, RWKV_HEAD, LANES), F32)],
        compiler_params=_params("parallel"),
        name="rwkv_chunk",
    )(r, lw, k, v, an, bb)


def _rwkv_scan_kernel(rhat_ref, yhat_ref, g_ref, h_ref, gate_ref, bonus_ref, lnw_ref, lnb_ref,
                      ones_ref, y_ref, state):
    t = RWKV_CHUNK
    batch, pairs = state.shape[0], state.shape[1]
    n_chunks = rhat_ref.shape[1] // t
    lane = lax.broadcasted_iota(jnp.int32, (RWKV_HEAD, LANES), 1)
    left = lane < RWKV_HEAD

    @pl.when(pl.program_id(0) == 0)
    def _():
        state[...] = jnp.zeros_like(state)

    s = [[state[b, j] for j in range(pairs)] for b in range(batch)]
    y_rows = []
    for b in range(batch):
        chunk_rows = []
        for c in range(n_chunks):
            rs = slice(c * t, (c + 1) * t)
            tiles = []
            for j in range(pairs):
                ls = slice(j * LANES, (j + 1) * LANES)
                s_bd = _pair_blockdiag(s[b][j], left)
                tiles.append(yhat_ref[b, rs, ls] + _dot_nt(rhat_ref[b, rs, ls].astype(BF16), s_bd))
                s[b][j] = _dot(s[b][j].astype(BF16), g_ref[b, c, j]) + h_ref[b, c, j]
            chunk_rows.append(jnp.concatenate(tiles, axis=1))
        y_rows.append(jnp.concatenate(chunk_rows, axis=0))
    for b in range(batch):
        for j in range(pairs):
            state[b, j] = s[b][j]

    ones = ones_ref[...]
    inv = 1.0 / RWKV_HEAD
    for b in range(batch):
        y = y_rows[b]
        cen = y - _dot(y.astype(BF16), ones) * inv
        var = _dot((cen * cen).astype(BF16), ones) * inv
        yn = cen * lax.rsqrt(var + RWKV_GN_EPS) * lnw_ref[...] + lnb_ref[...]
        y_ref[b] = (yn + bonus_ref[b]) * gate_ref[b]


def _rwkv_scan(rhat, yhat, g_mat, h_mat, gate, bonus, ln_w, ln_b, head_ones, batch, seq):
    n, width = rhat.shape
    t = RWKV_CHUNK
    cps = RWKV_CHUNKS_PER_STEP
    pairs = width // LANES
    nc = seq // t
    tok3 = lambda a: a.reshape(batch, seq, width)
    tok = pl.BlockSpec((batch, cps * t, width), lambda i: (0, i, 0))
    out = pl.pallas_call(
        _rwkv_scan_kernel,
        grid=(nc // cps,),
        in_specs=[tok, tok,
                  pl.BlockSpec((batch, cps, pairs, LANES, LANES), lambda i: (0, i, 0, 0, 0)),
                  pl.BlockSpec((batch, cps, pairs, RWKV_HEAD, LANES), lambda i: (0, i, 0, 0, 0)),
                  tok, tok, _full(ln_w.shape), _full(ln_b.shape), _full(head_ones.shape)],
        out_specs=tok,
        out_shape=jax.ShapeDtypeStruct((batch, seq, width), F32),
        scratch_shapes=[pltpu.VMEM((batch, pairs, RWKV_HEAD, LANES), F32)],
        compiler_params=_params("arbitrary"),
        name="rwkv_scan",
    )(tok3(rhat), tok3(yhat), g_mat.reshape(batch, nc, pairs, LANES, LANES),
      h_mat.reshape(batch, nc, pairs, RWKV_HEAD, LANES), tok3(gate), tok3(bonus), ln_w, ln_b, head_ones)
    return out.reshape(n, width)


def _post_mix_kernel(n_experts, x_ref, yc_ref, u_ref, yr_ref, d_ref, wglu_ref, wout_ref, gain_ref,
                     rw_ref, rb_ref, x1_o, hm_o, gate_o, lpos_o, meta_o, tot_o, running):
    tm = x_ref.shape[0]
    width = yc_ref.shape[1]

    @pl.when(pl.program_id(0) == 0)
    def _():
        running[...] = jnp.zeros_like(running)

    y = yc_ref[...] + d_ref[...] * u_ref[...]
    y = 0.5 * y * (1.0 + jnp.tanh(0.7978845608028654 * (y + 0.044715 * (y * y * y))))
    y = y * _sigmoid(_dot(y.astype(BF16), wglu_ref[...]))
    x1 = (x_ref[...] + _dot(y.astype(BF16), wout_ref[:width, :])
          + _dot(yr_ref[...].astype(BF16), wout_ref[width:, :]))
    x1_o[...] = x1
    hm = _rms(x1, gain_ref[...])
    hm_o[...] = hm.astype(BF16)
    logits = _dot(hm.astype(BF16), rw_ref[...]) + rb_ref[...]

    lane = lax.broadcasted_iota(jnp.int32, logits.shape, 1)
    lanef = lane.astype(F32)
    sel = jnp.zeros(logits.shape, F32)
    idx_cols, val_cols = [], []
    work = logits
    for _ in range(TOP_K):
        m = jnp.max(work, axis=-1, keepdims=True)
        pick = jnp.min(jnp.where(work == m, lanef, float(LANES)), axis=-1, keepdims=True)
        hit = lanef == pick
        sel = jnp.where(hit, 1.0, sel)
        work = jnp.where(hit, -jnp.inf, work)
        idx_cols.append(pick)
        val_cols.append(m)
    exps = [jnp.exp(vv - val_cols[0]) for vv in val_cols]
    denom = exps[0] + exps[1] + exps[2] + exps[3]

    row = lax.broadcasted_iota(jnp.int32, (tm, tm), 0)
    col = lax.broadcasted_iota(jnp.int32, (tm, tm), 1)
    before = jnp.where(row > col, 1.0, 0.0).astype(BF16)
    local = _dot(before, sel.astype(BF16))
    cnt = jnp.sum(sel, axis=0, keepdims=True)
    cnt_al = jnp.floor((cnt + (RUN_ALIGN - 1)) * (1.0 / RUN_ALIGN)) * RUN_ALIGN
    erow = lax.broadcasted_iota(jnp.int32, (LANES, LANES), 0)
    ecol = lax.broadcasted_iota(jnp.int32, (LANES, LANES), 1)
    upper = jnp.where(erow < ecol, 1.0, 0.0).astype(BF16)
    toff = _dot(jnp.broadcast_to(cnt_al, (SUBLANES, LANES)).astype(BF16), upper)[0:1]
    tbase = running[...]
    gate_out = jnp.zeros(logits.shape, F32)
    lpos_out = jnp.zeros(logits.shape, F32)
    for j in range(TOP_K):
        lp = jnp.sum(jnp.where(lanef == idx_cols[j], local + toff, 0.0), axis=-1, keepdims=True)
        gate_out = jnp.where(lane == j, exps[j] / denom, gate_out)
        lpos_out = jnp.where(lane == j, lp, lpos_out)
    gate_o[...] = gate_out
    lpos_o[...] = lpos_out.astype(jnp.int32)
    srow = lax.broadcasted_iota(jnp.int32, (SUBLANES, LANES), 0)
    meta = jnp.where(srow == 0, cnt, jnp.where(srow == 1, tbase, jnp.where(srow == 2, toff, 0.0)))
    meta_o[...] = meta.astype(jnp.int32)
    running[...] = tbase + cnt_al
    tot_o[...] = jnp.broadcast_to(running[...], tot_o.shape).astype(jnp.int32)
    del n_experts


def _post_mix(x2d, yconv, u, y_rwkv, s5_d, w_glu, w_out, gain, rw, rb, n_experts):
    n, d = x2d.shape
    width = yconv.shape[1]
    tm = MOE_TILE
    row = lambda i: (i, 0)
    tok_d = pl.BlockSpec((tm, d), row)
    tok_w = pl.BlockSpec((tm, width), row)
    tok_l = pl.BlockSpec((tm, LANES), row)
    return pl.pallas_call(
        functools.partial(_post_mix_kernel, n_experts),
        grid=(n // tm,),
        in_specs=[tok_d, tok_w, tok_w, tok_w, _full(s5_d.shape), _full(w_glu.shape),
                  _full(w_out.shape), _full(gain.shape), _full(rw.shape), _full(rb.shape)],
        out_specs=[tok_d, tok_d, tok_l, tok_l,
                   pl.BlockSpec((SUBLANES, LANES), row), _full((SUBLANES, LANES))],
        out_shape=[jax.ShapeDtypeStruct((n, d), F32), jax.ShapeDtypeStruct((n, d), BF16),
                   jax.ShapeDtypeStruct((n, LANES), F32), jax.ShapeDtypeStruct((n, LANES), jnp.int32),
                   jax.ShapeDtypeStruct((n // tm * SUBLANES, LANES), jnp.int32),
                   jax.ShapeDtypeStruct((SUBLANES, LANES), jnp.int32)],
        scratch_shapes=[pltpu.VMEM((1, LANES), F32)],
        compiler_params=_params("arbitrary"),
        name="post_mix",
    )(x2d, yconv, u, y_rwkv, s5_d, w_glu, w_out, gain, rw, rb)


def _sorted_rows(tm, n_experts):
    return -(-(tm * TOP_K + n_experts * (RUN_ALIGN - 1)) // LANES) * LANES


RUN_CHUNKS = (8, 4, 2, 1)


def _start_runs(n_experts, pstart_ref, meta_ref, make_copy):
    big = RUN_CHUNKS[0]
    for e in range(n_experts):
        groups = (meta_ref[0, e] + (RUN_ALIGN - 1)) // RUN_ALIGN
        seg = pstart_ref[e] + meta_ref[1, e]
        loc = meta_ref[2, e]
        n_big = groups // big

        def body(g, c, seg=seg, loc=loc):
            off = g * (big * RUN_ALIGN)
            make_copy(pl.multiple_of(loc + off, RUN_ALIGN), pl.multiple_of(seg + off, RUN_ALIGN),
                      big * RUN_ALIGN).start(priority=e % 2)
            return c

        lax.fori_loop(0, n_big, body, 0)
        done = n_big * big
        for size in RUN_CHUNKS[1:]:
            take = ((groups - done) // size) > 0

            @pl.when(take)
            def _(done=done, size=size, seg=seg, loc=loc):
                off = done * RUN_ALIGN
                make_copy(pl.multiple_of(loc + off, RUN_ALIGN), pl.multiple_of(seg + off, RUN_ALIGN),
                          size * RUN_ALIGN).start(priority=e % 2)

            done = done + jnp.where(take, size, 0)


def _wait_runs(n_experts, meta_ref, make_copy):
    groups = 0
    for e in range(n_experts):
        groups = groups + (meta_ref[0, e] + (RUN_ALIGN - 1)) // RUN_ALIGN

    def body(g, c):
        make_copy(0, 0, RUN_ALIGN).wait()
        return c

    lax.fori_loop(0, groups, body, 0)


def _dispatch_kernel(n_experts, pstart_ref, tot_ref, meta_ref, lpos_ref, hm_ref, xs_ref, srt, zero, sem):
    tm = hm_ref.shape[0]
    rows = srt.shape[0]
    pos_t = jnp.transpose(lpos_ref[...].astype(F32))
    rid = lax.broadcasted_iota(jnp.int32, (rows, tm), 0).astype(F32)
    perm = jnp.zeros((rows, tm), F32)
    for j in range(TOP_K):
        perm = perm + jnp.where(rid == pos_t[j:j + 1, :], 1.0, 0.0)
    srt[...] = _dot(perm.astype(BF16), hm_ref[...])

    def make_copy(loc, dst, size):
        return pltpu.make_async_copy(srt.at[pl.ds(loc, size)], xs_ref.at[pl.ds(dst, size)], sem)

    _start_runs(n_experts, pstart_ref, meta_ref, make_copy)
    _wait_runs(n_experts, meta_ref, make_copy)

    @pl.when(pl.program_id(0) == pl.num_programs(0) - 1)
    def _():
        zero[...] = jnp.zeros_like(zero)
        for wait in (False, True):
            for e in range(n_experts):
                used = tot_ref[e]
                start = pstart_ref[e] + used
                groups = (-used % EXPERT_TILE) // RUN_ALIGN

                def body(g, c, start=start):
                    cp = pltpu.make_async_copy(
                        zero, xs_ref.at[pl.ds(pl.multiple_of(start + g * RUN_ALIGN, RUN_ALIGN), RUN_ALIGN)], sem)
                    if wait:
                        cp.wait()
                    else:
                        cp.start()
                    return c

                lax.fori_loop(0, groups, body, 0)

        last = n_experts - 1
        end = pstart_ref[last] + tot_ref[last] + (-tot_ref[last] % EXPERT_TILE)
        srt[0:EXPERT_TILE, :] = jnp.zeros((EXPERT_TILE, srt.shape[1]), srt.dtype)
        for wait in (False, True):
            def tail(b, c):
                cp = pltpu.make_async_copy(
                    srt.at[pl.ds(0, EXPERT_TILE)],
                    xs_ref.at[pl.ds(pl.multiple_of(end + b * EXPERT_TILE, EXPERT_TILE), EXPERT_TILE)], sem)
                if wait:
                    cp.wait()
                else:
                    cp.start()
                return c

            lax.fori_loop(0, (xs_ref.shape[0] - end) // EXPERT_TILE, tail, 0)


def _dispatch(hm, lpos, meta, pstarts, totals, m_pad, n_experts):
    n, dh = hm.shape
    tm = MOE_TILE
    rows = _sorted_rows(tm, n_experts)
    grid_spec = pltpu.PrefetchScalarGridSpec(
        num_scalar_prefetch=2,
        grid=(n // tm,),
        in_specs=[pl.BlockSpec((SUBLANES, LANES), lambda i, ps, tt: (i, 0), memory_space=pltpu.SMEM),
                  pl.BlockSpec((tm, LANES), lambda i, ps, tt: (i, 0)),
                  pl.BlockSpec((tm, dh), lambda i, ps, tt: (i, 0))],
        out_specs=pl.BlockSpec(memory_space=pl.ANY),
        scratch_shapes=[pltpu.VMEM((rows, dh), F32), pltpu.VMEM((RUN_ALIGN, dh), F32),
                        pltpu.SemaphoreType.DMA(())],
    )
    return pl.pallas_call(
        functools.partial(_dispatch_kernel, n_experts),
        grid_spec=grid_spec,
        out_shape=jax.ShapeDtypeStruct((m_pad, dh), F32),
        compiler_params=_params("arbitrary"),
        name="dispatch",
    )(pstarts, totals, meta, lpos, hm)


def _experts_kernel(be_ref, nu_ref, xs_ref, wg_ref, bg_ref, wu_ref, bu_ref, wd_ref, bd_ref, ys_ref,
                    wg_s, wu_s, wd_s):
    i = pl.program_id(0)
    changed = jnp.logical_or(i == 0, be_ref[i] != be_ref[jnp.maximum(i - 1, 0)])

    @pl.when(changed)
    def _():
        wg_s[...] = wg_ref[0].astype(BF16)
        wu_s[...] = wu_ref[0].astype(BF16)
        wd_s[...] = wd_ref[0].astype(BF16)

    @pl.when(i < nu_ref[0])
    def _():
        xb = xs_ref[...].astype(BF16)
        gt = jnp.minimum(_dot(xb, wg_s[...]) + bg_ref[0], SWIGLU_LIMIT)
        up = jnp.clip(_dot(xb, wu_s[...]) + bu_ref[0], -SWIGLU_LIMIT, SWIGLU_LIMIT)
        act = (up + 1.0) * gt * _sigmoid(SWIGLU_ALPHA * gt)
        ys_ref[...] = _dot(act.astype(BF16), wd_s[...]) + bd_ref[0]

    @pl.when(i >= nu_ref[0])
    def _():
        ys_ref[...] = jnp.zeros_like(ys_ref)


def _experts(xs, block_e, n_used, wg, bg, wu, bu, wd, bd):
    m_pad, dh = xs.shape
    tmb = EXPERT_TILE
    d, dff = wg.shape[1], wg.shape[2]
    wmap = lambda i, be, nu: (be[i], 0, 0)
    grid_spec = pltpu.PrefetchScalarGridSpec(
        num_scalar_prefetch=2,
        grid=(m_pad // tmb,),
        in_specs=[pl.BlockSpec((tmb, dh), lambda i, be, nu: (jnp.where(i < nu[0], i, 0), 0)),
                  pl.BlockSpec((1, d, dff), wmap), pl.BlockSpec((1, 1, dff), wmap),
                  pl.BlockSpec((1, d, dff), wmap), pl.BlockSpec((1, 1, dff), wmap),
                  pl.BlockSpec((1, dff, d), wmap), pl.BlockSpec((1, 1, d), wmap)],
        out_specs=pl.BlockSpec((tmb, dh), lambda i, be, nu: (i, 0)),
        scratch_shapes=[pltpu.VMEM((d, dff), BF16), pltpu.VMEM((d, dff), BF16), pltpu.VMEM((dff, d), BF16)],
    )
    return pl.pallas_call(
        _experts_kernel,
        grid_spec=grid_spec,
        out_shape=jax.ShapeDtypeStruct((m_pad, dh), F32),
        compiler_params=_params("arbitrary"),
        name="experts",
    )(block_e, n_used, xs, wg, bg, wu, bu, wd, bd)


def _final_kernel(n_experts, pstart_ref, meta_ref, lpos_ref, x1_ref, gate_ref, p_ref, ys_ref, gple_ref,
                  wproj_ref, wgate_ref, gfin_ref, out_ref, buf, sem):
    tm = x1_ref.shape[0]
    rows = buf.shape[0]

    @pl.when(pl.program_id(0) == 0)
    def _():
        buf[...] = jnp.zeros_like(buf)

    def make_copy(loc, src, size):
        return pltpu.make_async_copy(ys_ref.at[pl.ds(src, size)], buf.at[pl.ds(loc, size)], sem)

    _start_runs(n_experts, pstart_ref, meta_ref, make_copy)

    cid = lax.broadcasted_iota(jnp.int32, (tm, rows), 1)
    lpos = lpos_ref[...]
    gates = gate_ref[...]
    comb = jnp.zeros((tm, rows), F32)
    for j in range(TOP_K):
        comb = comb + jnp.where(cid == lpos[:, j:j + 1], gates[:, j:j + 1], 0.0)

    _wait_runs(n_experts, meta_ref, make_copy)
    x2 = x1_ref[...] + _dot(comb.astype(BF16), buf[...].astype(BF16))
    gate = _sigmoid(_dot(_rms(x2, gple_ref[...]).astype(BF16), wgate_ref[...]))
    x3 = x2 + _dot(p_ref[...].astype(BF16), wproj_ref[...]) * gate
    out_ref[...] = _rms(x3, gfin_ref[...])


def _final(x1, gates, lpos, meta, pstarts, p2d, ys, g_ple, w_proj, w_gate, g_fin, n_experts):
    n, d = x1.shape
    tm = MOE_TILE
    rows = _sorted_rows(tm, n_experts)
    row = lambda i, ps: (i, 0)
    const = lambda shape: pl.BlockSpec(shape, lambda i, ps: (0,) * len(shape))
    grid_spec = pltpu.PrefetchScalarGridSpec(
        num_scalar_prefetch=1,
        grid=(n // tm,),
        in_specs=[pl.BlockSpec((SUBLANES, LANES), row, memory_space=pltpu.SMEM),
                  pl.BlockSpec((tm, LANES), row), pl.BlockSpec((tm, d), row), pl.BlockSpec((tm, LANES), row),
                  pl.BlockSpec((tm, p2d.shape[1]), row), pl.BlockSpec(memory_space=pl.ANY),
                  const(g_ple.shape), const(w_proj.shape), const(w_gate.shape), const(g_fin.shape)],
        out_specs=pl.BlockSpec((tm, d), row),
        scratch_shapes=[pltpu.VMEM((rows, ys.shape[1]), ys.dtype), pltpu.SemaphoreType.DMA(())],
    )
    return pl.pallas_call(
        functools.partial(_final_kernel, n_experts),
        grid_spec=grid_spec,
        out_shape=jax.ShapeDtypeStruct((n, d), F32),
        compiler_params=_params("arbitrary"),
        name="final",
    )(pstarts, meta, lpos, x1, gates, p2d, ys, g_ple, w_proj, w_gate, g_fin)


def _head_ones(width):
    hd = jnp.arange(width) // RWKV_HEAD
    return (hd[:, None] == hd[None, :]).astype(BF16)


def _layer(x2d, p2d, batch, seq, norm_mix_g, w_in, s5_lam_re, s5_lam_im, s5_log_dt, s5_b_re, s5_b_im,
           s5_c_re, s5_c_im, s5_d, s5_w_glu, mu_rkv, mu_wag, w0, w1, w2, a0, a1, a2, g1, g2, k_k, k_a,
           r_k, ln_w, ln_b, w_out, norm_moe_g, router_w, router_b, wg, bg, wu, bu, wd, bd,
           norm_ple_g, ple_w_proj, ple_w_gate):
    n, d = x2d.shape
    width = w_in.shape[1] // 4
    n_experts = router_w.shape[1]
    row2 = lambda t: t.reshape(1, -1).astype(F32)
    head_ones = _head_ones(width)
    vecs = jnp.zeros((SUBLANES, width), F32)
    vecs = vecs.at[0].set(w0).at[1].set(a0).at[2].set(k_k).at[3].set(k_a).at[4].set(r_k.reshape(-1))

    w_ext, w2_cat = _mix_in_weights(w_in, w1, a1, g1, mu_wag, w2, a2, g2)
    u, r, lw, k, v, an, bb, gate, bonus = _mix_in(
        x2d, seq, row2(norm_mix_g), w_ext, mu_rkv.astype(F32), vecs, w2_cat, head_ones)

    tables = _s5_tables(s5_lam_re, s5_lam_im, s5_log_dt, s5_b_re, s5_b_im, s5_c_re, s5_c_im,
                        seq // S5_CHUNK)
    yconv = _s5_conv(u, batch, seq, tables)

    rhat, yhat, g_mat, h_mat = _rwkv_chunks(r, lw, k, v, an, bb)
    y_rwkv = _rwkv_scan(rhat, yhat, g_mat, h_mat, gate, bonus, row2(ln_w), row2(ln_b), head_ones,
                        batch, seq)

    rw = jnp.zeros((d, LANES), BF16).at[:, :n_experts].set(router_w.astype(BF16))
    rb = jnp.full((1, LANES), NEG_BIG, F32).at[0, :n_experts].set(router_b.astype(F32))
    x1, hm, gates, lpos, meta, totals = _post_mix(
        x2d, yconv, u, y_rwkv, row2(s5_d), s5_w_glu.astype(BF16), w_out.astype(BF16),
        row2(norm_moe_g), rw, rb, n_experts)

    tmb = EXPERT_TILE
    n_tiles = n // MOE_TILE
    max_rows = n * TOP_K + n_tiles * n_experts * (RUN_ALIGN - 1)
    n_blocks = -(-max_rows // tmb) + n_experts
    seg = totals[0, :n_experts].astype(jnp.int32)
    padded = ((seg + tmb - 1) // tmb) * tmb
    pends = jnp.cumsum(padded)
    pstarts = (pends - padded).astype(jnp.int32)
    block_start = jnp.arange(n_blocks, dtype=jnp.int32) * tmb
    block_e = jnp.minimum(jnp.sum((pends[None, :] <= block_start[:, None]).astype(jnp.int32), axis=1),
                          n_experts - 1).astype(jnp.int32)
    n_used = (pends[-1] // tmb).astype(jnp.int32).reshape(1)

    xs = _dispatch(hm, lpos, meta, pstarts, seg, n_blocks * tmb, n_experts)
    ys = _experts(xs, block_e, n_used, wg, bg.reshape(n_experts, 1, -1).astype(F32),
                  wu, bu.reshape(n_experts, 1, -1).astype(F32), wd,
                  bd.reshape(n_experts, 1, -1).astype(F32))
    return (x1, gates, lpos, meta, pstarts, ys, p2d, row2(norm_ple_g), ple_w_proj.astype(BF16),
            ple_w_gate.astype(BF16), n_experts)


def kernel(x, p, norm_mix_g, w_in, s5_lam_re, s5_lam_im, s5_log_dt, s5_b_re, s5_b_im, s5_c_re, s5_c_im, s5_d, s5_w_glu, rwkv_mu_rkv, rwkv_mu_wag, rwkv_w0, rwkv_w1, rwkv_w2, rwkv_a0, rwkv_a1, rwkv_a2, rwkv_g1, rwkv_g2, rwkv_k_k, rwkv_k_a, rwkv_r_k, rwkv_ln_w, rwkv_ln_b, w_out, norm_moe_g, router_w, router_b, exp_w_gate, exp_b_gate, exp_w_up, exp_b_up, exp_w_down, exp_b_down, norm_ple_g, ple_w_proj, ple_w_gate, final_norm_g):
    batch, seq, d = x.shape
    assert w_in.shape[0] == 1, "the final kernel fuses the last RMSNorm: single-layer stacks only"
    i = 0
    x1, gates, lpos, meta, pstarts, ys, p2d, g_ple, w_proj, w_gate, n_experts = _layer(
        x.reshape(batch * seq, d), p[i].reshape(batch * seq, -1), batch, seq, norm_mix_g[i], w_in[i],
        s5_lam_re[i], s5_lam_im[i], s5_log_dt[i], s5_b_re[i], s5_b_im[i], s5_c_re[i], s5_c_im[i],
        s5_d[i], s5_w_glu[i], rwkv_mu_rkv[i], rwkv_mu_wag[i], rwkv_w0[i], rwkv_w1[i], rwkv_w2[i],
        rwkv_a0[i], rwkv_a1[i], rwkv_a2[i], rwkv_g1[i], rwkv_g2[i], rwkv_k_k[i], rwkv_k_a[i],
        rwkv_r_k[i], rwkv_ln_w[i], rwkv_ln_b[i], w_out[i], norm_moe_g[i], router_w[i],
        router_b[i], exp_w_gate[i], exp_b_gate[i], exp_w_up[i], exp_b_up[i], exp_w_down[i],
        exp_b_down[i], norm_ple_g[i], ple_w_proj[i], ple_w_gate[i])
    out = _final(x1, gates, lpos, meta, pstarts, p2d, ys, g_ple, w_proj, w_gate,
                 final_norm_g.reshape(1, -1).astype(F32), n_experts)
    return out.reshape(batch, seq, d)
```

```python
import functools

import jax
import jax.numpy as jnp
from jax import lax
from jax.experimental import pallas as pl
from jax.experimental.pallas import tpu as pltpu

F32 = jnp.float32
BF16 = jnp.bfloat16

S5_GROUP = 16
S5_STATE = 64
RWKV_HEAD = 64
DECAY_LORA = 64
ICLR_LORA = 64
TOP_K = 4
RMS_EPS = 1e-6
RWKV_GN_EPS = 64e-5
LAMBDA_RE_MAX = -1e-4
SWIGLU_LIMIT = 7.0
SWIGLU_ALPHA = 1.702

LANES = 128
SUBLANES = 8
VMEM_LIMIT = 56 * 1024 * 1024

S5_CHUNK = 64
RWKV_CHUNK = 64
RWKV_SUB = 16
TOKEN_TILE = 512
EXPERT_TILE = 512
MOE_TILE = 512
RUN_ALIGN = 8
NEG_BIG = -1e30


def _dot(a, b):
    return jnp.dot(a, b, preferred_element_type=F32)


def _dot_nt(a, b):
    return lax.dot_general(a, b, (((1,), (1,)), ((), ())), preferred_element_type=F32)


def _dot_tn(a, b):
    return lax.dot_general(a, b, (((0,), (0,)), ((), ())), preferred_element_type=F32)


def _rms(t, gain):
    return t * lax.rsqrt(jnp.mean(t * t, axis=-1, keepdims=True) + RMS_EPS) * gain


def _sigmoid(t):
    return 1.0 / (1.0 + jnp.exp(-t))


def _params(*sem):
    return pltpu.CompilerParams(dimension_semantics=sem, vmem_limit_bytes=VMEM_LIMIT)


def _full(shape):
    return pl.BlockSpec(shape, lambda *_: (0,) * len(shape))


def _mix_in_kernel(seq, x_ref, xp_ref, gain_ref, win_ref, murkv_ref, vec_ref, w2_ref, ones_ref,
                   u_o, r_o, lw_o, k_o, v_o, an_o, bb_o, g_o, bonus_o):
    tm = x_ref.shape[0]
    width = u_o.shape[1]
    lora = w2_ref.shape[0]
    gain = gain_ref[...]
    h = _rms(x_ref[...], gain)
    keep = jnp.where((pl.program_id(0) * tm) % seq == 0, 0.0, 1.0)
    hp = _rms(xp_ref[...], gain) * keep
    h_ext = jnp.concatenate([hp, h], axis=0).astype(BF16)
    proj = _dot(h_ext, win_ref[...])
    shifted = pltpu.roll(proj[:, width:4 * width + lora], 1, 0)[SUBLANES:]
    cur = proj[SUBLANES:]
    u_o[...] = cur[:, :width]

    l1 = cur[:, 4 * width + lora:] + shifted[:, 3 * width:]
    lane = lax.broadcasted_iota(jnp.int32, l1.shape, 1)
    hidden = jnp.where(lane < DECAY_LORA, jnp.tanh(l1),
                       jnp.where(lane < DECAY_LORA + ICLR_LORA, l1, _sigmoid(l1))).astype(BF16)
    z3 = _dot(hidden, w2_ref[...])

    w0 = vec_ref[0:1, :]
    a0 = vec_ref[1:2, :]
    k_k = vec_ref[2:3, :]
    k_a = vec_ref[3:4, :]
    r_k = vec_ref[4:5, :]
    nz = -(w0 + z3[:, :width])
    softplus = jnp.maximum(nz, 0.0) + jnp.log(1.0 + jnp.exp(-jnp.abs(nz)))
    lw_o[...] = -jnp.exp(-softplus - 0.5)
    a = _sigmoid(a0 + z3[:, width:2 * width])
    g_o[...] = z3[:, 2 * width:]

    r_cur = cur[:, width:2 * width]
    k_cur = cur[:, 2 * width:3 * width]
    v_cur = cur[:, 3 * width:4 * width]
    r = r_cur + (shifted[:, :width] - r_cur) * murkv_ref[0:1, :]
    k = k_cur + (shifted[:, width:2 * width] - k_cur) * murkv_ref[1:2, :]
    v = v_cur + (shifted[:, 2 * width:3 * width] - v_cur) * murkv_ref[2:3, :]

    ones = ones_ref[...]
    kk = k * k_k
    norm = jnp.sqrt(_dot((kk * kk).astype(BF16), ones))
    kkn = kk / jnp.maximum(norm, 1e-12)
    k2 = k * (1.0 + (a - 1.0) * k_a)
    r_o[...] = r
    k_o[...] = k2
    v_o[...] = v
    an_o[...] = -kkn
    bb_o[...] = kkn * a
    bonus_o[...] = _dot((r * k2 * r_k).astype(BF16), ones) * v


def _mix_in(x2d, seq, gain, w_ext, mu_rkv, vecs, w2_cat, head_ones):
    n, d = x2d.shape
    width = w2_cat.shape[1] // 3
    tm = TOKEN_TILE
    prev_blocks = tm // SUBLANES
    row = lambda i: (i, 0)
    out = jax.ShapeDtypeStruct((n, width), F32)
    return pl.pallas_call(
        functools.partial(_mix_in_kernel, seq),
        grid=(n // tm,),
        in_specs=[
            pl.BlockSpec((tm, d), row),
            pl.BlockSpec((SUBLANES, d), lambda i: (jnp.maximum(i * prev_blocks - 1, 0), 0)),
            _full(gain.shape), _full(w_ext.shape), _full(mu_rkv.shape), _full(vecs.shape),
            _full(w2_cat.shape), _full(head_ones.shape),
        ],
        out_specs=[pl.BlockSpec((tm, width), row)] * 9,
        out_shape=[out] * 9,
        compiler_params=_params("parallel"),
        name="mix_in",
    )(x2d, x2d, gain, w_ext, mu_rkv, vecs, w2_cat, head_ones)


def _mix_in_weights(w_in, w1, a1, g1, mu_wag, w2, a2, g2):
    l1 = jnp.concatenate([w1, a1, g1], axis=1).astype(F32)
    mu = jnp.concatenate([jnp.broadcast_to(mu_wag[j][:, None], (w.shape[0], w.shape[1]))
                          for j, w in enumerate((w1, a1, g1))], axis=1).astype(F32)
    w_ext = jnp.concatenate([w_in.astype(F32), mu * l1, (1.0 - mu) * l1], axis=1).astype(BF16)
    width = w2.shape[1]
    z = lambda rows: jnp.zeros((rows, width), F32)
    w2_cat = jnp.concatenate([
        jnp.concatenate([w2.astype(F32), z(w2.shape[0]), z(w2.shape[0])], axis=1),
        jnp.concatenate([z(a2.shape[0]), a2.astype(F32), z(a2.shape[0])], axis=1),
        jnp.concatenate([z(g2.shape[0]), z(g2.shape[0]), g2.astype(F32)], axis=1)], axis=0).astype(BF16)
    return w_ext, w2_cat


def _s5_tables(lam_re, lam_im, log_dt, b_re, b_im, c_re, c_im, n_chunks):
    t = S5_CHUNK
    hi = lax.Precision.HIGHEST
    lre = jnp.minimum(lam_re.astype(F32), LAMBDA_RE_MAX)
    lim = lam_im.astype(F32)
    dt = jnp.exp(log_dt.astype(F32))[:, None]
    mag = jnp.exp(lre * dt)
    lb_re = mag * jnp.cos(lim * dt)
    lb_im = mag * jnp.sin(lim * dt)
    den = lre * lre + lim * lim
    z_re = lb_re - 1.0
    coef_re = ((z_re * lre + lb_im * lim) / den)[..., None]
    coef_im = ((lb_im * lre - z_re * lim) / den)[..., None]
    b_re = b_re.astype(F32)
    b_im = b_im.astype(F32)
    bb_re = coef_re * b_re - coef_im * b_im
    bb_im = coef_re * b_im + coef_im * b_re
    c_re = c_re.astype(F32)
    c_im = c_im.astype(F32)

    def power(e):
        e = e.astype(F32)[:, None, None]
        m = jnp.exp(e * (lre * dt))
        ang = e * (lim * dt)
        return m * jnp.cos(ang), m * jnp.sin(ang)

    p_re, p_im = power(jnp.arange(t + 1))
    ab_re = p_re[:t, :, :, None] * bb_re - p_im[:t, :, :, None] * bb_im
    ab_im = p_re[:t, :, :, None] * bb_im + p_im[:t, :, :, None] * bb_re
    kern = (jnp.einsum('gcp,tgpd->gtcd', c_re, ab_re, precision=hi)
            - jnp.einsum('gcp,tgpd->gtcd', c_im, ab_im, precision=hi))
    g = kern.shape[0]
    c = S5_GROUP
    kvec = kern.transpose(0, 3, 2, 1).reshape(g, c, c // 2, 2 * t)
    st_re = ab_re[::-1].transpose(1, 3, 0, 2).reshape(g, c * t, S5_STATE)
    st_im = ab_im[::-1].transpose(1, 3, 0, 2).reshape(g, c * t, S5_STATE)
    state_map = jnp.concatenate([st_re, st_im], axis=2).astype(BF16)
    q_re, q_im = p_re[1:], p_im[1:]
    ca_re = c_re[:, :, None, :] * q_re.transpose(1, 0, 2)[:, None] - c_im[:, :, None, :] * q_im.transpose(1, 0, 2)[:, None]
    ca_im = c_re[:, :, None, :] * q_im.transpose(1, 0, 2)[:, None] + c_im[:, :, None, :] * q_re.transpose(1, 0, 2)[:, None]
    cross = jnp.concatenate([ca_re.transpose(0, 3, 1, 2).reshape(g, S5_STATE, c * t),
                             -ca_im.transpose(0, 3, 1, 2).reshape(g, S5_STATE, c * t)], axis=1).astype(BF16)
    n_steps = max(1, (n_chunks - 1).bit_length())
    s_re, s_im = power(t * (2 ** jnp.arange(n_steps)))
    tab = jnp.stack([jnp.concatenate([s_re, s_re], axis=-1),
                     jnp.concatenate([-s_im, s_im], axis=-1)], axis=1)
    tab = tab.transpose(2, 0, 1, 3).reshape(g, 2 * n_steps, 2 * S5_STATE)
    return kvec, state_map, cross, tab, n_steps


def _s5_conv_kernel(n_chunks, n_steps, u_ref, kvec_ref, st_ref, cross_ref, tab_ref, y_ref, conv):
    t = S5_CHUNK
    row = lax.broadcasted_iota(jnp.int32, (t, LANES), 0)
    lane = lax.broadcasted_iota(jnp.int32, (t, LANES), 1)
    causal = jnp.bitwise_and(lane, t - 1) >= row
    for cin in range(S5_GROUP):
        for cp in range(S5_GROUP // 2):
            base = jnp.broadcast_to(kvec_ref[0, cin, cp:cp + 1, :], (t, LANES))
            blk = pltpu.roll(base, 0, 1, stride=1, stride_axis=0)
            conv[cin * t:(cin + 1) * t, cp * LANES:(cp + 1) * LANES] = jnp.where(causal, blk, 0.0).astype(BF16)
    u = u_ref[0]
    y = _dot(u, conv[...])
    x = _dot(u, st_ref[0])
    chunk = lax.broadcasted_iota(jnp.int32, x.shape, 0) % n_chunks
    half = x.shape[1] // 2
    for j in range(n_steps):
        sh = 1 << j
        xs = jnp.where(chunk >= sh, pltpu.roll(x, sh, 0), 0.0)
        x = x + tab_ref[0, 2 * j:2 * j + 1, :] * xs + tab_ref[0, 2 * j + 1:2 * j + 2, :] * pltpu.roll(xs, half, 1)
    x_in = jnp.where(chunk >= 1, pltpu.roll(x, 1, 0), 0.0)
    y_ref[0] = y + _dot(x_in.astype(BF16), cross_ref[0])


def _s5_conv(u2d, batch, seq, tables):
    kvec, state_map, cross, tab, n_steps = tables
    n, width = u2d.shape
    g = width // S5_GROUP
    t = S5_CHUNK
    n_chunks = seq // t
    ct = S5_GROUP * t
    ut = u2d.astype(BF16).reshape(batch, n_chunks, t, g, S5_GROUP).transpose(3, 0, 1, 4, 2)
    ut = ut.reshape(g, batch * n_chunks, ct)
    rows = batch * n_chunks
    y = pl.pallas_call(
        functools.partial(_s5_conv_kernel, n_chunks, n_steps),
        grid=(g,),
        in_specs=[
            pl.BlockSpec((1, rows, ct), lambda i: (i, 0, 0)),
            pl.BlockSpec((1,) + kvec.shape[1:], lambda i: (i, 0, 0, 0)),
            pl.BlockSpec((1,) + state_map.shape[1:], lambda i: (i, 0, 0)),
            pl.BlockSpec((1,) + cross.shape[1:], lambda i: (i, 0, 0)),
            pl.BlockSpec((1,) + tab.shape[1:], lambda i: (i, 0, 0)),
        ],
        out_specs=pl.BlockSpec((1, rows, ct), lambda i: (i, 0, 0)),
        out_shape=jax.ShapeDtypeStruct((g, rows, ct), F32),
        scratch_shapes=[pltpu.VMEM((ct, ct), BF16)],
        compiler_params=_params("parallel"),
        name="s5_conv",
    )(ut, kvec, state_map, cross, tab)
    return y.reshape(g, batch, n_chunks, S5_GROUP, t).transpose(1, 2, 4, 0, 3).reshape(n, width)


def _pair_blockdiag(y, left):
    return jnp.concatenate([jnp.where(left, y, 0.0), jnp.where(left, 0.0, y)], axis=0).astype(BF16)


def _rwkv_chunk_kernel(r_ref, lw_ref, k_ref, v_ref, an_ref, bb_ref, rhat_o, yhat_o, g_o, h_o):
    t = RWKV_CHUNK
    rows, width = r_ref.shape
    n_chunks = rows // t
    pairs = width // LANES
    row = lax.broadcasted_iota(jnp.int32, (t, LANES), 0)
    lane = lax.broadcasted_iota(jnp.int32, (t, LANES), 1)
    col = jnp.bitwise_and(lane, RWKV_HEAD - 1)
    left = lane < RWKV_HEAD
    incl = row >= col
    strict = row > col
    same_blk = (row // RWKV_SUB) == (col // RWKV_SUB)
    eye = jnp.where(row == col, 1.0, 0.0).astype(F32)
    brow = lax.broadcasted_iota(jnp.int32, (LANES, LANES), 0)
    bcol = lax.broadcasted_iota(jnp.int32, (LANES, LANES), 1)
    same_head = (brow // RWKV_HEAD) == (bcol // RWKV_HEAD)
    eye_full = brow == bcol
    crow = lax.broadcasted_iota(jnp.int32, (rows, rows), 0)
    ccol = lax.broadcasted_iota(jnp.int32, (rows, rows), 1)
    tril = jnp.where(crow >= ccol, jnp.where((crow // t) == (ccol // t), 1.0, 0.0), 0.0).astype(BF16)

    lw = lw_ref[...]
    p1 = lw.astype(BF16)
    rem = lw - p1.astype(F32)
    p2 = rem.astype(BF16)
    p3 = (rem - p2.astype(F32)).astype(BF16)
    cs = _dot(tril, p1) + _dot(tril, p2) + _dot(tril, p3)
    a_t = an_ref[...] * jnp.exp(cs - lw)
    r_t = r_ref[...] * jnp.exp(cs)
    p_inv = jnp.exp(-cs)
    b_t = bb_ref[...] * p_inv
    k_t = k_ref[...] * p_inv

    def bd(y):
        return _pair_blockdiag(y, left)

    def pmm(x, y_bd):
        return _dot(x.astype(BF16), y_bd)

    units = [(c, j) for c in range(n_chunks) for j in range(pairs)]

    def tile(arr, c, j):
        return arr[c * t:(c + 1) * t, j * LANES:(j + 1) * LANES]

    a2 = [tile(a_t, c, j) for c, j in units]
    r2 = [tile(r_t, c, j) for c, j in units]
    v2 = [tile(v_ref[...], c, j) for c, j in units]
    ar = [jnp.concatenate([a, r], axis=0).astype(BF16) for a, r in zip(a2, r2)]
    prod = [_dot_nt(x, jnp.concatenate([bd(tile(b_t, c, j)), bd(tile(k_t, c, j))], axis=0))
            for x, (c, j) in zip(ar, units)]
    l_ab = [jnp.where(strict, p[:t, :LANES], 0.0) for p in prod]
    l_rb = [jnp.where(incl, p[t:, :LANES], 0.0) for p in prod]
    l_ak = [jnp.where(strict, p[:t, LANES:], 0.0) for p in prod]
    l_rk = [jnp.where(incl, p[t:, LANES:], 0.0) for p in prod]
    l_d = [jnp.where(same_blk, m, 0.0) for m in l_ab]
    l_o = [m - d for m, d in zip(l_ab, l_d)]
    l2 = [pmm(m, bd(m)) for m in l_d]
    q1 = [pmm(eye + m, bd(eye + s)) for m, s in zip(l_d, l2)]
    l4 = [pmm(m, bd(m)) for m in l2]
    q2 = [pmm(q, bd(eye + s)) for q, s in zip(q1, l4)]
    l8 = [pmm(m, bd(m)) for m in l4]
    d_inv = [pmm(q, bd(eye + s)) for q, s in zip(q2, l8)]
    akv = [pmm(jnp.concatenate([m, n], axis=0), bd(v)) for m, n, v in zip(l_ak, l_rk, v2)]
    x1 = [z[:t] for z in akv]

    def bd2(z1, z2):
        return jnp.concatenate([bd(z1), bd(z2)], axis=1)

    dz = [pmm(d, bd2(a, x)) for d, a, x in zip(d_inv, a2, x1)]
    n1 = [pmm(d, bd(o)) for d, o in zip(d_inv, l_o)]
    n2 = [pmm(m, bd(m)) for m in n1]
    t1 = [z + pmm(m, bd2(z[:, :LANES], z[:, LANES:])) for z, m in zip(dz, n2)]
    wu = [z + pmm(m, bd2(z[:, :LANES], z[:, LANES:])) for z, m in zip(t1, n1)]
    ry = [pmm(m, bd2(z[:, :LANES], z[:, LANES:])) for m, z in zip(l_rb, wu)]
    rk_v = [z[t:] for z in akv]

    for i, (c, j) in enumerate(units):
        rs = slice(c * t, (c + 1) * t)
        ls = slice(j * LANES, (j + 1) * LANES)
        rhat_o[rs, ls] = r2[i] + ry[i][:, :LANES]
        yhat_o[rs, ls] = ry[i][:, LANES:] + rk_v[i]
        cs_c = cs[rs, ls]
        cs_end = cs_c[t - 1:t, :]
        p_end = jnp.exp(cs_end - cs_c)
        b_h = (bb_ref[rs, ls] * p_end).astype(BF16)
        k_h = (k_ref[rs, ls] * p_end).astype(BF16)
        g_full = _dot_tn(wu[i][:, :LANES].astype(BF16), b_h)
        g_bd = jnp.where(same_head, g_full, 0.0) + jnp.where(eye_full, jnp.exp(cs_end), 0.0)
        g_o[c, j] = g_bd.astype(BF16)
        uv = jnp.concatenate([wu[i][:, LANES:], v2[i]], axis=0).astype(BF16)
        h_full = _dot_tn(uv, jnp.concatenate([b_h, k_h], axis=0))
        h_o[c, j] = jnp.where(left, h_full[:RWKV_HEAD], h_full[RWKV_HEAD:])


RWKV_CHUNKS_PER_STEP = 4


def _rwkv_chunks(r, lw, k, v, an, bb):
    n, width = r.shape
    t = RWKV_CHUNK
    cps = RWKV_CHUNKS_PER_STEP
    pairs = width // LANES
    nck = n // t
    tok = pl.BlockSpec((cps * t, width), lambda i: (i, 0))
    return pl.pallas_call(
        _rwkv_chunk_kernel,
        grid=(nck // cps,),
        in_specs=[tok] * 6,
        out_specs=[tok, tok,
                   pl.BlockSpec((cps, pairs, LANES, LANES), lambda i: (i, 0, 0, 0)),
                   pl.BlockSpec((cps, pairs, RWKV_HEAD, LANES), lambda i: (i, 0, 0, 0))],
        out_shape=[jax.ShapeDtypeStruct((n, width), F32)] * 2
        + [jax.ShapeDtypeStruct((nck, pairs, LANES, LANES), BF16),
           jax.ShapeDtypeStruct((nck, pairs, RWKV_HEAD, LANES), F32)],
        compiler_params=_params("parallel"),
        name="rwkv_chunk",
    )(r, lw, k, v, an, bb)


def _rwkv_scan_kernel(rhat_ref, yhat_ref, g_ref, h_ref, gate_ref, bonus_ref, lnw_ref, lnb_ref,
                      ones_ref, y_ref, state):
    t = RWKV_CHUNK
    batch, pairs = state.shape[0], state.shape[1]
    n_chunks = rhat_ref.shape[1] // t
    lane = lax.broadcasted_iota(jnp.int32, (RWKV_HEAD, LANES), 1)
    left = lane < RWKV_HEAD

    @pl.when(pl.program_id(0) == 0)
    def _():
        state[...] = jnp.zeros_like(state)

    s = [[state[b, j] for j in range(pairs)] for b in range(batch)]
    y_rows = []
    for b in range(batch):
        chunk_rows = []
        for c in range(n_chunks):
            rs = slice(c * t, (c + 1) * t)
            tiles = []
            for j in range(pairs):
                ls = slice(j * LANES, (j + 1) * LANES)
                s_bd = _pair_blockdiag(s[b][j], left)
                tiles.append(yhat_ref[b, rs, ls] + _dot_nt(rhat_ref[b, rs, ls].astype(BF16), s_bd))
                s[b][j] = _dot(s[b][j].astype(BF16), g_ref[b, c, j]) + h_ref[b, c, j]
            chunk_rows.append(jnp.concatenate(tiles, axis=1))
        y_rows.append(jnp.concatenate(chunk_rows, axis=0))
    for b in range(batch):
        for j in range(pairs):
            state[b, j] = s[b][j]

    ones = ones_ref[...]
    inv = 1.0 / RWKV_HEAD
    for b in range(batch):
        y = y_rows[b]
        cen = y - _dot(y.astype(BF16), ones) * inv
        var = _dot((cen * cen).astype(BF16), ones) * inv
        yn = cen * lax.rsqrt(var + RWKV_GN_EPS) * lnw_ref[...] + lnb_ref[...]
        y_ref[b] = (yn + bonus_ref[b]) * gate_ref[b]


def _rwkv_scan(rhat, yhat, g_mat, h_mat, gate, bonus, ln_w, ln_b, head_ones, batch, seq):
    n, width = rhat.shape
    t = RWKV_CHUNK
    cps = RWKV_CHUNKS_PER_STEP
    pairs = width // LANES
    nc = seq // t
    tok3 = lambda a: a.reshape(batch, seq, width)
    tok = pl.BlockSpec((batch, cps * t, width), lambda i: (0, i, 0))
    out = pl.pallas_call(
        _rwkv_scan_kernel,
        grid=(nc // cps,),
        in_specs=[tok, tok,
                  pl.BlockSpec((batch, cps, pairs, LANES, LANES), lambda i: (0, i, 0, 0, 0)),
                  pl.BlockSpec((batch, cps, pairs, RWKV_HEAD, LANES), lambda i: (0, i, 0, 0, 0)),
                  tok, tok, _full(ln_w.shape), _full(ln_b.shape), _full(head_ones.shape)],
        out_specs=tok,
        out_shape=jax.ShapeDtypeStruct((batch, seq, width), F32),
        scratch_shapes=[pltpu.VMEM((batch, pairs, RWKV_HEAD, LANES), F32)],
        compiler_params=_params("arbitrary"),
        name="rwkv_scan",
    )(tok3(rhat), tok3(yhat), g_mat.reshape(batch, nc, pairs, LANES, LANES),
      h_mat.reshape(batch, nc, pairs, RWKV_HEAD, LANES), tok3(gate), tok3(bonus), ln_w, ln_b, head_ones)
    return out.reshape(n, width)


def _post_mix_kernel(n_experts, x_ref, yc_ref, u_ref, yr_ref, d_ref, wglu_ref, wout_ref, gain_ref,
                     rw_ref, rb_ref, x1_o, hm_o, gate_o, lpos_o, meta_o, tot_o, running):
    tm = x_ref.shape[0]
    width = yc_ref.shape[1]

    @pl.when(pl.program_id(0) == 0)
    def _():
        running[...] = jnp.zeros_like(running)

    y = yc_ref[...] + d_ref[...] * u_ref[...]
    y = 0.5 * y * (1.0 + jnp.tanh(0.7978845608028654 * (y + 0.044715 * (y * y * y))))
    y = y * _sigmoid(_dot(y.astype(BF16), wglu_ref[...]))
    x1 = (x_ref[...] + _dot(y.astype(BF16), wout_ref[:width, :])
          + _dot(yr_ref[...].astype(BF16), wout_ref[width:, :]))
    x1_o[...] = x1
    hm = _rms(x1, gain_ref[...])
    hm_o[...] = hm.astype(BF16)
    logits = _dot(hm.astype(BF16), rw_ref[...]) + rb_ref[...]

    lane = lax.broadcasted_iota(jnp.int32, logits.shape, 1)
    lanef = lane.astype(F32)
    sel = jnp.zeros(logits.shape, F32)
    idx_cols, val_cols = [], []
    work = logits
    for _ in range(TOP_K):
        m = jnp.max(work, axis=-1, keepdims=True)
        pick = jnp.min(jnp.where(work == m, lanef, float(LANES)), axis=-1, keepdims=True)
        hit = lanef == pick
        sel = jnp.where(hit, 1.0, sel)
        work = jnp.where(hit, -jnp.inf, work)
        idx_cols.append(pick)
        val_cols.append(m)
    exps = [jnp.exp(vv - val_cols[0]) for vv in val_cols]
    denom = exps[0] + exps[1] + exps[2] + exps[3]

    row = lax.broadcasted_iota(jnp.int32, (tm, tm), 0)
    col = lax.broadcasted_iota(jnp.int32, (tm, tm), 1)
    before = jnp.where(row > col, 1.0, 0.0).astype(BF16)
    local = _dot(before, sel.astype(BF16))
    cnt = jnp.sum(sel, axis=0, keepdims=True)
    cnt_al = jnp.floor((cnt + (RUN_ALIGN - 1)) * (1.0 / RUN_ALIGN)) * RUN_ALIGN
    erow = lax.broadcasted_iota(jnp.int32, (LANES, LANES), 0)
    ecol = lax.broadcasted_iota(jnp.int32, (LANES, LANES), 1)
    upper = jnp.where(erow < ecol, 1.0, 0.0).astype(BF16)
    toff = _dot(jnp.broadcast_to(cnt_al, (SUBLANES, LANES)).astype(BF16), upper)[0:1]
    tbase = running[...]
    gate_out = jnp.zeros(logits.shape, F32)
    lpos_out = jnp.zeros(logits.shape, F32)
    for j in range(TOP_K):
        lp = jnp.sum(jnp.where(lanef == idx_cols[j], local + toff, 0.0), axis=-1, keepdims=True)
        gate_out = jnp.where(lane == j, exps[j] / denom, gate_out)
        lpos_out = jnp.where(lane == j, lp, lpos_out)
    gate_o[...] = gate_out
    lpos_o[...] = lpos_out.astype(jnp.int32)
    srow = lax.broadcasted_iota(jnp.int32, (SUBLANES, LANES), 0)
    meta = jnp.where(srow == 0, cnt, jnp.where(srow == 1, tbase, jnp.where(srow == 2, toff, 0.0)))
    meta_o[...] = meta.astype(jnp.int32)
    running[...] = tbase + cnt_al
    tot_o[...] = jnp.broadcast_to(running[...], tot_o.shape).astype(jnp.int32)
    del n_experts


def _post_mix(x2d, yconv, u, y_rwkv, s5_d, w_glu, w_out, gain, rw, rb, n_experts):
    n, d = x2d.shape
    width = yconv.shape[1]
    tm = MOE_TILE
    row = lambda i: (i, 0)
    tok_d = pl.BlockSpec((tm, d), row)
    tok_w = pl.BlockSpec((tm, width), row)
    tok_l = pl.BlockSpec((tm, LANES), row)
    return pl.pallas_call(
        functools.partial(_post_mix_kernel, n_experts),
        grid=(n // tm,),
        in_specs=[tok_d, tok_w, tok_w, tok_w, _full(s5_d.shape), _full(w_glu.shape),
                  _full(w_out.shape), _full(gain.shape), _full(rw.shape), _full(rb.shape)],
        out_specs=[tok_d, tok_d, tok_l, tok_l,
                   pl.BlockSpec((SUBLANES, LANES), row), _full((SUBLANES, LANES))],
        out_shape=[jax.ShapeDtypeStruct((n, d), F32), jax.ShapeDtypeStruct((n, d), BF16),
                   jax.ShapeDtypeStruct((n, LANES), F32), jax.ShapeDtypeStruct((n, LANES), jnp.int32),
                   jax.ShapeDtypeStruct((n // tm * SUBLANES, LANES), jnp.int32),
                   jax.ShapeDtypeStruct((SUBLANES, LANES), jnp.int32)],
        scratch_shapes=[pltpu.VMEM((1, LANES), F32)],
        compiler_params=_params("arbitrary"),
        name="post_mix",
    )(x2d, yconv, u, y_rwkv, s5_d, w_glu, w_out, gain, rw, rb)


def _sorted_rows(tm, n_experts):
    return -(-(tm * TOP_K + n_experts * (RUN_ALIGN - 1)) // LANES) * LANES


RUN_CHUNKS = (8, 4, 2, 1)


def _start_runs(n_experts, pstart_ref, meta_ref, make_copy):
    big = RUN_CHUNKS[0]
    for e in range(n_experts):
        groups = (meta_ref[0, e] + (RUN_ALIGN - 1)) // RUN_ALIGN
        seg = pstart_ref[e] + meta_ref[1, e]
        loc = meta_ref[2, e]
        n_big = groups // big

        def body(g, c, seg=seg, loc=loc):
            off = g * (big * RUN_ALIGN)
            make_copy(pl.multiple_of(loc + off, RUN_ALIGN), pl.multiple_of(seg + off, RUN_ALIGN),
                      big * RUN_ALIGN).start(priority=e % 2)
            return c

        lax.fori_loop(0, n_big, body, 0)
        done = n_big * big
        for size in RUN_CHUNKS[1:]:
            take = ((groups - done) // size) > 0

            @pl.when(take)
            def _(done=done, size=size, seg=seg, loc=loc):
                off = done * RUN_ALIGN
                make_copy(pl.multiple_of(loc + off, RUN_ALIGN), pl.multiple_of(seg + off, RUN_ALIGN),
                          size * RUN_ALIGN).start(priority=e % 2)

            done = done + jnp.where(take, size, 0)


def _wait_runs(n_experts, meta_ref, make_copy):
    groups = 0
    for e in range(n_experts):
        groups = groups + (meta_ref[0, e] + (RUN_ALIGN - 1)) // RUN_ALIGN

    def body(g, c):
        make_copy(0, 0, RUN_ALIGN).wait()
        return c

    lax.fori_loop(0, groups, body, 0)


def _dispatch_kernel(n_experts, pstart_ref, tot_ref, meta_ref, meta_prev_ref, lpos_ref, hm_ref, xs_ref,
                     srt, zero, sem):
    tm = hm_ref.shape[0]
    rows = srt.shape[1]
    i = pl.program_id(0)
    cur = i % 2
    pos_t = jnp.transpose(lpos_ref[...].astype(F32))
    rid = lax.broadcasted_iota(jnp.int32, (rows, tm), 0).astype(F32)
    perm = jnp.zeros((rows, tm), F32)
    for j in range(TOP_K):
        perm = perm + jnp.where(rid == pos_t[j:j + 1, :], 1.0, 0.0)
    srt[cur] = _dot(perm.astype(BF16), hm_ref[...])

    def copies(slot):
        def make_copy(loc, dst, size):
            return pltpu.make_async_copy(srt.at[slot, pl.ds(loc, size)], xs_ref.at[pl.ds(dst, size)],
                                         sem.at[slot])
        return make_copy

    _start_runs(n_experts, pstart_ref, meta_ref, copies(cur))

    @pl.when(i > 0)
    def _():
        _wait_runs(n_experts, meta_prev_ref, copies(1 - cur))

    @pl.when(i == pl.num_programs(0) - 1)
    def _():
        _wait_runs(n_experts, meta_ref, copies(cur))

    @pl.when(pl.program_id(0) == pl.num_programs(0) - 1)
    def _():
        zero[...] = jnp.zeros_like(zero)
        for wait in (False, True):
            for e in range(n_experts):
                used = tot_ref[e]
                start = pstart_ref[e] + used
                groups = (-used % EXPERT_TILE) // RUN_ALIGN

                def body(g, c, start=start):
                    cp = pltpu.make_async_copy(
                        zero, xs_ref.at[pl.ds(pl.multiple_of(start + g * RUN_ALIGN, RUN_ALIGN), RUN_ALIGN)],
                        sem.at[0])
                    if wait:
                        cp.wait()
                    else:
                        cp.start()
                    return c

                lax.fori_loop(0, groups, body, 0)

        last = n_experts - 1
        end = pstart_ref[last] + tot_ref[last] + (-tot_ref[last] % EXPERT_TILE)
        srt[0, 0:EXPERT_TILE, :] = jnp.zeros((EXPERT_TILE, srt.shape[2]), srt.dtype)
        for wait in (False, True):
            def tail(b, c):
                cp = pltpu.make_async_copy(
                    srt.at[0, pl.ds(0, EXPERT_TILE)],
                    xs_ref.at[pl.ds(pl.multiple_of(end + b * EXPERT_TILE, EXPERT_TILE), EXPERT_TILE)], sem.at[0])
                if wait:
                    cp.wait()
                else:
                    cp.start()
                return c

            lax.fori_loop(0, (xs_ref.shape[0] - end) // EXPERT_TILE, tail, 0)


def _dispatch(hm, lpos, meta, pstarts, totals, m_pad, n_experts):
    n, dh = hm.shape
    tm = MOE_TILE
    rows = _sorted_rows(tm, n_experts)
    grid_spec = pltpu.PrefetchScalarGridSpec(
        num_scalar_prefetch=2,
        grid=(n // tm,),
        in_specs=[pl.BlockSpec((SUBLANES, LANES), lambda i, ps, tt: (i, 0), memory_space=pltpu.SMEM),
                  pl.BlockSpec((SUBLANES, LANES), lambda i, ps, tt: (jnp.maximum(i - 1, 0), 0),
                               memory_space=pltpu.SMEM),
                  pl.BlockSpec((tm, LANES), lambda i, ps, tt: (i, 0)),
                  pl.BlockSpec((tm, dh), lambda i, ps, tt: (i, 0))],
        out_specs=pl.BlockSpec(memory_space=pl.ANY),
        scratch_shapes=[pltpu.VMEM((2, rows, dh), F32), pltpu.VMEM((RUN_ALIGN, dh), F32),
                        pltpu.SemaphoreType.DMA((2,))],
    )
    return pl.pallas_call(
        functools.partial(_dispatch_kernel, n_experts),
        grid_spec=grid_spec,
        out_shape=jax.ShapeDtypeStruct((m_pad, dh), F32),
        compiler_params=_params("arbitrary"),
        name="dispatch",
    )(pstarts, totals, meta, meta, lpos, hm)


def _experts_kernel(be_ref, nu_ref, xs_ref, wg_ref, bg_ref, wu_ref, bu_ref, wd_ref, bd_ref, ys_ref,
                    wg_s, wu_s, wd_s):
    i = pl.program_id(0)
    changed = jnp.logical_or(i == 0, be_ref[i] != be_ref[jnp.maximum(i - 1, 0)])

    @pl.when(changed)
    def _():
        wg_s[...] = wg_ref[0].astype(BF16)
        wu_s[...] = wu_ref[0].astype(BF16)
        wd_s[...] = wd_ref[0].astype(BF16)

    @pl.when(i < nu_ref[0])
    def _():
        xb = xs_ref[...].astype(BF16)
        gt = jnp.minimum(_dot(xb, wg_s[...]) + bg_ref[0], SWIGLU_LIMIT)
        up = jnp.clip(_dot(xb, wu_s[...]) + bu_ref[0], -SWIGLU_LIMIT, SWIGLU_LIMIT)
        act = (up + 1.0) * gt * _sigmoid(SWIGLU_ALPHA * gt)
        ys_ref[...] = _dot(act.astype(BF16), wd_s[...]) + bd_ref[0]

    @pl.when(i >= nu_ref[0])
    def _():
        ys_ref[...] = jnp.zeros_like(ys_ref)


def _experts(xs, block_e, n_used, wg, bg, wu, bu, wd, bd):
    m_pad, dh = xs.shape
    tmb = EXPERT_TILE
    d, dff = wg.shape[1], wg.shape[2]
    wmap = lambda i, be, nu: (be[i], 0, 0)
    grid_spec = pltpu.PrefetchScalarGridSpec(
        num_scalar_prefetch=2,
        grid=(m_pad // tmb,),
        in_specs=[pl.BlockSpec((tmb, dh), lambda i, be, nu: (jnp.where(i < nu[0], i, 0), 0)),
                  pl.BlockSpec((1, d, dff), wmap), pl.BlockSpec((1, 1, dff), wmap),
                  pl.BlockSpec((1, d, dff), wmap), pl.BlockSpec((1, 1, dff), wmap),
                  pl.BlockSpec((1, dff, d), wmap), pl.BlockSpec((1, 1, d), wmap)],
        out_specs=pl.BlockSpec((tmb, dh), lambda i, be, nu: (i, 0)),
        scratch_shapes=[pltpu.VMEM((d, dff), BF16), pltpu.VMEM((d, dff), BF16), pltpu.VMEM((dff, d), BF16)],
    )
    return pl.pallas_call(
        _experts_kernel,
        grid_spec=grid_spec,
        out_shape=jax.ShapeDtypeStruct((m_pad, dh), F32),
        compiler_params=_params("arbitrary"),
        name="experts",
    )(block_e, n_used, xs, wg, bg, wu, bu, wd, bd)


def _final_kernel(n_experts, pstart_ref, meta_ref, meta_next_ref, lpos_ref, x1_ref, gate_ref, p_ref, ys_ref,
                  gple_ref, wproj_ref, wgate_ref, gfin_ref, out_ref, buf, sem):
    tm = x1_ref.shape[0]
    rows = buf.shape[1]
    i = pl.program_id(0)
    cur = i % 2

    def copies(slot):
        def make_copy(loc, src, size):
            return pltpu.make_async_copy(ys_ref.at[pl.ds(src, size)], buf.at[slot, pl.ds(loc, size)],
                                         sem.at[slot])
        return make_copy

    @pl.when(i == 0)
    def _():
        buf[...] = jnp.zeros_like(buf)
        _start_runs(n_experts, pstart_ref, meta_ref, copies(0))

    @pl.when(i + 1 < pl.num_programs(0))
    def _():
        _start_runs(n_experts, pstart_ref, meta_next_ref, copies(1 - cur))

    cid = lax.broadcasted_iota(jnp.int32, (tm, rows), 1)
    lpos = lpos_ref[...]
    gates = gate_ref[...]
    comb = jnp.zeros((tm, rows), F32)
    for j in range(TOP_K):
        comb = comb + jnp.where(cid == lpos[:, j:j + 1], gates[:, j:j + 1], 0.0)

    _wait_runs(n_experts, meta_ref, copies(cur))
    x2 = x1_ref[...] + _dot(comb.astype(BF16), buf[cur].astype(BF16))
    gate = _sigmoid(_dot(_rms(x2, gple_ref[...]).astype(BF16), wgate_ref[...]))
    x3 = x2 + _dot(p_ref[...].astype(BF16), wproj_ref[...]) * gate
    out_ref[...] = _rms(x3, gfin_ref[...])


def _final(x1, gates, lpos, meta, pstarts, p2d, ys, g_ple, w_proj, w_gate, g_fin, n_experts):
    n, d = x1.shape
    tm = MOE_TILE
    steps = n // tm
    rows = _sorted_rows(tm, n_experts)
    row = lambda i, ps: (i, 0)
    const = lambda shape: pl.BlockSpec(shape, lambda i, ps: (0,) * len(shape))
    grid_spec = pltpu.PrefetchScalarGridSpec(
        num_scalar_prefetch=1,
        grid=(steps,),
        in_specs=[pl.BlockSpec((SUBLANES, LANES), row, memory_space=pltpu.SMEM),
                  pl.BlockSpec((SUBLANES, LANES), lambda i, ps: (jnp.minimum(i + 1, steps - 1), 0),
                               memory_space=pltpu.SMEM),
                  pl.BlockSpec((tm, LANES), row), pl.BlockSpec((tm, d), row), pl.BlockSpec((tm, LANES), row),
                  pl.BlockSpec((tm, p2d.shape[1]), row), pl.BlockSpec(memory_space=pl.ANY),
                  const(g_ple.shape), const(w_proj.shape), const(w_gate.shape), const(g_fin.shape)],
        out_specs=pl.BlockSpec((tm, d), row),
        scratch_shapes=[pltpu.VMEM((2, rows, ys.shape[1]), ys.dtype), pltpu.SemaphoreType.DMA((2,))],
    )
    return pl.pallas_call(
        functools.partial(_final_kernel, n_experts),
        grid_spec=grid_spec,
        out_shape=jax.ShapeDtypeStruct((n, d), F32),
        compiler_params=_params("arbitrary"),
        name="final",
    )(pstarts, meta, meta, lpos, x1, gates, p2d, ys, g_ple, w_proj, w_gate, g_fin)


def _head_ones(width):
    hd = jnp.arange(width) // RWKV_HEAD
    return (hd[:, None] == hd[None, :]).astype(BF16)


def _layer(x2d, p2d, batch, seq, norm_mix_g, w_in, s5_lam_re, s5_lam_im, s5_log_dt, s5_b_re, s5_b_im,
           s5_c_re, s5_c_im, s5_d, s5_w_glu, mu_rkv, mu_wag, w0, w1, w2, a0, a1, a2, g1, g2, k_k, k_a,
           r_k, ln_w, ln_b, w_out, norm_moe_g, router_w, router_b, wg, bg, wu, bu, wd, bd,
           norm_ple_g, ple_w_proj, ple_w_gate):
    n, d = x2d.shape
    width = w_in.shape[1] // 4
    n_experts = router_w.shape[1]
    row2 = lambda t: t.reshape(1, -1).astype(F32)
    head_ones = _head_ones(width)
    vecs = jnp.zeros((SUBLANES, width), F32)
    vecs = vecs.at[0].set(w0).at[1].set(a0).at[2].set(k_k).at[3].set(k_a).at[4].set(r_k.reshape(-1))

    w_ext, w2_cat = _mix_in_weights(w_in, w1, a1, g1, mu_wag, w2, a2, g2)
    u, r, lw, k, v, an, bb, gate, bonus = _mix_in(
        x2d, seq, row2(norm_mix_g), w_ext, mu_rkv.astype(F32), vecs, w2_cat, head_ones)

    tables = _s5_tables(s5_lam_re, s5_lam_im, s5_log_dt, s5_b_re, s5_b_im, s5_c_re, s5_c_im,
                        seq // S5_CHUNK)
    yconv = _s5_conv(u, batch, seq, tables)

    rhat, yhat, g_mat, h_mat = _rwkv_chunks(r, lw, k, v, an, bb)
    y_rwkv = _rwkv_scan(rhat, yhat, g_mat, h_mat, gate, bonus, row2(ln_w), row2(ln_b), head_ones,
                        batch, seq)

    rw = jnp.zeros((d, LANES), BF16).at[:, :n_experts].set(router_w.astype(BF16))
    rb = jnp.full((1, LANES), NEG_BIG, F32).at[0, :n_experts].set(router_b.astype(F32))
    x1, hm, gates, lpos, meta, totals = _post_mix(
        x2d, yconv, u, y_rwkv, row2(s5_d), s5_w_glu.astype(BF16), w_out.astype(BF16),
        row2(norm_moe_g), rw, rb, n_experts)

    tmb = EXPERT_TILE
    n_tiles = n // MOE_TILE
    max_rows = n * TOP_K + n_tiles * n_experts * (RUN_ALIGN - 1)
    n_blocks = -(-max_rows // tmb) + n_experts
    seg = totals[0, :n_experts].astype(jnp.int32)
    padded = ((seg + tmb - 1) // tmb) * tmb
    pends = jnp.cumsum(padded)
    pstarts = (pends - padded).astype(jnp.int32)
    block_start = jnp.arange(n_blocks, dtype=jnp.int32) * tmb
    block_e = jnp.minimum(jnp.sum((pends[None, :] <= block_start[:, None]).astype(jnp.int32), axis=1),
                          n_experts - 1).astype(jnp.int32)
    n_used = (pends[-1] // tmb).astype(jnp.int32).reshape(1)

    xs = _dispatch(hm, lpos, meta, pstarts, seg, n_blocks * tmb, n_experts)
    ys = _experts(xs, block_e, n_used, wg, bg.reshape(n_experts, 1, -1).astype(F32),
                  wu, bu.reshape(n_experts, 1, -1).astype(F32), wd,
                  bd.reshape(n_experts, 1, -1).astype(F32))
    return (x1, gates, lpos, meta, pstarts, ys, p2d, row2(norm_ple_g), ple_w_proj.astype(BF16),
            ple_w_gate.astype(BF16), n_experts)


def kernel(x, p, norm_mix_g, w_in, s5_lam_re, s5_lam_im, s5_log_dt, s5_b_re, s5_b_im, s5_c_re, s5_c_im, s5_d, s5_w_glu, rwkv_mu_rkv, rwkv_mu_wag, rwkv_w0, rwkv_w1, rwkv_w2, rwkv_a0, rwkv_a1, rwkv_a2, rwkv_g1, rwkv_g2, rwkv_k_k, rwkv_k_a, rwkv_r_k, rwkv_ln_w, rwkv_ln_b, w_out, norm_moe_g, router_w, router_b, exp_w_gate, exp_b_gate, exp_w_up, exp_b_up, exp_w_down, exp_b_down, norm_ple_g, ple_w_proj, ple_w_gate, final_norm_g):
    batch, seq, d = x.shape
    assert w_in.shape[0] == 1, "the final kernel fuses the last RMSNorm: single-layer stacks only"
    i = 0
    x1, gates, lpos, meta, pstarts, ys, p2d, g_ple, w_proj, w_gate, n_experts = _layer(
        x.reshape(batch * seq, d), p[i].reshape(batch * seq, -1), batch, seq, norm_mix_g[i], w_in[i],
        s5_lam_re[i], s5_lam_im[i], s5_log_dt[i], s5_b_re[i], s5_b_im[i], s5_c_re[i], s5_c_im[i],
        s5_d[i], s5_w_glu[i], rwkv_mu_rkv[i], rwkv_mu_wag[i], rwkv_w0[i], rwkv_w1[i], rwkv_w2[i],
        rwkv_a0[i], rwkv_a1[i], rwkv_a2[i], rwkv_g1[i], rwkv_g2[i], rwkv_k_k[i], rwkv_k_a[i],
        rwkv_r_k[i], rwkv_ln_w[i], rwkv_ln_b[i], w_out[i], norm_moe_g[i], router_w[i],
        router_b[i], exp_w_gate[i], exp_b_gate[i], exp_w_up[i], exp_b_up[i], exp_w_down[i],
        exp_b_down[i], norm_ple_g[i], ple_w_proj[i], ple_w_gate[i])
    out = _final(x1, gates, lpos, meta, pstarts, p2d, ys, g_ple, w_proj, w_gate,
                 final_norm_g.reshape(1, -1).astype(F32), n_experts)
    return out.reshape(batch, seq, d)
```

```python
import functools

import jax
import jax.numpy as jnp
from jax import lax
from jax.experimental import pallas as pl
from jax.experimental.pallas import tpu as pltpu

F32 = jnp.float32
BF16 = jnp.bfloat16

S5_GROUP = 16
S5_STATE = 64
RWKV_HEAD = 64
DECAY_LORA = 64
ICLR_LORA = 64
TOP_K = 4
RMS_EPS = 1e-6
RWKV_GN_EPS = 64e-5
LAMBDA_RE_MAX = -1e-4
SWIGLU_LIMIT = 7.0
SWIGLU_ALPHA = 1.702

LANES = 128
SUBLANES = 8
VMEM_LIMIT = 56 * 1024 * 1024

S5_CHUNK = 64
RWKV_CHUNK = 64
RWKV_SUB = 16
TOKEN_TILE = 512
EXPERT_TILE = 512
MOE_TILE = 512
RUN_ALIGN = 8
NEG_BIG = -1e30


def _dot(a, b):
    return jnp.dot(a, b, preferred_element_type=F32)


def _dot_nt(a, b):
    return lax.dot_general(a, b, (((1,), (1,)), ((), ())), preferred_element_type=F32)


def _dot_tn(a, b):
    return lax.dot_general(a, b, (((0,), (0,)), ((), ())), preferred_element_type=F32)


def _rms(t, gain):
    return t * lax.rsqrt(jnp.mean(t * t, axis=-1, keepdims=True) + RMS_EPS) * gain


def _sigmoid(t):
    return 1.0 / (1.0 + jnp.exp(-t))


def _params(*sem):
    return pltpu.CompilerParams(dimension_semantics=sem, vmem_limit_bytes=VMEM_LIMIT)


def _full(shape):
    return pl.BlockSpec(shape, lambda *_: (0,) * len(shape))


def _mix_in_kernel(seq, x_ref, xp_ref, gain_ref, win_ref, murkv_ref, vec_ref, w2_ref, ones_ref,
                   u_o, r_o, lw_o, k_o, v_o, an_o, bb_o, g_o, bonus_o, ub_o):
    tm = x_ref.shape[0]
    width = u_o.shape[1]
    lora = w2_ref.shape[0]
    gain = gain_ref[...]
    h = _rms(x_ref[...], gain)
    keep = jnp.where((pl.program_id(0) * tm) % seq == 0, 0.0, 1.0)
    hp = _rms(xp_ref[...], gain) * keep
    h_ext = jnp.concatenate([hp, h], axis=0).astype(BF16)
    proj = _dot(h_ext, win_ref[...])
    shifted = pltpu.roll(proj[:, width:4 * width + lora], 1, 0)[SUBLANES:]
    cur = proj[SUBLANES:]
    u_o[...] = cur[:, :width]
    ub_o[...] = cur[:, :width].astype(BF16)

    l1 = cur[:, 4 * width + lora:] + shifted[:, 3 * width:]
    lane = lax.broadcasted_iota(jnp.int32, l1.shape, 1)
    hidden = jnp.where(lane < DECAY_LORA, jnp.tanh(l1),
                       jnp.where(lane < DECAY_LORA + ICLR_LORA, l1, _sigmoid(l1))).astype(BF16)
    z3 = _dot(hidden, w2_ref[...])

    w0 = vec_ref[0:1, :]
    a0 = vec_ref[1:2, :]
    k_k = vec_ref[2:3, :]
    k_a = vec_ref[3:4, :]
    r_k = vec_ref[4:5, :]
    nz = -(w0 + z3[:, :width])
    softplus = jnp.maximum(nz, 0.0) + jnp.log(1.0 + jnp.exp(-jnp.abs(nz)))
    lw_o[...] = -jnp.exp(-softplus - 0.5)
    a = _sigmoid(a0 + z3[:, width:2 * width])
    g_o[...] = z3[:, 2 * width:]

    r_cur = cur[:, width:2 * width]
    k_cur = cur[:, 2 * width:3 * width]
    v_cur = cur[:, 3 * width:4 * width]
    r = r_cur + (shifted[:, :width] - r_cur) * murkv_ref[0:1, :]
    k = k_cur + (shifted[:, width:2 * width] - k_cur) * murkv_ref[1:2, :]
    v = v_cur + (shifted[:, 2 * width:3 * width] - v_cur) * murkv_ref[2:3, :]

    ones = ones_ref[...]
    kk = k * k_k
    norm = jnp.sqrt(_dot((kk * kk).astype(BF16), ones))
    kkn = kk / jnp.maximum(norm, 1e-12)
    k2 = k * (1.0 + (a - 1.0) * k_a)
    r_o[...] = r
    k_o[...] = k2
    v_o[...] = v
    an_o[...] = -kkn
    bb_o[...] = kkn * a
    bonus_o[...] = _dot((r * k2 * r_k).astype(BF16), ones) * v


def _mix_in(x2d, seq, gain, w_ext, mu_rkv, vecs, w2_cat, head_ones):
    n, d = x2d.shape
    width = w2_cat.shape[1] // 3
    tm = TOKEN_TILE
    prev_blocks = tm // SUBLANES
    row = lambda i: (i, 0)
    out = jax.ShapeDtypeStruct((n, width), F32)
    return pl.pallas_call(
        functools.partial(_mix_in_kernel, seq),
        grid=(n // tm,),
        in_specs=[
            pl.BlockSpec((tm, d), row),
            pl.BlockSpec((SUBLANES, d), lambda i: (jnp.maximum(i * prev_blocks - 1, 0), 0)),
            _full(gain.shape), _full(w_ext.shape), _full(mu_rkv.shape), _full(vecs.shape),
            _full(w2_cat.shape), _full(head_ones.shape),
        ],
        out_specs=[pl.BlockSpec((tm, width), row)] * 10,
        out_shape=[out] * 9 + [jax.ShapeDtypeStruct((n, width), BF16)],
        compiler_params=_params("parallel"),
        name="mix_in",
    )(x2d, x2d, gain, w_ext, mu_rkv, vecs, w2_cat, head_ones)


def _mix_in_weights(w_in, w1, a1, g1, mu_wag, w2, a2, g2):
    l1 = jnp.concatenate([w1, a1, g1], axis=1).astype(F32)
    mu = jnp.concatenate([jnp.broadcast_to(mu_wag[j][:, None], (w.shape[0], w.shape[1]))
                          for j, w in enumerate((w1, a1, g1))], axis=1).astype(F32)
    w_ext = jnp.concatenate([w_in.astype(F32), mu * l1, (1.0 - mu) * l1], axis=1).astype(BF16)
    width = w2.shape[1]
    z = lambda rows: jnp.zeros((rows, width), F32)
    w2_cat = jnp.concatenate([
        jnp.concatenate([w2.astype(F32), z(w2.shape[0]), z(w2.shape[0])], axis=1),
        jnp.concatenate([z(a2.shape[0]), a2.astype(F32), z(a2.shape[0])], axis=1),
        jnp.concatenate([z(g2.shape[0]), z(g2.shape[0]), g2.astype(F32)], axis=1)], axis=0).astype(BF16)
    return w_ext, w2_cat


def _s5_tables(lam_re, lam_im, log_dt, b_re, b_im, c_re, c_im, n_chunks):
    t = S5_CHUNK
    hi = lax.Precision.HIGHEST
    lre = jnp.minimum(lam_re.astype(F32), LAMBDA_RE_MAX)
    lim = lam_im.astype(F32)
    dt = jnp.exp(log_dt.astype(F32))[:, None]
    mag = jnp.exp(lre * dt)
    lb_re = mag * jnp.cos(lim * dt)
    lb_im = mag * jnp.sin(lim * dt)
    den = lre * lre + lim * lim
    z_re = lb_re - 1.0
    coef_re = ((z_re * lre + lb_im * lim) / den)[..., None]
    coef_im = ((lb_im * lre - z_re * lim) / den)[..., None]
    b_re = b_re.astype(F32)
    b_im = b_im.astype(F32)
    bb_re = coef_re * b_re - coef_im * b_im
    bb_im = coef_re * b_im + coef_im * b_re
    c_re = c_re.astype(F32)
    c_im = c_im.astype(F32)

    def power(e):
        e = e.astype(F32)[:, None, None]
        m = jnp.exp(e * (lre * dt))
        ang = e * (lim * dt)
        return m * jnp.cos(ang), m * jnp.sin(ang)

    g = lre.shape[0]
    c = S5_GROUP
    p_re, p_im = power(jnp.arange(t + 1))
    cb_re = (c_re[:, None, :, :] * bb_re.transpose(0, 2, 1)[:, :, None, :]
             - c_im[:, None, :, :] * bb_im.transpose(0, 2, 1)[:, :, None, :])
    cb_im = (c_re[:, None, :, :] * bb_im.transpose(0, 2, 1)[:, :, None, :]
             + c_im[:, None, :, :] * bb_re.transpose(0, 2, 1)[:, :, None, :])
    cb = jnp.concatenate([cb_re, -cb_im], axis=-1).reshape(g, c * c, 2 * S5_STATE)
    pw = jnp.concatenate([p_re[:t], p_im[:t]], axis=-1).transpose(1, 2, 0)
    kvec = jnp.einsum('gkp,gpt->gkt', cb, pw, precision=hi).reshape(g, c, c // 2, 2 * t)
    r_re, r_im = power(t - 1 - jnp.arange(t))
    st_re = r_re[:, :, :, None] * bb_re - r_im[:, :, :, None] * bb_im
    st_im = r_re[:, :, :, None] * bb_im + r_im[:, :, :, None] * bb_re
    state_map = jnp.concatenate([st_re.transpose(1, 3, 0, 2).reshape(g, c * t, S5_STATE),
                                 st_im.transpose(1, 3, 0, 2).reshape(g, c * t, S5_STATE)],
                                axis=2).astype(BF16)
    q_re, q_im = p_re[1:], p_im[1:]
    ca_re = c_re[:, :, None, :] * q_re.transpose(1, 0, 2)[:, None] - c_im[:, :, None, :] * q_im.transpose(1, 0, 2)[:, None]
    ca_im = c_re[:, :, None, :] * q_im.transpose(1, 0, 2)[:, None] + c_im[:, :, None, :] * q_re.transpose(1, 0, 2)[:, None]
    cross = jnp.concatenate([ca_re.transpose(0, 3, 1, 2).reshape(g, S5_STATE, c * t),
                             -ca_im.transpose(0, 3, 1, 2).reshape(g, S5_STATE, c * t)], axis=1).astype(BF16)
    n_steps = max(1, (n_chunks - 1).bit_length())
    s_re, s_im = power(t * (2 ** jnp.arange(n_steps)))
    tab = jnp.stack([jnp.concatenate([s_re, s_re], axis=-1),
                     jnp.concatenate([-s_im, s_im], axis=-1)], axis=1)
    tab = tab.transpose(2, 0, 1, 3).reshape(g, 2 * n_steps, 2 * S5_STATE)
    return kvec, state_map, cross, tab, n_steps


def _s5_conv_kernel(n_chunks, n_steps, u_ref, kvec_ref, st_ref, cross_ref, tab_ref, y_ref, conv):
    t = S5_CHUNK
    row = lax.broadcasted_iota(jnp.int32, (t, LANES), 0)
    lane = lax.broadcasted_iota(jnp.int32, (t, LANES), 1)
    causal = jnp.bitwise_and(lane, t - 1) >= row
    for cin in range(S5_GROUP):
        for cp in range(S5_GROUP // 2):
            base = jnp.broadcast_to(kvec_ref[0, cin, cp:cp + 1, :], (t, LANES))
            blk = pltpu.roll(base, 0, 1, stride=1, stride_axis=0)
            conv[cin * t:(cin + 1) * t, cp * LANES:(cp + 1) * LANES] = jnp.where(causal, blk, 0.0).astype(BF16)
    u = u_ref[0]
    y = _dot(u, conv[...])
    x = _dot(u, st_ref[0])
    chunk = lax.broadcasted_iota(jnp.int32, x.shape, 0) % n_chunks
    half = x.shape[1] // 2
    for j in range(n_steps):
        sh = 1 << j
        xs = jnp.where(chunk >= sh, pltpu.roll(x, sh, 0), 0.0)
        x = x + tab_ref[0, 2 * j:2 * j + 1, :] * xs + tab_ref[0, 2 * j + 1:2 * j + 2, :] * pltpu.roll(xs, half, 1)
    x_in = jnp.where(chunk >= 1, pltpu.roll(x, 1, 0), 0.0)
    y_ref[0] = y + _dot(x_in.astype(BF16), cross_ref[0])


def _s5_conv(u2d, batch, seq, tables):
    kvec, state_map, cross, tab, n_steps = tables
    n, width = u2d.shape
    g = width // S5_GROUP
    t = S5_CHUNK
    n_chunks = seq // t
    ct = S5_GROUP * t
    ut = u2d.reshape(batch, n_chunks, t, g, S5_GROUP).transpose(3, 0, 1, 4, 2)
    ut = ut.reshape(g, batch * n_chunks, ct)
    rows = batch * n_chunks
    y = pl.pallas_call(
        functools.partial(_s5_conv_kernel, n_chunks, n_steps),
        grid=(g,),
        in_specs=[
            pl.BlockSpec((1, rows, ct), lambda i: (i, 0, 0)),
            pl.BlockSpec((1,) + kvec.shape[1:], lambda i: (i, 0, 0, 0)),
            pl.BlockSpec((1,) + state_map.shape[1:], lambda i: (i, 0, 0)),
            pl.BlockSpec((1,) + cross.shape[1:], lambda i: (i, 0, 0)),
            pl.BlockSpec((1,) + tab.shape[1:], lambda i: (i, 0, 0)),
        ],
        out_specs=pl.BlockSpec((1, rows, ct), lambda i: (i, 0, 0)),
        out_shape=jax.ShapeDtypeStruct((g, rows, ct), F32),
        scratch_shapes=[pltpu.VMEM((ct, ct), BF16)],
        compiler_params=_params("parallel"),
        name="s5_conv",
    )(ut, kvec, state_map, cross, tab)
    return y.reshape(g, batch, n_chunks, S5_GROUP, t).transpose(1, 2, 4, 0, 3).reshape(n, width)


def _pair_blockdiag(y, left):
    return jnp.concatenate([jnp.where(left, y, 0.0), jnp.where(left, 0.0, y)], axis=0).astype(BF16)


def _rwkv_chunk_kernel(r_ref, lw_ref, k_ref, v_ref, an_ref, bb_ref, rhat_o, yhat_o, g_o, h_o):
    t = RWKV_CHUNK
    rows, width = r_ref.shape
    n_chunks = rows // t
    pairs = width // LANES
    row = lax.broadcasted_iota(jnp.int32, (t, LANES), 0)
    lane = lax.broadcasted_iota(jnp.int32, (t, LANES), 1)
    col = jnp.bitwise_and(lane, RWKV_HEAD - 1)
    left = lane < RWKV_HEAD
    incl = row >= col
    strict = row > col
    same_blk = (row // RWKV_SUB) == (col // RWKV_SUB)
    eye = jnp.where(row == col, 1.0, 0.0).astype(F32)
    brow = lax.broadcasted_iota(jnp.int32, (LANES, LANES), 0)
    bcol = lax.broadcasted_iota(jnp.int32, (LANES, LANES), 1)
    same_head = (brow // RWKV_HEAD) == (bcol // RWKV_HEAD)
    eye_full = brow == bcol
    crow = lax.broadcasted_iota(jnp.int32, (rows, rows), 0)
    ccol = lax.broadcasted_iota(jnp.int32, (rows, rows), 1)
    tril = jnp.where(crow >= ccol, jnp.where((crow // t) == (ccol // t), 1.0, 0.0), 0.0).astype(BF16)

    lw = lw_ref[...]
    p1 = lw.astype(BF16)
    rem = lw - p1.astype(F32)
    p2 = rem.astype(BF16)
    p3 = (rem - p2.astype(F32)).astype(BF16)
    cs = _dot(tril, p1) + _dot(tril, p2) + _dot(tril, p3)
    a_t = an_ref[...] * jnp.exp(cs - lw)
    r_t = r_ref[...] * jnp.exp(cs)
    p_inv = jnp.exp(-cs)
    b_t = bb_ref[...] * p_inv
    k_t = k_ref[...] * p_inv

    def bd(y):
        return _pair_blockdiag(y, left)

    def pmm(x, y_bd):
        return _dot(x.astype(BF16), y_bd)

    units = [(c, j) for c in range(n_chunks) for j in range(pairs)]

    def tile(arr, c, j):
        return arr[c * t:(c + 1) * t, j * LANES:(j + 1) * LANES]

    a2 = [tile(a_t, c, j) for c, j in units]
    r2 = [tile(r_t, c, j) for c, j in units]
    v2 = [tile(v_ref[...], c, j) for c, j in units]
    ar = [jnp.concatenate([a, r], axis=0).astype(BF16) for a, r in zip(a2, r2)]
    prod = [_dot_nt(x, jnp.concatenate([bd(tile(b_t, c, j)), bd(tile(k_t, c, j))], axis=0))
            for x, (c, j) in zip(ar, units)]
    l_ab = [jnp.where(strict, p[:t, :LANES], 0.0) for p in prod]
    l_rb = [jnp.where(incl, p[t:, :LANES], 0.0) for p in prod]
    l_ak = [jnp.where(strict, p[:t, LANES:], 0.0) for p in prod]
    l_rk = [jnp.where(incl, p[t:, LANES:], 0.0) for p in prod]
    l_d = [jnp.where(same_blk, m, 0.0) for m in l_ab]
    l_o = [m - d for m, d in zip(l_ab, l_d)]
    l2 = [pmm(m, bd(m)) for m in l_d]
    q1 = [pmm(eye + m, bd(eye + s)) for m, s in zip(l_d, l2)]
    l4 = [pmm(m, bd(m)) for m in l2]
    q2 = [pmm(q, bd(eye + s)) for q, s in zip(q1, l4)]
    l8 = [pmm(m, bd(m)) for m in l4]
    d_inv = [pmm(q, bd(eye + s)) for q, s in zip(q2, l8)]
    akv = [pmm(jnp.concatenate([m, n], axis=0), bd(v)) for m, n, v in zip(l_ak, l_rk, v2)]
    x1 = [z[:t] for z in akv]

    def bd2(z1, z2):
        return jnp.concatenate([bd(z1), bd(z2)], axis=1)

    dz = [pmm(d, bd2(a, x)) for d, a, x in zip(d_inv, a2, x1)]
    n1 = [pmm(d, bd(o)) for d, o in zip(d_inv, l_o)]
    n2 = [pmm(m, bd(m)) for m in n1]
    t1 = [z + pmm(m, bd2(z[:, :LANES], z[:, LANES:])) for z, m in zip(dz, n2)]
    wu = [z + pmm(m, bd2(z[:, :LANES], z[:, LANES:])) for z, m in zip(t1, n1)]
    ry = [pmm(m, bd2(z[:, :LANES], z[:, LANES:])) for m, z in zip(l_rb, wu)]
    rk_v = [z[t:] for z in akv]

    for i, (c, j) in enumerate(units):
        rs = slice(c * t, (c + 1) * t)
        ls = slice(j * LANES, (j + 1) * LANES)
        rhat_o[rs, ls] = r2[i] + ry[i][:, :LANES]
        yhat_o[rs, ls] = ry[i][:, LANES:] + rk_v[i]
        cs_c = cs[rs, ls]
        cs_end = cs_c[t - 1:t, :]
        p_end = jnp.exp(cs_end - cs_c)
        b_h = (bb_ref[rs, ls] * p_end).astype(BF16)
        k_h = (k_ref[rs, ls] * p_end).astype(BF16)
        g_full = _dot_tn(wu[i][:, :LANES].astype(BF16), b_h)
        g_bd = jnp.where(same_head, g_full, 0.0) + jnp.where(eye_full, jnp.exp(cs_end), 0.0)
        g_o[c, j] = g_bd.astype(BF16)
        uv = jnp.concatenate([wu[i][:, LANES:], v2[i]], axis=0).astype(BF16)
        h_full = _dot_tn(uv, jnp.concatenate([b_h, k_h], axis=0))
        h_o[c, j] = jnp.where(left, h_full[:RWKV_HEAD], h_full[RWKV_HEAD:])


RWKV_CHUNKS_PER_STEP = 4


def _rwkv_chunks(r, lw, k, v, an, bb):
    n, width = r.shape
    t = RWKV_CHUNK
    cps = RWKV_CHUNKS_PER_STEP
    pairs = width // LANES
    nck = n // t
    tok = pl.BlockSpec((cps * t, width), lambda i: (i, 0))
    return pl.pallas_call(
        _rwkv_chunk_kernel,
        grid=(nck // cps,),
        in_specs=[tok] * 6,
        out_specs=[tok, tok,
                   pl.BlockSpec((cps, pairs, LANES, LANES), lambda i: (i, 0, 0, 0)),
                   pl.BlockSpec((cps, pairs, RWKV_HEAD, LANES), lambda i: (i, 0, 0, 0))],
        out_shape=[jax.ShapeDtypeStruct((n, width), F32)] * 2
        + [jax.ShapeDtypeStruct((nck, pairs, LANES, LANES), BF16),
           jax.ShapeDtypeStruct((nck, pairs, RWKV_HEAD, LANES), F32)],
        compiler_params=_params("parallel"),
        name="rwkv_chunk",
    )(r, lw, k, v, an, bb)


def _rwkv_scan_kernel(rhat_ref, yhat_ref, g_ref, h_ref, gate_ref, bonus_ref, lnw_ref, lnb_ref,
                      ones_ref, y_ref, state):
    t = RWKV_CHUNK
    batch, pairs = state.shape[0], state.shape[1]
    n_chunks = rhat_ref.shape[1] // t
    lane = lax.broadcasted_iota(jnp.int32, (RWKV_HEAD, LANES), 1)
    left = lane < RWKV_HEAD

    @pl.when(pl.program_id(0) == 0)
    def _():
        state[...] = jnp.zeros_like(state)

    s = [[state[b, j] for j in range(pairs)] for b in range(batch)]
    y_rows = []
    for b in range(batch):
        chunk_rows = []
        for c in range(n_chunks):
            rs = slice(c * t, (c + 1) * t)
            tiles = []
            for j in range(pairs):
                ls = slice(j * LANES, (j + 1) * LANES)
                s_bd = _pair_blockdiag(s[b][j], left)
                tiles.append(yhat_ref[b, rs, ls] + _dot_nt(rhat_ref[b, rs, ls].astype(BF16), s_bd))
                s[b][j] = _dot(s[b][j].astype(BF16), g_ref[b, c, j]) + h_ref[b, c, j]
            chunk_rows.append(jnp.concatenate(tiles, axis=1))
        y_rows.append(jnp.concatenate(chunk_rows, axis=0))
    for b in range(batch):
        for j in range(pairs):
            state[b, j] = s[b][j]

    ones = ones_ref[...]
    inv = 1.0 / RWKV_HEAD
    for b in range(batch):
        y = y_rows[b]
        cen = y - _dot(y.astype(BF16), ones) * inv
        var = _dot((cen * cen).astype(BF16), ones) * inv
        yn = cen * lax.rsqrt(var + RWKV_GN_EPS) * lnw_ref[...] + lnb_ref[...]
        y_ref[b] = (yn + bonus_ref[b]) * gate_ref[b]


def _rwkv_scan(rhat, yhat, g_mat, h_mat, gate, bonus, ln_w, ln_b, head_ones, batch, seq):
    n, width = rhat.shape
    t = RWKV_CHUNK
    cps = RWKV_CHUNKS_PER_STEP
    pairs = width // LANES
    nc = seq // t
    tok3 = lambda a: a.reshape(batch, seq, width)
    tok = pl.BlockSpec((batch, cps * t, width), lambda i: (0, i, 0))
    out = pl.pallas_call(
        _rwkv_scan_kernel,
        grid=(nc // cps,),
        in_specs=[tok, tok,
                  pl.BlockSpec((batch, cps, pairs, LANES, LANES), lambda i: (0, i, 0, 0, 0)),
                  pl.BlockSpec((batch, cps, pairs, RWKV_HEAD, LANES), lambda i: (0, i, 0, 0, 0)),
                  tok, tok, _full(ln_w.shape), _full(ln_b.shape), _full(head_ones.shape)],
        out_specs=tok,
        out_shape=jax.ShapeDtypeStruct((batch, seq, width), F32),
        scratch_shapes=[pltpu.VMEM((batch, pairs, RWKV_HEAD, LANES), F32)],
        compiler_params=_params("arbitrary"),
        name="rwkv_scan",
    )(tok3(rhat), tok3(yhat), g_mat.reshape(batch, nc, pairs, LANES, LANES),
      h_mat.reshape(batch, nc, pairs, RWKV_HEAD, LANES), tok3(gate), tok3(bonus), ln_w, ln_b, head_ones)
    return out.reshape(n, width)


def _post_mix_kernel(n_experts, x_ref, yc_ref, u_ref, yr_ref, d_ref, wglu_ref, wout_ref, gain_ref,
                     rw_ref, rb_ref, x1_o, hm_o, gate_o, lpos_o, meta_o, tot_o, running):
    tm = x_ref.shape[0]
    width = yc_ref.shape[1]

    @pl.when(pl.program_id(0) == 0)
    def _():
        running[...] = jnp.zeros_like(running)

    y = yc_ref[...] + d_ref[...] * u_ref[...]
    y = 0.5 * y * (1.0 + jnp.tanh(0.7978845608028654 * (y + 0.044715 * (y * y * y))))
    y = y * _sigmoid(_dot(y.astype(BF16), wglu_ref[...]))
    x1 = (x_ref[...] + _dot(y.astype(BF16), wout_ref[:width, :])
          + _dot(yr_ref[...].astype(BF16), wout_ref[width:, :]))
    x1_o[...] = x1
    hm = _rms(x1, gain_ref[...])
    hm_o[...] = hm.astype(BF16)
    logits = _dot(hm.astype(BF16), rw_ref[...]) + rb_ref[...]

    lane = lax.broadcasted_iota(jnp.int32, logits.shape, 1)
    lanef = lane.astype(F32)
    sel = jnp.zeros(logits.shape, F32)
    idx_cols, val_cols = [], []
    work = logits
    for _ in range(TOP_K):
        m = jnp.max(work, axis=-1, keepdims=True)
        pick = jnp.min(jnp.where(work == m, lanef, float(LANES)), axis=-1, keepdims=True)
        hit = lanef == pick
        sel = jnp.where(hit, 1.0, sel)
        work = jnp.where(hit, -jnp.inf, work)
        idx_cols.append(pick)
        val_cols.append(m)
    exps = [jnp.exp(vv - val_cols[0]) for vv in val_cols]
    denom = exps[0] + exps[1] + exps[2] + exps[3]

    row = lax.broadcasted_iota(jnp.int32, (tm, tm), 0)
    col = lax.broadcasted_iota(jnp.int32, (tm, tm), 1)
    before = jnp.where(row > col, 1.0, 0.0).astype(BF16)
    local = _dot(before, sel.astype(BF16))
    cnt = jnp.sum(sel, axis=0, keepdims=True)
    cnt_al = jnp.floor((cnt + (RUN_ALIGN - 1)) * (1.0 / RUN_ALIGN)) * RUN_ALIGN
    erow = lax.broadcasted_iota(jnp.int32, (LANES, LANES), 0)
    ecol = lax.broadcasted_iota(jnp.int32, (LANES, LANES), 1)
    upper = jnp.where(erow < ecol, 1.0, 0.0).astype(BF16)
    toff = _dot(jnp.broadcast_to(cnt_al, (SUBLANES, LANES)).astype(BF16), upper)[0:1]
    tbase = running[...]
    gate_out = jnp.zeros(logits.shape, F32)
    lpos_out = jnp.zeros(logits.shape, F32)
    for j in range(TOP_K):
        lp = jnp.sum(jnp.where(lanef == idx_cols[j], local + toff, 0.0), axis=-1, keepdims=True)
        gate_out = jnp.where(lane == j, exps[j] / denom, gate_out)
        lpos_out = jnp.where(lane == j, lp, lpos_out)
    gate_o[...] = gate_out
    lpos_o[...] = lpos_out.astype(jnp.int32)
    srow = lax.broadcasted_iota(jnp.int32, (SUBLANES, LANES), 0)
    meta = jnp.where(srow == 0, cnt, jnp.where(srow == 1, tbase, jnp.where(srow == 2, toff, 0.0)))
    meta_o[...] = meta.astype(jnp.int32)
    running[...] = tbase + cnt_al
    tot_o[...] = jnp.broadcast_to(running[...], tot_o.shape).astype(jnp.int32)
    del n_experts


def _post_mix(x2d, yconv, u, y_rwkv, s5_d, w_glu, w_out, gain, rw, rb, n_experts):
    n, d = x2d.shape
    width = yconv.shape[1]
    tm = MOE_TILE
    row = lambda i: (i, 0)
    tok_d = pl.BlockSpec((tm, d), row)
    tok_w = pl.BlockSpec((tm, width), row)
    tok_l = pl.BlockSpec((tm, LANES), row)
    return pl.pallas_call(
        functools.partial(_post_mix_kernel, n_experts),
        grid=(n // tm,),
        in_specs=[tok_d, tok_w, tok_w, tok_w, _full(s5_d.shape), _full(w_glu.shape),
                  _full(w_out.shape), _full(gain.shape), _full(rw.shape), _full(rb.shape)],
        out_specs=[tok_d, tok_d, tok_l, tok_l,
                   pl.BlockSpec((SUBLANES, LANES), row), _full((SUBLANES, LANES))],
        out_shape=[jax.ShapeDtypeStruct((n, d), F32), jax.ShapeDtypeStruct((n, d), BF16),
                   jax.ShapeDtypeStruct((n, LANES), F32), jax.ShapeDtypeStruct((n, LANES), jnp.int32),
                   jax.ShapeDtypeStruct((n // tm * SUBLANES, LANES), jnp.int32),
                   jax.ShapeDtypeStruct((SUBLANES, LANES), jnp.int32)],
        scratch_shapes=[pltpu.VMEM((1, LANES), F32)],
        compiler_params=_params("arbitrary"),
        name="post_mix",
    )(x2d, yconv, u, y_rwkv, s5_d, w_glu, w_out, gain, rw, rb)


def _sorted_rows(tm, n_experts):
    return -(-(tm * TOP_K + n_experts * (RUN_ALIGN - 1)) // LANES) * LANES


RUN_CHUNKS = (8, 4, 2, 1)


def _start_runs(n_experts, pstart_ref, meta_ref, make_copy):
    big = RUN_CHUNKS[0]
    for e in range(n_experts):
        groups = (meta_ref[0, e] + (RUN_ALIGN - 1)) // RUN_ALIGN
        seg = pstart_ref[e] + meta_ref[1, e]
        loc = meta_ref[2, e]
        n_big = groups // big

        def body(g, c, seg=seg, loc=loc):
            off = g * (big * RUN_ALIGN)
            make_copy(pl.multiple_of(loc + off, RUN_ALIGN), pl.multiple_of(seg + off, RUN_ALIGN),
                      big * RUN_ALIGN).start(priority=e % 2)
            return c

        lax.fori_loop(0, n_big, body, 0)
        done = n_big * big
        for size in RUN_CHUNKS[1:]:
            take = ((groups - done) // size) > 0

            @pl.when(take)
            def _(done=done, size=size, seg=seg, loc=loc):
                off = done * RUN_ALIGN
                make_copy(pl.multiple_of(loc + off, RUN_ALIGN), pl.multiple_of(seg + off, RUN_ALIGN),
                          size * RUN_ALIGN).start(priority=e % 2)

            done = done + jnp.where(take, size, 0)


def _wait_runs(n_experts, meta_ref, make_copy):
    big = RUN_CHUNKS[0]
    groups = 0
    for e in range(n_experts):
        groups = groups + (meta_ref[0, e] + (RUN_ALIGN - 1)) // RUN_ALIGN

    def wait_big(g, c):
        make_copy(0, 0, big * RUN_ALIGN).wait()
        return c

    def wait_small(g, c):
        make_copy(0, 0, RUN_ALIGN).wait()
        return c

    lax.fori_loop(0, groups // big, wait_big, 0)
    lax.fori_loop(0, groups % big, wait_small, 0)


def _dispatch_kernel(n_experts, pstart_ref, tot_ref, meta_ref, meta_prev_ref, lpos_ref, hm_ref, xs_ref,
                     srt, zero, sem):
    tm = hm_ref.shape[0]
    rows = srt.shape[1]
    i = pl.program_id(0)
    cur = i % 2
    pos_t = jnp.transpose(lpos_ref[...].astype(F32))
    rid = lax.broadcasted_iota(jnp.int32, (rows, tm), 0).astype(F32)
    perm = jnp.zeros((rows, tm), F32)
    for j in range(TOP_K):
        perm = jnp.where(rid == pos_t[j:j + 1, :], 1.0, perm)
    srt[cur] = _dot(perm.astype(BF16), hm_ref[...])

    def copies(slot):
        def make_copy(loc, dst, size):
            return pltpu.make_async_copy(srt.at[slot, pl.ds(loc, size)], xs_ref.at[pl.ds(dst, size)],
                                         sem.at[slot])
        return make_copy

    _start_runs(n_experts, pstart_ref, meta_ref, copies(cur))

    @pl.when(i > 0)
    def _():
        _wait_runs(n_experts, meta_prev_ref, copies(1 - cur))

    @pl.when(i == pl.num_programs(0) - 1)
    def _():
        _wait_runs(n_experts, meta_ref, copies(cur))

    @pl.when(pl.program_id(0) == pl.num_programs(0) - 1)
    def _():
        zero[...] = jnp.zeros_like(zero)
        for wait in (False, True):
            for e in range(n_experts):
                used = tot_ref[e]
                start = pstart_ref[e] + used
                groups = (-used % EXPERT_TILE) // RUN_ALIGN

                def body(g, c, start=start):
                    cp = pltpu.make_async_copy(
                        zero, xs_ref.at[pl.ds(pl.multiple_of(start + g * RUN_ALIGN, RUN_ALIGN), RUN_ALIGN)],
                        sem.at[0])
                    if wait:
                        cp.wait()
                    else:
                        cp.start()
                    return c

                lax.fori_loop(0, groups, body, 0)

        last = n_experts - 1
        end = pstart_ref[last] + tot_ref[last] + (-tot_ref[last] % EXPERT_TILE)
        srt[0, 0:EXPERT_TILE, :] = jnp.zeros((EXPERT_TILE, srt.shape[2]), srt.dtype)
        for wait in (False, True):
            def tail(b, c):
                cp = pltpu.make_async_copy(
                    srt.at[0, pl.ds(0, EXPERT_TILE)],
                    xs_ref.at[pl.ds(pl.multiple_of(end + b * EXPERT_TILE, EXPERT_TILE), EXPERT_TILE)], sem.at[0])
                if wait:
                    cp.wait()
                else:
                    cp.start()
                return c

            lax.fori_loop(0, (xs_ref.shape[0] - end) // EXPERT_TILE, tail, 0)


def _dispatch(hm, lpos, meta, pstarts, totals, m_pad, n_experts):
    n, dh = hm.shape
    tm = MOE_TILE
    rows = _sorted_rows(tm, n_experts)
    grid_spec = pltpu.PrefetchScalarGridSpec(
        num_scalar_prefetch=2,
        grid=(n // tm,),
        in_specs=[pl.BlockSpec((SUBLANES, LANES), lambda i, ps, tt: (i, 0), memory_space=pltpu.SMEM),
                  pl.BlockSpec((SUBLANES, LANES), lambda i, ps, tt: (jnp.maximum(i - 1, 0), 0),
                               memory_space=pltpu.SMEM),
                  pl.BlockSpec((tm, LANES), lambda i, ps, tt: (i, 0)),
                  pl.BlockSpec((tm, dh), lambda i, ps, tt: (i, 0))],
        out_specs=pl.BlockSpec(memory_space=pl.ANY),
        scratch_shapes=[pltpu.VMEM((2, rows, dh), F32), pltpu.VMEM((RUN_ALIGN, dh), F32),
                        pltpu.SemaphoreType.DMA((2,))],
    )
    return pl.pallas_call(
        functools.partial(_dispatch_kernel, n_experts),
        grid_spec=grid_spec,
        out_shape=jax.ShapeDtypeStruct((m_pad, dh), F32),
        compiler_params=_params("arbitrary"),
        name="dispatch",
    )(pstarts, totals, meta, meta, lpos, hm)


def _experts_kernel(be_ref, nu_ref, xs_ref, wg_ref, bg_ref, wu_ref, bu_ref, wd_ref, bd_ref, ys_ref,
                    wg_s, wu_s, wd_s):
    i = pl.program_id(0)
    changed = jnp.logical_or(i == 0, be_ref[i] != be_ref[jnp.maximum(i - 1, 0)])

    @pl.when(changed)
    def _():
        wg_s[...] = wg_ref[0].astype(BF16)
        wu_s[...] = wu_ref[0].astype(BF16)
        wd_s[...] = wd_ref[0].astype(BF16)

    @pl.when(i < nu_ref[0])
    def _():
        xb = xs_ref[...].astype(BF16)
        gt = jnp.minimum(_dot(xb, wg_s[...]) + bg_ref[0], SWIGLU_LIMIT)
        up = jnp.clip(_dot(xb, wu_s[...]) + bu_ref[0], -SWIGLU_LIMIT, SWIGLU_LIMIT)
        act = (up + 1.0) * gt * _sigmoid(SWIGLU_ALPHA * gt)
        ys_ref[...] = _dot(act.astype(BF16), wd_s[...]) + bd_ref[0]

    @pl.when(i >= nu_ref[0])
    def _():
        ys_ref[...] = jnp.zeros_like(ys_ref)


def _experts(xs, block_e, n_used, wg, bg, wu, bu, wd, bd):
    m_pad, dh = xs.shape
    tmb = EXPERT_TILE
    d, dff = wg.shape[1], wg.shape[2]
    wmap = lambda i, be, nu: (be[i], 0, 0)
    grid_spec = pltpu.PrefetchScalarGridSpec(
        num_scalar_prefetch=2,
        grid=(m_pad // tmb,),
        in_specs=[pl.BlockSpec((tmb, dh), lambda i, be, nu: (jnp.where(i < nu[0], i, 0), 0)),
                  pl.BlockSpec((1, d, dff), wmap), pl.BlockSpec((1, 1, dff), wmap),
                  pl.BlockSpec((1, d, dff), wmap), pl.BlockSpec((1, 1, dff), wmap),
                  pl.BlockSpec((1, dff, d), wmap), pl.BlockSpec((1, 1, d), wmap)],
        out_specs=pl.BlockSpec((tmb, dh), lambda i, be, nu: (i, 0)),
        scratch_shapes=[pltpu.VMEM((d, dff), BF16), pltpu.VMEM((d, dff), BF16), pltpu.VMEM((dff, d), BF16)],
    )
    return pl.pallas_call(
        _experts_kernel,
        grid_spec=grid_spec,
        out_shape=jax.ShapeDtypeStruct((m_pad, dh), F32),
        compiler_params=_params("arbitrary"),
        name="experts",
    )(block_e, n_used, xs, wg, bg, wu, bu, wd, bd)


def _final_kernel(n_experts, pstart_ref, meta_ref, meta_next_ref, lpos_ref, x1_ref, gate_ref, p_ref, ys_ref,
                  gple_ref, wproj_ref, wgate_ref, gfin_ref, out_ref, buf, sem):
    tm = x1_ref.shape[0]
    rows = buf.shape[1]
    i = pl.program_id(0)
    cur = i % 2

    def copies(slot):
        def make_copy(loc, src, size):
            return pltpu.make_async_copy(ys_ref.at[pl.ds(src, size)], buf.at[slot, pl.ds(loc, size)],
                                         sem.at[slot])
        return make_copy

    @pl.when(i == 0)
    def _():
        buf[...] = jnp.zeros_like(buf)
        _start_runs(n_experts, pstart_ref, meta_ref, copies(0))

    @pl.when(i + 1 < pl.num_programs(0))
    def _():
        _start_runs(n_experts, pstart_ref, meta_next_ref, copies(1 - cur))

    cid = lax.broadcasted_iota(jnp.int32, (tm, rows), 1)
    lpos = lpos_ref[...]
    gates = gate_ref[...]
    comb = jnp.zeros((tm, rows), F32)
    for j in range(TOP_K):
        comb = jnp.where(cid == lpos[:, j:j + 1], gates[:, j:j + 1], comb)

    _wait_runs(n_experts, meta_ref, copies(cur))
    x2 = x1_ref[...] + _dot(comb.astype(BF16), buf[cur].astype(BF16))
    gate = _sigmoid(_dot(_rms(x2, gple_ref[...]).astype(BF16), wgate_ref[...]))
    x3 = x2 + _dot(p_ref[...].astype(BF16), wproj_ref[...]) * gate
    out_ref[...] = _rms(x3, gfin_ref[...])


def _final(x1, gates, lpos, meta, pstarts, p2d, ys, g_ple, w_proj, w_gate, g_fin, n_experts):
    n, d = x1.shape
    tm = MOE_TILE
    steps = n // tm
    rows = _sorted_rows(tm, n_experts)
    row = lambda i, ps: (i, 0)
    const = lambda shape: pl.BlockSpec(shape, lambda i, ps: (0,) * len(shape))
    grid_spec = pltpu.PrefetchScalarGridSpec(
        num_scalar_prefetch=1,
        grid=(steps,),
        in_specs=[pl.BlockSpec((SUBLANES, LANES), row, memory_space=pltpu.SMEM),
                  pl.BlockSpec((SUBLANES, LANES), lambda i, ps: (jnp.minimum(i + 1, steps - 1), 0),
                               memory_space=pltpu.SMEM),
                  pl.BlockSpec((tm, LANES), row), pl.BlockSpec((tm, d), row), pl.BlockSpec((tm, LANES), row),
                  pl.BlockSpec((tm, p2d.shape[1]), row), pl.BlockSpec(memory_space=pl.ANY),
                  const(g_ple.shape), const(w_proj.shape), const(w_gate.shape), const(g_fin.shape)],
        out_specs=pl.BlockSpec((tm, d), row),
        scratch_shapes=[pltpu.VMEM((2, rows, ys.shape[1]), ys.dtype), pltpu.SemaphoreType.DMA((2,))],
    )
    return pl.pallas_call(
        functools.partial(_final_kernel, n_experts),
        grid_spec=grid_spec,
        out_shape=jax.ShapeDtypeStruct((n, d), F32),
        compiler_params=_params("arbitrary"),
        name="final",
    )(pstarts, meta, meta, lpos, x1, gates, p2d, ys, g_ple, w_proj, w_gate, g_fin)


def _head_ones(width):
    hd = jnp.arange(width) // RWKV_HEAD
    return (hd[:, None] == hd[None, :]).astype(BF16)


def _layer(x2d, p2d, batch, seq, norm_mix_g, w_in, s5_lam_re, s5_lam_im, s5_log_dt, s5_b_re, s5_b_im,
           s5_c_re, s5_c_im, s5_d, s5_w_glu, mu_rkv, mu_wag, w0, w1, w2, a0, a1, a2, g1, g2, k_k, k_a,
           r_k, ln_w, ln_b, w_out, norm_moe_g, router_w, router_b, wg, bg, wu, bu, wd, bd,
           norm_ple_g, ple_w_proj, ple_w_gate):
    n, d = x2d.shape
    width = w_in.shape[1] // 4
    n_experts = router_w.shape[1]
    row2 = lambda t: t.reshape(1, -1).astype(F32)
    head_ones = _head_ones(width)
    vecs = jnp.zeros((SUBLANES, width), F32)
    vecs = vecs.at[0].set(w0).at[1].set(a0).at[2].set(k_k).at[3].set(k_a).at[4].set(r_k.reshape(-1))

    w_ext, w2_cat = _mix_in_weights(w_in, w1, a1, g1, mu_wag, w2, a2, g2)
    u, r, lw, k, v, an, bb, gate, bonus, u_bf = _mix_in(
        x2d, seq, row2(norm_mix_g), w_ext, mu_rkv.astype(F32), vecs, w2_cat, head_ones)

    tables = _s5_tables(s5_lam_re, s5_lam_im, s5_log_dt, s5_b_re, s5_b_im, s5_c_re, s5_c_im,
                        seq // S5_CHUNK)
    yconv = _s5_conv(u_bf, batch, seq, tables)

    rhat, yhat, g_mat, h_mat = _rwkv_chunks(r, lw, k, v, an, bb)
    y_rwkv = _rwkv_scan(rhat, yhat, g_mat, h_mat, gate, bonus, row2(ln_w), row2(ln_b), head_ones,
                        batch, seq)

    rw = jnp.zeros((d, LANES), BF16).at[:, :n_experts].set(router_w.astype(BF16))
    rb = jnp.full((1, LANES), NEG_BIG, F32).at[0, :n_experts].set(router_b.astype(F32))
    x1, hm, gates, lpos, meta, totals = _post_mix(
        x2d, yconv, u, y_rwkv, row2(s5_d), s5_w_glu.astype(BF16), w_out.astype(BF16),
        row2(norm_moe_g), rw, rb, n_experts)

    tmb = EXPERT_TILE
    n_tiles = n // MOE_TILE
    max_rows = n * TOP_K + n_tiles * n_experts * (RUN_ALIGN - 1)
    n_blocks = -(-max_rows // tmb) + n_experts
    seg = totals[0, :n_experts].astype(jnp.int32)
    padded = ((seg + tmb - 1) // tmb) * tmb
    pends = jnp.cumsum(padded)
    pstarts = (pends - padded).astype(jnp.int32)
    block_start = jnp.arange(n_blocks, dtype=jnp.int32) * tmb
    block_e = jnp.minimum(jnp.sum((pends[None, :] <= block_start[:, None]).astype(jnp.int32), axis=1),
                          n_experts - 1).astype(jnp.int32)
    n_used = (pends[-1] // tmb).astype(jnp.int32).reshape(1)

    xs = _dispatch(hm, lpos, meta, pstarts, seg, n_blocks * tmb, n_experts)
    ys = _experts(xs, block_e, n_used, wg, bg.reshape(n_experts, 1, -1).astype(F32),
                  wu, bu.reshape(n_experts, 1, -1).astype(F32), wd,
                  bd.reshape(n_experts, 1, -1).astype(F32))
    return (x1, gates, lpos, meta, pstarts, ys, p2d, row2(norm_ple_g), ple_w_proj.astype(BF16),
            ple_w_gate.astype(BF16), n_experts)


def kernel(x, p, norm_mix_g, w_in, s5_lam_re, s5_lam_im, s5_log_dt, s5_b_re, s5_b_im, s5_c_re, s5_c_im, s5_d, s5_w_glu, rwkv_mu_rkv, rwkv_mu_wag, rwkv_w0, rwkv_w1, rwkv_w2, rwkv_a0, rwkv_a1, rwkv_a2, rwkv_g1, rwkv_g2, rwkv_k_k, rwkv_k_a, rwkv_r_k, rwkv_ln_w, rwkv_ln_b, w_out, norm_moe_g, router_w, router_b, exp_w_gate, exp_b_gate, exp_w_up, exp_b_up, exp_w_down, exp_b_down, norm_ple_g, ple_w_proj, ple_w_gate, final_norm_g):
    batch, seq, d = x.shape
    assert w_in.shape[0] == 1, "the final kernel fuses the last RMSNorm: single-layer stacks only"
    i = 0
    x1, gates, lpos, meta, pstarts, ys, p2d, g_ple, w_proj, w_gate, n_experts = _layer(
        x.reshape(batch * seq, d), p[i].reshape(batch * seq, -1), batch, seq, norm_mix_g[i], w_in[i],
        s5_lam_re[i], s5_lam_im[i], s5_log_dt[i], s5_b_re[i], s5_b_im[i], s5_c_re[i], s5_c_im[i],
        s5_d[i], s5_w_glu[i], rwkv_mu_rkv[i], rwkv_mu_wag[i], rwkv_w0[i], rwkv_w1[i], rwkv_w2[i],
        rwkv_a0[i], rwkv_a1[i], rwkv_a2[i], rwkv_g1[i], rwkv_g2[i], rwkv_k_k[i], rwkv_k_a[i],
        rwkv_r_k[i], rwkv_ln_w[i], rwkv_ln_b[i], w_out[i], norm_moe_g[i], router_w[i],
        router_b[i], exp_w_gate[i], exp_b_gate[i], exp_w_up[i], exp_b_up[i], exp_w_down[i],
        exp_b_down[i], norm_ple_g[i], ple_w_proj[i], ple_w_gate[i])
    out = _final(x1, gates, lpos, meta, pstarts, p2d, ys, g_ple, w_proj, w_gate,
                 final_norm_g.reshape(1, -1).astype(F32), n_experts)
    return out.reshape(batch, seq, d)
```

```python
import functools

import jax
import jax.numpy as jnp
from jax import lax
from jax.experimental import pallas as pl
from jax.experimental.pallas import tpu as pltpu

F32 = jnp.float32
BF16 = jnp.bfloat16

S5_GROUP = 16
S5_STATE = 64
RWKV_HEAD = 64
DECAY_LORA = 64
ICLR_LORA = 64
TOP_K = 4
RMS_EPS = 1e-6
RWKV_GN_EPS = 64e-5
LAMBDA_RE_MAX = -1e-4
SWIGLU_LIMIT = 7.0
SWIGLU_ALPHA = 1.702

LANES = 128
SUBLANES = 8
VMEM_LIMIT = 56 * 1024 * 1024

S5_CHUNK = 64
RWKV_CHUNK = 64
RWKV_SUB = 16
TOKEN_TILE = 512
EXPERT_TILE = 512
MOE_TILE = 512
RUN_ALIGN = 8
NEG_BIG = -1e30


def _dot(a, b):
    return jnp.dot(a, b, preferred_element_type=F32)


def _dot_nt(a, b):
    return lax.dot_general(a, b, (((1,), (1,)), ((), ())), preferred_element_type=F32)


def _dot_tn(a, b):
    return lax.dot_general(a, b, (((0,), (0,)), ((), ())), preferred_element_type=F32)


def _rms(t, gain):
    return t * lax.rsqrt(jnp.mean(t * t, axis=-1, keepdims=True) + RMS_EPS) * gain


def _sigmoid(t):
    return 1.0 / (1.0 + jnp.exp(-t))


def _params(*sem):
    return pltpu.CompilerParams(dimension_semantics=sem, vmem_limit_bytes=VMEM_LIMIT)


def _full(shape):
    return pl.BlockSpec(shape, lambda *_: (0,) * len(shape))


def _mix_in_kernel(seq, x_ref, xp_ref, gain_ref, win_ref, murkv_ref, vec_ref, w2_ref, ones_ref,
                   u_o, r_o, lw_o, k_o, v_o, an_o, bb_o, g_o, bonus_o, ub_o):
    tm = x_ref.shape[0]
    width = u_o.shape[1]
    lora = w2_ref.shape[0]
    gain = gain_ref[...]
    h = _rms(x_ref[...], gain)
    keep = jnp.where((pl.program_id(0) * tm) % seq == 0, 0.0, 1.0)
    hp = _rms(xp_ref[...], gain) * keep
    h_ext = jnp.concatenate([hp, h], axis=0).astype(BF16)
    proj = _dot(h_ext, win_ref[...])
    shifted = pltpu.roll(proj[:, width:4 * width + lora], 1, 0)[SUBLANES:]
    cur = proj[SUBLANES:]
    u_o[...] = cur[:, :width]
    ub_o[...] = cur[:, :width].astype(BF16)

    l1 = cur[:, 4 * width + lora:] + shifted[:, 3 * width:]
    lane = lax.broadcasted_iota(jnp.int32, l1.shape, 1)
    hidden = jnp.where(lane < DECAY_LORA, jnp.tanh(l1),
                       jnp.where(lane < DECAY_LORA + ICLR_LORA, l1, _sigmoid(l1))).astype(BF16)
    z3 = _dot(hidden, w2_ref[...])

    w0 = vec_ref[0:1, :]
    a0 = vec_ref[1:2, :]
    k_k = vec_ref[2:3, :]
    k_a = vec_ref[3:4, :]
    r_k = vec_ref[4:5, :]
    nz = -(w0 + z3[:, :width])
    softplus = jnp.maximum(nz, 0.0) + jnp.log(1.0 + jnp.exp(-jnp.abs(nz)))
    lw_o[...] = -jnp.exp(-softplus - 0.5)
    a = _sigmoid(a0 + z3[:, width:2 * width])
    g_o[...] = z3[:, 2 * width:]

    r_cur = cur[:, width:2 * width]
    k_cur = cur[:, 2 * width:3 * width]
    v_cur = cur[:, 3 * width:4 * width]
    r = r_cur + (shifted[:, :width] - r_cur) * murkv_ref[0:1, :]
    k = k_cur + (shifted[:, width:2 * width] - k_cur) * murkv_ref[1:2, :]
    v = v_cur + (shifted[:, 2 * width:3 * width] - v_cur) * murkv_ref[2:3, :]

    ones = ones_ref[...]
    kk = k * k_k
    norm = jnp.sqrt(_dot((kk * kk).astype(BF16), ones))
    kkn = kk / jnp.maximum(norm, 1e-12)
    k2 = k * (1.0 + (a - 1.0) * k_a)
    r_o[...] = r
    k_o[...] = k2
    v_o[...] = v
    an_o[...] = -kkn
    bb_o[...] = kkn * a
    bonus_o[...] = _dot((r * k2 * r_k).astype(BF16), ones) * v


def _mix_in(x2d, seq, gain, w_ext, mu_rkv, vecs, w2_cat, head_ones):
    n, d = x2d.shape
    width = w2_cat.shape[1] // 3
    tm = TOKEN_TILE
    prev_blocks = tm // SUBLANES
    row = lambda i: (i, 0)
    out = jax.ShapeDtypeStruct((n, width), F32)
    return pl.pallas_call(
        functools.partial(_mix_in_kernel, seq),
        grid=(n // tm,),
        in_specs=[
            pl.BlockSpec((tm, d), row),
            pl.BlockSpec((SUBLANES, d), lambda i: (jnp.maximum(i * prev_blocks - 1, 0), 0)),
            _full(gain.shape), _full(w_ext.shape), _full(mu_rkv.shape), _full(vecs.shape),
            _full(w2_cat.shape), _full(head_ones.shape),
        ],
        out_specs=[pl.BlockSpec((tm, width), row)] * 10,
        out_shape=[out] * 9 + [jax.ShapeDtypeStruct((n, width), BF16)],
        compiler_params=_params("parallel"),
        name="mix_in",
    )(x2d, x2d, gain, w_ext, mu_rkv, vecs, w2_cat, head_ones)


def _mix_in_weights(w_in, w1, a1, g1, mu_wag, w2, a2, g2):
    l1 = jnp.concatenate([w1, a1, g1], axis=1).astype(F32)
    mu = jnp.concatenate([jnp.broadcast_to(mu_wag[j][:, None], (w.shape[0], w.shape[1]))
                          for j, w in enumerate((w1, a1, g1))], axis=1).astype(F32)
    w_ext = jnp.concatenate([w_in.astype(F32), mu * l1, (1.0 - mu) * l1], axis=1).astype(BF16)
    width = w2.shape[1]
    z = lambda rows: jnp.zeros((rows, width), F32)
    w2_cat = jnp.concatenate([
        jnp.concatenate([w2.astype(F32), z(w2.shape[0]), z(w2.shape[0])], axis=1),
        jnp.concatenate([z(a2.shape[0]), a2.astype(F32), z(a2.shape[0])], axis=1),
        jnp.concatenate([z(g2.shape[0]), z(g2.shape[0]), g2.astype(F32)], axis=1)], axis=0).astype(BF16)
    return w_ext, w2_cat


def _s5_tables(lam_re, lam_im, log_dt, b_re, b_im, c_re, c_im, n_chunks):
    t = S5_CHUNK
    hi = lax.Precision.HIGHEST
    lre = jnp.minimum(lam_re.astype(F32), LAMBDA_RE_MAX)
    lim = lam_im.astype(F32)
    dt = jnp.exp(log_dt.astype(F32))[:, None]
    mag = jnp.exp(lre * dt)
    lb_re = mag * jnp.cos(lim * dt)
    lb_im = mag * jnp.sin(lim * dt)
    den = lre * lre + lim * lim
    z_re = lb_re - 1.0
    coef_re = ((z_re * lre + lb_im * lim) / den)[..., None]
    coef_im = ((lb_im * lre - z_re * lim) / den)[..., None]
    b_re = b_re.astype(F32)
    b_im = b_im.astype(F32)
    bb_re = coef_re * b_re - coef_im * b_im
    bb_im = coef_re * b_im + coef_im * b_re
    c_re = c_re.astype(F32)
    c_im = c_im.astype(F32)

    def power(e):
        e = e.astype(F32)[:, None, None]
        m = jnp.exp(e * (lre * dt))
        ang = e * (lim * dt)
        return m * jnp.cos(ang), m * jnp.sin(ang)

    g = lre.shape[0]
    c = S5_GROUP
    p_re, p_im = power(jnp.arange(t + 1))
    cb_re = (c_re[:, None, :, :] * bb_re.transpose(0, 2, 1)[:, :, None, :]
             - c_im[:, None, :, :] * bb_im.transpose(0, 2, 1)[:, :, None, :])
    cb_im = (c_re[:, None, :, :] * bb_im.transpose(0, 2, 1)[:, :, None, :]
             + c_im[:, None, :, :] * bb_re.transpose(0, 2, 1)[:, :, None, :])
    cb = jnp.concatenate([cb_re, -cb_im], axis=-1).reshape(g, c * c, 2 * S5_STATE)
    pw = jnp.concatenate([p_re[:t], p_im[:t]], axis=-1).transpose(1, 2, 0)
    kvec = jnp.einsum('gkp,gpt->gkt', cb, pw, precision=hi).reshape(g, c, c // 2, 2 * t)
    r_re, r_im = power(t - 1 - jnp.arange(t))
    st_re = r_re[:, :, :, None] * bb_re - r_im[:, :, :, None] * bb_im
    st_im = r_re[:, :, :, None] * bb_im + r_im[:, :, :, None] * bb_re
    state_map = jnp.concatenate([st_re.transpose(1, 3, 0, 2).reshape(g, c * t, S5_STATE),
                                 st_im.transpose(1, 3, 0, 2).reshape(g, c * t, S5_STATE)],
                                axis=2).astype(BF16)
    q_re, q_im = p_re[1:], p_im[1:]
    ca_re = c_re[:, :, None, :] * q_re.transpose(1, 0, 2)[:, None] - c_im[:, :, None, :] * q_im.transpose(1, 0, 2)[:, None]
    ca_im = c_re[:, :, None, :] * q_im.transpose(1, 0, 2)[:, None] + c_im[:, :, None, :] * q_re.transpose(1, 0, 2)[:, None]
    cross = jnp.concatenate([ca_re.transpose(0, 3, 1, 2).reshape(g, S5_STATE, c * t),
                             -ca_im.transpose(0, 3, 1, 2).reshape(g, S5_STATE, c * t)], axis=1).astype(BF16)
    n_steps = max(1, (n_chunks - 1).bit_length())
    s_re, s_im = power(t * (2 ** jnp.arange(n_steps)))
    tab = jnp.stack([jnp.concatenate([s_re, s_re], axis=-1),
                     jnp.concatenate([-s_im, s_im], axis=-1)], axis=1)
    tab = tab.transpose(2, 0, 1, 3).reshape(g, 2 * n_steps, 2 * S5_STATE)
    return kvec, state_map, cross, tab, n_steps


def _s5_conv_kernel(n_chunks, n_steps, u_ref, kvec_ref, st_ref, cross_ref, tab_ref, y_ref, conv):
    t = S5_CHUNK
    row = lax.broadcasted_iota(jnp.int32, (t, LANES), 0)
    lane = lax.broadcasted_iota(jnp.int32, (t, LANES), 1)
    causal = jnp.bitwise_and(lane, t - 1) >= row
    for cin in range(S5_GROUP):
        for cp in range(S5_GROUP // 2):
            base = jnp.broadcast_to(kvec_ref[0, cin, cp:cp + 1, :], (t, LANES))
            blk = pltpu.roll(base, 0, 1, stride=1, stride_axis=0)
            conv[cin * t:(cin + 1) * t, cp * LANES:(cp + 1) * LANES] = jnp.where(causal, blk, 0.0).astype(BF16)
    u = u_ref[0]
    y = _dot(u, conv[...])
    x = _dot(u, st_ref[0])
    chunk = lax.broadcasted_iota(jnp.int32, x.shape, 0) % n_chunks
    half = x.shape[1] // 2
    for j in range(n_steps):
        sh = 1 << j
        xs = jnp.where(chunk >= sh, pltpu.roll(x, sh, 0), 0.0)
        x = x + tab_ref[0, 2 * j:2 * j + 1, :] * xs + tab_ref[0, 2 * j + 1:2 * j + 2, :] * pltpu.roll(xs, half, 1)
    x_in = jnp.where(chunk >= 1, pltpu.roll(x, 1, 0), 0.0)
    y_ref[0] = y + _dot(x_in.astype(BF16), cross_ref[0])


def _s5_conv(u2d, batch, seq, tables):
    kvec, state_map, cross, tab, n_steps = tables
    n, width = u2d.shape
    g = width // S5_GROUP
    t = S5_CHUNK
    n_chunks = seq // t
    ct = S5_GROUP * t
    ut = u2d.reshape(batch, n_chunks, t, g, S5_GROUP).transpose(3, 0, 1, 4, 2)
    ut = ut.reshape(g, batch * n_chunks, ct)
    rows = batch * n_chunks
    y = pl.pallas_call(
        functools.partial(_s5_conv_kernel, n_chunks, n_steps),
        grid=(g,),
        in_specs=[
            pl.BlockSpec((1, rows, ct), lambda i: (i, 0, 0)),
            pl.BlockSpec((1,) + kvec.shape[1:], lambda i: (i, 0, 0, 0)),
            pl.BlockSpec((1,) + state_map.shape[1:], lambda i: (i, 0, 0)),
            pl.BlockSpec((1,) + cross.shape[1:], lambda i: (i, 0, 0)),
            pl.BlockSpec((1,) + tab.shape[1:], lambda i: (i, 0, 0)),
        ],
        out_specs=pl.BlockSpec((1, rows, ct), lambda i: (i, 0, 0)),
        out_shape=jax.ShapeDtypeStruct((g, rows, ct), F32),
        scratch_shapes=[pltpu.VMEM((ct, ct), BF16)],
        compiler_params=_params("parallel"),
        name="s5_conv",
    )(ut, kvec, state_map, cross, tab)
    return y.reshape(g, batch, n_chunks, S5_GROUP, t).transpose(1, 2, 4, 0, 3).reshape(n, width)


def _pair_blockdiag(y, left):
    return jnp.concatenate([jnp.where(left, y, 0.0), jnp.where(left, 0.0, y)], axis=0).astype(BF16)


def _rwkv_chunk_kernel(r_ref, lw_ref, k_ref, v_ref, an_ref, bb_ref, rhat_o, yhat_o, g_o, h_o):
    t = RWKV_CHUNK
    rows, width = r_ref.shape
    n_chunks = rows // t
    pairs = width // LANES
    row = lax.broadcasted_iota(jnp.int32, (t, LANES), 0)
    lane = lax.broadcasted_iota(jnp.int32, (t, LANES), 1)
    col = jnp.bitwise_and(lane, RWKV_HEAD - 1)
    left = lane < RWKV_HEAD
    incl = row >= col
    strict = row > col
    same_blk = (row // RWKV_SUB) == (col // RWKV_SUB)
    eye = jnp.where(row == col, 1.0, 0.0).astype(F32)
    brow = lax.broadcasted_iota(jnp.int32, (LANES, LANES), 0)
    bcol = lax.broadcasted_iota(jnp.int32, (LANES, LANES), 1)
    same_head = (brow // RWKV_HEAD) == (bcol // RWKV_HEAD)
    eye_full = brow == bcol
    crow = lax.broadcasted_iota(jnp.int32, (rows, rows), 0)
    ccol = lax.broadcasted_iota(jnp.int32, (rows, rows), 1)
    tril = jnp.where(crow >= ccol, jnp.where((crow // t) == (ccol // t), 1.0, 0.0), 0.0).astype(BF16)

    lw = lw_ref[...]
    p1 = lw.astype(BF16)
    rem = lw - p1.astype(F32)
    p2 = rem.astype(BF16)
    p3 = (rem - p2.astype(F32)).astype(BF16)
    cs = _dot(tril, p1) + _dot(tril, p2) + _dot(tril, p3)
    a_t = an_ref[...] * jnp.exp(cs - lw)
    r_t = r_ref[...] * jnp.exp(cs)
    p_inv = jnp.exp(-cs)
    b_t = bb_ref[...] * p_inv
    k_t = k_ref[...] * p_inv

    def bd(y):
        return _pair_blockdiag(y, left)

    def pmm(x, y_bd):
        return _dot(x.astype(BF16), y_bd)

    units = [(c, j) for c in range(n_chunks) for j in range(pairs)]

    def tile(arr, c, j):
        return arr[c * t:(c + 1) * t, j * LANES:(j + 1) * LANES]

    a2 = [tile(a_t, c, j) for c, j in units]
    r2 = [tile(r_t, c, j) for c, j in units]
    v2 = [tile(v_ref[...], c, j) for c, j in units]
    ar = [jnp.concatenate([a, r], axis=0).astype(BF16) for a, r in zip(a2, r2)]
    prod = [_dot_nt(x, jnp.concatenate([bd(tile(b_t, c, j)), bd(tile(k_t, c, j))], axis=0))
            for x, (c, j) in zip(ar, units)]
    l_ab = [jnp.where(strict, p[:t, :LANES], 0.0) for p in prod]
    l_rb = [jnp.where(incl, p[t:, :LANES], 0.0) for p in prod]
    l_ak = [jnp.where(strict, p[:t, LANES:], 0.0) for p in prod]
    l_rk = [jnp.where(incl, p[t:, LANES:], 0.0) for p in prod]
    l_d = [jnp.where(same_blk, m, 0.0) for m in l_ab]
    l_o = [m - d for m, d in zip(l_ab, l_d)]
    l2 = [pmm(m, bd(m)) for m in l_d]
    q1 = [pmm(eye + m, bd(eye + s)) for m, s in zip(l_d, l2)]
    l4 = [pmm(m, bd(m)) for m in l2]
    q2 = [pmm(q, bd(eye + s)) for q, s in zip(q1, l4)]
    l8 = [pmm(m, bd(m)) for m in l4]
    d_inv = [pmm(q, bd(eye + s)) for q, s in zip(q2, l8)]
    akv = [pmm(jnp.concatenate([m, n], axis=0), bd(v)) for m, n, v in zip(l_ak, l_rk, v2)]
    x1 = [z[:t] for z in akv]

    def bd2(z1, z2):
        return jnp.concatenate([bd(z1), bd(z2)], axis=1)

    dz = [pmm(d, bd2(a, x)) for d, a, x in zip(d_inv, a2, x1)]
    n1 = [pmm(d, bd(o)) for d, o in zip(d_inv, l_o)]
    n2 = [pmm(m, bd(m)) for m in n1]
    t1 = [z + pmm(m, bd2(z[:, :LANES], z[:, LANES:])) for z, m in zip(dz, n2)]
    wu = [z + pmm(m, bd2(z[:, :LANES], z[:, LANES:])) for z, m in zip(t1, n1)]
    ry = [pmm(m, bd2(z[:, :LANES], z[:, LANES:])) for m, z in zip(l_rb, wu)]
    rk_v = [z[t:] for z in akv]

    for i, (c, j) in enumerate(units):
        rs = slice(c * t, (c + 1) * t)
        ls = slice(j * LANES, (j + 1) * LANES)
        rhat_o[rs, ls] = r2[i] + ry[i][:, :LANES]
        yhat_o[rs, ls] = ry[i][:, LANES:] + rk_v[i]
        cs_c = cs[rs, ls]
        cs_end = cs_c[t - 1:t, :]
        p_end = jnp.exp(cs_end - cs_c)
        b_h = (bb_ref[rs, ls] * p_end).astype(BF16)
        k_h = (k_ref[rs, ls] * p_end).astype(BF16)
        g_full = _dot_tn(wu[i][:, :LANES].astype(BF16), b_h)
        g_bd = jnp.where(same_head, g_full, 0.0) + jnp.where(eye_full, jnp.exp(cs_end), 0.0)
        g_o[c, j] = g_bd.astype(BF16)
        uv = jnp.concatenate([wu[i][:, LANES:], v2[i]], axis=0).astype(BF16)
        h_full = _dot_tn(uv, jnp.concatenate([b_h, k_h], axis=0))
        h_o[c, j] = jnp.where(left, h_full[:RWKV_HEAD], h_full[RWKV_HEAD:])


RWKV_CHUNKS_PER_STEP = 4


def _rwkv_chunks(r, lw, k, v, an, bb):
    n, width = r.shape
    t = RWKV_CHUNK
    cps = RWKV_CHUNKS_PER_STEP
    pairs = width // LANES
    nck = n // t
    tok = pl.BlockSpec((cps * t, width), lambda i: (i, 0))
    return pl.pallas_call(
        _rwkv_chunk_kernel,
        grid=(nck // cps,),
        in_specs=[tok] * 6,
        out_specs=[tok, tok,
                   pl.BlockSpec((cps, pairs, LANES, LANES), lambda i: (i, 0, 0, 0)),
                   pl.BlockSpec((cps, pairs, RWKV_HEAD, LANES), lambda i: (i, 0, 0, 0))],
        out_shape=[jax.ShapeDtypeStruct((n, width), F32)] * 2
        + [jax.ShapeDtypeStruct((nck, pairs, LANES, LANES), BF16),
           jax.ShapeDtypeStruct((nck, pairs, RWKV_HEAD, LANES), F32)],
        compiler_params=_params("parallel"),
        name="rwkv_chunk",
    )(r, lw, k, v, an, bb)


def _rwkv_scan_kernel(rhat_ref, yhat_ref, g_ref, h_ref, gate_ref, bonus_ref, lnw_ref, lnb_ref,
                      ones_ref, y_ref, state):
    t = RWKV_CHUNK
    batch, pairs = state.shape[0], state.shape[1]
    n_chunks = rhat_ref.shape[1] // t
    lane = lax.broadcasted_iota(jnp.int32, (RWKV_HEAD, LANES), 1)
    left = lane < RWKV_HEAD

    @pl.when(pl.program_id(0) == 0)
    def _():
        state[...] = jnp.zeros_like(state)

    s = [[state[b, j] for j in range(pairs)] for b in range(batch)]
    y_rows = []
    for b in range(batch):
        chunk_rows = []
        for c in range(n_chunks):
            rs = slice(c * t, (c + 1) * t)
            tiles = []
            for j in range(pairs):
                ls = slice(j * LANES, (j + 1) * LANES)
                s_bd = _pair_blockdiag(s[b][j], left)
                tiles.append(yhat_ref[b, rs, ls] + _dot_nt(rhat_ref[b, rs, ls].astype(BF16), s_bd))
                s[b][j] = _dot(s[b][j].astype(BF16), g_ref[b, c, j]) + h_ref[b, c, j]
            chunk_rows.append(jnp.concatenate(tiles, axis=1))
        y_rows.append(jnp.concatenate(chunk_rows, axis=0))
    for b in range(batch):
        for j in range(pairs):
            state[b, j] = s[b][j]

    ones = ones_ref[...]
    inv = 1.0 / RWKV_HEAD
    for b in range(batch):
        y = y_rows[b]
        cen = y - _dot(y.astype(BF16), ones) * inv
        var = _dot((cen * cen).astype(BF16), ones) * inv
        yn = cen * lax.rsqrt(var + RWKV_GN_EPS) * lnw_ref[...] + lnb_ref[...]
        y_ref[b] = (yn + bonus_ref[b]) * gate_ref[b]


def _rwkv_scan(rhat, yhat, g_mat, h_mat, gate, bonus, ln_w, ln_b, head_ones, batch, seq):
    n, width = rhat.shape
    t = RWKV_CHUNK
    cps = RWKV_CHUNKS_PER_STEP
    pairs = width // LANES
    nc = seq // t
    tok3 = lambda a: a.reshape(batch, seq, width)
    tok = pl.BlockSpec((batch, cps * t, width), lambda i: (0, i, 0))
    out = pl.pallas_call(
        _rwkv_scan_kernel,
        grid=(nc // cps,),
        in_specs=[tok, tok,
                  pl.BlockSpec((batch, cps, pairs, LANES, LANES), lambda i: (0, i, 0, 0, 0)),
                  pl.BlockSpec((batch, cps, pairs, RWKV_HEAD, LANES), lambda i: (0, i, 0, 0, 0)),
                  tok, tok, _full(ln_w.shape), _full(ln_b.shape), _full(head_ones.shape)],
        out_specs=tok,
        out_shape=jax.ShapeDtypeStruct((batch, seq, width), F32),
        scratch_shapes=[pltpu.VMEM((batch, pairs, RWKV_HEAD, LANES), F32)],
        compiler_params=_params("arbitrary"),
        name="rwkv_scan",
    )(tok3(rhat), tok3(yhat), g_mat.reshape(batch, nc, pairs, LANES, LANES),
      h_mat.reshape(batch, nc, pairs, RWKV_HEAD, LANES), tok3(gate), tok3(bonus), ln_w, ln_b, head_ones)
    return out.reshape(n, width)


def _post_mix_kernel(n_experts, x_ref, yc_ref, u_ref, yr_ref, d_ref, wglu_ref, wout_ref, gain_ref,
                     rw_ref, rb_ref, x1_o, hm_o, gate_o, lpos_o, meta_o, tot_o, running):
    tm = x_ref.shape[0]
    width = yc_ref.shape[1]

    @pl.when(pl.program_id(0) == 0)
    def _():
        running[...] = jnp.zeros_like(running)

    y = yc_ref[...] + d_ref[...] * u_ref[...]
    y = 0.5 * y * (1.0 + jnp.tanh(0.7978845608028654 * (y + 0.044715 * (y * y * y))))
    y = y * _sigmoid(_dot(y.astype(BF16), wglu_ref[...]))
    x1 = (x_ref[...] + _dot(y.astype(BF16), wout_ref[:width, :])
          + _dot(yr_ref[...].astype(BF16), wout_ref[width:, :]))
    x1_o[...] = x1
    hm = _rms(x1, gain_ref[...])
    hm_o[...] = hm.astype(BF16)
    logits = _dot(hm.astype(BF16), rw_ref[...]) + rb_ref[...]

    lane = lax.broadcasted_iota(jnp.int32, logits.shape, 1)
    lanef = lane.astype(F32)
    sel = jnp.zeros(logits.shape, F32)
    idx_cols, val_cols = [], []
    work = logits
    for _ in range(TOP_K):
        m = jnp.max(work, axis=-1, keepdims=True)
        pick = jnp.min(jnp.where(work == m, lanef, float(LANES)), axis=-1, keepdims=True)
        hit = lanef == pick
        sel = jnp.where(hit, 1.0, sel)
        work = jnp.where(hit, -jnp.inf, work)
        idx_cols.append(pick)
        val_cols.append(m)
    exps = [jnp.exp(vv - val_cols[0]) for vv in val_cols]
    denom = exps[0] + exps[1] + exps[2] + exps[3]

    row = lax.broadcasted_iota(jnp.int32, (tm, tm), 0)
    col = lax.broadcasted_iota(jnp.int32, (tm, tm), 1)
    before = jnp.where(row > col, 1.0, 0.0).astype(BF16)
    local = _dot(before, sel.astype(BF16))
    cnt = jnp.sum(sel, axis=0, keepdims=True)
    cnt_al = jnp.floor((cnt + (RUN_ALIGN - 1)) * (1.0 / RUN_ALIGN)) * RUN_ALIGN
    erow = lax.broadcasted_iota(jnp.int32, (LANES, LANES), 0)
    ecol = lax.broadcasted_iota(jnp.int32, (LANES, LANES), 1)
    upper = jnp.where(erow < ecol, 1.0, 0.0).astype(BF16)
    toff = _dot(jnp.broadcast_to(cnt_al, (SUBLANES, LANES)).astype(BF16), upper)[0:1]
    tbase = running[...]
    gate_out = jnp.zeros(logits.shape, F32)
    lpos_out = jnp.zeros(logits.shape, F32)
    for j in range(TOP_K):
        lp = jnp.sum(jnp.where(lanef == idx_cols[j], local + toff, 0.0), axis=-1, keepdims=True)
        gate_out = jnp.where(lane == j, exps[j] / denom, gate_out)
        lpos_out = jnp.where(lane == j, lp, lpos_out)
    gate_o[...] = gate_out
    lpos_o[...] = lpos_out.astype(jnp.int32)
    srow = lax.broadcasted_iota(jnp.int32, (SUBLANES, LANES), 0)
    meta = jnp.where(srow == 0, cnt, jnp.where(srow == 1, tbase, jnp.where(srow == 2, toff, 0.0)))
    meta_o[...] = meta.astype(jnp.int32)
    running[...] = tbase + cnt_al
    tot_o[...] = jnp.broadcast_to(running[...], tot_o.shape).astype(jnp.int32)
    del n_experts


def _post_mix(x2d, yconv, u, y_rwkv, s5_d, w_glu, w_out, gain, rw, rb, n_experts):
    n, d = x2d.shape
    width = yconv.shape[1]
    tm = MOE_TILE
    row = lambda i: (i, 0)
    tok_d = pl.BlockSpec((tm, d), row)
    tok_w = pl.BlockSpec((tm, width), row)
    tok_l = pl.BlockSpec((tm, LANES), row)
    return pl.pallas_call(
        functools.partial(_post_mix_kernel, n_experts),
        grid=(n // tm,),
        in_specs=[tok_d, tok_w, tok_w, tok_w, _full(s5_d.shape), _full(w_glu.shape),
                  _full(w_out.shape), _full(gain.shape), _full(rw.shape), _full(rb.shape)],
        out_specs=[tok_d, tok_d, tok_l, tok_l,
                   pl.BlockSpec((SUBLANES, LANES), row), _full((SUBLANES, LANES))],
        out_shape=[jax.ShapeDtypeStruct((n, d), F32), jax.ShapeDtypeStruct((n, d), BF16),
                   jax.ShapeDtypeStruct((n, LANES), F32), jax.ShapeDtypeStruct((n, LANES), jnp.int32),
                   jax.ShapeDtypeStruct((n // tm * SUBLANES, LANES), jnp.int32),
                   jax.ShapeDtypeStruct((SUBLANES, LANES), jnp.int32)],
        scratch_shapes=[pltpu.VMEM((1, LANES), F32)],
        compiler_params=_params("arbitrary"),
        name="post_mix",
    )(x2d, yconv, u, y_rwkv, s5_d, w_glu, w_out, gain, rw, rb)


def _sorted_rows(tm, n_experts):
    return -(-(tm * TOP_K + n_experts * (RUN_ALIGN - 1)) // LANES) * LANES


RUN_BIG = 8


def _groups(count):
    return lax.shift_right_logical(count + (RUN_ALIGN - 1), RUN_ALIGN.bit_length() - 1)


def _start_runs(n_experts, pstart_ref, meta_ref, make_copy):
    shift = RUN_BIG.bit_length() - 1
    for e in range(n_experts):
        groups = _groups(meta_ref[0, e])
        seg = pstart_ref[e] + meta_ref[1, e]
        loc = meta_ref[2, e]
        n_big = lax.shift_right_logical(groups, shift)
        rem = groups & (RUN_BIG - 1)

        def body(g, c, seg=seg, loc=loc):
            off = g * (RUN_BIG * RUN_ALIGN)
            make_copy(pl.multiple_of(loc + off, RUN_ALIGN), pl.multiple_of(seg + off, RUN_ALIGN),
                      RUN_BIG * RUN_ALIGN).start(priority=e % 2)
            return c

        lax.fori_loop(0, n_big, body, 0)
        size = RUN_BIG // 2
        while size >= 1:
            off = (n_big * RUN_BIG + (rem & (RUN_BIG - 2 * size))) * RUN_ALIGN

            @pl.when((rem & size) != 0)
            def _(off=off, size=size, seg=seg, loc=loc):
                make_copy(pl.multiple_of(loc + off, RUN_ALIGN), pl.multiple_of(seg + off, RUN_ALIGN),
                          size * RUN_ALIGN).start(priority=e % 2)

            size //= 2


def _wait_runs(n_experts, meta_ref, make_copy):
    groups = 0
    for e in range(n_experts):
        groups = groups + _groups(meta_ref[0, e])

    def wait_big(g, c):
        make_copy(0, 0, RUN_BIG * RUN_ALIGN).wait()
        return c

    def wait_small(g, c):
        make_copy(0, 0, RUN_ALIGN).wait()
        return c

    lax.fori_loop(0, lax.shift_right_logical(groups, RUN_BIG.bit_length() - 1), wait_big, 0)
    lax.fori_loop(0, groups & (RUN_BIG - 1), wait_small, 0)


def _dispatch_kernel(n_experts, pstart_ref, tot_ref, meta_ref, meta_prev_ref, lpos_ref, hm_ref, xs_ref,
                     srt, zero, sem):
    tm = hm_ref.shape[0]
    rows = srt.shape[1]
    i = pl.program_id(0)
    cur = i % 2
    pos_t = jnp.transpose(lpos_ref[...].astype(F32))
    rid = lax.broadcasted_iota(jnp.int32, (rows, tm), 0).astype(F32)
    perm = jnp.zeros((rows, tm), F32)
    for j in range(TOP_K):
        perm = jnp.where(rid == pos_t[j:j + 1, :], 1.0, perm)
    srt[cur] = _dot(perm.astype(BF16), hm_ref[...])

    def copies(slot):
        def make_copy(loc, dst, size):
            return pltpu.make_async_copy(srt.at[slot, pl.ds(loc, size)], xs_ref.at[pl.ds(dst, size)],
                                         sem.at[slot])
        return make_copy

    _start_runs(n_experts, pstart_ref, meta_ref, copies(cur))

    @pl.when(i > 0)
    def _():
        _wait_runs(n_experts, meta_prev_ref, copies(1 - cur))

    @pl.when(i == pl.num_programs(0) - 1)
    def _():
        _wait_runs(n_experts, meta_ref, copies(cur))

    @pl.when(pl.program_id(0) == pl.num_programs(0) - 1)
    def _():
        zero[...] = jnp.zeros_like(zero)
        for wait in (False, True):
            for e in range(n_experts):
                used = tot_ref[e]
                start = pstart_ref[e] + used
                groups = lax.shift_right_logical((-used) & (EXPERT_TILE - 1), RUN_ALIGN.bit_length() - 1)

                def body(g, c, start=start):
                    cp = pltpu.make_async_copy(
                        zero, xs_ref.at[pl.ds(pl.multiple_of(start + g * RUN_ALIGN, RUN_ALIGN), RUN_ALIGN)],
                        sem.at[0])
                    if wait:
                        cp.wait()
                    else:
                        cp.start()
                    return c

                lax.fori_loop(0, groups, body, 0)

        last = n_experts - 1
        end = pstart_ref[last] + tot_ref[last] + ((-tot_ref[last]) & (EXPERT_TILE - 1))
        srt[0, 0:EXPERT_TILE, :] = jnp.zeros((EXPERT_TILE, srt.shape[2]), srt.dtype)
        for wait in (False, True):
            def tail(b, c):
                cp = pltpu.make_async_copy(
                    srt.at[0, pl.ds(0, EXPERT_TILE)],
                    xs_ref.at[pl.ds(pl.multiple_of(end + b * EXPERT_TILE, EXPERT_TILE), EXPERT_TILE)], sem.at[0])
                if wait:
                    cp.wait()
                else:
                    cp.start()
                return c

            lax.fori_loop(0, lax.shift_right_logical(xs_ref.shape[0] - end, EXPERT_TILE.bit_length() - 1), tail, 0)


def _dispatch(hm, lpos, meta, pstarts, totals, m_pad, n_experts):
    n, dh = hm.shape
    tm = MOE_TILE
    rows = _sorted_rows(tm, n_experts)
    grid_spec = pltpu.PrefetchScalarGridSpec(
        num_scalar_prefetch=2,
        grid=(n // tm,),
        in_specs=[pl.BlockSpec((SUBLANES, LANES), lambda i, ps, tt: (i, 0), memory_space=pltpu.SMEM),
                  pl.BlockSpec((SUBLANES, LANES), lambda i, ps, tt: (jnp.maximum(i - 1, 0), 0),
                               memory_space=pltpu.SMEM),
                  pl.BlockSpec((tm, LANES), lambda i, ps, tt: (i, 0)),
                  pl.BlockSpec((tm, dh), lambda i, ps, tt: (i, 0))],
        out_specs=pl.BlockSpec(memory_space=pl.ANY),
        scratch_shapes=[pltpu.VMEM((2, rows, dh), F32), pltpu.VMEM((RUN_ALIGN, dh), F32),
                        pltpu.SemaphoreType.DMA((2,))],
    )
    return pl.pallas_call(
        functools.partial(_dispatch_kernel, n_experts),
        grid_spec=grid_spec,
        out_shape=jax.ShapeDtypeStruct((m_pad, dh), F32),
        compiler_params=_params("arbitrary"),
        name="dispatch",
    )(pstarts, totals, meta, meta, lpos, hm)


def _experts_kernel(be_ref, nu_ref, xs_ref, wg_ref, bg_ref, wu_ref, bu_ref, wd_ref, bd_ref, ys_ref,
                    wg_s, wu_s, wd_s):
    i = pl.program_id(0)
    changed = jnp.logical_or(i == 0, be_ref[i] != be_ref[jnp.maximum(i - 1, 0)])

    @pl.when(changed)
    def _():
        wg_s[...] = wg_ref[0].astype(BF16)
        wu_s[...] = wu_ref[0].astype(BF16)
        wd_s[...] = wd_ref[0].astype(BF16)

    @pl.when(i < nu_ref[0])
    def _():
        xb = xs_ref[...].astype(BF16)
        gt = jnp.minimum(_dot(xb, wg_s[...]) + bg_ref[0], SWIGLU_LIMIT)
        up = jnp.clip(_dot(xb, wu_s[...]) + bu_ref[0], -SWIGLU_LIMIT, SWIGLU_LIMIT)
        act = (up + 1.0) * gt * _sigmoid(SWIGLU_ALPHA * gt)
        ys_ref[...] = _dot(act.astype(BF16), wd_s[...]) + bd_ref[0]

    @pl.when(i >= nu_ref[0])
    def _():
        ys_ref[...] = jnp.zeros_like(ys_ref)


def _experts(xs, block_e, n_used, wg, bg, wu, bu, wd, bd):
    m_pad, dh = xs.shape
    tmb = EXPERT_TILE
    d, dff = wg.shape[1], wg.shape[2]
    wmap = lambda i, be, nu: (be[i], 0, 0)
    grid_spec = pltpu.PrefetchScalarGridSpec(
        num_scalar_prefetch=2,
        grid=(m_pad // tmb,),
        in_specs=[pl.BlockSpec((tmb, dh), lambda i, be, nu: (jnp.where(i < nu[0], i, 0), 0)),
                  pl.BlockSpec((1, d, dff), wmap), pl.BlockSpec((1, 1, dff), wmap),
                  pl.BlockSpec((1, d, dff), wmap), pl.BlockSpec((1, 1, dff), wmap),
                  pl.BlockSpec((1, dff, d), wmap), pl.BlockSpec((1, 1, d), wmap)],
        out_specs=pl.BlockSpec((tmb, dh), lambda i, be, nu: (i, 0)),
        scratch_shapes=[pltpu.VMEM((d, dff), BF16), pltpu.VMEM((d, dff), BF16), pltpu.VMEM((dff, d), BF16)],
    )
    return pl.pallas_call(
        _experts_kernel,
        grid_spec=grid_spec,
        out_shape=jax.ShapeDtypeStruct((m_pad, dh), F32),
        compiler_params=_params("arbitrary"),
        name="experts",
    )(block_e, n_used, xs, wg, bg, wu, bu, wd, bd)


def _final_kernel(n_experts, pstart_ref, meta_ref, meta_next_ref, lpos_ref, x1_ref, gate_ref, p_ref, ys_ref,
                  gple_ref, wproj_ref, wgate_ref, gfin_ref, out_ref, buf, sem):
    tm = x1_ref.shape[0]
    rows = buf.shape[1]
    i = pl.program_id(0)
    cur = i % 2

    def copies(slot):
        def make_copy(loc, src, size):
            return pltpu.make_async_copy(ys_ref.at[pl.ds(src, size)], buf.at[slot, pl.ds(loc, size)],
                                         sem.at[slot])
        return make_copy

    @pl.when(i == 0)
    def _():
        buf[...] = jnp.zeros_like(buf)
        _start_runs(n_experts, pstart_ref, meta_ref, copies(0))

    @pl.when(i + 1 < pl.num_programs(0))
    def _():
        _start_runs(n_experts, pstart_ref, meta_next_ref, copies(1 - cur))

    cid = lax.broadcasted_iota(jnp.int32, (tm, rows), 1)
    lpos = lpos_ref[...]
    gates = gate_ref[...]
    comb = jnp.zeros((tm, rows), F32)
    for j in range(TOP_K):
        comb = jnp.where(cid == lpos[:, j:j + 1], gates[:, j:j + 1], comb)

    _wait_runs(n_experts, meta_ref, copies(cur))
    x2 = x1_ref[...] + _dot(comb.astype(BF16), buf[cur].astype(BF16))
    gate = _sigmoid(_dot(_rms(x2, gple_ref[...]).astype(BF16), wgate_ref[...]))
    x3 = x2 + _dot(p_ref[...].astype(BF16), wproj_ref[...]) * gate
    out_ref[...] = _rms(x3, gfin_ref[...])


def _final(x1, gates, lpos, meta, pstarts, p2d, ys, g_ple, w_proj, w_gate, g_fin, n_experts):
    n, d = x1.shape
    tm = MOE_TILE
    steps = n // tm
    rows = _sorted_rows(tm, n_experts)
    row = lambda i, ps: (i, 0)
    const = lambda shape: pl.BlockSpec(shape, lambda i, ps: (0,) * len(shape))
    grid_spec = pltpu.PrefetchScalarGridSpec(
        num_scalar_prefetch=1,
        grid=(steps,),
        in_specs=[pl.BlockSpec((SUBLANES, LANES), row, memory_space=pltpu.SMEM),
                  pl.BlockSpec((SUBLANES, LANES), lambda i, ps: (jnp.minimum(i + 1, steps - 1), 0),
                               memory_space=pltpu.SMEM),
                  pl.BlockSpec((tm, LANES), row), pl.BlockSpec((tm, d), row), pl.BlockSpec((tm, LANES), row),
                  pl.BlockSpec((tm, p2d.shape[1]), row), pl.BlockSpec(memory_space=pl.ANY),
                  const(g_ple.shape), const(w_proj.shape), const(w_gate.shape), const(g_fin.shape)],
        out_specs=pl.BlockSpec((tm, d), row),
        scratch_shapes=[pltpu.VMEM((2, rows, ys.shape[1]), ys.dtype), pltpu.SemaphoreType.DMA((2,))],
    )
    return pl.pallas_call(
        functools.partial(_final_kernel, n_experts),
        grid_spec=grid_spec,
        out_shape=jax.ShapeDtypeStruct((n, d), F32),
        compiler_params=_params("arbitrary"),
        name="final",
    )(pstarts, meta, meta, lpos, x1, gates, p2d, ys, g_ple, w_proj, w_gate, g_fin)


def _head_ones(width):
    hd = jnp.arange(width) // RWKV_HEAD
    return (hd[:, None] == hd[None, :]).astype(BF16)


def _layer(x2d, p2d, batch, seq, norm_mix_g, w_in, s5_lam_re, s5_lam_im, s5_log_dt, s5_b_re, s5_b_im,
           s5_c_re, s5_c_im, s5_d, s5_w_glu, mu_rkv, mu_wag, w0, w1, w2, a0, a1, a2, g1, g2, k_k, k_a,
           r_k, ln_w, ln_b, w_out, norm_moe_g, router_w, router_b, wg, bg, wu, bu, wd, bd,
           norm_ple_g, ple_w_proj, ple_w_gate):
    n, d = x2d.shape
    width = w_in.shape[1] // 4
    n_experts = router_w.shape[1]
    row2 = lambda t: t.reshape(1, -1).astype(F32)
    head_ones = _head_ones(width)
    vecs = jnp.zeros((SUBLANES, width), F32)
    vecs = vecs.at[0].set(w0).at[1].set(a0).at[2].set(k_k).at[3].set(k_a).at[4].set(r_k.reshape(-1))

    w_ext, w2_cat = _mix_in_weights(w_in, w1, a1, g1, mu_wag, w2, a2, g2)
    u, r, lw, k, v, an, bb, gate, bonus, u_bf = _mix_in(
        x2d, seq, row2(norm_mix_g), w_ext, mu_rkv.astype(F32), vecs, w2_cat, head_ones)

    tables = _s5_tables(s5_lam_re, s5_lam_im, s5_log_dt, s5_b_re, s5_b_im, s5_c_re, s5_c_im,
                        seq // S5_CHUNK)
    yconv = _s5_conv(u_bf, batch, seq, tables)

    rhat, yhat, g_mat, h_mat = _rwkv_chunks(r, lw, k, v, an, bb)
    y_rwkv = _rwkv_scan(rhat, yhat, g_mat, h_mat, gate, bonus, row2(ln_w), row2(ln_b), head_ones,
                        batch, seq)

    rw = jnp.zeros((d, LANES), BF16).at[:, :n_experts].set(router_w.astype(BF16))
    rb = jnp.full((1, LANES), NEG_BIG, F32).at[0, :n_experts].set(router_b.astype(F32))
    x1, hm, gates, lpos, meta, totals = _post_mix(
        x2d, yconv, u, y_rwkv, row2(s5_d), s5_w_glu.astype(BF16), w_out.astype(BF16),
        row2(norm_moe_g), rw, rb, n_experts)

    tmb = EXPERT_TILE
    n_tiles = n // MOE_TILE
    max_rows = n * TOP_K + n_tiles * n_experts * (RUN_ALIGN - 1)
    n_blocks = -(-max_rows // tmb) + n_experts
    seg = totals[0, :n_experts].astype(jnp.int32)
    padded = ((seg + tmb - 1) // tmb) * tmb
    pends = jnp.cumsum(padded)
    pstarts = (pends - padded).astype(jnp.int32)
    block_start = jnp.arange(n_blocks, dtype=jnp.int32) * tmb
    block_e = jnp.minimum(jnp.sum((pends[None, :] <= block_start[:, None]).astype(jnp.int32), axis=1),
                          n_experts - 1).astype(jnp.int32)
    n_used = (pends[-1] // tmb).astype(jnp.int32).reshape(1)

    xs = _dispatch(hm, lpos, meta, pstarts, seg, n_blocks * tmb, n_experts)
    ys = _experts(xs, block_e, n_used, wg, bg.reshape(n_experts, 1, -1).astype(F32),
                  wu, bu.reshape(n_experts, 1, -1).astype(F32), wd,
                  bd.reshape(n_experts, 1, -1).astype(F32))
    return (x1, gates, lpos, meta, pstarts, ys, p2d, row2(norm_ple_g), ple_w_proj.astype(BF16),
            ple_w_gate.astype(BF16), n_experts)


def kernel(x, p, norm_mix_g, w_in, s5_lam_re, s5_lam_im, s5_log_dt, s5_b_re, s5_b_im, s5_c_re, s5_c_im, s5_d, s5_w_glu, rwkv_mu_rkv, rwkv_mu_wag, rwkv_w0, rwkv_w1, rwkv_w2, rwkv_a0, rwkv_a1, rwkv_a2, rwkv_g1, rwkv_g2, rwkv_k_k, rwkv_k_a, rwkv_r_k, rwkv_ln_w, rwkv_ln_b, w_out, norm_moe_g, router_w, router_b, exp_w_gate, exp_b_gate, exp_w_up, exp_b_up, exp_w_down, exp_b_down, norm_ple_g, ple_w_proj, ple_w_gate, final_norm_g):
    batch, seq, d = x.shape
    assert w_in.shape[0] == 1, "the final kernel fuses the last RMSNorm: single-layer stacks only"
    i = 0
    x1, gates, lpos, meta, pstarts, ys, p2d, g_ple, w_proj, w_gate, n_experts = _layer(
        x.reshape(batch * seq, d), p[i].reshape(batch * seq, -1), batch, seq, norm_mix_g[i], w_in[i],
        s5_lam_re[i], s5_lam_im[i], s5_log_dt[i], s5_b_re[i], s5_b_im[i], s5_c_re[i], s5_c_im[i],
        s5_d[i], s5_w_glu[i], rwkv_mu_rkv[i], rwkv_mu_wag[i], rwkv_w0[i], rwkv_w1[i], rwkv_w2[i],
        rwkv_a0[i], rwkv_a1[i], rwkv_a2[i], rwkv_g1[i], rwkv_g2[i], rwkv_k_k[i], rwkv_k_a[i],
        rwkv_r_k[i], rwkv_ln_w[i], rwkv_ln_b[i], w_out[i], norm_moe_g[i], router_w[i],
        router_b[i], exp_w_gate[i], exp_b_gate[i], exp_w_up[i], exp_b_up[i], exp_w_down[i],
        exp_b_down[i], norm_ple_g[i], ple_w_proj[i], ple_w_gate[i])
    out = _final(x1, gates, lpos, meta, pstarts, p2d, ys, g_ple, w_proj, w_gate,
                 final_norm_g.reshape(1, -1).astype(F32), n_experts)
    return out.reshape(batch, seq, d)
```

```python
import functools

import jax
import jax.numpy as jnp
from jax import lax
from jax.experimental import pallas as pl
from jax.experimental.pallas import tpu as pltpu

F32 = jnp.float32
BF16 = jnp.bfloat16

S5_GROUP = 16
S5_STATE = 64
RWKV_HEAD = 64
DECAY_LORA = 64
ICLR_LORA = 64
DECAY_SCALE = 0.6065306597126334
TOP_K = 4
RMS_EPS = 1e-6
RWKV_GN_EPS = 64e-5
LAMBDA_RE_MAX = -1e-4
SWIGLU_LIMIT = 7.0
SWIGLU_ALPHA = 1.702

LANES = 128
SUBLANES = 8
VMEM_LIMIT = 56 * 1024 * 1024

S5_CHUNK = 64
RWKV_CHUNK = 64
RWKV_SUB = 16
TOKEN_TILE = 512
EXPERT_TILE = 512
MOE_TILE = 512
RUN_ALIGN = 8
NEG_BIG = -1e30


def _dot(a, b):
    return jnp.dot(a, b, preferred_element_type=F32)


def _dot_nt(a, b):
    return lax.dot_general(a, b, (((1,), (1,)), ((), ())), preferred_element_type=F32)


def _dot_tn(a, b):
    return lax.dot_general(a, b, (((0,), (0,)), ((), ())), preferred_element_type=F32)


def _rms(t, gain):
    return t * lax.rsqrt(jnp.mean(t * t, axis=-1, keepdims=True) + RMS_EPS) * gain


def _sigmoid(t):
    return 1.0 / (1.0 + jnp.exp(-t))


def _params(*sem):
    return pltpu.CompilerParams(dimension_semantics=sem, vmem_limit_bytes=VMEM_LIMIT)


def _full(shape):
    return pl.BlockSpec(shape, lambda *_: (0,) * len(shape))


def _mix_in_kernel(seq, x_ref, xp_ref, gain_ref, win_ref, murkv_ref, vec_ref, w2_ref, ones_ref,
                   u_o, r_o, lw_o, k_o, v_o, an_o, bb_o, g_o, bonus_o, ub_o):
    tm = x_ref.shape[0]
    width = u_o.shape[1]
    lora = w2_ref.shape[0]
    gain = gain_ref[...]
    h = _rms(x_ref[...], gain)
    keep = jnp.where((pl.program_id(0) * tm) % seq == 0, 0.0, 1.0)
    hp = _rms(xp_ref[...], gain) * keep
    h_ext = jnp.concatenate([hp, h], axis=0).astype(BF16)
    proj = _dot(h_ext, win_ref[...])
    shifted = pltpu.roll(proj[:, width:4 * width + lora], 1, 0)[SUBLANES:]
    cur = proj[SUBLANES:]
    u_o[...] = cur[:, :width]
    ub_o[...] = cur[:, :width].astype(BF16)

    l1 = cur[:, 4 * width + lora:] + shifted[:, 3 * width:]
    lane = lax.broadcasted_iota(jnp.int32, l1.shape, 1)
    hidden = jnp.where(lane < DECAY_LORA, jnp.tanh(l1),
                       jnp.where(lane < DECAY_LORA + ICLR_LORA, l1, _sigmoid(l1))).astype(BF16)
    z3 = _dot(hidden, w2_ref[...])

    w0 = vec_ref[0:1, :]
    a0 = vec_ref[1:2, :]
    k_k = vec_ref[2:3, :]
    k_a = vec_ref[3:4, :]
    r_k = vec_ref[4:5, :]
    lw_o[...] = (-DECAY_SCALE) * _sigmoid(w0 + z3[:, :width])
    a = _sigmoid(a0 + z3[:, width:2 * width])
    g_o[...] = z3[:, 2 * width:]

    r_cur = cur[:, width:2 * width]
    k_cur = cur[:, 2 * width:3 * width]
    v_cur = cur[:, 3 * width:4 * width]
    r = r_cur + (shifted[:, :width] - r_cur) * murkv_ref[0:1, :]
    k = k_cur + (shifted[:, width:2 * width] - k_cur) * murkv_ref[1:2, :]
    v = v_cur + (shifted[:, 2 * width:3 * width] - v_cur) * murkv_ref[2:3, :]

    ones = ones_ref[...]
    kk = k * k_k
    ssq = _dot((kk * kk).astype(BF16), ones)
    kkn = kk * jnp.minimum(lax.rsqrt(ssq), 1e12)
    k2 = k * (1.0 + (a - 1.0) * k_a)
    r_o[...] = r
    k_o[...] = k2
    v_o[...] = v
    an_o[...] = -kkn
    bb_o[...] = kkn * a
    bonus_o[...] = _dot((r * k2 * r_k).astype(BF16), ones) * v


def _mix_in(x2d, seq, gain, w_ext, mu_rkv, vecs, w2_cat, head_ones):
    n, d = x2d.shape
    width = w2_cat.shape[1] // 3
    tm = TOKEN_TILE
    prev_blocks = tm // SUBLANES
    row = lambda i: (i, 0)
    out = jax.ShapeDtypeStruct((n, width), F32)
    return pl.pallas_call(
        functools.partial(_mix_in_kernel, seq),
        grid=(n // tm,),
        in_specs=[
            pl.BlockSpec((tm, d), row),
            pl.BlockSpec((SUBLANES, d), lambda i: (jnp.maximum(i * prev_blocks - 1, 0), 0)),
            _full(gain.shape), _full(w_ext.shape), _full(mu_rkv.shape), _full(vecs.shape),
            _full(w2_cat.shape), _full(head_ones.shape),
        ],
        out_specs=[pl.BlockSpec((tm, width), row)] * 10,
        out_shape=[out] * 9 + [jax.ShapeDtypeStruct((n, width), BF16)],
        compiler_params=_params("parallel"),
        name="mix_in",
    )(x2d, x2d, gain, w_ext, mu_rkv, vecs, w2_cat, head_ones)


def _mix_in_weights(w_in, w1, a1, g1, mu_wag, w2, a2, g2):
    l1 = jnp.concatenate([w1, a1, g1], axis=1).astype(F32)
    mu = jnp.concatenate([jnp.broadcast_to(mu_wag[j][:, None], (w.shape[0], w.shape[1]))
                          for j, w in enumerate((w1, a1, g1))], axis=1).astype(F32)
    w_ext = jnp.concatenate([w_in.astype(F32), mu * l1, (1.0 - mu) * l1], axis=1).astype(BF16)
    width = w2.shape[1]
    z = lambda rows: jnp.zeros((rows, width), F32)
    w2_cat = jnp.concatenate([
        jnp.concatenate([w2.astype(F32), z(w2.shape[0]), z(w2.shape[0])], axis=1),
        jnp.concatenate([z(a2.shape[0]), a2.astype(F32), z(a2.shape[0])], axis=1),
        jnp.concatenate([z(g2.shape[0]), z(g2.shape[0]), g2.astype(F32)], axis=1)], axis=0).astype(BF16)
    return w_ext, w2_cat


def _s5_tables(lam_re, lam_im, log_dt, b_re, b_im, c_re, c_im, n_chunks):
    t = S5_CHUNK
    hi = lax.Precision.HIGHEST
    lre = jnp.minimum(lam_re.astype(F32), LAMBDA_RE_MAX)
    lim = lam_im.astype(F32)
    dt = jnp.exp(log_dt.astype(F32))[:, None]
    mag = jnp.exp(lre * dt)
    lb_re = mag * jnp.cos(lim * dt)
    lb_im = mag * jnp.sin(lim * dt)
    den = lre * lre + lim * lim
    z_re = lb_re - 1.0
    coef_re = ((z_re * lre + lb_im * lim) / den)[..., None]
    coef_im = ((lb_im * lre - z_re * lim) / den)[..., None]
    b_re = b_re.astype(F32)
    b_im = b_im.astype(F32)
    bb_re = coef_re * b_re - coef_im * b_im
    bb_im = coef_re * b_im + coef_im * b_re
    c_re = c_re.astype(F32)
    c_im = c_im.astype(F32)

    def power(e):
        e = e.astype(F32)[:, None, None]
        m = jnp.exp(e * (lre * dt))
        ang = e * (lim * dt)
        return m * jnp.cos(ang), m * jnp.sin(ang)

    g = lre.shape[0]
    c = S5_GROUP
    p_re, p_im = power(jnp.arange(t + 1))
    cb_re = (c_re[:, None, :, :] * bb_re.transpose(0, 2, 1)[:, :, None, :]
             - c_im[:, None, :, :] * bb_im.transpose(0, 2, 1)[:, :, None, :])
    cb_im = (c_re[:, None, :, :] * bb_im.transpose(0, 2, 1)[:, :, None, :]
             + c_im[:, None, :, :] * bb_re.transpose(0, 2, 1)[:, :, None, :])
    cb = jnp.concatenate([cb_re, -cb_im], axis=-1).reshape(g, c * c, 2 * S5_STATE)
    pw = jnp.concatenate([p_re[:t], p_im[:t]], axis=-1).transpose(1, 2, 0)
    kvec = jnp.einsum('gkp,gpt->gkt', cb, pw, precision=hi).reshape(g, c, c // 2, 2 * t)
    r_re, r_im = power(t - 1 - jnp.arange(t))
    st_re = r_re[:, :, :, None] * bb_re - r_im[:, :, :, None] * bb_im
    st_im = r_re[:, :, :, None] * bb_im + r_im[:, :, :, None] * bb_re
    state_map = jnp.concatenate([st_re.transpose(1, 3, 0, 2).reshape(g, c * t, S5_STATE),
                                 st_im.transpose(1, 3, 0, 2).reshape(g, c * t, S5_STATE)],
                                axis=2).astype(BF16)
    q_re, q_im = p_re[1:], p_im[1:]
    ca_re = c_re[:, :, None, :] * q_re.transpose(1, 0, 2)[:, None] - c_im[:, :, None, :] * q_im.transpose(1, 0, 2)[:, None]
    ca_im = c_re[:, :, None, :] * q_im.transpose(1, 0, 2)[:, None] + c_im[:, :, None, :] * q_re.transpose(1, 0, 2)[:, None]
    cross = jnp.concatenate([ca_re.transpose(0, 3, 1, 2).reshape(g, S5_STATE, c * t),
                             -ca_im.transpose(0, 3, 1, 2).reshape(g, S5_STATE, c * t)], axis=1).astype(BF16)
    n_steps = max(1, (n_chunks - 1).bit_length())
    s_re, s_im = power(t * (2 ** jnp.arange(n_steps)))
    tab = jnp.stack([jnp.concatenate([s_re, s_re], axis=-1),
                     jnp.concatenate([-s_im, s_im], axis=-1)], axis=1)
    tab = tab.transpose(2, 0, 1, 3).reshape(g, 2 * n_steps, 2 * S5_STATE)
    return kvec, state_map, cross, tab, n_steps


def _s5_conv_kernel(n_chunks, n_steps, u_ref, kvec_ref, st_ref, cross_ref, tab_ref, y_ref, conv):
    t = S5_CHUNK
    row = lax.broadcasted_iota(jnp.int32, (t, LANES), 0)
    lane = lax.broadcasted_iota(jnp.int32, (t, LANES), 1)
    causal = jnp.bitwise_and(lane, t - 1) >= row
    for cin in range(S5_GROUP):
        for cp in range(S5_GROUP // 2):
            base = jnp.broadcast_to(kvec_ref[0, cin, cp:cp + 1, :], (t, LANES))
            blk = pltpu.roll(base, 0, 1, stride=1, stride_axis=0)
            conv[cin * t:(cin + 1) * t, cp * LANES:(cp + 1) * LANES] = jnp.where(causal, blk, 0.0).astype(BF16)
    u = u_ref[0]
    y = _dot(u, conv[...])
    x = _dot(u, st_ref[0])
    chunk = lax.broadcasted_iota(jnp.int32, x.shape, 0) % n_chunks
    half = x.shape[1] // 2
    for j in range(n_steps):
        sh = 1 << j
        xs = jnp.where(chunk >= sh, pltpu.roll(x, sh, 0), 0.0)
        x = x + tab_ref[0, 2 * j:2 * j + 1, :] * xs + tab_ref[0, 2 * j + 1:2 * j + 2, :] * pltpu.roll(xs, half, 1)
    x_in = jnp.where(chunk >= 1, pltpu.roll(x, 1, 0), 0.0)
    y_ref[0] = (y + _dot(x_in.astype(BF16), cross_ref[0])).astype(y_ref.dtype)


def _s5_conv(u2d, batch, seq, tables):
    kvec, state_map, cross, tab, n_steps = tables
    n, width = u2d.shape
    g = width // S5_GROUP
    t = S5_CHUNK
    n_chunks = seq // t
    ct = S5_GROUP * t
    ut = u2d.reshape(batch, n_chunks, t, g, S5_GROUP).transpose(3, 0, 1, 4, 2)
    ut = ut.reshape(g, batch * n_chunks, ct)
    rows = batch * n_chunks
    y = pl.pallas_call(
        functools.partial(_s5_conv_kernel, n_chunks, n_steps),
        grid=(g,),
        in_specs=[
            pl.BlockSpec((1, rows, ct), lambda i: (i, 0, 0)),
            pl.BlockSpec((1,) + kvec.shape[1:], lambda i: (i, 0, 0, 0)),
            pl.BlockSpec((1,) + state_map.shape[1:], lambda i: (i, 0, 0)),
            pl.BlockSpec((1,) + cross.shape[1:], lambda i: (i, 0, 0)),
            pl.BlockSpec((1,) + tab.shape[1:], lambda i: (i, 0, 0)),
        ],
        out_specs=pl.BlockSpec((1, rows, ct), lambda i: (i, 0, 0)),
        out_shape=jax.ShapeDtypeStruct((g, rows, ct), BF16),
        scratch_shapes=[pltpu.VMEM((ct, ct), BF16)],
        compiler_params=_params("parallel"),
        name="s5_conv",
    )(ut, kvec, state_map, cross, tab)
    return y.reshape(g, batch, n_chunks, S5_GROUP, t).transpose(1, 2, 4, 0, 3).reshape(n, width)


def _pair_blockdiag(y, left):
    return jnp.concatenate([jnp.where(left, y, 0.0), jnp.where(left, 0.0, y)], axis=0).astype(BF16)


def _rwkv_chunk_kernel(r_ref, lw_ref, k_ref, v_ref, an_ref, bb_ref, rhat_o, yhat_o, g_o, h_o):
    t = RWKV_CHUNK
    rows, width = r_ref.shape
    n_chunks = rows // t
    pairs = width // LANES
    row = lax.broadcasted_iota(jnp.int32, (t, LANES), 0)
    lane = lax.broadcasted_iota(jnp.int32, (t, LANES), 1)
    col = jnp.bitwise_and(lane, RWKV_HEAD - 1)
    left = lane < RWKV_HEAD
    incl = row >= col
    strict = row > col
    same_blk = (row // RWKV_SUB) == (col // RWKV_SUB)
    eye = jnp.where(row == col, 1.0, 0.0).astype(F32)
    brow = lax.broadcasted_iota(jnp.int32, (LANES, LANES), 0)
    bcol = lax.broadcasted_iota(jnp.int32, (LANES, LANES), 1)
    same_head = (brow // RWKV_HEAD) == (bcol // RWKV_HEAD)
    eye_full = brow == bcol
    crow = lax.broadcasted_iota(jnp.int32, (rows, rows), 0)
    ccol = lax.broadcasted_iota(jnp.int32, (rows, rows), 1)
    tril = jnp.where(crow >= ccol, jnp.where((crow // t) == (ccol // t), 1.0, 0.0), 0.0).astype(BF16)

    lw = lw_ref[...]
    p1 = lw.astype(BF16)
    rem = lw - p1.astype(F32)
    p2 = rem.astype(BF16)
    p3 = (rem - p2.astype(F32)).astype(BF16)
    cs = _dot(tril, p1) + _dot(tril, p2) + _dot(tril, p3)
    a_t = an_ref[...] * jnp.exp(cs - lw)
    r_t = r_ref[...] * jnp.exp(cs)
    p_inv = jnp.exp(-cs)
    b_t = bb_ref[...] * p_inv
    k_t = k_ref[...] * p_inv

    def bd(y):
        return _pair_blockdiag(y, left)

    def pmm(x, y_bd):
        return _dot(x.astype(BF16), y_bd)

    units = [(c, j) for c in range(n_chunks) for j in range(pairs)]

    def tile(arr, c, j):
        return arr[c * t:(c + 1) * t, j * LANES:(j + 1) * LANES]

    a2 = [tile(a_t, c, j) for c, j in units]
    r2 = [tile(r_t, c, j) for c, j in units]
    v2 = [tile(v_ref[...], c, j) for c, j in units]
    ar = [jnp.concatenate([a, r], axis=0).astype(BF16) for a, r in zip(a2, r2)]
    prod = [_dot_nt(x, jnp.concatenate([bd(tile(b_t, c, j)), bd(tile(k_t, c, j))], axis=0))
            for x, (c, j) in zip(ar, units)]
    l_ab = [jnp.where(strict, p[:t, :LANES], 0.0) for p in prod]
    l_rb = [jnp.where(incl, p[t:, :LANES], 0.0) for p in prod]
    l_ak = [jnp.where(strict, p[:t, LANES:], 0.0) for p in prod]
    l_rk = [jnp.where(incl, p[t:, LANES:], 0.0) for p in prod]
    l_d = [jnp.where(same_blk, m, 0.0) for m in l_ab]
    l_o = [m - d for m, d in zip(l_ab, l_d)]
    l2 = [pmm(m, bd(m)) for m in l_d]
    q1 = [pmm(eye + m, bd(eye + s)) for m, s in zip(l_d, l2)]
    l4 = [pmm(m, bd(m)) for m in l2]
    q2 = [pmm(q, bd(eye + s)) for q, s in zip(q1, l4)]
    l8 = [pmm(m, bd(m)) for m in l4]
    d_inv = [pmm(q, bd(eye + s)) for q, s in zip(q2, l8)]
    akv = [pmm(jnp.concatenate([m, n], axis=0), bd(v)) for m, n, v in zip(l_ak, l_rk, v2)]
    x1 = [z[:t] for z in akv]

    def bd2(z1, z2):
        return jnp.concatenate([bd(z1), bd(z2)], axis=1)

    dz = [pmm(d, bd2(a, x)) for d, a, x in zip(d_inv, a2, x1)]
    n1 = [pmm(d, bd(o)) for d, o in zip(d_inv, l_o)]
    n2 = [pmm(m, bd(m)) for m in n1]
    t1 = [z + pmm(m, bd2(z[:, :LANES], z[:, LANES:])) for z, m in zip(dz, n2)]
    wu = [z + pmm(m, bd2(z[:, :LANES], z[:, LANES:])) for z, m in zip(t1, n1)]
    ry = [pmm(m, bd2(z[:, :LANES], z[:, LANES:])) for m, z in zip(l_rb, wu)]
    rk_v = [z[t:] for z in akv]

    for i, (c, j) in enumerate(units):
        rs = slice(c * t, (c + 1) * t)
        ls = slice(j * LANES, (j + 1) * LANES)
        rhat_o[rs, ls] = r2[i] + ry[i][:, :LANES]
        yhat_o[rs, ls] = ry[i][:, LANES:] + rk_v[i]
        cs_c = cs[rs, ls]
        cs_end = cs_c[t - 1:t, :]
        p_end = jnp.exp(cs_end - cs_c)
        b_h = (bb_ref[rs, ls] * p_end).astype(BF16)
        k_h = (k_ref[rs, ls] * p_end).astype(BF16)
        g_full = _dot_tn(wu[i][:, :LANES].astype(BF16), b_h)
        g_bd = jnp.where(same_head, g_full, 0.0) + jnp.where(eye_full, jnp.exp(cs_end), 0.0)
        g_o[c, j] = g_bd.astype(BF16)
        uv = jnp.concatenate([wu[i][:, LANES:], v2[i]], axis=0).astype(BF16)
        h_full = _dot_tn(uv, jnp.concatenate([b_h, k_h], axis=0))
        h_o[c, j] = jnp.where(left, h_full[:RWKV_HEAD], h_full[RWKV_HEAD:])


RWKV_CHUNKS_PER_STEP = 4


def _rwkv_chunks(r, lw, k, v, an, bb):
    n, width = r.shape
    t = RWKV_CHUNK
    cps = RWKV_CHUNKS_PER_STEP
    pairs = width // LANES
    nck = n // t
    tok = pl.BlockSpec((cps * t, width), lambda i: (i, 0))
    return pl.pallas_call(
        _rwkv_chunk_kernel,
        grid=(nck // cps,),
        in_specs=[tok] * 6,
        out_specs=[tok, tok,
                   pl.BlockSpec((cps, pairs, LANES, LANES), lambda i: (i, 0, 0, 0)),
                   pl.BlockSpec((cps, pairs, RWKV_HEAD, LANES), lambda i: (i, 0, 0, 0))],
        out_shape=[jax.ShapeDtypeStruct((n, width), F32)] * 2
        + [jax.ShapeDtypeStruct((nck, pairs, LANES, LANES), BF16),
           jax.ShapeDtypeStruct((nck, pairs, RWKV_HEAD, LANES), F32)],
        compiler_params=_params("parallel"),
        name="rwkv_chunk",
    )(r, lw, k, v, an, bb)


def _rwkv_scan_kernel(rhat_ref, yhat_ref, g_ref, h_ref, gate_ref, bonus_ref, lnw_ref, lnb_ref,
                      ones_ref, y_ref, state):
    t = RWKV_CHUNK
    batch, pairs = state.shape[0], state.shape[1]
    n_chunks = rhat_ref.shape[1] // t
    lane = lax.broadcasted_iota(jnp.int32, (RWKV_HEAD, LANES), 1)
    left = lane < RWKV_HEAD

    @pl.when(pl.program_id(0) == 0)
    def _():
        state[...] = jnp.zeros_like(state)

    s = [[state[b, j] for j in range(pairs)] for b in range(batch)]
    y_rows = []
    for b in range(batch):
        chunk_rows = []
        for c in range(n_chunks):
            rs = slice(c * t, (c + 1) * t)
            tiles = []
            for j in range(pairs):
                ls = slice(j * LANES, (j + 1) * LANES)
                s_bd = _pair_blockdiag(s[b][j], left)
                tiles.append(yhat_ref[b, rs, ls] + _dot_nt(rhat_ref[b, rs, ls].astype(BF16), s_bd))
                s[b][j] = _dot(s[b][j].astype(BF16), g_ref[b, c, j]) + h_ref[b, c, j]
            chunk_rows.append(jnp.concatenate(tiles, axis=1))
        y_rows.append(jnp.concatenate(chunk_rows, axis=0))
    for b in range(batch):
        for j in range(pairs):
            state[b, j] = s[b][j]

    ones = ones_ref[...]
    inv = 1.0 / RWKV_HEAD
    for b in range(batch):
        y = y_rows[b]
        cen = y - _dot(y.astype(BF16), ones) * inv
        var = _dot((cen * cen).astype(BF16), ones) * inv
        yn = cen * lax.rsqrt(var + RWKV_GN_EPS) * lnw_ref[...] + lnb_ref[...]
        y_ref[b] = (yn + bonus_ref[b]) * gate_ref[b]


def _rwkv_scan(rhat, yhat, g_mat, h_mat, gate, bonus, ln_w, ln_b, head_ones, batch, seq):
    n, width = rhat.shape
    t = RWKV_CHUNK
    cps = RWKV_CHUNKS_PER_STEP
    pairs = width // LANES
    nc = seq // t
    tok3 = lambda a: a.reshape(batch, seq, width)
    tok = pl.BlockSpec((batch, cps * t, width), lambda i: (0, i, 0))
    out = pl.pallas_call(
        _rwkv_scan_kernel,
        grid=(nc // cps,),
        in_specs=[tok, tok,
                  pl.BlockSpec((batch, cps, pairs, LANES, LANES), lambda i: (0, i, 0, 0, 0)),
                  pl.BlockSpec((batch, cps, pairs, RWKV_HEAD, LANES), lambda i: (0, i, 0, 0, 0)),
                  tok, tok, _full(ln_w.shape), _full(ln_b.shape), _full(head_ones.shape)],
        out_specs=tok,
        out_shape=jax.ShapeDtypeStruct((batch, seq, width), F32),
        scratch_shapes=[pltpu.VMEM((batch, pairs, RWKV_HEAD, LANES), F32)],
        compiler_params=_params("arbitrary"),
        name="rwkv_scan",
    )(tok3(rhat), tok3(yhat), g_mat.reshape(batch, nc, pairs, LANES, LANES),
      h_mat.reshape(batch, nc, pairs, RWKV_HEAD, LANES), tok3(gate), tok3(bonus), ln_w, ln_b, head_ones)
    return out.reshape(n, width)


def _post_mix_kernel(n_experts, x_ref, yc_ref, u_ref, yr_ref, d_ref, wglu_ref, wout_ref, gain_ref,
                     rw_ref, rb_ref, x1_o, hm_o, gate_o, lpos_o, meta_o, tot_o, running):
    tm = x_ref.shape[0]
    width = yc_ref.shape[1]

    @pl.when(pl.program_id(0) == 0)
    def _():
        running[...] = jnp.zeros_like(running)

    y = yc_ref[...].astype(F32) + d_ref[...] * u_ref[...]
    y = 0.5 * y * (1.0 + jnp.tanh(0.7978845608028654 * (y + 0.044715 * (y * y * y))))
    y = y * _sigmoid(_dot(y.astype(BF16), wglu_ref[...]))
    x1 = (x_ref[...] + _dot(y.astype(BF16), wout_ref[:width, :])
          + _dot(yr_ref[...].astype(BF16), wout_ref[width:, :]))
    x1_o[...] = x1
    hm = _rms(x1, gain_ref[...])
    hm_o[...] = hm.astype(BF16)
    logits = _dot(hm.astype(BF16), rw_ref[...]) + rb_ref[...]

    lane = lax.broadcasted_iota(jnp.int32, logits.shape, 1)
    lanef = lane.astype(F32)
    sel = jnp.zeros(logits.shape, F32)
    idx_cols, val_cols = [], []
    work = logits
    for _ in range(TOP_K):
        m = jnp.max(work, axis=-1, keepdims=True)
        pick = jnp.min(jnp.where(work == m, lanef, float(LANES)), axis=-1, keepdims=True)
        hit = lanef == pick
        sel = jnp.where(hit, 1.0, sel)
        work = jnp.where(hit, -jnp.inf, work)
        idx_cols.append(pick)
        val_cols.append(m)
    exps = [jnp.exp(vv - val_cols[0]) for vv in val_cols]
    denom = exps[0] + exps[1] + exps[2] + exps[3]

    row = lax.broadcasted_iota(jnp.int32, (tm, tm), 0)
    col = lax.broadcasted_iota(jnp.int32, (tm, tm), 1)
    before = jnp.where(row > col, 1.0, 0.0).astype(BF16)
    local = _dot(before, sel.astype(BF16))
    cnt = jnp.sum(sel, axis=0, keepdims=True)
    cnt_al = jnp.floor((cnt + (RUN_ALIGN - 1)) * (1.0 / RUN_ALIGN)) * RUN_ALIGN
    erow = lax.broadcasted_iota(jnp.int32, (LANES, LANES), 0)
    ecol = lax.broadcasted_iota(jnp.int32, (LANES, LANES), 1)
    upper = jnp.where(erow < ecol, 1.0, 0.0).astype(BF16)
    toff = _dot(jnp.broadcast_to(cnt_al, (SUBLANES, LANES)).astype(BF16), upper)[0:1]
    tbase = running[...]
    gate_out = jnp.zeros(logits.shape, F32)
    lpos_out = jnp.zeros(logits.shape, F32)
    for j in range(TOP_K):
        lp = jnp.sum(jnp.where(lanef == idx_cols[j], local + toff, 0.0), axis=-1, keepdims=True)
        gate_out = jnp.where(lane == j, exps[j] / denom, gate_out)
        lpos_out = jnp.where(lane == j, lp, lpos_out)
    gate_o[...] = gate_out
    lpos_o[...] = lpos_out.astype(jnp.int32)
    srow = lax.broadcasted_iota(jnp.int32, (SUBLANES, LANES), 0)
    meta = jnp.where(srow == 0, cnt, jnp.where(srow == 1, tbase, jnp.where(srow == 2, toff, 0.0)))
    meta_o[...] = meta.astype(jnp.int32)
    running[...] = tbase + cnt_al
    tot_o[...] = jnp.broadcast_to(running[...], tot_o.shape).astype(jnp.int32)
    del n_experts


def _post_mix(x2d, yconv, u, y_rwkv, s5_d, w_glu, w_out, gain, rw, rb, n_experts):
    n, d = x2d.shape
    width = yconv.shape[1]
    tm = MOE_TILE
    row = lambda i: (i, 0)
    tok_d = pl.BlockSpec((tm, d), row)
    tok_w = pl.BlockSpec((tm, width), row)
    tok_l = pl.BlockSpec((tm, LANES), row)
    return pl.pallas_call(
        functools.partial(_post_mix_kernel, n_experts),
        grid=(n // tm,),
        in_specs=[tok_d, tok_w, tok_w, tok_w, _full(s5_d.shape), _full(w_glu.shape),
                  _full(w_out.shape), _full(gain.shape), _full(rw.shape), _full(rb.shape)],
        out_specs=[tok_d, tok_d, tok_l, tok_l,
                   pl.BlockSpec((SUBLANES, LANES), row), _full((SUBLANES, LANES))],
        out_shape=[jax.ShapeDtypeStruct((n, d), F32), jax.ShapeDtypeStruct((n, d), BF16),
                   jax.ShapeDtypeStruct((n, LANES), F32), jax.ShapeDtypeStruct((n, LANES), jnp.int32),
                   jax.ShapeDtypeStruct((n // tm * SUBLANES, LANES), jnp.int32),
                   jax.ShapeDtypeStruct((SUBLANES, LANES), jnp.int32)],
        scratch_shapes=[pltpu.VMEM((1, LANES), F32)],
        compiler_params=_params("arbitrary"),
        name="post_mix",
    )(x2d, yconv, u, y_rwkv, s5_d, w_glu, w_out, gain, rw, rb)


def _sorted_rows(tm, n_experts):
    return -(-(tm * TOP_K + n_experts * (RUN_ALIGN - 1)) // LANES) * LANES


RUN_BIG = 8


def _groups(count):
    return lax.shift_right_logical(count + (RUN_ALIGN - 1), RUN_ALIGN.bit_length() - 1)


def _start_runs(n_experts, pstart_ref, meta_ref, make_copy):
    shift = RUN_BIG.bit_length() - 1
    for e in range(n_experts):
        groups = _groups(meta_ref[0, e])
        seg = pstart_ref[e] + meta_ref[1, e]
        loc = meta_ref[2, e]
        n_big = lax.shift_right_logical(groups, shift)
        rem = groups & (RUN_BIG - 1)

        def body(g, c, seg=seg, loc=loc):
            off = g * (RUN_BIG * RUN_ALIGN)
            make_copy(pl.multiple_of(loc + off, RUN_ALIGN), pl.multiple_of(seg + off, RUN_ALIGN),
                      RUN_BIG * RUN_ALIGN).start(priority=e % 2)
            return c

        lax.fori_loop(0, n_big, body, 0)
        size = RUN_BIG // 2
        while size >= 1:
            off = (n_big * RUN_BIG + (rem & (RUN_BIG - 2 * size))) * RUN_ALIGN

            @pl.when((rem & size) != 0)
            def _(off=off, size=size, seg=seg, loc=loc):
                make_copy(pl.multiple_of(loc + off, RUN_ALIGN), pl.multiple_of(seg + off, RUN_ALIGN),
                          size * RUN_ALIGN).start(priority=e % 2)

            size //= 2


def _wait_runs(n_experts, meta_ref, make_copy):
    groups = 0
    for e in range(n_experts):
        groups = groups + _groups(meta_ref[0, e])

    def wait_big(g, c):
        make_copy(0, 0, RUN_BIG * RUN_ALIGN).wait()
        return c

    def wait_small(g, c):
        make_copy(0, 0, RUN_ALIGN).wait()
        return c

    lax.fori_loop(0, lax.shift_right_logical(groups, RUN_BIG.bit_length() - 1), wait_big, 0)
    lax.fori_loop(0, groups & (RUN_BIG - 1), wait_small, 0)


def _dispatch_kernel(n_experts, pstart_ref, tot_ref, meta_ref, meta_prev_ref, lpos_ref, hm_ref, xs_ref,
                     srt, zero, sem):
    tm = hm_ref.shape[0]
    rows = srt.shape[1]
    i = pl.program_id(0)
    cur = i % 2
    pos_t = jnp.transpose(lpos_ref[...].astype(F32))
    rid = lax.broadcasted_iota(jnp.int32, (rows, tm), 0).astype(F32)
    perm = jnp.zeros((rows, tm), F32)
    for j in range(TOP_K):
        perm = jnp.where(rid == pos_t[j:j + 1, :], 1.0, perm)
    srt[cur] = _dot(perm.astype(BF16), hm_ref[...])

    def copies(slot):
        def make_copy(loc, dst, size):
            return pltpu.make_async_copy(srt.at[slot, pl.ds(loc, size)], xs_ref.at[pl.ds(dst, size)],
                                         sem.at[slot])
        return make_copy

    _start_runs(n_experts, pstart_ref, meta_ref, copies(cur))

    @pl.when(i > 0)
    def _():
        _wait_runs(n_experts, meta_prev_ref, copies(1 - cur))

    @pl.when(i == pl.num_programs(0) - 1)
    def _():
        _wait_runs(n_experts, meta_ref, copies(cur))

    @pl.when(pl.program_id(0) == pl.num_programs(0) - 1)
    def _():
        zero[...] = jnp.zeros_like(zero)
        for wait in (False, True):
            for e in range(n_experts):
                used = tot_ref[e]
                start = pstart_ref[e] + used
                groups = lax.shift_right_logical((-used) & (EXPERT_TILE - 1), RUN_ALIGN.bit_length() - 1)

                def body(g, c, start=start):
                    cp = pltpu.make_async_copy(
                        zero, xs_ref.at[pl.ds(pl.multiple_of(start + g * RUN_ALIGN, RUN_ALIGN), RUN_ALIGN)],
                        sem.at[0])
                    if wait:
                        cp.wait()
                    else:
                        cp.start()
                    return c

                lax.fori_loop(0, groups, body, 0)

        last = n_experts - 1
        end = pstart_ref[last] + tot_ref[last] + ((-tot_ref[last]) & (EXPERT_TILE - 1))
        srt[0, 0:EXPERT_TILE, :] = jnp.zeros((EXPERT_TILE, srt.shape[2]), srt.dtype)
        for wait in (False, True):
            def tail(b, c):
                cp = pltpu.make_async_copy(
                    srt.at[0, pl.ds(0, EXPERT_TILE)],
                    xs_ref.at[pl.ds(pl.multiple_of(end + b * EXPERT_TILE, EXPERT_TILE), EXPERT_TILE)], sem.at[0])
                if wait:
                    cp.wait()
                else:
                    cp.start()
                return c

            lax.fori_loop(0, lax.shift_right_logical(xs_ref.shape[0] - end, EXPERT_TILE.bit_length() - 1), tail, 0)


def _dispatch(hm, lpos, meta, pstarts, totals, m_pad, n_experts):
    n, dh = hm.shape
    tm = MOE_TILE
    rows = _sorted_rows(tm, n_experts)
    grid_spec = pltpu.PrefetchScalarGridSpec(
        num_scalar_prefetch=2,
        grid=(n // tm,),
        in_specs=[pl.BlockSpec((SUBLANES, LANES), lambda i, ps, tt: (i, 0), memory_space=pltpu.SMEM),
                  pl.BlockSpec((SUBLANES, LANES), lambda i, ps, tt: (jnp.maximum(i - 1, 0), 0),
                               memory_space=pltpu.SMEM),
                  pl.BlockSpec((tm, LANES), lambda i, ps, tt: (i, 0)),
                  pl.BlockSpec((tm, dh), lambda i, ps, tt: (i, 0))],
        out_specs=pl.BlockSpec(memory_space=pl.ANY),
        scratch_shapes=[pltpu.VMEM((2, rows, dh), F32), pltpu.VMEM((RUN_ALIGN, dh), F32),
                        pltpu.SemaphoreType.DMA((2,))],
    )
    return pl.pallas_call(
        functools.partial(_dispatch_kernel, n_experts),
        grid_spec=grid_spec,
        out_shape=jax.ShapeDtypeStruct((m_pad, dh), F32),
        compiler_params=_params("arbitrary"),
        name="dispatch",
    )(pstarts, totals, meta, meta, lpos, hm)


def _experts_kernel(be_ref, nu_ref, xs_ref, wg_ref, bg_ref, wu_ref, bu_ref, wd_ref, bd_ref, ys_ref,
                    wg_s, wu_s, wd_s):
    i = pl.program_id(0)
    changed = jnp.logical_or(i == 0, be_ref[i] != be_ref[jnp.maximum(i - 1, 0)])

    @pl.when(changed)
    def _():
        wg_s[...] = wg_ref[0].astype(BF16)
        wu_s[...] = wu_ref[0].astype(BF16)
        wd_s[...] = wd_ref[0].astype(BF16)

    @pl.when(i < nu_ref[0])
    def _():
        xb = xs_ref[...].astype(BF16)
        gt = jnp.minimum(_dot(xb, wg_s[...]) + bg_ref[0], SWIGLU_LIMIT)
        up = jnp.clip(_dot(xb, wu_s[...]) + bu_ref[0], -SWIGLU_LIMIT, SWIGLU_LIMIT)
        act = (up + 1.0) * gt * _sigmoid(SWIGLU_ALPHA * gt)
        ys_ref[...] = _dot(act.astype(BF16), wd_s[...]) + bd_ref[0]

    @pl.when(i >= nu_ref[0])
    def _():
        ys_ref[...] = jnp.zeros_like(ys_ref)


def _experts(xs, block_e, n_used, wg, bg, wu, bu, wd, bd):
    m_pad, dh = xs.shape
    tmb = EXPERT_TILE
    d, dff = wg.shape[1], wg.shape[2]
    wmap = lambda i, be, nu: (be[i], 0, 0)
    grid_spec = pltpu.PrefetchScalarGridSpec(
        num_scalar_prefetch=2,
        grid=(m_pad // tmb,),
        in_specs=[pl.BlockSpec((tmb, dh), lambda i, be, nu: (jnp.where(i < nu[0], i, 0), 0)),
                  pl.BlockSpec((1, d, dff), wmap), pl.BlockSpec((1, 1, dff), wmap),
                  pl.BlockSpec((1, d, dff), wmap), pl.BlockSpec((1, 1, dff), wmap),
                  pl.BlockSpec((1, dff, d), wmap), pl.BlockSpec((1, 1, d), wmap)],
        out_specs=pl.BlockSpec((tmb, dh), lambda i, be, nu: (i, 0)),
        scratch_shapes=[pltpu.VMEM((d, dff), BF16), pltpu.VMEM((d, dff), BF16), pltpu.VMEM((dff, d), BF16)],
    )
    return pl.pallas_call(
        _experts_kernel,
        grid_spec=grid_spec,
        out_shape=jax.ShapeDtypeStruct((m_pad, dh), F32),
        compiler_params=_params("arbitrary"),
        name="experts",
    )(block_e, n_used, xs, wg, bg, wu, bu, wd, bd)


def _final_kernel(n_experts, pstart_ref, meta_ref, meta_next_ref, lpos_ref, x1_ref, gate_ref, p_ref, ys_ref,
                  gple_ref, wproj_ref, wgate_ref, gfin_ref, out_ref, buf, sem):
    tm = x1_ref.shape[0]
    rows = buf.shape[1]
    i = pl.program_id(0)
    cur = i % 2

    def copies(slot):
        def make_copy(loc, src, size):
            return pltpu.make_async_copy(ys_ref.at[pl.ds(src, size)], buf.at[slot, pl.ds(loc, size)],
                                         sem.at[slot])
        return make_copy

    @pl.when(i == 0)
    def _():
        buf[...] = jnp.zeros_like(buf)
        _start_runs(n_experts, pstart_ref, meta_ref, copies(0))

    @pl.when(i + 1 < pl.num_programs(0))
    def _():
        _start_runs(n_experts, pstart_ref, meta_next_ref, copies(1 - cur))

    cid = lax.broadcasted_iota(jnp.int32, (tm, rows), 1)
    lpos = lpos_ref[...]
    gates = gate_ref[...]
    comb = jnp.zeros((tm, rows), F32)
    for j in range(TOP_K):
        comb = jnp.where(cid == lpos[:, j:j + 1], gates[:, j:j + 1], comb)

    _wait_runs(n_experts, meta_ref, copies(cur))
    x2 = x1_ref[...] + _dot(comb.astype(BF16), buf[cur].astype(BF16))
    gate = _sigmoid(_dot(_rms(x2, gple_ref[...]).astype(BF16), wgate_ref[...]))
    x3 = x2 + _dot(p_ref[...].astype(BF16), wproj_ref[...]) * gate
    out_ref[...] = _rms(x3, gfin_ref[...])


def _final(x1, gates, lpos, meta, pstarts, p2d, ys, g_ple, w_proj, w_gate, g_fin, n_experts):
    n, d = x1.shape
    tm = MOE_TILE
    steps = n // tm
    rows = _sorted_rows(tm, n_experts)
    row = lambda i, ps: (i, 0)
    const = lambda shape: pl.BlockSpec(shape, lambda i, ps: (0,) * len(shape))
    grid_spec = pltpu.PrefetchScalarGridSpec(
        num_scalar_prefetch=1,
        grid=(steps,),
        in_specs=[pl.BlockSpec((SUBLANES, LANES), row, memory_space=pltpu.SMEM),
                  pl.BlockSpec((SUBLANES, LANES), lambda i, ps: (jnp.minimum(i + 1, steps - 1), 0),
                               memory_space=pltpu.SMEM),
                  pl.BlockSpec((tm, LANES), row), pl.BlockSpec((tm, d), row), pl.BlockSpec((tm, LANES), row),
                  pl.BlockSpec((tm, p2d.shape[1]), row), pl.BlockSpec(memory_space=pl.ANY),
                  const(g_ple.shape), const(w_proj.shape), const(w_gate.shape), const(g_fin.shape)],
        out_specs=pl.BlockSpec((tm, d), row),
        scratch_shapes=[pltpu.VMEM((2, rows, ys.shape[1]), ys.dtype), pltpu.SemaphoreType.DMA((2,))],
    )
    return pl.pallas_call(
        functools.partial(_final_kernel, n_experts),
        grid_spec=grid_spec,
        out_shape=jax.ShapeDtypeStruct((n, d), F32),
        compiler_params=_params("arbitrary"),
        name="final",
    )(pstarts, meta, meta, lpos, x1, gates, p2d, ys, g_ple, w_proj, w_gate, g_fin)


def _head_ones(width):
    hd = jnp.arange(width) // RWKV_HEAD
    return (hd[:, None] == hd[None, :]).astype(BF16)


def _layer(x2d, p2d, batch, seq, norm_mix_g, w_in, s5_lam_re, s5_lam_im, s5_log_dt, s5_b_re, s5_b_im,
           s5_c_re, s5_c_im, s5_d, s5_w_glu, mu_rkv, mu_wag, w0, w1, w2, a0, a1, a2, g1, g2, k_k, k_a,
           r_k, ln_w, ln_b, w_out, norm_moe_g, router_w, router_b, wg, bg, wu, bu, wd, bd,
           norm_ple_g, ple_w_proj, ple_w_gate):
    n, d = x2d.shape
    width = w_in.shape[1] // 4
    n_experts = router_w.shape[1]
    row2 = lambda t: t.reshape(1, -1).astype(F32)
    head_ones = _head_ones(width)
    vecs = jnp.zeros((SUBLANES, width), F32)
    vecs = vecs.at[0].set(w0).at[1].set(a0).at[2].set(k_k).at[3].set(k_a).at[4].set(r_k.reshape(-1))

    w_ext, w2_cat = _mix_in_weights(w_in, w1, a1, g1, mu_wag, w2, a2, g2)
    u, r, lw, k, v, an, bb, gate, bonus, u_bf = _mix_in(
        x2d, seq, row2(norm_mix_g), w_ext, mu_rkv.astype(F32), vecs, w2_cat, head_ones)

    tables = _s5_tables(s5_lam_re, s5_lam_im, s5_log_dt, s5_b_re, s5_b_im, s5_c_re, s5_c_im,
                        seq // S5_CHUNK)
    yconv = _s5_conv(u_bf, batch, seq, tables)

    rhat, yhat, g_mat, h_mat = _rwkv_chunks(r, lw, k, v, an, bb)
    y_rwkv = _rwkv_scan(rhat, yhat, g_mat, h_mat, gate, bonus, row2(ln_w), row2(ln_b), head_ones,
                        batch, seq)

    rw = jnp.zeros((d, LANES), BF16).at[:, :n_experts].set(router_w.astype(BF16))
    rb = jnp.full((1, LANES), NEG_BIG, F32).at[0, :n_experts].set(router_b.astype(F32))
    x1, hm, gates, lpos, meta, totals = _post_mix(
        x2d, yconv, u, y_rwkv, row2(s5_d), s5_w_glu.astype(BF16), w_out.astype(BF16),
        row2(norm_moe_g), rw, rb, n_experts)

    tmb = EXPERT_TILE
    n_tiles = n // MOE_TILE
    max_rows = n * TOP_K + n_tiles * n_experts * (RUN_ALIGN - 1)
    n_blocks = -(-max_rows // tmb) + n_experts
    seg = totals[0, :n_experts].astype(jnp.int32)
    padded = ((seg + tmb - 1) // tmb) * tmb
    pends = jnp.cumsum(padded)
    pstarts = (pends - padded).astype(jnp.int32)
    block_start = jnp.arange(n_blocks, dtype=jnp.int32) * tmb
    block_e = jnp.minimum(jnp.sum((pends[None, :] <= block_start[:, None]).astype(jnp.int32), axis=1),
                          n_experts - 1).astype(jnp.int32)
    n_used = (pends[-1] // tmb).astype(jnp.int32).reshape(1)

    xs = _dispatch(hm, lpos, meta, pstarts, seg, n_blocks * tmb, n_experts)
    ys = _experts(xs, block_e, n_used, wg, bg.reshape(n_experts, 1, -1).astype(F32),
                  wu, bu.reshape(n_experts, 1, -1).astype(F32), wd,
                  bd.reshape(n_experts, 1, -1).astype(F32))
    return (x1, gates, lpos, meta, pstarts, ys, p2d, row2(norm_ple_g), ple_w_proj.astype(BF16),
            ple_w_gate.astype(BF16), n_experts)


def kernel(x, p, norm_mix_g, w_in, s5_lam_re, s5_lam_im, s5_log_dt, s5_b_re, s5_b_im, s5_c_re, s5_c_im, s5_d, s5_w_glu, rwkv_mu_rkv, rwkv_mu_wag, rwkv_w0, rwkv_w1, rwkv_w2, rwkv_a0, rwkv_a1, rwkv_a2, rwkv_g1, rwkv_g2, rwkv_k_k, rwkv_k_a, rwkv_r_k, rwkv_ln_w, rwkv_ln_b, w_out, norm_moe_g, router_w, router_b, exp_w_gate, exp_b_gate, exp_w_up, exp_b_up, exp_w_down, exp_b_down, norm_ple_g, ple_w_proj, ple_w_gate, final_norm_g):
    batch, seq, d = x.shape
    assert w_in.shape[0] == 1, "the final kernel fuses the last RMSNorm: single-layer stacks only"
    i = 0
    x1, gates, lpos, meta, pstarts, ys, p2d, g_ple, w_proj, w_gate, n_experts = _layer(
        x.reshape(batch * seq, d), p[i].reshape(batch * seq, -1), batch, seq, norm_mix_g[i], w_in[i],
        s5_lam_re[i], s5_lam_im[i], s5_log_dt[i], s5_b_re[i], s5_b_im[i], s5_c_re[i], s5_c_im[i],
        s5_d[i], s5_w_glu[i], rwkv_mu_rkv[i], rwkv_mu_wag[i], rwkv_w0[i], rwkv_w1[i], rwkv_w2[i],
        rwkv_a0[i], rwkv_a1[i], rwkv_a2[i], rwkv_g1[i], rwkv_g2[i], rwkv_k_k[i], rwkv_k_a[i],
        rwkv_r_k[i], rwkv_ln_w[i], rwkv_ln_b[i], w_out[i], norm_moe_g[i], router_w[i],
        router_b[i], exp_w_gate[i], exp_b_gate[i], exp_w_up[i], exp_b_up[i], exp_w_down[i],
        exp_b_down[i], norm_ple_g[i], ple_w_proj[i], ple_w_gate[i])
    out = _final(x1, gates, lpos, meta, pstarts, p2d, ys, g_ple, w_proj, w_gate,
                 final_norm_g.reshape(1, -1).astype(F32), n_experts)
    return out.reshape(batch, seq, d)
```

```python
import functools

import jax
import jax.numpy as jnp
from jax import lax
from jax.experimental import pallas as pl
from jax.experimental.pallas import tpu as pltpu

F32 = jnp.float32
BF16 = jnp.bfloat16

S5_GROUP = 16
S5_STATE = 64
RWKV_HEAD = 64
DECAY_LORA = 64
ICLR_LORA = 64
DECAY_SCALE = 0.6065306597126334
TOP_K = 4
RMS_EPS = 1e-6
RWKV_GN_EPS = 64e-5
LAMBDA_RE_MAX = -1e-4
SWIGLU_LIMIT = 7.0
SWIGLU_ALPHA = 1.702

LANES = 128
SUBLANES = 8
VMEM_LIMIT = 56 * 1024 * 1024

S5_CHUNK = 64
RWKV_CHUNK = 64
RWKV_SUB = 16
TOKEN_TILE = 512
EXPERT_TILE = 512
MOE_TILE = 512
RUN_ALIGN = 8
NEG_BIG = -1e30


def _dot(a, b):
    return jnp.dot(a, b, preferred_element_type=F32)


def _dot_nt(a, b):
    return lax.dot_general(a, b, (((1,), (1,)), ((), ())), preferred_element_type=F32)


def _dot_tn(a, b):
    return lax.dot_general(a, b, (((0,), (0,)), ((), ())), preferred_element_type=F32)


def _rms(t, gain):
    return t * lax.rsqrt(jnp.mean(t * t, axis=-1, keepdims=True) + RMS_EPS) * gain


def _sigmoid(t):
    return 1.0 / (1.0 + jnp.exp(-t))


def _params(*sem):
    return pltpu.CompilerParams(dimension_semantics=sem, vmem_limit_bytes=VMEM_LIMIT)


def _full(shape):
    return pl.BlockSpec(shape, lambda *_: (0,) * len(shape))


def _mix_in_kernel(seq, x_ref, xp_ref, gain_ref, win_ref, murkv_ref, vec_ref, w2_ref, ones_ref,
                   u_o, r_o, lw_o, k_o, v_o, an_o, bb_o, g_o, bonus_o, ub_o):
    tm = x_ref.shape[0]
    width = u_o.shape[1]
    lora = w2_ref.shape[0]
    gain = gain_ref[...]
    h = _rms(x_ref[...], gain)
    keep = jnp.where((pl.program_id(0) * tm) % seq == 0, 0.0, 1.0)
    hp = _rms(xp_ref[...], gain) * keep
    h_ext = jnp.concatenate([hp, h], axis=0).astype(BF16)
    proj = _dot(h_ext, win_ref[...])
    shifted = pltpu.roll(proj[:, width:4 * width + lora], 1, 0)[SUBLANES:]
    cur = proj[SUBLANES:]
    u_o[...] = cur[:, :width]
    ub_o[...] = cur[:, :width].astype(BF16)

    l1 = cur[:, 4 * width + lora:] + shifted[:, 3 * width:]
    lane = lax.broadcasted_iota(jnp.int32, l1.shape, 1)
    hidden = jnp.where(lane < DECAY_LORA, jnp.tanh(l1),
                       jnp.where(lane < DECAY_LORA + ICLR_LORA, l1, _sigmoid(l1))).astype(BF16)
    z3 = _dot(hidden, w2_ref[...])

    w0 = vec_ref[0:1, :]
    a0 = vec_ref[1:2, :]
    k_k = vec_ref[2:3, :]
    k_a = vec_ref[3:4, :]
    r_k = vec_ref[4:5, :]
    lw_o[...] = (-DECAY_SCALE) * _sigmoid(w0 + z3[:, :width])
    a = _sigmoid(a0 + z3[:, width:2 * width])
    g_o[...] = z3[:, 2 * width:]

    r_cur = cur[:, width:2 * width]
    k_cur = cur[:, 2 * width:3 * width]
    v_cur = cur[:, 3 * width:4 * width]
    r = r_cur + (shifted[:, :width] - r_cur) * murkv_ref[0:1, :]
    k = k_cur + (shifted[:, width:2 * width] - k_cur) * murkv_ref[1:2, :]
    v = v_cur + (shifted[:, 2 * width:3 * width] - v_cur) * murkv_ref[2:3, :]

    ones = ones_ref[...]
    kk = k * k_k
    ssq = _dot((kk * kk).astype(BF16), ones)
    kkn = kk * jnp.minimum(lax.rsqrt(ssq), 1e12)
    k2 = k * (1.0 + (a - 1.0) * k_a)
    r_o[...] = r
    k_o[...] = k2
    v_o[...] = v
    an_o[...] = -kkn
    bb_o[...] = kkn * a
    bonus_o[...] = _dot((r * k2 * r_k).astype(BF16), ones) * v


def _mix_in(x2d, seq, gain, w_ext, mu_rkv, vecs, w2_cat, head_ones):
    n, d = x2d.shape
    width = w2_cat.shape[1] // 3
    tm = TOKEN_TILE
    prev_blocks = tm // SUBLANES
    row = lambda i: (i, 0)
    out = jax.ShapeDtypeStruct((n, width), F32)
    return pl.pallas_call(
        functools.partial(_mix_in_kernel, seq),
        grid=(n // tm,),
        in_specs=[
            pl.BlockSpec((tm, d), row),
            pl.BlockSpec((SUBLANES, d), lambda i: (jnp.maximum(i * prev_blocks - 1, 0), 0)),
            _full(gain.shape), _full(w_ext.shape), _full(mu_rkv.shape), _full(vecs.shape),
            _full(w2_cat.shape), _full(head_ones.shape),
        ],
        out_specs=[pl.BlockSpec((tm, width), row)] * 10,
        out_shape=[out] * 9 + [jax.ShapeDtypeStruct((n, width), BF16)],
        compiler_params=_params("parallel"),
        name="mix_in",
    )(x2d, x2d, gain, w_ext, mu_rkv, vecs, w2_cat, head_ones)


def _mix_in_weights(w_in, w1, a1, g1, mu_wag, w2, a2, g2):
    l1 = jnp.concatenate([w1, a1, g1], axis=1).astype(F32)
    mu = jnp.concatenate([jnp.broadcast_to(mu_wag[j][:, None], (w.shape[0], w.shape[1]))
                          for j, w in enumerate((w1, a1, g1))], axis=1).astype(F32)
    w_ext = jnp.concatenate([w_in.astype(F32), mu * l1, (1.0 - mu) * l1], axis=1).astype(BF16)
    width = w2.shape[1]
    z = lambda rows: jnp.zeros((rows, width), F32)
    w2_cat = jnp.concatenate([
        jnp.concatenate([w2.astype(F32), z(w2.shape[0]), z(w2.shape[0])], axis=1),
        jnp.concatenate([z(a2.shape[0]), a2.astype(F32), z(a2.shape[0])], axis=1),
        jnp.concatenate([z(g2.shape[0]), z(g2.shape[0]), g2.astype(F32)], axis=1)], axis=0).astype(BF16)
    return w_ext, w2_cat


def _s5_tables(lam_re, lam_im, log_dt, b_re, b_im, c_re, c_im, n_chunks):
    t = S5_CHUNK
    hi = lax.Precision.HIGHEST
    lre = jnp.minimum(lam_re.astype(F32), LAMBDA_RE_MAX)
    lim = lam_im.astype(F32)
    dt = jnp.exp(log_dt.astype(F32))[:, None]
    mag = jnp.exp(lre * dt)
    lb_re = mag * jnp.cos(lim * dt)
    lb_im = mag * jnp.sin(lim * dt)
    den = lre * lre + lim * lim
    z_re = lb_re - 1.0
    coef_re = ((z_re * lre + lb_im * lim) / den)[..., None]
    coef_im = ((lb_im * lre - z_re * lim) / den)[..., None]
    b_re = b_re.astype(F32)
    b_im = b_im.astype(F32)
    bb_re = coef_re * b_re - coef_im * b_im
    bb_im = coef_re * b_im + coef_im * b_re
    c_re = c_re.astype(F32)
    c_im = c_im.astype(F32)

    def power(e):
        e = e.astype(F32)[:, None, None]
        m = jnp.exp(e * (lre * dt))
        ang = e * (lim * dt)
        return m * jnp.cos(ang), m * jnp.sin(ang)

    g = lre.shape[0]
    c = S5_GROUP
    p_re, p_im = power(jnp.arange(t + 1))
    cb_re = (c_re[:, None, :, :] * bb_re.transpose(0, 2, 1)[:, :, None, :]
             - c_im[:, None, :, :] * bb_im.transpose(0, 2, 1)[:, :, None, :])
    cb_im = (c_re[:, None, :, :] * bb_im.transpose(0, 2, 1)[:, :, None, :]
             + c_im[:, None, :, :] * bb_re.transpose(0, 2, 1)[:, :, None, :])
    cb = jnp.concatenate([cb_re, -cb_im], axis=-1).reshape(g, c * c, 2 * S5_STATE)
    pw = jnp.concatenate([p_re[:t], p_im[:t]], axis=-1).transpose(1, 2, 0)
    kvec = jnp.einsum('gkp,gpt->gkt', cb, pw, precision=hi).reshape(g, c, c // 2, 2 * t)
    r_re, r_im = power(t - 1 - jnp.arange(t))
    st_re = r_re[:, :, :, None] * bb_re - r_im[:, :, :, None] * bb_im
    st_im = r_re[:, :, :, None] * bb_im + r_im[:, :, :, None] * bb_re
    state_map = jnp.concatenate([st_re.transpose(1, 3, 0, 2).reshape(g, c * t, S5_STATE),
                                 st_im.transpose(1, 3, 0, 2).reshape(g, c * t, S5_STATE)],
                                axis=2).astype(BF16)
    q_re, q_im = p_re[1:], p_im[1:]
    ca_re = c_re[:, :, None, :] * q_re.transpose(1, 0, 2)[:, None] - c_im[:, :, None, :] * q_im.transpose(1, 0, 2)[:, None]
    ca_im = c_re[:, :, None, :] * q_im.transpose(1, 0, 2)[:, None] + c_im[:, :, None, :] * q_re.transpose(1, 0, 2)[:, None]
    cross = jnp.concatenate([ca_re.transpose(0, 3, 1, 2).reshape(g, S5_STATE, c * t),
                             -ca_im.transpose(0, 3, 1, 2).reshape(g, S5_STATE, c * t)], axis=1).astype(BF16)
    n_steps = max(1, (n_chunks - 1).bit_length())
    s_re, s_im = power(t * (2 ** jnp.arange(n_steps)))
    tab = jnp.stack([jnp.concatenate([s_re, s_re], axis=-1),
                     jnp.concatenate([-s_im, s_im], axis=-1)], axis=1)
    tab = tab.transpose(2, 0, 1, 3).reshape(g, 2 * n_steps, 2 * S5_STATE)
    return kvec, state_map, cross, tab, n_steps


def _s5_conv_kernel(n_chunks, n_steps, u_ref, kvec_ref, st_ref, cross_ref, tab_ref, y_ref, conv):
    t = S5_CHUNK
    row = lax.broadcasted_iota(jnp.int32, (t, LANES), 0)
    lane = lax.broadcasted_iota(jnp.int32, (t, LANES), 1)
    causal = jnp.bitwise_and(lane, t - 1) >= row
    for cin in range(S5_GROUP):
        for cp in range(S5_GROUP // 2):
            base = jnp.broadcast_to(kvec_ref[0, cin, cp:cp + 1, :], (t, LANES))
            blk = pltpu.roll(base, 0, 1, stride=1, stride_axis=0)
            conv[cin * t:(cin + 1) * t, cp * LANES:(cp + 1) * LANES] = jnp.where(causal, blk, 0.0).astype(BF16)
    u = u_ref[0]
    y = _dot(u, conv[...])
    x = _dot(u, st_ref[0])
    chunk = lax.broadcasted_iota(jnp.int32, x.shape, 0) % n_chunks
    half = x.shape[1] // 2
    for j in range(n_steps):
        sh = 1 << j
        xs = jnp.where(chunk >= sh, pltpu.roll(x, sh, 0), 0.0)
        x = x + tab_ref[0, 2 * j:2 * j + 1, :] * xs + tab_ref[0, 2 * j + 1:2 * j + 2, :] * pltpu.roll(xs, half, 1)
    x_in = jnp.where(chunk >= 1, pltpu.roll(x, 1, 0), 0.0)
    y_ref[0] = (y + _dot(x_in.astype(BF16), cross_ref[0])).astype(y_ref.dtype)


def _s5_conv(u2d, batch, seq, tables):
    kvec, state_map, cross, tab, n_steps = tables
    n, width = u2d.shape
    g = width // S5_GROUP
    t = S5_CHUNK
    n_chunks = seq // t
    ct = S5_GROUP * t
    ut = u2d.T.reshape(g, S5_GROUP, batch * n_chunks, t).transpose(0, 2, 1, 3)
    ut = ut.reshape(g, batch * n_chunks, ct)
    rows = batch * n_chunks
    y = pl.pallas_call(
        functools.partial(_s5_conv_kernel, n_chunks, n_steps),
        grid=(g,),
        in_specs=[
            pl.BlockSpec((1, rows, ct), lambda i: (i, 0, 0)),
            pl.BlockSpec((1,) + kvec.shape[1:], lambda i: (i, 0, 0, 0)),
            pl.BlockSpec((1,) + state_map.shape[1:], lambda i: (i, 0, 0)),
            pl.BlockSpec((1,) + cross.shape[1:], lambda i: (i, 0, 0)),
            pl.BlockSpec((1,) + tab.shape[1:], lambda i: (i, 0, 0)),
        ],
        out_specs=pl.BlockSpec((1, rows, ct), lambda i: (i, 0, 0)),
        out_shape=jax.ShapeDtypeStruct((g, rows, ct), BF16),
        scratch_shapes=[pltpu.VMEM((ct, ct), BF16)],
        compiler_params=_params("parallel"),
        name="s5_conv",
    )(ut, kvec, state_map, cross, tab)
    return y.reshape(g, batch * n_chunks, S5_GROUP, t).transpose(0, 2, 1, 3).reshape(width, n).T


def _pair_blockdiag(y, left):
    return jnp.concatenate([jnp.where(left, y, 0.0), jnp.where(left, 0.0, y)], axis=0).astype(BF16)


def _rwkv_chunk_kernel(r_ref, lw_ref, k_ref, v_ref, an_ref, bb_ref, rhat_o, yhat_o, g_o, h_o):
    t = RWKV_CHUNK
    rows, width = r_ref.shape
    n_chunks = rows // t
    pairs = width // LANES
    row = lax.broadcasted_iota(jnp.int32, (t, LANES), 0)
    lane = lax.broadcasted_iota(jnp.int32, (t, LANES), 1)
    col = jnp.bitwise_and(lane, RWKV_HEAD - 1)
    left = lane < RWKV_HEAD
    incl = row >= col
    strict = row > col
    same_blk = (row // RWKV_SUB) == (col // RWKV_SUB)
    eye = jnp.where(row == col, 1.0, 0.0).astype(F32)
    brow = lax.broadcasted_iota(jnp.int32, (LANES, LANES), 0)
    bcol = lax.broadcasted_iota(jnp.int32, (LANES, LANES), 1)
    same_head = (brow // RWKV_HEAD) == (bcol // RWKV_HEAD)
    eye_full = brow == bcol
    crow = lax.broadcasted_iota(jnp.int32, (rows, rows), 0)
    ccol = lax.broadcasted_iota(jnp.int32, (rows, rows), 1)
    tril = jnp.where(crow >= ccol, jnp.where((crow // t) == (ccol // t), 1.0, 0.0), 0.0).astype(BF16)

    lw = lw_ref[...]
    p1 = lw.astype(BF16)
    rem = lw - p1.astype(F32)
    p2 = rem.astype(BF16)
    p3 = (rem - p2.astype(F32)).astype(BF16)
    cs = _dot(tril, p1) + _dot(tril, p2) + _dot(tril, p3)
    a_t = an_ref[...] * jnp.exp(cs - lw)
    r_t = r_ref[...] * jnp.exp(cs)
    p_inv = jnp.exp(-cs)
    b_t = bb_ref[...] * p_inv
    k_t = k_ref[...] * p_inv

    def bd(y):
        return _pair_blockdiag(y, left)

    def pmm(x, y_bd):
        return _dot(x.astype(BF16), y_bd)

    units = [(c, j) for c in range(n_chunks) for j in range(pairs)]

    def tile(arr, c, j):
        return arr[c * t:(c + 1) * t, j * LANES:(j + 1) * LANES]

    a2 = [tile(a_t, c, j) for c, j in units]
    r2 = [tile(r_t, c, j) for c, j in units]
    v2 = [tile(v_ref[...], c, j) for c, j in units]
    ar = [jnp.concatenate([a, r], axis=0).astype(BF16) for a, r in zip(a2, r2)]
    prod = [_dot_nt(x, jnp.concatenate([bd(tile(b_t, c, j)), bd(tile(k_t, c, j))], axis=0))
            for x, (c, j) in zip(ar, units)]
    l_ab = [jnp.where(strict, p[:t, :LANES], 0.0) for p in prod]
    l_rb = [jnp.where(incl, p[t:, :LANES], 0.0) for p in prod]
    l_ak = [jnp.where(strict, p[:t, LANES:], 0.0) for p in prod]
    l_rk = [jnp.where(incl, p[t:, LANES:], 0.0) for p in prod]
    l_d = [jnp.where(same_blk, m, 0.0) for m in l_ab]
    l_o = [m - d for m, d in zip(l_ab, l_d)]
    l2 = [pmm(m, bd(m)) for m in l_d]
    q1 = [pmm(eye + m, bd(eye + s)) for m, s in zip(l_d, l2)]
    l4 = [pmm(m, bd(m)) for m in l2]
    q2 = [pmm(q, bd(eye + s)) for q, s in zip(q1, l4)]
    l8 = [pmm(m, bd(m)) for m in l4]
    d_inv = [pmm(q, bd(eye + s)) for q, s in zip(q2, l8)]
    akv = [pmm(jnp.concatenate([m, n], axis=0), bd(v)) for m, n, v in zip(l_ak, l_rk, v2)]
    x1 = [z[:t] for z in akv]

    def bd2(z1, z2):
        return jnp.concatenate([bd(z1), bd(z2)], axis=1)

    dz = [pmm(d, bd2(a, x)) for d, a, x in zip(d_inv, a2, x1)]
    n1 = [pmm(d, bd(o)) for d, o in zip(d_inv, l_o)]
    n2 = [pmm(m, bd(m)) for m in n1]
    t1 = [z + pmm(m, bd2(z[:, :LANES], z[:, LANES:])) for z, m in zip(dz, n2)]
    wu = [z + pmm(m, bd2(z[:, :LANES], z[:, LANES:])) for z, m in zip(t1, n1)]
    ry = [pmm(m, bd2(z[:, :LANES], z[:, LANES:])) for m, z in zip(l_rb, wu)]
    rk_v = [z[t:] for z in akv]

    for i, (c, j) in enumerate(units):
        rs = slice(c * t, (c + 1) * t)
        ls = slice(j * LANES, (j + 1) * LANES)
        rhat_o[rs, ls] = r2[i] + ry[i][:, :LANES]
        yhat_o[rs, ls] = ry[i][:, LANES:] + rk_v[i]
        cs_c = cs[rs, ls]
        cs_end = cs_c[t - 1:t, :]
        p_end = jnp.exp(cs_end - cs_c)
        b_h = (bb_ref[rs, ls] * p_end).astype(BF16)
        k_h = (k_ref[rs, ls] * p_end).astype(BF16)
        g_full = _dot_tn(wu[i][:, :LANES].astype(BF16), b_h)
        g_bd = jnp.where(same_head, g_full, 0.0) + jnp.where(eye_full, jnp.exp(cs_end), 0.0)
        g_o[c, j] = g_bd.astype(BF16)
        uv = jnp.concatenate([wu[i][:, LANES:], v2[i]], axis=0).astype(BF16)
        h_full = _dot_tn(uv, jnp.concatenate([b_h, k_h], axis=0))
        h_o[c, j] = jnp.where(left, h_full[:RWKV_HEAD], h_full[RWKV_HEAD:])


RWKV_CHUNKS_PER_STEP = 4


def _rwkv_chunks(r, lw, k, v, an, bb):
    n, width = r.shape
    t = RWKV_CHUNK
    cps = RWKV_CHUNKS_PER_STEP
    pairs = width // LANES
    nck = n // t
    tok = pl.BlockSpec((cps * t, width), lambda i: (i, 0))
    return pl.pallas_call(
        _rwkv_chunk_kernel,
        grid=(nck // cps,),
        in_specs=[tok] * 6,
        out_specs=[tok, tok,
                   pl.BlockSpec((cps, pairs, LANES, LANES), lambda i: (i, 0, 0, 0)),
                   pl.BlockSpec((cps, pairs, RWKV_HEAD, LANES), lambda i: (i, 0, 0, 0))],
        out_shape=[jax.ShapeDtypeStruct((n, width), F32)] * 2
        + [jax.ShapeDtypeStruct((nck, pairs, LANES, LANES), BF16),
           jax.ShapeDtypeStruct((nck, pairs, RWKV_HEAD, LANES), F32)],
        compiler_params=_params("parallel"),
        name="rwkv_chunk",
    )(r, lw, k, v, an, bb)


def _rwkv_scan_kernel(rhat_ref, yhat_ref, g_ref, h_ref, gate_ref, bonus_ref, lnw_ref, lnb_ref,
                      ones_ref, y_ref, state):
    t = RWKV_CHUNK
    batch, pairs = state.shape[0], state.shape[1]
    n_chunks = rhat_ref.shape[1] // t
    lane = lax.broadcasted_iota(jnp.int32, (RWKV_HEAD, LANES), 1)
    left = lane < RWKV_HEAD

    @pl.when(pl.program_id(0) == 0)
    def _():
        state[...] = jnp.zeros_like(state)

    s = [[state[b, j] for j in range(pairs)] for b in range(batch)]
    y_rows = []
    for b in range(batch):
        chunk_rows = []
        for c in range(n_chunks):
            rs = slice(c * t, (c + 1) * t)
            tiles = []
            for j in range(pairs):
                ls = slice(j * LANES, (j + 1) * LANES)
                s_bd = _pair_blockdiag(s[b][j], left)
                tiles.append(yhat_ref[b, rs, ls] + _dot_nt(rhat_ref[b, rs, ls].astype(BF16), s_bd))
                s[b][j] = _dot(s[b][j].astype(BF16), g_ref[b, c, j]) + h_ref[b, c, j]
            chunk_rows.append(jnp.concatenate(tiles, axis=1))
        y_rows.append(jnp.concatenate(chunk_rows, axis=0))
    for b in range(batch):
        for j in range(pairs):
            state[b, j] = s[b][j]

    ones = ones_ref[...]
    inv = 1.0 / RWKV_HEAD
    for b in range(batch):
        y = y_rows[b]
        cen = y - _dot(y.astype(BF16), ones) * inv
        var = _dot((cen * cen).astype(BF16), ones) * inv
        yn = cen * lax.rsqrt(var + RWKV_GN_EPS) * lnw_ref[...] + lnb_ref[...]
        y_ref[b] = (yn + bonus_ref[b]) * gate_ref[b]


def _rwkv_scan(rhat, yhat, g_mat, h_mat, gate, bonus, ln_w, ln_b, head_ones, batch, seq):
    n, width = rhat.shape
    t = RWKV_CHUNK
    cps = RWKV_CHUNKS_PER_STEP
    pairs = width // LANES
    nc = seq // t
    tok3 = lambda a: a.reshape(batch, seq, width)
    tok = pl.BlockSpec((batch, cps * t, width), lambda i: (0, i, 0))
    out = pl.pallas_call(
        _rwkv_scan_kernel,
        grid=(nc // cps,),
        in_specs=[tok, tok,
                  pl.BlockSpec((batch, cps, pairs, LANES, LANES), lambda i: (0, i, 0, 0, 0)),
                  pl.BlockSpec((batch, cps, pairs, RWKV_HEAD, LANES), lambda i: (0, i, 0, 0, 0)),
                  tok, tok, _full(ln_w.shape), _full(ln_b.shape), _full(head_ones.shape)],
        out_specs=tok,
        out_shape=jax.ShapeDtypeStruct((batch, seq, width), F32),
        scratch_shapes=[pltpu.VMEM((batch, pairs, RWKV_HEAD, LANES), F32)],
        compiler_params=_params("arbitrary"),
        name="rwkv_scan",
    )(tok3(rhat), tok3(yhat), g_mat.reshape(batch, nc, pairs, LANES, LANES),
      h_mat.reshape(batch, nc, pairs, RWKV_HEAD, LANES), tok3(gate), tok3(bonus), ln_w, ln_b, head_ones)
    return out.reshape(n, width)


def _post_mix_kernel(x_ref, yc_ref, u_ref, yr_ref, d_ref, wglu_ref, wout_ref, gain_ref,
                     rw_ref, rb_ref, x1_o, hm_o, gate_o, lpos_o, meta_o, tot_o, running):
    tm = x_ref.shape[0]
    width = yc_ref.shape[1]

    @pl.when(pl.program_id(0) == 0)
    def _():
        running[...] = jnp.zeros_like(running)

    y = yc_ref[...].astype(F32) + d_ref[...] * u_ref[...]
    y = 0.5 * y * (1.0 + jnp.tanh(0.7978845608028654 * (y + 0.044715 * (y * y * y))))
    y = y * _sigmoid(_dot(y.astype(BF16), wglu_ref[...]))
    x1 = (x_ref[...] + _dot(y.astype(BF16), wout_ref[:width, :])
          + _dot(yr_ref[...].astype(BF16), wout_ref[width:, :]))
    x1_o[...] = x1
    hm = _rms(x1, gain_ref[...])
    hm_o[...] = hm.astype(BF16)
    logits = _dot(hm.astype(BF16), rw_ref[...]) + rb_ref[...]

    lane = lax.broadcasted_iota(jnp.int32, logits.shape, 1)
    lanef = lane.astype(F32)
    sel = jnp.zeros(logits.shape, F32)
    idx_cols, val_cols = [], []
    work = logits
    for _ in range(TOP_K):
        m = jnp.max(work, axis=-1, keepdims=True)
        pick = jnp.min(jnp.where(work == m, lanef, float(LANES)), axis=-1, keepdims=True)
        hit = lanef == pick
        sel = jnp.where(hit, 1.0, sel)
        work = jnp.where(hit, -jnp.inf, work)
        idx_cols.append(pick)
        val_cols.append(m)
    exps = [jnp.exp(vv - val_cols[0]) for vv in val_cols]
    denom = exps[0] + exps[1] + exps[2] + exps[3]

    row = lax.broadcasted_iota(jnp.int32, (tm, tm), 0)
    col = lax.broadcasted_iota(jnp.int32, (tm, tm), 1)
    before = jnp.where(row > col, 1.0, 0.0).astype(BF16)
    local = _dot(before, sel.astype(BF16))
    cnt = jnp.sum(sel, axis=0, keepdims=True)
    cnt_al = jnp.floor((cnt + (RUN_ALIGN - 1)) * (1.0 / RUN_ALIGN)) * RUN_ALIGN
    erow = lax.broadcasted_iota(jnp.int32, (LANES, LANES), 0)
    ecol = lax.broadcasted_iota(jnp.int32, (LANES, LANES), 1)
    upper = jnp.where(erow < ecol, 1.0, 0.0).astype(BF16)
    toff = _dot(jnp.broadcast_to(cnt_al, (SUBLANES, LANES)).astype(BF16), upper)[0:1]
    tbase = running[...]
    gate_out = jnp.zeros(logits.shape, F32)
    lpos_out = jnp.zeros(logits.shape, F32)
    for j in range(TOP_K):
        lp = jnp.sum(jnp.where(lanef == idx_cols[j], local + toff, 0.0), axis=-1, keepdims=True)
        gate_out = jnp.where(lane == j, exps[j] / denom, gate_out)
        lpos_out = jnp.where(lane == j, lp, lpos_out)
    gate_o[...] = gate_out
    lpos_o[...] = lpos_out.astype(jnp.int32)
    srow = lax.broadcasted_iota(jnp.int32, (SUBLANES, LANES), 0)
    meta = jnp.where(srow == 0, cnt, jnp.where(srow == 1, tbase, jnp.where(srow == 2, toff, 0.0)))
    meta_o[...] = meta.astype(jnp.int32)
    running[...] = tbase + cnt_al
    tot_o[...] = jnp.broadcast_to(running[...], tot_o.shape).astype(jnp.int32)


def _post_mix(x2d, yconv, u, y_rwkv, s5_d, w_glu, w_out, gain, rw, rb):
    n, d = x2d.shape
    width = yconv.shape[1]
    tm = MOE_TILE
    row = lambda i: (i, 0)
    tok_d = pl.BlockSpec((tm, d), row)
    tok_w = pl.BlockSpec((tm, width), row)
    tok_l = pl.BlockSpec((tm, LANES), row)
    return pl.pallas_call(
        _post_mix_kernel,
        grid=(n // tm,),
        in_specs=[tok_d, tok_w, tok_w, tok_w, _full(s5_d.shape), _full(w_glu.shape),
                  _full(w_out.shape), _full(gain.shape), _full(rw.shape), _full(rb.shape)],
        out_specs=[tok_d, tok_d, tok_l, tok_l,
                   pl.BlockSpec((SUBLANES, LANES), row), _full((SUBLANES, LANES))],
        out_shape=[jax.ShapeDtypeStruct((n, d), F32), jax.ShapeDtypeStruct((n, d), BF16),
                   jax.ShapeDtypeStruct((n, LANES), F32), jax.ShapeDtypeStruct((n, LANES), jnp.int32),
                   jax.ShapeDtypeStruct((n // tm * SUBLANES, LANES), jnp.int32),
                   jax.ShapeDtypeStruct((SUBLANES, LANES), jnp.int32)],
        scratch_shapes=[pltpu.VMEM((1, LANES), F32)],
        compiler_params=_params("arbitrary"),
        name="post_mix",
    )(x2d, yconv, u, y_rwkv, s5_d, w_glu, w_out, gain, rw, rb)


def _sorted_rows(tm, n_experts, multiple):
    return -(-(tm * TOP_K + n_experts * (RUN_ALIGN - 1)) // multiple) * multiple


RUN_BIG = 8


def _groups(count):
    return lax.shift_right_logical(count + (RUN_ALIGN - 1), RUN_ALIGN.bit_length() - 1)


def _start_runs(n_experts, pstart_ref, meta_ref, make_copy):
    shift = RUN_BIG.bit_length() - 1
    for e in range(n_experts):
        groups = _groups(meta_ref[0, e])
        seg = pstart_ref[e] + meta_ref[1, e]
        loc = meta_ref[2, e]
        n_big = lax.shift_right_logical(groups, shift)
        rem = groups & (RUN_BIG - 1)

        def body(g, c, seg=seg, loc=loc):
            off = g * (RUN_BIG * RUN_ALIGN)
            make_copy(pl.multiple_of(loc + off, RUN_ALIGN), pl.multiple_of(seg + off, RUN_ALIGN),
                      RUN_BIG * RUN_ALIGN).start(priority=e % 2)
            return c

        lax.fori_loop(0, n_big, body, 0)
        size = RUN_BIG // 2
        while size >= 1:
            off = (n_big * RUN_BIG + (rem & (RUN_BIG - 2 * size))) * RUN_ALIGN

            @pl.when((rem & size) != 0)
            def _(off=off, size=size, seg=seg, loc=loc):
                make_copy(pl.multiple_of(loc + off, RUN_ALIGN), pl.multiple_of(seg + off, RUN_ALIGN),
                          size * RUN_ALIGN).start(priority=e % 2)

            size //= 2


def _wait_runs(n_experts, meta_ref, make_copy):
    groups = 0
    for e in range(n_experts):
        groups = groups + _groups(meta_ref[0, e])

    def wait_big(g, c):
        make_copy(0, 0, RUN_BIG * RUN_ALIGN).wait()
        return c

    def wait_small(g, c):
        make_copy(0, 0, RUN_ALIGN).wait()
        return c

    lax.fori_loop(0, lax.shift_right_logical(groups, RUN_BIG.bit_length() - 1), wait_big, 0)
    lax.fori_loop(0, groups & (RUN_BIG - 1), wait_small, 0)


def _dispatch_kernel(n_experts, pstart_ref, tot_ref, meta_ref, meta_prev_ref, lpos_ref, hm_ref, xs_ref,
                     srt, zero, sem):
    tm = hm_ref.shape[0]
    rows = srt.shape[1]
    i = pl.program_id(0)
    cur = i % 2
    pos_t = jnp.transpose(lpos_ref[...].astype(F32))
    rid = lax.broadcasted_iota(jnp.int32, (rows, tm), 0).astype(F32)
    perm = jnp.zeros((rows, tm), F32)
    for j in range(TOP_K):
        perm = jnp.where(rid == pos_t[j:j + 1, :], 1.0, perm)
    srt[cur] = _dot(perm.astype(BF16), hm_ref[...])

    def copies(slot):
        def make_copy(loc, dst, size):
            return pltpu.make_async_copy(srt.at[slot, pl.ds(loc, size)], xs_ref.at[pl.ds(dst, size)],
                                         sem.at[slot])
        return make_copy

    _start_runs(n_experts, pstart_ref, meta_ref, copies(cur))

    @pl.when(i > 0)
    def _():
        _wait_runs(n_experts, meta_prev_ref, copies(1 - cur))

    @pl.when(i == pl.num_programs(0) - 1)
    def _():
        _wait_runs(n_experts, meta_ref, copies(cur))

    @pl.when(pl.program_id(0) == pl.num_programs(0) - 1)
    def _():
        zero[...] = jnp.zeros_like(zero)
        for wait in (False, True):
            for e in range(n_experts):
                used = tot_ref[e]
                start = pstart_ref[e] + used
                groups = lax.shift_right_logical((-used) & (EXPERT_TILE - 1), RUN_ALIGN.bit_length() - 1)

                def body(g, c, start=start):
                    cp = pltpu.make_async_copy(
                        zero, xs_ref.at[pl.ds(pl.multiple_of(start + g * RUN_ALIGN, RUN_ALIGN), RUN_ALIGN)],
                        sem.at[0])
                    if wait:
                        cp.wait()
                    else:
                        cp.start()
                    return c

                lax.fori_loop(0, groups, body, 0)

        last = n_experts - 1
        end = pstart_ref[last] + tot_ref[last] + ((-tot_ref[last]) & (EXPERT_TILE - 1))
        srt[0, 0:EXPERT_TILE, :] = jnp.zeros((EXPERT_TILE, srt.shape[2]), srt.dtype)
        for wait in (False, True):
            def tail(b, c):
                cp = pltpu.make_async_copy(
                    srt.at[0, pl.ds(0, EXPERT_TILE)],
                    xs_ref.at[pl.ds(pl.multiple_of(end + b * EXPERT_TILE, EXPERT_TILE), EXPERT_TILE)], sem.at[0])
                if wait:
                    cp.wait()
                else:
                    cp.start()
                return c

            lax.fori_loop(0, lax.shift_right_logical(xs_ref.shape[0] - end, EXPERT_TILE.bit_length() - 1), tail, 0)


def _dispatch(hm, lpos, meta, pstarts, totals, m_pad, n_experts):
    n, dh = hm.shape
    tm = MOE_TILE
    rows = _sorted_rows(tm, n_experts, RUN_ALIGN)
    grid_spec = pltpu.PrefetchScalarGridSpec(
        num_scalar_prefetch=2,
        grid=(n // tm,),
        in_specs=[pl.BlockSpec((SUBLANES, LANES), lambda i, ps, tt: (i, 0), memory_space=pltpu.SMEM),
                  pl.BlockSpec((SUBLANES, LANES), lambda i, ps, tt: (jnp.maximum(i - 1, 0), 0),
                               memory_space=pltpu.SMEM),
                  pl.BlockSpec((tm, LANES), lambda i, ps, tt: (i, 0)),
                  pl.BlockSpec((tm, dh), lambda i, ps, tt: (i, 0))],
        out_specs=pl.BlockSpec(memory_space=pl.ANY),
        scratch_shapes=[pltpu.VMEM((2, rows, dh), F32), pltpu.VMEM((RUN_ALIGN, dh), F32),
                        pltpu.SemaphoreType.DMA((2,))],
    )
    return pl.pallas_call(
        functools.partial(_dispatch_kernel, n_experts),
        grid_spec=grid_spec,
        out_shape=jax.ShapeDtypeStruct((m_pad, dh), F32),
        compiler_params=_params("arbitrary"),
        name="dispatch",
    )(pstarts, totals, meta, meta, lpos, hm)


def _experts_kernel(be_ref, nu_ref, xs_ref, wg_ref, bg_ref, wu_ref, bu_ref, wd_ref, bd_ref, ys_ref,
                    wg_s, wu_s, wd_s):
    i = pl.program_id(0)
    changed = jnp.logical_or(i == 0, be_ref[i] != be_ref[jnp.maximum(i - 1, 0)])

    @pl.when(changed)
    def _():
        wg_s[...] = wg_ref[0].astype(BF16)
        wu_s[...] = wu_ref[0].astype(BF16)
        wd_s[...] = wd_ref[0].astype(BF16)

    @pl.when(i < nu_ref[0])
    def _():
        xb = xs_ref[...].astype(BF16)
        gt = jnp.minimum(_dot(xb, wg_s[...]) + bg_ref[0], SWIGLU_LIMIT)
        up = jnp.clip(_dot(xb, wu_s[...]) + bu_ref[0], -SWIGLU_LIMIT, SWIGLU_LIMIT)
        act = (up + 1.0) * gt * _sigmoid(SWIGLU_ALPHA * gt)
        ys_ref[...] = _dot(act.astype(BF16), wd_s[...]) + bd_ref[0]

    @pl.when(i >= nu_ref[0])
    def _():
        ys_ref[...] = jnp.zeros_like(ys_ref)


def _experts(xs, block_e, n_used, wg, bg, wu, bu, wd, bd):
    m_pad, dh = xs.shape
    tmb = EXPERT_TILE
    d, dff = wg.shape[1], wg.shape[2]
    wmap = lambda i, be, nu: (be[i], 0, 0)
    grid_spec = pltpu.PrefetchScalarGridSpec(
        num_scalar_prefetch=2,
        grid=(m_pad // tmb,),
        in_specs=[pl.BlockSpec((tmb, dh), lambda i, be, nu: (jnp.where(i < nu[0], i, 0), 0)),
                  pl.BlockSpec((1, d, dff), wmap), pl.BlockSpec((1, 1, dff), wmap),
                  pl.BlockSpec((1, d, dff), wmap), pl.BlockSpec((1, 1, dff), wmap),
                  pl.BlockSpec((1, dff, d), wmap), pl.BlockSpec((1, 1, d), wmap)],
        out_specs=pl.BlockSpec((tmb, dh), lambda i, be, nu: (i, 0)),
        scratch_shapes=[pltpu.VMEM((d, dff), BF16), pltpu.VMEM((d, dff), BF16), pltpu.VMEM((dff, d), BF16)],
    )
    return pl.pallas_call(
        _experts_kernel,
        grid_spec=grid_spec,
        out_shape=jax.ShapeDtypeStruct((m_pad, dh), F32),
        compiler_params=_params("arbitrary"),
        name="experts",
    )(block_e, n_used, xs, wg, bg, wu, bu, wd, bd)


def _final_kernel(n_experts, pstart_ref, meta_ref, meta_next_ref, lpos_ref, x1_ref, gate_ref, p_ref, ys_ref,
                  gple_ref, wproj_ref, wgate_ref, gfin_ref, out_ref, buf, sem):
    tm = x1_ref.shape[0]
    rows = buf.shape[1]
    i = pl.program_id(0)
    cur = i % 2

    def copies(slot):
        def make_copy(loc, src, size):
            return pltpu.make_async_copy(ys_ref.at[pl.ds(src, size)], buf.at[slot, pl.ds(loc, size)],
                                         sem.at[slot])
        return make_copy

    @pl.when(i == 0)
    def _():
        buf[...] = jnp.zeros_like(buf)
        _start_runs(n_experts, pstart_ref, meta_ref, copies(0))

    @pl.when(i + 1 < pl.num_programs(0))
    def _():
        _start_runs(n_experts, pstart_ref, meta_next_ref, copies(1 - cur))

    cid = lax.broadcasted_iota(jnp.int32, (tm, rows), 1)
    lpos = lpos_ref[...]
    gates = gate_ref[...]
    comb = jnp.zeros((tm, rows), F32)
    for j in range(TOP_K):
        comb = jnp.where(cid == lpos[:, j:j + 1], gates[:, j:j + 1], comb)

    _wait_runs(n_experts, meta_ref, copies(cur))
    x2 = x1_ref[...] + _dot(comb.astype(BF16), buf[cur].astype(BF16))
    gate = _sigmoid(_dot(_rms(x2, gple_ref[...]).astype(BF16), wgate_ref[...]))
    x3 = x2 + _dot(p_ref[...].astype(BF16), wproj_ref[...]) * gate
    out_ref[...] = _rms(x3, gfin_ref[...])


def _final(x1, gates, lpos, meta, pstarts, p2d, ys, g_ple, w_proj, w_gate, g_fin, n_experts):
    n, d = x1.shape
    tm = MOE_TILE
    steps = n // tm
    rows = _sorted_rows(tm, n_experts, LANES)
    row = lambda i, ps: (i, 0)
    const = lambda shape: pl.BlockSpec(shape, lambda i, ps: (0,) * len(shape))
    grid_spec = pltpu.PrefetchScalarGridSpec(
        num_scalar_prefetch=1,
        grid=(steps,),
        in_specs=[pl.BlockSpec((SUBLANES, LANES), row, memory_space=pltpu.SMEM),
                  pl.BlockSpec((SUBLANES, LANES), lambda i, ps: (jnp.minimum(i + 1, steps - 1), 0),
                               memory_space=pltpu.SMEM),
                  pl.BlockSpec((tm, LANES), row), pl.BlockSpec((tm, d), row), pl.BlockSpec((tm, LANES), row),
                  pl.BlockSpec((tm, p2d.shape[1]), row), pl.BlockSpec(memory_space=pl.ANY),
                  const(g_ple.shape), const(w_proj.shape), const(w_gate.shape), const(g_fin.shape)],
        out_specs=pl.BlockSpec((tm, d), row),
        scratch_shapes=[pltpu.VMEM((2, rows, ys.shape[1]), ys.dtype), pltpu.SemaphoreType.DMA((2,))],
    )
    return pl.pallas_call(
        functools.partial(_final_kernel, n_experts),
        grid_spec=grid_spec,
        out_shape=jax.ShapeDtypeStruct((n, d), F32),
        compiler_params=_params("arbitrary"),
        name="final",
    )(pstarts, meta, meta, lpos, x1, gates, p2d, ys, g_ple, w_proj, w_gate, g_fin)


def _head_ones(width):
    hd = jnp.arange(width) // RWKV_HEAD
    return (hd[:, None] == hd[None, :]).astype(BF16)


def _layer(x2d, p2d, batch, seq, norm_mix_g, w_in, s5_lam_re, s5_lam_im, s5_log_dt, s5_b_re, s5_b_im,
           s5_c_re, s5_c_im, s5_d, s5_w_glu, mu_rkv, mu_wag, w0, w1, w2, a0, a1, a2, g1, g2, k_k, k_a,
           r_k, ln_w, ln_b, w_out, norm_moe_g, router_w, router_b, wg, bg, wu, bu, wd, bd,
           norm_ple_g, ple_w_proj, ple_w_gate):
    n, d = x2d.shape
    width = w_in.shape[1] // 4
    n_experts = router_w.shape[1]
    row2 = lambda t: t.reshape(1, -1).astype(F32)
    head_ones = _head_ones(width)
    vecs = jnp.zeros((SUBLANES, width), F32)
    vecs = vecs.at[0].set(w0).at[1].set(a0).at[2].set(k_k).at[3].set(k_a).at[4].set(r_k.reshape(-1))

    w_ext, w2_cat = _mix_in_weights(w_in, w1, a1, g1, mu_wag, w2, a2, g2)
    u, r, lw, k, v, an, bb, gate, bonus, u_bf = _mix_in(
        x2d, seq, row2(norm_mix_g), w_ext, mu_rkv.astype(F32), vecs, w2_cat, head_ones)

    tables = _s5_tables(s5_lam_re, s5_lam_im, s5_log_dt, s5_b_re, s5_b_im, s5_c_re, s5_c_im,
                        seq // S5_CHUNK)
    yconv = _s5_conv(u_bf, batch, seq, tables)

    rhat, yhat, g_mat, h_mat = _rwkv_chunks(r, lw, k, v, an, bb)
    y_rwkv = _rwkv_scan(rhat, yhat, g_mat, h_mat, gate, bonus, row2(ln_w), row2(ln_b), head_ones,
                        batch, seq)

    rw = jnp.zeros((d, LANES), BF16).at[:, :n_experts].set(router_w.astype(BF16))
    rb = jnp.full((1, LANES), NEG_BIG, F32).at[0, :n_experts].set(router_b.astype(F32))
    x1, hm, gates, lpos, meta, totals = _post_mix(
        x2d, yconv, u, y_rwkv, row2(s5_d), s5_w_glu.astype(BF16), w_out.astype(BF16),
        row2(norm_moe_g), rw, rb)

    tmb = EXPERT_TILE
    n_tiles = n // MOE_TILE
    max_rows = n * TOP_K + n_tiles * n_experts * (RUN_ALIGN - 1)
    n_blocks = -(-max_rows // tmb) + n_experts
    seg = totals[0, :n_experts].astype(jnp.int32)
    padded = ((seg + tmb - 1) // tmb) * tmb
    pends = jnp.cumsum(padded)
    pstarts = (pends - padded).astype(jnp.int32)
    block_start = jnp.arange(n_blocks, dtype=jnp.int32) * tmb
    block_e = jnp.minimum(jnp.sum((pends[None, :] <= block_start[:, None]).astype(jnp.int32), axis=1),
                          n_experts - 1).astype(jnp.int32)
    n_used = (pends[-1] // tmb).astype(jnp.int32).reshape(1)

    xs = _dispatch(hm, lpos, meta, pstarts, seg, n_blocks * tmb, n_experts)
    ys = _experts(xs, block_e, n_used, wg, bg.reshape(n_experts, 1, -1).astype(F32),
                  wu, bu.reshape(n_experts, 1, -1).astype(F32), wd,
                  bd.reshape(n_experts, 1, -1).astype(F32))
    return (x1, gates, lpos, meta, pstarts, ys, p2d, row2(norm_ple_g), ple_w_proj.astype(BF16),
            ple_w_gate.astype(BF16), n_experts)


def kernel(x, p, norm_mix_g, w_in, s5_lam_re, s5_lam_im, s5_log_dt, s5_b_re, s5_b_im, s5_c_re, s5_c_im, s5_d, s5_w_glu, rwkv_mu_rkv, rwkv_mu_wag, rwkv_w0, rwkv_w1, rwkv_w2, rwkv_a0, rwkv_a1, rwkv_a2, rwkv_g1, rwkv_g2, rwkv_k_k, rwkv_k_a, rwkv_r_k, rwkv_ln_w, rwkv_ln_b, w_out, norm_moe_g, router_w, router_b, exp_w_gate, exp_b_gate, exp_w_up, exp_b_up, exp_w_down, exp_b_down, norm_ple_g, ple_w_proj, ple_w_gate, final_norm_g):
    batch, seq, d = x.shape
    assert w_in.shape[0] == 1, "the final kernel fuses the last RMSNorm: single-layer stacks only"
    i = 0
    x1, gates, lpos, meta, pstarts, ys, p2d, g_ple, w_proj, w_gate, n_experts = _layer(
        x.reshape(batch * seq, d), p[i].reshape(batch * seq, -1), batch, seq, norm_mix_g[i], w_in[i],
        s5_lam_re[i], s5_lam_im[i], s5_log_dt[i], s5_b_re[i], s5_b_im[i], s5_c_re[i], s5_c_im[i],
        s5_d[i], s5_w_glu[i], rwkv_mu_rkv[i], rwkv_mu_wag[i], rwkv_w0[i], rwkv_w1[i], rwkv_w2[i],
        rwkv_a0[i], rwkv_a1[i], rwkv_a2[i], rwkv_g1[i], rwkv_g2[i], rwkv_k_k[i], rwkv_k_a[i],
        rwkv_r_k[i], rwkv_ln_w[i], rwkv_ln_b[i], w_out[i], norm_moe_g[i], router_w[i],
        router_b[i], exp_w_gate[i], exp_b_gate[i], exp_w_up[i], exp_b_up[i], exp_w_down[i],
        exp_b_down[i], norm_ple_g[i], ple_w_proj[i], ple_w_gate[i])
    out = _final(x1, gates, lpos, meta, pstarts, p2d, ys, g_ple, w_proj, w_gate,
                 final_norm_g.reshape(1, -1).astype(F32), n_experts)
    return out.reshape(batch, seq, d)
```

```python
import functools

import jax
import jax.numpy as jnp
from jax import lax
from jax.experimental import pallas as pl
from jax.experimental.pallas import tpu as pltpu

F32 = jnp.float32
BF16 = jnp.bfloat16

S5_GROUP = 16
S5_STATE = 64
RWKV_HEAD = 64
DECAY_LORA = 64
ICLR_LORA = 64
DECAY_SCALE = 0.6065306597126334
TOP_K = 4
RMS_EPS = 1e-6
RWKV_GN_EPS = 64e-5
LAMBDA_RE_MAX = -1e-4
SWIGLU_LIMIT = 7.0
SWIGLU_ALPHA = 1.702

LANES = 128
SUBLANES = 8
VMEM_LIMIT = 56 * 1024 * 1024

S5_CHUNK = 64
RWKV_CHUNK = 64
RWKV_SUB = 16
TOKEN_TILE = 512
EXPERT_TILE = 512
MOE_TILE = 512
RUN_ALIGN = 8
NEG_BIG = -1e30


def _dot(a, b):
    return jnp.dot(a, b, preferred_element_type=F32)


def _dot_nt(a, b):
    return lax.dot_general(a, b, (((1,), (1,)), ((), ())), preferred_element_type=F32)


def _dot_tn(a, b):
    return lax.dot_general(a, b, (((0,), (0,)), ((), ())), preferred_element_type=F32)


def _rms(t, gain):
    return t * lax.rsqrt(jnp.mean(t * t, axis=-1, keepdims=True) + RMS_EPS) * gain


def _sigmoid(t):
    return 1.0 / (1.0 + jnp.exp(-t))


def _params(*sem):
    return pltpu.CompilerParams(dimension_semantics=sem, vmem_limit_bytes=VMEM_LIMIT)


def _full(shape):
    return pl.BlockSpec(shape, lambda *_: (0,) * len(shape))


def _mix_in_kernel(seq, x_ref, xp_ref, gain_ref, win_ref, murkv_ref, vec_ref, w2_ref, ones_ref,
                   u_o, r_o, lw_o, k_o, v_o, an_o, bb_o, g_o, bonus_o, ub_o):
    tm = x_ref.shape[0]
    width = u_o.shape[1]
    lora = w2_ref.shape[0]
    gain = gain_ref[...]
    h = _rms(x_ref[...], gain)
    keep = jnp.where((pl.program_id(0) * tm) % seq == 0, 0.0, 1.0)
    hp = _rms(xp_ref[...], gain) * keep
    h_ext = jnp.concatenate([hp, h], axis=0).astype(BF16)
    proj = _dot(h_ext, win_ref[...])
    shifted = pltpu.roll(proj[:, width:4 * width + lora], 1, 0)[SUBLANES:]
    cur = proj[SUBLANES:]
    u_o[...] = cur[:, :width]
    ub_o[...] = cur[:, :width].astype(BF16)

    l1 = cur[:, 4 * width + lora:] + shifted[:, 3 * width:]
    lane = lax.broadcasted_iota(jnp.int32, l1.shape, 1)
    hidden = jnp.where(lane < DECAY_LORA, jnp.tanh(l1),
                       jnp.where(lane < DECAY_LORA + ICLR_LORA, l1, _sigmoid(l1))).astype(BF16)
    z3 = _dot(hidden, w2_ref[...])

    w0 = vec_ref[0:1, :]
    a0 = vec_ref[1:2, :]
    k_k = vec_ref[2:3, :]
    k_a = vec_ref[3:4, :]
    r_k = vec_ref[4:5, :]
    lw_o[...] = (-DECAY_SCALE) * _sigmoid(w0 + z3[:, :width])
    a = _sigmoid(a0 + z3[:, width:2 * width])
    g_o[...] = z3[:, 2 * width:]

    r_cur = cur[:, width:2 * width]
    k_cur = cur[:, 2 * width:3 * width]
    v_cur = cur[:, 3 * width:4 * width]
    r = r_cur + (shifted[:, :width] - r_cur) * murkv_ref[0:1, :]
    k = k_cur + (shifted[:, width:2 * width] - k_cur) * murkv_ref[1:2, :]
    v = v_cur + (shifted[:, 2 * width:3 * width] - v_cur) * murkv_ref[2:3, :]

    ones = ones_ref[...]
    kk = k * k_k
    ssq = _dot((kk * kk).astype(BF16), ones)
    kkn = kk * jnp.minimum(lax.rsqrt(ssq), 1e12)
    k2 = k * (1.0 + (a - 1.0) * k_a)
    r_o[...] = r
    k_o[...] = k2
    v_o[...] = v
    an_o[...] = -kkn
    bb_o[...] = kkn * a
    bonus_o[...] = _dot((r * k2 * r_k).astype(BF16), ones) * v


def _mix_in(x2d, seq, gain, w_ext, mu_rkv, vecs, w2_cat, head_ones):
    n, d = x2d.shape
    width = w2_cat.shape[1] // 3
    tm = TOKEN_TILE
    prev_blocks = tm // SUBLANES
    row = lambda i: (i, 0)
    out = jax.ShapeDtypeStruct((n, width), F32)
    return pl.pallas_call(
        functools.partial(_mix_in_kernel, seq),
        grid=(n // tm,),
        in_specs=[
            pl.BlockSpec((tm, d), row),
            pl.BlockSpec((SUBLANES, d), lambda i: (jnp.maximum(i * prev_blocks - 1, 0), 0)),
            _full(gain.shape), _full(w_ext.shape), _full(mu_rkv.shape), _full(vecs.shape),
            _full(w2_cat.shape), _full(head_ones.shape),
        ],
        out_specs=[pl.BlockSpec((tm, width), row)] * 10,
        out_shape=[out] * 9 + [jax.ShapeDtypeStruct((n, width), BF16)],
        compiler_params=_params("parallel"),
        name="mix_in",
    )(x2d, x2d, gain, w_ext, mu_rkv, vecs, w2_cat, head_ones)


def _mix_in_weights(w_in, w1, a1, g1, mu_wag, w2, a2, g2):
    l1 = jnp.concatenate([w1, a1, g1], axis=1).astype(F32)
    mu = jnp.concatenate([jnp.broadcast_to(mu_wag[j][:, None], (w.shape[0], w.shape[1]))
                          for j, w in enumerate((w1, a1, g1))], axis=1).astype(F32)
    w_ext = jnp.concatenate([w_in.astype(F32), mu * l1, (1.0 - mu) * l1], axis=1).astype(BF16)
    width = w2.shape[1]
    z = lambda rows: jnp.zeros((rows, width), F32)
    w2_cat = jnp.concatenate([
        jnp.concatenate([w2.astype(F32), z(w2.shape[0]), z(w2.shape[0])], axis=1),
        jnp.concatenate([z(a2.shape[0]), a2.astype(F32), z(a2.shape[0])], axis=1),
        jnp.concatenate([z(g2.shape[0]), z(g2.shape[0]), g2.astype(F32)], axis=1)], axis=0).astype(BF16)
    return w_ext, w2_cat


def _s5_tables(lam_re, lam_im, log_dt, b_re, b_im, c_re, c_im, n_chunks):
    t = S5_CHUNK
    hi = lax.Precision.HIGHEST
    lre = jnp.minimum(lam_re.astype(F32), LAMBDA_RE_MAX)
    lim = lam_im.astype(F32)
    dt = jnp.exp(log_dt.astype(F32))[:, None]
    mag = jnp.exp(lre * dt)
    lb_re = mag * jnp.cos(lim * dt)
    lb_im = mag * jnp.sin(lim * dt)
    den = lre * lre + lim * lim
    z_re = lb_re - 1.0
    coef_re = ((z_re * lre + lb_im * lim) / den)[..., None]
    coef_im = ((lb_im * lre - z_re * lim) / den)[..., None]
    b_re = b_re.astype(F32)
    b_im = b_im.astype(F32)
    bb_re = coef_re * b_re - coef_im * b_im
    bb_im = coef_re * b_im + coef_im * b_re
    c_re = c_re.astype(F32)
    c_im = c_im.astype(F32)

    def power(e):
        e = e.astype(F32)[:, None, None]
        m = jnp.exp(e * (lre * dt))
        ang = e * (lim * dt)
        return m * jnp.cos(ang), m * jnp.sin(ang)

    g = lre.shape[0]
    c = S5_GROUP
    p_re, p_im = power(jnp.arange(t + 1))
    cb_re = (c_re[:, None, :, :] * bb_re.transpose(0, 2, 1)[:, :, None, :]
             - c_im[:, None, :, :] * bb_im.transpose(0, 2, 1)[:, :, None, :])
    cb_im = (c_re[:, None, :, :] * bb_im.transpose(0, 2, 1)[:, :, None, :]
             + c_im[:, None, :, :] * bb_re.transpose(0, 2, 1)[:, :, None, :])
    cb = jnp.concatenate([cb_re, -cb_im], axis=-1).reshape(g, c * c, 2 * S5_STATE)
    pw = jnp.concatenate([p_re[:t], p_im[:t]], axis=-1).transpose(1, 2, 0)
    kvec = jnp.einsum('gkp,gpt->gkt', cb, pw, precision=hi).reshape(g, c, c // 2, 2 * t)
    r_re, r_im = power(t - 1 - jnp.arange(t))
    st_re = r_re[:, :, :, None] * bb_re - r_im[:, :, :, None] * bb_im
    st_im = r_re[:, :, :, None] * bb_im + r_im[:, :, :, None] * bb_re
    state_map = jnp.concatenate([st_re.transpose(1, 3, 0, 2).reshape(g, c * t, S5_STATE),
                                 st_im.transpose(1, 3, 0, 2).reshape(g, c * t, S5_STATE)],
                                axis=2).astype(BF16)
    q_re, q_im = p_re[1:], p_im[1:]
    ca_re = c_re[:, :, None, :] * q_re.transpose(1, 0, 2)[:, None] - c_im[:, :, None, :] * q_im.transpose(1, 0, 2)[:, None]
    ca_im = c_re[:, :, None, :] * q_im.transpose(1, 0, 2)[:, None] + c_im[:, :, None, :] * q_re.transpose(1, 0, 2)[:, None]
    cross = jnp.concatenate([ca_re.transpose(0, 3, 1, 2).reshape(g, S5_STATE, c * t),
                             -ca_im.transpose(0, 3, 1, 2).reshape(g, S5_STATE, c * t)], axis=1).astype(BF16)
    n_steps = max(1, (n_chunks - 1).bit_length())
    s_re, s_im = power(t * (2 ** jnp.arange(n_steps)))
    tab = jnp.stack([jnp.concatenate([s_re, s_re], axis=-1),
                     jnp.concatenate([-s_im, s_im], axis=-1)], axis=1)
    tab = tab.transpose(2, 0, 1, 3).reshape(g, 2 * n_steps, 2 * S5_STATE)
    return kvec, state_map, cross, tab, n_steps


def _s5_conv_kernel(n_chunks, n_steps, u_ref, kvec_ref, st_ref, cross_ref, tab_ref, y_ref, conv):
    t = S5_CHUNK
    row = lax.broadcasted_iota(jnp.int32, (t, LANES), 0)
    lane = lax.broadcasted_iota(jnp.int32, (t, LANES), 1)
    causal = jnp.bitwise_and(lane, t - 1) >= row
    for cin in range(S5_GROUP):
        for cp in range(S5_GROUP // 2):
            base = jnp.broadcast_to(kvec_ref[0, cin, cp:cp + 1, :], (t, LANES))
            blk = pltpu.roll(base, 0, 1, stride=1, stride_axis=0)
            conv[cin * t:(cin + 1) * t, cp * LANES:(cp + 1) * LANES] = jnp.where(causal, blk, 0.0).astype(BF16)
    u = jnp.concatenate([u_ref[0, cin] for cin in range(S5_GROUP)], axis=1)
    y = _dot(u, conv[...])
    x = _dot(u, st_ref[0])
    chunk = lax.broadcasted_iota(jnp.int32, x.shape, 0) % n_chunks
    half = x.shape[1] // 2
    for j in range(n_steps):
        sh = 1 << j
        xs = jnp.where(chunk >= sh, pltpu.roll(x, sh, 0), 0.0)
        x = x + tab_ref[0, 2 * j:2 * j + 1, :] * xs + tab_ref[0, 2 * j + 1:2 * j + 2, :] * pltpu.roll(xs, half, 1)
    x_in = jnp.where(chunk >= 1, pltpu.roll(x, 1, 0), 0.0)
    y = (y + _dot(x_in.astype(BF16), cross_ref[0])).astype(y_ref.dtype)
    for cout in range(S5_GROUP):
        y_ref[0, cout] = y[:, cout * t:(cout + 1) * t]


def _s5_conv(u2d, batch, seq, tables):
    kvec, state_map, cross, tab, n_steps = tables
    n, width = u2d.shape
    g = width // S5_GROUP
    t = S5_CHUNK
    n_chunks = seq // t
    ct = S5_GROUP * t
    ut = u2d.T.reshape(g, S5_GROUP, batch * n_chunks, t)
    rows = batch * n_chunks
    slab = pl.BlockSpec((1, S5_GROUP, rows, t), lambda i: (i, 0, 0, 0))
    y = pl.pallas_call(
        functools.partial(_s5_conv_kernel, n_chunks, n_steps),
        grid=(g,),
        in_specs=[
            slab,
            pl.BlockSpec((1,) + kvec.shape[1:], lambda i: (i, 0, 0, 0)),
            pl.BlockSpec((1,) + state_map.shape[1:], lambda i: (i, 0, 0)),
            pl.BlockSpec((1,) + cross.shape[1:], lambda i: (i, 0, 0)),
            pl.BlockSpec((1,) + tab.shape[1:], lambda i: (i, 0, 0)),
        ],
        out_specs=slab,
        out_shape=jax.ShapeDtypeStruct((g, S5_GROUP, rows, t), BF16),
        scratch_shapes=[pltpu.VMEM((ct, ct), BF16)],
        compiler_params=_params("parallel"),
        name="s5_conv",
    )(ut, kvec, state_map, cross, tab)
    return y.reshape(width, n).T


def _pair_blockdiag(y, left):
    return jnp.concatenate([jnp.where(left, y, 0.0), jnp.where(left, 0.0, y)], axis=0).astype(BF16)


def _rwkv_chunk_kernel(r_ref, lw_ref, k_ref, v_ref, an_ref, bb_ref, rhat_o, yhat_o, g_o, h_o):
    t = RWKV_CHUNK
    rows, width = r_ref.shape
    n_chunks = rows // t
    pairs = width // LANES
    row = lax.broadcasted_iota(jnp.int32, (t, LANES), 0)
    lane = lax.broadcasted_iota(jnp.int32, (t, LANES), 1)
    col = jnp.bitwise_and(lane, RWKV_HEAD - 1)
    left = lane < RWKV_HEAD
    incl = row >= col
    strict = row > col
    same_blk = (row // RWKV_SUB) == (col // RWKV_SUB)
    eye = jnp.where(row == col, 1.0, 0.0).astype(F32)
    brow = lax.broadcasted_iota(jnp.int32, (LANES, LANES), 0)
    bcol = lax.broadcasted_iota(jnp.int32, (LANES, LANES), 1)
    same_head = (brow // RWKV_HEAD) == (bcol // RWKV_HEAD)
    eye_full = brow == bcol
    crow = lax.broadcasted_iota(jnp.int32, (rows, rows), 0)
    ccol = lax.broadcasted_iota(jnp.int32, (rows, rows), 1)
    tril = jnp.where(crow >= ccol, jnp.where((crow // t) == (ccol // t), 1.0, 0.0), 0.0).astype(BF16)

    lw = lw_ref[...]
    p1 = lw.astype(BF16)
    rem = lw - p1.astype(F32)
    p2 = rem.astype(BF16)
    p3 = (rem - p2.astype(F32)).astype(BF16)
    cs = _dot(tril, p1) + _dot(tril, p2) + _dot(tril, p3)
    a_t = an_ref[...] * jnp.exp(cs - lw)
    r_t = r_ref[...] * jnp.exp(cs)
    p_inv = jnp.exp(-cs)
    b_t = bb_ref[...] * p_inv
    k_t = k_ref[...] * p_inv

    def bd(y):
        return _pair_blockdiag(y, left)

    def pmm(x, y_bd):
        return _dot(x.astype(BF16), y_bd)

    units = [(c, j) for c in range(n_chunks) for j in range(pairs)]

    def tile(arr, c, j):
        return arr[c * t:(c + 1) * t, j * LANES:(j + 1) * LANES]

    a2 = [tile(a_t, c, j) for c, j in units]
    r2 = [tile(r_t, c, j) for c, j in units]
    v2 = [tile(v_ref[...], c, j) for c, j in units]
    ar = [jnp.concatenate([a, r], axis=0).astype(BF16) for a, r in zip(a2, r2)]
    prod = [_dot_nt(x, jnp.concatenate([bd(tile(b_t, c, j)), bd(tile(k_t, c, j))], axis=0))
            for x, (c, j) in zip(ar, units)]
    l_ab = [jnp.where(strict, p[:t, :LANES], 0.0) for p in prod]
    l_rb = [jnp.where(incl, p[t:, :LANES], 0.0) for p in prod]
    l_ak = [jnp.where(strict, p[:t, LANES:], 0.0) for p in prod]
    l_rk = [jnp.where(incl, p[t:, LANES:], 0.0) for p in prod]
    l_d = [jnp.where(same_blk, m, 0.0) for m in l_ab]
    l_o = [m - d for m, d in zip(l_ab, l_d)]
    l2 = [pmm(m, bd(m)) for m in l_d]
    q1 = [pmm(eye + m, bd(eye + s)) for m, s in zip(l_d, l2)]
    l4 = [pmm(m, bd(m)) for m in l2]
    q2 = [pmm(q, bd(eye + s)) for q, s in zip(q1, l4)]
    l8 = [pmm(m, bd(m)) for m in l4]
    d_inv = [pmm(q, bd(eye + s)) for q, s in zip(q2, l8)]
    akv = [pmm(jnp.concatenate([m, n], axis=0), bd(v)) for m, n, v in zip(l_ak, l_rk, v2)]
    x1 = [z[:t] for z in akv]

    def bd2(z1, z2):
        return jnp.concatenate([bd(z1), bd(z2)], axis=1)

    dz = [pmm(d, bd2(a, x)) for d, a, x in zip(d_inv, a2, x1)]
    n1 = [pmm(d, bd(o)) for d, o in zip(d_inv, l_o)]
    n2 = [pmm(m, bd(m)) for m in n1]
    t1 = [z + pmm(m, bd2(z[:, :LANES], z[:, LANES:])) for z, m in zip(dz, n2)]
    wu = [z + pmm(m, bd2(z[:, :LANES], z[:, LANES:])) for z, m in zip(t1, n1)]
    ry = [pmm(m, bd2(z[:, :LANES], z[:, LANES:])) for m, z in zip(l_rb, wu)]
    rk_v = [z[t:] for z in akv]

    for i, (c, j) in enumerate(units):
        rs = slice(c * t, (c + 1) * t)
        ls = slice(j * LANES, (j + 1) * LANES)
        rhat_o[rs, ls] = r2[i] + ry[i][:, :LANES]
        yhat_o[rs, ls] = ry[i][:, LANES:] + rk_v[i]
        cs_c = cs[rs, ls]
        cs_end = cs_c[t - 1:t, :]
        p_end = jnp.exp(cs_end - cs_c)
        b_h = (bb_ref[rs, ls] * p_end).astype(BF16)
        k_h = (k_ref[rs, ls] * p_end).astype(BF16)
        g_full = _dot_tn(wu[i][:, :LANES].astype(BF16), b_h)
        g_bd = jnp.where(same_head, g_full, 0.0) + jnp.where(eye_full, jnp.exp(cs_end), 0.0)
        g_o[c, j] = g_bd.astype(BF16)
        uv = jnp.concatenate([wu[i][:, LANES:], v2[i]], axis=0).astype(BF16)
        h_full = _dot_tn(uv, jnp.concatenate([b_h, k_h], axis=0))
        h_o[c, j] = jnp.where(left, h_full[:RWKV_HEAD], h_full[RWKV_HEAD:])


RWKV_CHUNKS_PER_STEP = 4


def _rwkv_chunks(r, lw, k, v, an, bb):
    n, width = r.shape
    t = RWKV_CHUNK
    cps = RWKV_CHUNKS_PER_STEP
    pairs = width // LANES
    nck = n // t
    tok = pl.BlockSpec((cps * t, width), lambda i: (i, 0))
    return pl.pallas_call(
        _rwkv_chunk_kernel,
        grid=(nck // cps,),
        in_specs=[tok] * 6,
        out_specs=[tok, tok,
                   pl.BlockSpec((cps, pairs, LANES, LANES), lambda i: (i, 0, 0, 0)),
                   pl.BlockSpec((cps, pairs, RWKV_HEAD, LANES), lambda i: (i, 0, 0, 0))],
        out_shape=[jax.ShapeDtypeStruct((n, width), F32)] * 2
        + [jax.ShapeDtypeStruct((nck, pairs, LANES, LANES), BF16),
           jax.ShapeDtypeStruct((nck, pairs, RWKV_HEAD, LANES), F32)],
        compiler_params=_params("parallel"),
        name="rwkv_chunk",
    )(r, lw, k, v, an, bb)


def _rwkv_scan_kernel(rhat_ref, yhat_ref, g_ref, h_ref, gate_ref, bonus_ref, lnw_ref, lnb_ref,
                      ones_ref, y_ref, state):
    t = RWKV_CHUNK
    batch, pairs = state.shape[0], state.shape[1]
    n_chunks = rhat_ref.shape[1] // t
    lane = lax.broadcasted_iota(jnp.int32, (RWKV_HEAD, LANES), 1)
    left = lane < RWKV_HEAD

    @pl.when(pl.program_id(0) == 0)
    def _():
        state[...] = jnp.zeros_like(state)

    s = [[state[b, j] for j in range(pairs)] for b in range(batch)]
    y_rows = []
    for b in range(batch):
        chunk_rows = []
        for c in range(n_chunks):
            rs = slice(c * t, (c + 1) * t)
            tiles = []
            for j in range(pairs):
                ls = slice(j * LANES, (j + 1) * LANES)
                s_bd = _pair_blockdiag(s[b][j], left)
                tiles.append(yhat_ref[b, rs, ls] + _dot_nt(rhat_ref[b, rs, ls].astype(BF16), s_bd))
                s[b][j] = _dot(s[b][j].astype(BF16), g_ref[b, c, j]) + h_ref[b, c, j]
            chunk_rows.append(jnp.concatenate(tiles, axis=1))
        y_rows.append(jnp.concatenate(chunk_rows, axis=0))
    for b in range(batch):
        for j in range(pairs):
            state[b, j] = s[b][j]

    ones = ones_ref[...]
    inv = 1.0 / RWKV_HEAD
    for b in range(batch):
        y = y_rows[b]
        cen = y - _dot(y.astype(BF16), ones) * inv
        var = _dot((cen * cen).astype(BF16), ones) * inv
        yn = cen * lax.rsqrt(var + RWKV_GN_EPS) * lnw_ref[...] + lnb_ref[...]
        y_ref[b] = (yn + bonus_ref[b]) * gate_ref[b]


def _rwkv_scan(rhat, yhat, g_mat, h_mat, gate, bonus, ln_w, ln_b, head_ones, batch, seq):
    n, width = rhat.shape
    t = RWKV_CHUNK
    cps = RWKV_CHUNKS_PER_STEP
    pairs = width // LANES
    nc = seq // t
    tok3 = lambda a: a.reshape(batch, seq, width)
    tok = pl.BlockSpec((batch, cps * t, width), lambda i: (0, i, 0))
    out = pl.pallas_call(
        _rwkv_scan_kernel,
        grid=(nc // cps,),
        in_specs=[tok, tok,
                  pl.BlockSpec((batch, cps, pairs, LANES, LANES), lambda i: (0, i, 0, 0, 0)),
                  pl.BlockSpec((batch, cps, pairs, RWKV_HEAD, LANES), lambda i: (0, i, 0, 0, 0)),
                  tok, tok, _full(ln_w.shape), _full(ln_b.shape), _full(head_ones.shape)],
        out_specs=tok,
        out_shape=jax.ShapeDtypeStruct((batch, seq, width), F32),
        scratch_shapes=[pltpu.VMEM((batch, pairs, RWKV_HEAD, LANES), F32)],
        compiler_params=_params("arbitrary"),
        name="rwkv_scan",
    )(tok3(rhat), tok3(yhat), g_mat.reshape(batch, nc, pairs, LANES, LANES),
      h_mat.reshape(batch, nc, pairs, RWKV_HEAD, LANES), tok3(gate), tok3(bonus), ln_w, ln_b, head_ones)
    return out.reshape(n, width)


def _post_mix_kernel(x_ref, yc_ref, u_ref, yr_ref, d_ref, wglu_ref, wout_ref, gain_ref,
                     rw_ref, rb_ref, x1_o, hm_o, gate_o, lpos_o, meta_o, tot_o, running):
    tm = x_ref.shape[0]
    width = yc_ref.shape[1]

    @pl.when(pl.program_id(0) == 0)
    def _():
        running[...] = jnp.zeros_like(running)

    y = yc_ref[...].astype(F32) + d_ref[...] * u_ref[...]
    y = 0.5 * y * (1.0 + jnp.tanh(0.7978845608028654 * (y + 0.044715 * (y * y * y))))
    y = y * _sigmoid(_dot(y.astype(BF16), wglu_ref[...]))
    x1 = (x_ref[...] + _dot(y.astype(BF16), wout_ref[:width, :])
          + _dot(yr_ref[...].astype(BF16), wout_ref[width:, :]))
    x1_o[...] = x1
    hm = _rms(x1, gain_ref[...])
    hm_o[...] = hm.astype(BF16)
    logits = _dot(hm.astype(BF16), rw_ref[...]) + rb_ref[...]

    lane = lax.broadcasted_iota(jnp.int32, logits.shape, 1)
    lanef = lane.astype(F32)
    sel = jnp.zeros(logits.shape, F32)
    idx_cols, val_cols = [], []
    work = logits
    for _ in range(TOP_K):
        m = jnp.max(work, axis=-1, keepdims=True)
        pick = jnp.min(jnp.where(work == m, lanef, float(LANES)), axis=-1, keepdims=True)
        hit = lanef == pick
        sel = jnp.where(hit, 1.0, sel)
        work = jnp.where(hit, -jnp.inf, work)
        idx_cols.append(pick)
        val_cols.append(m)
    exps = [jnp.exp(vv - val_cols[0]) for vv in val_cols]
    denom = exps[0] + exps[1] + exps[2] + exps[3]

    row = lax.broadcasted_iota(jnp.int32, (tm, tm), 0)
    col = lax.broadcasted_iota(jnp.int32, (tm, tm), 1)
    before = jnp.where(row > col, 1.0, 0.0).astype(BF16)
    local = _dot(before, sel.astype(BF16))
    cnt = jnp.sum(sel, axis=0, keepdims=True)
    cnt_al = jnp.floor((cnt + (RUN_ALIGN - 1)) * (1.0 / RUN_ALIGN)) * RUN_ALIGN
    erow = lax.broadcasted_iota(jnp.int32, (LANES, LANES), 0)
    ecol = lax.broadcasted_iota(jnp.int32, (LANES, LANES), 1)
    upper = jnp.where(erow < ecol, 1.0, 0.0).astype(BF16)
    toff = _dot(jnp.broadcast_to(cnt_al, (SUBLANES, LANES)).astype(BF16), upper)[0:1]
    tbase = running[...]
    gate_out = jnp.zeros(logits.shape, F32)
    lpos_out = jnp.zeros(logits.shape, F32)
    for j in range(TOP_K):
        lp = jnp.sum(jnp.where(lanef == idx_cols[j], local + toff, 0.0), axis=-1, keepdims=True)
        gate_out = jnp.where(lane == j, exps[j] / denom, gate_out)
        lpos_out = jnp.where(lane == j, lp, lpos_out)
    gate_o[...] = gate_out
    lpos_o[...] = lpos_out.astype(jnp.int32)
    srow = lax.broadcasted_iota(jnp.int32, (SUBLANES, LANES), 0)
    meta = jnp.where(srow == 0, cnt, jnp.where(srow == 1, tbase, jnp.where(srow == 2, toff, 0.0)))
    meta_o[...] = meta.astype(jnp.int32)
    running[...] = tbase + cnt_al
    tot_o[...] = jnp.broadcast_to(running[...], tot_o.shape).astype(jnp.int32)


def _post_mix(x2d, yconv, u, y_rwkv, s5_d, w_glu, w_out, gain, rw, rb):
    n, d = x2d.shape
    width = yconv.shape[1]
    tm = MOE_TILE
    row = lambda i: (i, 0)
    tok_d = pl.BlockSpec((tm, d), row)
    tok_w = pl.BlockSpec((tm, width), row)
    tok_l = pl.BlockSpec((tm, LANES), row)
    return pl.pallas_call(
        _post_mix_kernel,
        grid=(n // tm,),
        in_specs=[tok_d, tok_w, tok_w, tok_w, _full(s5_d.shape), _full(w_glu.shape),
                  _full(w_out.shape), _full(gain.shape), _full(rw.shape), _full(rb.shape)],
        out_specs=[tok_d, tok_d, tok_l, tok_l,
                   pl.BlockSpec((SUBLANES, LANES), row), _full((SUBLANES, LANES))],
        out_shape=[jax.ShapeDtypeStruct((n, d), F32), jax.ShapeDtypeStruct((n, d), BF16),
                   jax.ShapeDtypeStruct((n, LANES), F32), jax.ShapeDtypeStruct((n, LANES), jnp.int32),
                   jax.ShapeDtypeStruct((n // tm * SUBLANES, LANES), jnp.int32),
                   jax.ShapeDtypeStruct((SUBLANES, LANES), jnp.int32)],
        scratch_shapes=[pltpu.VMEM((1, LANES), F32)],
        compiler_params=_params("arbitrary"),
        name="post_mix",
    )(x2d, yconv, u, y_rwkv, s5_d, w_glu, w_out, gain, rw, rb)


def _sorted_rows(tm, n_experts, multiple):
    return -(-(tm * TOP_K + n_experts * (RUN_ALIGN - 1)) // multiple) * multiple


RUN_BIG = 8


def _groups(count):
    return lax.shift_right_logical(count + (RUN_ALIGN - 1), RUN_ALIGN.bit_length() - 1)


def _start_runs(n_experts, pstart_ref, meta_ref, make_copy):
    shift = RUN_BIG.bit_length() - 1
    for e in range(n_experts):
        groups = _groups(meta_ref[0, e])
        seg = pstart_ref[e] + meta_ref[1, e]
        loc = meta_ref[2, e]
        n_big = lax.shift_right_logical(groups, shift)
        rem = groups & (RUN_BIG - 1)

        def body(g, c, seg=seg, loc=loc):
            off = g * (RUN_BIG * RUN_ALIGN)
            make_copy(pl.multiple_of(loc + off, RUN_ALIGN), pl.multiple_of(seg + off, RUN_ALIGN),
                      RUN_BIG * RUN_ALIGN).start(priority=e % 2)
            return c

        lax.fori_loop(0, n_big, body, 0)
        size = RUN_BIG // 2
        while size >= 1:
            off = (n_big * RUN_BIG + (rem & (RUN_BIG - 2 * size))) * RUN_ALIGN

            @pl.when((rem & size) != 0)
            def _(off=off, size=size, seg=seg, loc=loc):
                make_copy(pl.multiple_of(loc + off, RUN_ALIGN), pl.multiple_of(seg + off, RUN_ALIGN),
                          size * RUN_ALIGN).start(priority=e % 2)

            size //= 2


def _wait_runs(n_experts, meta_ref, make_copy):
    groups = 0
    for e in range(n_experts):
        groups = groups + _groups(meta_ref[0, e])

    def wait_big(g, c):
        make_copy(0, 0, RUN_BIG * RUN_ALIGN).wait()
        return c

    def wait_small(g, c):
        make_copy(0, 0, RUN_ALIGN).wait()
        return c

    lax.fori_loop(0, lax.shift_right_logical(groups, RUN_BIG.bit_length() - 1), wait_big, 0)
    lax.fori_loop(0, groups & (RUN_BIG - 1), wait_small, 0)


def _dispatch_kernel(n_experts, pstart_ref, tot_ref, meta_ref, meta_prev_ref, lpos_ref, hm_ref, xs_ref,
                     srt, zero, sem):
    tm = hm_ref.shape[0]
    rows = srt.shape[1]
    i = pl.program_id(0)
    cur = i % 2
    pos_t = jnp.transpose(lpos_ref[...].astype(F32))
    rid = lax.broadcasted_iota(jnp.int32, (rows, tm), 0).astype(F32)
    perm = jnp.zeros((rows, tm), F32)
    for j in range(TOP_K):
        perm = jnp.where(rid == pos_t[j:j + 1, :], 1.0, perm)
    srt[cur] = _dot(perm.astype(BF16), hm_ref[...])

    def copies(slot):
        def make_copy(loc, dst, size):
            return pltpu.make_async_copy(srt.at[slot, pl.ds(loc, size)], xs_ref.at[pl.ds(dst, size)],
                                         sem.at[slot])
        return make_copy

    _start_runs(n_experts, pstart_ref, meta_ref, copies(cur))

    @pl.when(i > 0)
    def _():
        _wait_runs(n_experts, meta_prev_ref, copies(1 - cur))

    @pl.when(i == pl.num_programs(0) - 1)
    def _():
        _wait_runs(n_experts, meta_ref, copies(cur))

    @pl.when(pl.program_id(0) == pl.num_programs(0) - 1)
    def _():
        zero[...] = jnp.zeros_like(zero)
        for wait in (False, True):
            for e in range(n_experts):
                used = tot_ref[e]
                start = pstart_ref[e] + used
                groups = lax.shift_right_logical((-used) & (EXPERT_TILE - 1), RUN_ALIGN.bit_length() - 1)

                def body(g, c, start=start):
                    cp = pltpu.make_async_copy(
                        zero, xs_ref.at[pl.ds(pl.multiple_of(start + g * RUN_ALIGN, RUN_ALIGN), RUN_ALIGN)],
                        sem.at[0])
                    if wait:
                        cp.wait()
                    else:
                        cp.start()
                    return c

                lax.fori_loop(0, groups, body, 0)

        last = n_experts - 1
        end = pstart_ref[last] + tot_ref[last] + ((-tot_ref[last]) & (EXPERT_TILE - 1))
        srt[0, 0:EXPERT_TILE, :] = jnp.zeros((EXPERT_TILE, srt.shape[2]), srt.dtype)
        for wait in (False, True):
            def tail(b, c):
                cp = pltpu.make_async_copy(
                    srt.at[0, pl.ds(0, EXPERT_TILE)],
                    xs_ref.at[pl.ds(pl.multiple_of(end + b * EXPERT_TILE, EXPERT_TILE), EXPERT_TILE)], sem.at[0])
                if wait:
                    cp.wait()
                else:
                    cp.start()
                return c

            lax.fori_loop(0, lax.shift_right_logical(xs_ref.shape[0] - end, EXPERT_TILE.bit_length() - 1), tail, 0)


def _dispatch(hm, lpos, meta, pstarts, totals, m_pad, n_experts):
    n, dh = hm.shape
    tm = MOE_TILE
    rows = _sorted_rows(tm, n_experts, RUN_ALIGN)
    grid_spec = pltpu.PrefetchScalarGridSpec(
        num_scalar_prefetch=2,
        grid=(n // tm,),
        in_specs=[pl.BlockSpec((SUBLANES, LANES), lambda i, ps, tt: (i, 0), memory_space=pltpu.SMEM),
                  pl.BlockSpec((SUBLANES, LANES), lambda i, ps, tt: (jnp.maximum(i - 1, 0), 0),
                               memory_space=pltpu.SMEM),
                  pl.BlockSpec((tm, LANES), lambda i, ps, tt: (i, 0)),
                  pl.BlockSpec((tm, dh), lambda i, ps, tt: (i, 0))],
        out_specs=pl.BlockSpec(memory_space=pl.ANY),
        scratch_shapes=[pltpu.VMEM((2, rows, dh), F32), pltpu.VMEM((RUN_ALIGN, dh), F32),
                        pltpu.SemaphoreType.DMA((2,))],
    )
    return pl.pallas_call(
        functools.partial(_dispatch_kernel, n_experts),
        grid_spec=grid_spec,
        out_shape=jax.ShapeDtypeStruct((m_pad, dh), F32),
        compiler_params=_params("arbitrary"),
        name="dispatch",
    )(pstarts, totals, meta, meta, lpos, hm)


def _experts_kernel(be_ref, nu_ref, xs_ref, wg_ref, bg_ref, wu_ref, bu_ref, wd_ref, bd_ref, ys_ref,
                    wg_s, wu_s, wd_s):
    i = pl.program_id(0)
    changed = jnp.logical_or(i == 0, be_ref[i] != be_ref[jnp.maximum(i - 1, 0)])

    @pl.when(changed)
    def _():
        wg_s[...] = wg_ref[0].astype(BF16)
        wu_s[...] = wu_ref[0].astype(BF16)
        wd_s[...] = wd_ref[0].astype(BF16)

    @pl.when(i < nu_ref[0])
    def _():
        xb = xs_ref[...].astype(BF16)
        gt = jnp.minimum(_dot(xb, wg_s[...]) + bg_ref[0], SWIGLU_LIMIT)
        up = jnp.clip(_dot(xb, wu_s[...]) + bu_ref[0], -SWIGLU_LIMIT, SWIGLU_LIMIT)
        act = (up + 1.0) * gt * _sigmoid(SWIGLU_ALPHA * gt)
        ys_ref[...] = _dot(act.astype(BF16), wd_s[...]) + bd_ref[0]

    @pl.when(i >= nu_ref[0])
    def _():
        ys_ref[...] = jnp.zeros_like(ys_ref)


def _experts(xs, block_e, n_used, wg, bg, wu, bu, wd, bd):
    m_pad, dh = xs.shape
    tmb = EXPERT_TILE
    d, dff = wg.shape[1], wg.shape[2]
    wmap = lambda i, be, nu: (be[i], 0, 0)
    grid_spec = pltpu.PrefetchScalarGridSpec(
        num_scalar_prefetch=2,
        grid=(m_pad // tmb,),
        in_specs=[pl.BlockSpec((tmb, dh), lambda i, be, nu: (jnp.where(i < nu[0], i, 0), 0)),
                  pl.BlockSpec((1, d, dff), wmap), pl.BlockSpec((1, 1, dff), wmap),
                  pl.BlockSpec((1, d, dff), wmap), pl.BlockSpec((1, 1, dff), wmap),
                  pl.BlockSpec((1, dff, d), wmap), pl.BlockSpec((1, 1, d), wmap)],
        out_specs=pl.BlockSpec((tmb, dh), lambda i, be, nu: (i, 0)),
        scratch_shapes=[pltpu.VMEM((d, dff), BF16), pltpu.VMEM((d, dff), BF16), pltpu.VMEM((dff, d), BF16)],
    )
    return pl.pallas_call(
        _experts_kernel,
        grid_spec=grid_spec,
        out_shape=jax.ShapeDtypeStruct((m_pad, dh), F32),
        compiler_params=_params("arbitrary"),
        name="experts",
    )(block_e, n_used, xs, wg, bg, wu, bu, wd, bd)


def _final_kernel(n_experts, pstart_ref, meta_ref, meta_next_ref, lpos_ref, x1_ref, gate_ref, p_ref, ys_ref,
                  gple_ref, wproj_ref, wgate_ref, gfin_ref, out_ref, buf, sem):
    tm = x1_ref.shape[0]
    rows = buf.shape[1]
    i = pl.program_id(0)
    cur = i % 2

    def copies(slot):
        def make_copy(loc, src, size):
            return pltpu.make_async_copy(ys_ref.at[pl.ds(src, size)], buf.at[slot, pl.ds(loc, size)],
                                         sem.at[slot])
        return make_copy

    @pl.when(i == 0)
    def _():
        buf[...] = jnp.zeros_like(buf)
        _start_runs(n_experts, pstart_ref, meta_ref, copies(0))

    @pl.when(i + 1 < pl.num_programs(0))
    def _():
        _start_runs(n_experts, pstart_ref, meta_next_ref, copies(1 - cur))

    cid = lax.broadcasted_iota(jnp.int32, (tm, rows), 1)
    lpos = lpos_ref[...]
    gates = gate_ref[...]
    comb = jnp.zeros((tm, rows), F32)
    for j in range(TOP_K):
        comb = jnp.where(cid == lpos[:, j:j + 1], gates[:, j:j + 1], comb)

    _wait_runs(n_experts, meta_ref, copies(cur))
    x2 = x1_ref[...] + _dot(comb.astype(BF16), buf[cur].astype(BF16))
    gate = _sigmoid(_dot(_rms(x2, gple_ref[...]).astype(BF16), wgate_ref[...]))
    x3 = x2 + _dot(p_ref[...].astype(BF16), wproj_ref[...]) * gate
    out_ref[...] = _rms(x3, gfin_ref[...])


def _final(x1, gates, lpos, meta, pstarts, p2d, ys, g_ple, w_proj, w_gate, g_fin, n_experts):
    n, d = x1.shape
    tm = MOE_TILE
    steps = n // tm
    rows = _sorted_rows(tm, n_experts, LANES)
    row = lambda i, ps: (i, 0)
    const = lambda shape: pl.BlockSpec(shape, lambda i, ps: (0,) * len(shape))
    grid_spec = pltpu.PrefetchScalarGridSpec(
        num_scalar_prefetch=1,
        grid=(steps,),
        in_specs=[pl.BlockSpec((SUBLANES, LANES), row, memory_space=pltpu.SMEM),
                  pl.BlockSpec((SUBLANES, LANES), lambda i, ps: (jnp.minimum(i + 1, steps - 1), 0),
                               memory_space=pltpu.SMEM),
                  pl.BlockSpec((tm, LANES), row), pl.BlockSpec((tm, d), row), pl.BlockSpec((tm, LANES), row),
                  pl.BlockSpec((tm, p2d.shape[1]), row), pl.BlockSpec(memory_space=pl.ANY),
                  const(g_ple.shape), const(w_proj.shape), const(w_gate.shape), const(g_fin.shape)],
        out_specs=pl.BlockSpec((tm, d), row),
        scratch_shapes=[pltpu.VMEM((2, rows, ys.shape[1]), ys.dtype), pltpu.SemaphoreType.DMA((2,))],
    )
    return pl.pallas_call(
        functools.partial(_final_kernel, n_experts),
        grid_spec=grid_spec,
        out_shape=jax.ShapeDtypeStruct((n, d), F32),
        compiler_params=_params("arbitrary"),
        name="final",
    )(pstarts, meta, meta, lpos, x1, gates, p2d, ys, g_ple, w_proj, w_gate, g_fin)


def _head_ones(width):
    hd = jnp.arange(width) // RWKV_HEAD
    return (hd[:, None] == hd[None, :]).astype(BF16)


def _layer(x2d, p2d, batch, seq, norm_mix_g, w_in, s5_lam_re, s5_lam_im, s5_log_dt, s5_b_re, s5_b_im,
           s5_c_re, s5_c_im, s5_d, s5_w_glu, mu_rkv, mu_wag, w0, w1, w2, a0, a1, a2, g1, g2, k_k, k_a,
           r_k, ln_w, ln_b, w_out, norm_moe_g, router_w, router_b, wg, bg, wu, bu, wd, bd,
           norm_ple_g, ple_w_proj, ple_w_gate):
    n, d = x2d.shape
    width = w_in.shape[1] // 4
    n_experts = router_w.shape[1]
    row2 = lambda t: t.reshape(1, -1).astype(F32)
    head_ones = _head_ones(width)
    vecs = jnp.zeros((SUBLANES, width), F32)
    vecs = vecs.at[0].set(w0).at[1].set(a0).at[2].set(k_k).at[3].set(k_a).at[4].set(r_k.reshape(-1))

    w_ext, w2_cat = _mix_in_weights(w_in, w1, a1, g1, mu_wag, w2, a2, g2)
    u, r, lw, k, v, an, bb, gate, bonus, u_bf = _mix_in(
        x2d, seq, row2(norm_mix_g), w_ext, mu_rkv.astype(F32), vecs, w2_cat, head_ones)

    tables = _s5_tables(s5_lam_re, s5_lam_im, s5_log_dt, s5_b_re, s5_b_im, s5_c_re, s5_c_im,
                        seq // S5_CHUNK)
    yconv = _s5_conv(u_bf, batch, seq, tables)

    rhat, yhat, g_mat, h_mat = _rwkv_chunks(r, lw, k, v, an, bb)
    y_rwkv = _rwkv_scan(rhat, yhat, g_mat, h_mat, gate, bonus, row2(ln_w), row2(ln_b), head_ones,
                        batch, seq)

    rw = jnp.zeros((d, LANES), BF16).at[:, :n_experts].set(router_w.astype(BF16))
    rb = jnp.full((1, LANES), NEG_BIG, F32).at[0, :n_experts].set(router_b.astype(F32))
    x1, hm, gates, lpos, meta, totals = _post_mix(
        x2d, yconv, u, y_rwkv, row2(s5_d), s5_w_glu.astype(BF16), w_out.astype(BF16),
        row2(norm_moe_g), rw, rb)

    tmb = EXPERT_TILE
    n_tiles = n // MOE_TILE
    max_rows = n * TOP_K + n_tiles * n_experts * (RUN_ALIGN - 1)
    n_blocks = -(-max_rows // tmb) + n_experts
    seg = totals[0, :n_experts].astype(jnp.int32)
    padded = ((seg + tmb - 1) // tmb) * tmb
    pends = jnp.cumsum(padded)
    pstarts = (pends - padded).astype(jnp.int32)
    block_start = jnp.arange(n_blocks, dtype=jnp.int32) * tmb
    block_e = jnp.minimum(jnp.sum((pends[None, :] <= block_start[:, None]).astype(jnp.int32), axis=1),
                          n_experts - 1).astype(jnp.int32)
    n_used = (pends[-1] // tmb).astype(jnp.int32).reshape(1)

    xs = _dispatch(hm, lpos, meta, pstarts, seg, n_blocks * tmb, n_experts)
    ys = _experts(xs, block_e, n_used, wg, bg.reshape(n_experts, 1, -1).astype(F32),
                  wu, bu.reshape(n_experts, 1, -1).astype(F32), wd,
                  bd.reshape(n_experts, 1, -1).astype(F32))
    return (x1, gates, lpos, meta, pstarts, ys, p2d, row2(norm_ple_g), ple_w_proj.astype(BF16),
            ple_w_gate.astype(BF16), n_experts)


def kernel(x, p, norm_mix_g, w_in, s5_lam_re, s5_lam_im, s5_log_dt, s5_b_re, s5_b_im, s5_c_re, s5_c_im, s5_d, s5_w_glu, rwkv_mu_rkv, rwkv_mu_wag, rwkv_w0, rwkv_w1, rwkv_w2, rwkv_a0, rwkv_a1, rwkv_a2, rwkv_g1, rwkv_g2, rwkv_k_k, rwkv_k_a, rwkv_r_k, rwkv_ln_w, rwkv_ln_b, w_out, norm_moe_g, router_w, router_b, exp_w_gate, exp_b_gate, exp_w_up, exp_b_up, exp_w_down, exp_b_down, norm_ple_g, ple_w_proj, ple_w_gate, final_norm_g):
    batch, seq, d = x.shape
    assert w_in.shape[0] == 1, "the final kernel fuses the last RMSNorm: single-layer stacks only"
    i = 0
    x1, gates, lpos, meta, pstarts, ys, p2d, g_ple, w_proj, w_gate, n_experts = _layer(
        x.reshape(batch * seq, d), p[i].reshape(batch * seq, -1), batch, seq, norm_mix_g[i], w_in[i],
        s5_lam_re[i], s5_lam_im[i], s5_log_dt[i], s5_b_re[i], s5_b_im[i], s5_c_re[i], s5_c_im[i],
        s5_d[i], s5_w_glu[i], rwkv_mu_rkv[i], rwkv_mu_wag[i], rwkv_w0[i], rwkv_w1[i], rwkv_w2[i],
        rwkv_a0[i], rwkv_a1[i], rwkv_a2[i], rwkv_g1[i], rwkv_g2[i], rwkv_k_k[i], rwkv_k_a[i],
        rwkv_r_k[i], rwkv_ln_w[i], rwkv_ln_b[i], w_out[i], norm_moe_g[i], router_w[i],
        router_b[i], exp_w_gate[i], exp_b_gate[i], exp_w_up[i], exp_b_up[i], exp_w_down[i],
        exp_b_down[i], norm_ple_g[i], ple_w_proj[i], ple_w_gate[i])
    out = _final(x1, gates, lpos, meta, pstarts, p2d, ys, g_ple, w_proj, w_gate,
                 final_norm_g.reshape(1, -1).astype(F32), n_experts)
    return out.reshape(batch, seq, d)
```

```python
import functools

import jax
import jax.numpy as jnp
from jax import lax
from jax.experimental import pallas as pl
from jax.experimental.pallas import tpu as pltpu

F32 = jnp.float32
BF16 = jnp.bfloat16

S5_GROUP = 16
S5_STATE = 64
RWKV_HEAD = 64
DECAY_LORA = 64
ICLR_LORA = 64
DECAY_SCALE = 0.6065306597126334
TOP_K = 4
RMS_EPS = 1e-6
RWKV_GN_EPS = 64e-5
LAMBDA_RE_MAX = -1e-4
SWIGLU_LIMIT = 7.0
SWIGLU_ALPHA = 1.702

LANES = 128
SUBLANES = 8
VMEM_LIMIT = 56 * 1024 * 1024

S5_CHUNK = 64
RWKV_CHUNK = 64
RWKV_SUB = 16
TOKEN_TILE = 512
EXPERT_TILE = 512
MOE_TILE = 512
RUN_ALIGN = 8
SORT_BLOCK = 256
NEG_BIG = -1e30


def _dot(a, b):
    return jnp.dot(a, b, preferred_element_type=F32)


def _dot_nt(a, b):
    return lax.dot_general(a, b, (((1,), (1,)), ((), ())), preferred_element_type=F32)


def _dot_tn(a, b):
    return lax.dot_general(a, b, (((0,), (0,)), ((), ())), preferred_element_type=F32)


def _rms(t, gain):
    return t * lax.rsqrt(jnp.mean(t * t, axis=-1, keepdims=True) + RMS_EPS) * gain


def _sigmoid(t):
    return 1.0 / (1.0 + jnp.exp(-t))


def _params(*sem):
    return pltpu.CompilerParams(dimension_semantics=sem, vmem_limit_bytes=VMEM_LIMIT)


def _full(shape):
    return pl.BlockSpec(shape, lambda *_: (0,) * len(shape))


def _mix_in_kernel(seq, x_ref, xp_ref, gain_ref, win_ref, murkv_ref, vec_ref, w2_ref, ones_ref,
                   u_o, r_o, lw_o, k_o, v_o, an_o, bb_o, g_o, bonus_o, ub_o):
    tm = x_ref.shape[0]
    width = u_o.shape[1]
    lora = w2_ref.shape[0]
    gain = gain_ref[...]
    h = _rms(x_ref[...], gain)
    keep = jnp.where((pl.program_id(0) * tm) % seq == 0, 0.0, 1.0)
    hp = _rms(xp_ref[...], gain) * keep
    h_ext = jnp.concatenate([hp, h], axis=0).astype(BF16)
    proj = _dot(h_ext, win_ref[...])
    shifted = pltpu.roll(proj[:, width:4 * width + lora], 1, 0)[SUBLANES:]
    cur = proj[SUBLANES:]
    u_o[...] = cur[:, :width]
    ub_o[...] = cur[:, :width].astype(BF16)

    l1 = cur[:, 4 * width + lora:] + shifted[:, 3 * width:]
    lane = lax.broadcasted_iota(jnp.int32, l1.shape, 1)
    hidden = jnp.where(lane < DECAY_LORA, jnp.tanh(l1),
                       jnp.where(lane < DECAY_LORA + ICLR_LORA, l1, _sigmoid(l1))).astype(BF16)
    z3 = _dot(hidden, w2_ref[...])

    w0 = vec_ref[0:1, :]
    a0 = vec_ref[1:2, :]
    k_k = vec_ref[2:3, :]
    k_a = vec_ref[3:4, :]
    r_k = vec_ref[4:5, :]
    lw_o[...] = (-DECAY_SCALE) * _sigmoid(w0 + z3[:, :width])
    a = _sigmoid(a0 + z3[:, width:2 * width])
    g_o[...] = z3[:, 2 * width:]

    r_cur = cur[:, width:2 * width]
    k_cur = cur[:, 2 * width:3 * width]
    v_cur = cur[:, 3 * width:4 * width]
    r = r_cur + (shifted[:, :width] - r_cur) * murkv_ref[0:1, :]
    k = k_cur + (shifted[:, width:2 * width] - k_cur) * murkv_ref[1:2, :]
    v = v_cur + (shifted[:, 2 * width:3 * width] - v_cur) * murkv_ref[2:3, :]

    ones = ones_ref[...]
    kk = k * k_k
    ssq = _dot((kk * kk).astype(BF16), ones)
    kkn = kk * jnp.minimum(lax.rsqrt(ssq), 1e12)
    k2 = k * (1.0 + (a - 1.0) * k_a)
    r_o[...] = r
    k_o[...] = k2
    v_o[...] = v
    an_o[...] = -kkn
    bb_o[...] = kkn * a
    bonus_o[...] = _dot((r * k2 * r_k).astype(BF16), ones) * v


def _mix_in(x2d, seq, gain, w_ext, mu_rkv, vecs, w2_cat, head_ones):
    n, d = x2d.shape
    width = w2_cat.shape[1] // 3
    tm = TOKEN_TILE
    prev_blocks = tm // SUBLANES
    row = lambda i: (i, 0)
    out = jax.ShapeDtypeStruct((n, width), F32)
    return pl.pallas_call(
        functools.partial(_mix_in_kernel, seq),
        grid=(n // tm,),
        in_specs=[
            pl.BlockSpec((tm, d), row),
            pl.BlockSpec((SUBLANES, d), lambda i: (jnp.maximum(i * prev_blocks - 1, 0), 0)),
            _full(gain.shape), _full(w_ext.shape), _full(mu_rkv.shape), _full(vecs.shape),
            _full(w2_cat.shape), _full(head_ones.shape),
        ],
        out_specs=[pl.BlockSpec((tm, width), row)] * 10,
        out_shape=[out] * 9 + [jax.ShapeDtypeStruct((n, width), BF16)],
        compiler_params=_params("parallel"),
        name="mix_in",
    )(x2d, x2d, gain, w_ext, mu_rkv, vecs, w2_cat, head_ones)


def _mix_in_weights(w_in, w1, a1, g1, mu_wag, w2, a2, g2):
    l1 = jnp.concatenate([w1, a1, g1], axis=1).astype(F32)
    mu = jnp.concatenate([jnp.broadcast_to(mu_wag[j][:, None], (w.shape[0], w.shape[1]))
                          for j, w in enumerate((w1, a1, g1))], axis=1).astype(F32)
    w_ext = jnp.concatenate([w_in.astype(F32), mu * l1, (1.0 - mu) * l1], axis=1).astype(BF16)
    width = w2.shape[1]
    z = lambda rows: jnp.zeros((rows, width), F32)
    w2_cat = jnp.concatenate([
        jnp.concatenate([w2.astype(F32), z(w2.shape[0]), z(w2.shape[0])], axis=1),
        jnp.concatenate([z(a2.shape[0]), a2.astype(F32), z(a2.shape[0])], axis=1),
        jnp.concatenate([z(g2.shape[0]), z(g2.shape[0]), g2.astype(F32)], axis=1)], axis=0).astype(BF16)
    return w_ext, w2_cat


def _s5_tables(lam_re, lam_im, log_dt, b_re, b_im, c_re, c_im, n_chunks):
    t = S5_CHUNK
    hi = lax.Precision.HIGHEST
    lre = jnp.minimum(lam_re.astype(F32), LAMBDA_RE_MAX)
    lim = lam_im.astype(F32)
    dt = jnp.exp(log_dt.astype(F32))[:, None]
    mag = jnp.exp(lre * dt)
    lb_re = mag * jnp.cos(lim * dt)
    lb_im = mag * jnp.sin(lim * dt)
    den = lre * lre + lim * lim
    z_re = lb_re - 1.0
    coef_re = ((z_re * lre + lb_im * lim) / den)[..., None]
    coef_im = ((lb_im * lre - z_re * lim) / den)[..., None]
    b_re = b_re.astype(F32)
    b_im = b_im.astype(F32)
    bb_re = coef_re * b_re - coef_im * b_im
    bb_im = coef_re * b_im + coef_im * b_re
    c_re = c_re.astype(F32)
    c_im = c_im.astype(F32)

    def power(e):
        e = e.astype(F32)[:, None, None]
        m = jnp.exp(e * (lre * dt))
        ang = e * (lim * dt)
        return m * jnp.cos(ang), m * jnp.sin(ang)

    g = lre.shape[0]
    c = S5_GROUP
    p_re, p_im = power(jnp.arange(t + 1))
    cb_re = (c_re[:, None, :, :] * bb_re.transpose(0, 2, 1)[:, :, None, :]
             - c_im[:, None, :, :] * bb_im.transpose(0, 2, 1)[:, :, None, :])
    cb_im = (c_re[:, None, :, :] * bb_im.transpose(0, 2, 1)[:, :, None, :]
             + c_im[:, None, :, :] * bb_re.transpose(0, 2, 1)[:, :, None, :])
    cb = jnp.concatenate([cb_re, -cb_im], axis=-1).reshape(g, c * c, 2 * S5_STATE)
    pw = jnp.concatenate([p_re[:t], p_im[:t]], axis=-1).transpose(1, 2, 0)
    kvec = jnp.einsum('gkp,gpt->gkt', cb, pw, precision=hi).reshape(g, c, c // 2, 2 * t)
    r_re, r_im = power(t - 1 - jnp.arange(t))
    st_re = r_re[:, :, :, None] * bb_re - r_im[:, :, :, None] * bb_im
    st_im = r_re[:, :, :, None] * bb_im + r_im[:, :, :, None] * bb_re
    state_map = jnp.concatenate([st_re.transpose(1, 3, 0, 2).reshape(g, c * t, S5_STATE),
                                 st_im.transpose(1, 3, 0, 2).reshape(g, c * t, S5_STATE)],
                                axis=2).astype(BF16)
    q_re, q_im = p_re[1:], p_im[1:]
    ca_re = c_re[:, :, None, :] * q_re.transpose(1, 0, 2)[:, None] - c_im[:, :, None, :] * q_im.transpose(1, 0, 2)[:, None]
    ca_im = c_re[:, :, None, :] * q_im.transpose(1, 0, 2)[:, None] + c_im[:, :, None, :] * q_re.transpose(1, 0, 2)[:, None]
    cross = jnp.concatenate([ca_re.transpose(0, 3, 1, 2).reshape(g, S5_STATE, c * t),
                             -ca_im.transpose(0, 3, 1, 2).reshape(g, S5_STATE, c * t)], axis=1).astype(BF16)
    n_steps = max(1, (n_chunks - 1).bit_length())
    s_re, s_im = power(t * (2 ** jnp.arange(n_steps)))
    tab = jnp.stack([jnp.concatenate([s_re, s_re], axis=-1),
                     jnp.concatenate([-s_im, s_im], axis=-1)], axis=1)
    tab = tab.transpose(2, 0, 1, 3).reshape(g, 2 * n_steps, 2 * S5_STATE)
    return kvec, state_map, cross, tab, n_steps


def _s5_conv_kernel(n_chunks, n_steps, u_ref, kvec_ref, st_ref, cross_ref, tab_ref, y_ref, conv):
    t = S5_CHUNK
    row = lax.broadcasted_iota(jnp.int32, (t, LANES), 0)
    lane = lax.broadcasted_iota(jnp.int32, (t, LANES), 1)
    causal = jnp.bitwise_and(lane, t - 1) >= row
    for cin in range(S5_GROUP):
        for cp in range(S5_GROUP // 2):
            base = jnp.broadcast_to(kvec_ref[0, cin, cp:cp + 1, :], (t, LANES))
            blk = pltpu.roll(base, 0, 1, stride=1, stride_axis=0)
            conv[cin * t:(cin + 1) * t, cp * LANES:(cp + 1) * LANES] = jnp.where(causal, blk, 0.0).astype(BF16)
    u = jnp.concatenate([u_ref[0, cin] for cin in range(S5_GROUP)], axis=1)
    y = _dot(u, conv[...])
    x = _dot(u, st_ref[0])
    chunk = lax.broadcasted_iota(jnp.int32, x.shape, 0) % n_chunks
    half = x.shape[1] // 2
    for j in range(n_steps):
        sh = 1 << j
        xs = jnp.where(chunk >= sh, pltpu.roll(x, sh, 0), 0.0)
        x = x + tab_ref[0, 2 * j:2 * j + 1, :] * xs + tab_ref[0, 2 * j + 1:2 * j + 2, :] * pltpu.roll(xs, half, 1)
    x_in = jnp.where(chunk >= 1, pltpu.roll(x, 1, 0), 0.0)
    y = (y + _dot(x_in.astype(BF16), cross_ref[0])).astype(y_ref.dtype)
    for cout in range(S5_GROUP):
        y_ref[0, cout] = y[:, cout * t:(cout + 1) * t]


def _s5_conv(u2d, batch, seq, tables):
    kvec, state_map, cross, tab, n_steps = tables
    n, width = u2d.shape
    g = width // S5_GROUP
    t = S5_CHUNK
    n_chunks = seq // t
    ct = S5_GROUP * t
    ut = u2d.T.reshape(g, S5_GROUP, batch * n_chunks, t)
    rows = batch * n_chunks
    slab = pl.BlockSpec((1, S5_GROUP, rows, t), lambda i: (i, 0, 0, 0))
    y = pl.pallas_call(
        functools.partial(_s5_conv_kernel, n_chunks, n_steps),
        grid=(g,),
        in_specs=[
            slab,
            pl.BlockSpec((1,) + kvec.shape[1:], lambda i: (i, 0, 0, 0)),
            pl.BlockSpec((1,) + state_map.shape[1:], lambda i: (i, 0, 0)),
            pl.BlockSpec((1,) + cross.shape[1:], lambda i: (i, 0, 0)),
            pl.BlockSpec((1,) + tab.shape[1:], lambda i: (i, 0, 0)),
        ],
        out_specs=slab,
        out_shape=jax.ShapeDtypeStruct((g, S5_GROUP, rows, t), BF16),
        scratch_shapes=[pltpu.VMEM((ct, ct), BF16)],
        compiler_params=_params("parallel"),
        name="s5_conv",
    )(ut, kvec, state_map, cross, tab)
    return y.reshape(width, n).T


def _pair_blockdiag(y, left):
    return jnp.concatenate([jnp.where(left, y, 0.0), jnp.where(left, 0.0, y)], axis=0).astype(BF16)


def _rwkv_chunk_kernel(r_ref, lw_ref, k_ref, v_ref, an_ref, bb_ref, rhat_o, yhat_o, g_o, h_o):
    t = RWKV_CHUNK
    rows, width = r_ref.shape
    n_chunks = rows // t
    pairs = width // LANES
    row = lax.broadcasted_iota(jnp.int32, (t, LANES), 0)
    lane = lax.broadcasted_iota(jnp.int32, (t, LANES), 1)
    col = jnp.bitwise_and(lane, RWKV_HEAD - 1)
    left = lane < RWKV_HEAD
    incl = row >= col
    strict = row > col
    same_blk = (row // RWKV_SUB) == (col // RWKV_SUB)
    eye = jnp.where(row == col, 1.0, 0.0).astype(F32)
    brow = lax.broadcasted_iota(jnp.int32, (LANES, LANES), 0)
    bcol = lax.broadcasted_iota(jnp.int32, (LANES, LANES), 1)
    same_head = (brow // RWKV_HEAD) == (bcol // RWKV_HEAD)
    eye_full = brow == bcol
    crow = lax.broadcasted_iota(jnp.int32, (rows, rows), 0)
    ccol = lax.broadcasted_iota(jnp.int32, (rows, rows), 1)
    tril = jnp.where(crow >= ccol, jnp.where((crow // t) == (ccol // t), 1.0, 0.0), 0.0).astype(BF16)

    lw = lw_ref[...]
    p1 = lw.astype(BF16)
    p2 = (lw - p1.astype(F32)).astype(BF16)
    cs = _dot(tril, p1) + _dot(tril, p2)
    a_t = an_ref[...] * jnp.exp(cs - lw)
    r_t = r_ref[...] * jnp.exp(cs)
    p_inv = jnp.exp(-cs)
    b_t = bb_ref[...] * p_inv
    k_t = k_ref[...] * p_inv

    def bd(y):
        return _pair_blockdiag(y, left)

    def pmm(x, y_bd):
        return _dot(x.astype(BF16), y_bd)

    units = [(c, j) for c in range(n_chunks) for j in range(pairs)]

    def tile(arr, c, j):
        return arr[c * t:(c + 1) * t, j * LANES:(j + 1) * LANES]

    a2 = [tile(a_t, c, j) for c, j in units]
    r2 = [tile(r_t, c, j) for c, j in units]
    v2 = [tile(v_ref[...], c, j) for c, j in units]
    ar = [jnp.concatenate([a, r], axis=0).astype(BF16) for a, r in zip(a2, r2)]
    prod = [_dot_nt(x, jnp.concatenate([bd(tile(b_t, c, j)), bd(tile(k_t, c, j))], axis=0))
            for x, (c, j) in zip(ar, units)]
    l_ab = [jnp.where(strict, p[:t, :LANES], 0.0) for p in prod]
    l_rb = [jnp.where(incl, p[t:, :LANES], 0.0) for p in prod]
    l_ak = [jnp.where(strict, p[:t, LANES:], 0.0) for p in prod]
    l_rk = [jnp.where(incl, p[t:, LANES:], 0.0) for p in prod]
    l_d = [jnp.where(same_blk, m, 0.0) for m in l_ab]
    l_o = [m - d for m, d in zip(l_ab, l_d)]
    l2 = [pmm(m, bd(m)) for m in l_d]
    q0 = [eye + m for m in l_d]
    s1 = [pmm(m, jnp.concatenate([bd(m), bd(q)], axis=1)) for m, q in zip(l2, q0)]
    l4 = [z[:, :LANES] for z in s1]
    q1 = [q + z[:, LANES:] for q, z in zip(q0, s1)]
    s2 = [pmm(m, jnp.concatenate([bd(m), bd(q)], axis=1)) for m, q in zip(l4, q1)]
    l8 = [z[:, :LANES] for z in s2]
    q2 = [q + z[:, LANES:] for q, z in zip(q1, s2)]
    d_inv = [q + pmm(m, bd(q)) for m, q in zip(l8, q2)]
    akv = [pmm(jnp.concatenate([m, n], axis=0), bd(v)) for m, n, v in zip(l_ak, l_rk, v2)]
    x1 = [z[:t] for z in akv]

    def bd2(z1, z2):
        return jnp.concatenate([bd(z1), bd(z2)], axis=1)

    dz = [pmm(d, bd2(a, x)) for d, a, x in zip(d_inv, a2, x1)]
    n1 = [pmm(d, bd(o)) for d, o in zip(d_inv, l_o)]
    n2 = [pmm(m, bd(m)) for m in n1]
    t1 = [z + pmm(m, bd2(z[:, :LANES], z[:, LANES:])) for z, m in zip(dz, n2)]
    wu = [z + pmm(m, bd2(z[:, :LANES], z[:, LANES:])) for z, m in zip(t1, n1)]
    ry = [pmm(m, bd2(z[:, :LANES], z[:, LANES:])) for m, z in zip(l_rb, wu)]
    rk_v = [z[t:] for z in akv]

    for i, (c, j) in enumerate(units):
        rs = slice(c * t, (c + 1) * t)
        ls = slice(j * LANES, (j + 1) * LANES)
        rhat_o[rs, ls] = r2[i] + ry[i][:, :LANES]
        yhat_o[rs, ls] = ry[i][:, LANES:] + rk_v[i]
        cs_c = cs[rs, ls]
        cs_end = cs_c[t - 1:t, :]
        p_end = jnp.exp(cs_end - cs_c)
        b_h = (bb_ref[rs, ls] * p_end).astype(BF16)
        k_h = (k_ref[rs, ls] * p_end).astype(BF16)
        g_full = _dot_tn(wu[i][:, :LANES].astype(BF16), b_h)
        g_bd = jnp.where(same_head, g_full, 0.0) + jnp.where(eye_full, jnp.exp(cs_end), 0.0)
        g_o[c, j] = g_bd.astype(BF16)
        uv = jnp.concatenate([wu[i][:, LANES:], v2[i]], axis=0).astype(BF16)
        h_full = _dot_tn(uv, jnp.concatenate([b_h, k_h], axis=0))
        h_o[c, j] = jnp.where(left, h_full[:RWKV_HEAD], h_full[RWKV_HEAD:])


RWKV_CHUNKS_PER_STEP = 4


def _rwkv_chunks(r, lw, k, v, an, bb):
    n, width = r.shape
    t = RWKV_CHUNK
    cps = RWKV_CHUNKS_PER_STEP
    pairs = width // LANES
    nck = n // t
    tok = pl.BlockSpec((cps * t, width), lambda i: (i, 0))
    return pl.pallas_call(
        _rwkv_chunk_kernel,
        grid=(nck // cps,),
        in_specs=[tok] * 6,
        out_specs=[tok, tok,
                   pl.BlockSpec((cps, pairs, LANES, LANES), lambda i: (i, 0, 0, 0)),
                   pl.BlockSpec((cps, pairs, RWKV_HEAD, LANES), lambda i: (i, 0, 0, 0))],
        out_shape=[jax.ShapeDtypeStruct((n, width), F32)] * 2
        + [jax.ShapeDtypeStruct((nck, pairs, LANES, LANES), BF16),
           jax.ShapeDtypeStruct((nck, pairs, RWKV_HEAD, LANES), F32)],
        compiler_params=_params("parallel"),
        name="rwkv_chunk",
    )(r, lw, k, v, an, bb)


def _rwkv_scan_kernel(rhat_ref, yhat_ref, g_ref, h_ref, gate_ref, bonus_ref, lnw_ref, lnb_ref,
                      ones_ref, y_ref, state):
    t = RWKV_CHUNK
    batch, pairs = state.shape[0], state.shape[1]
    n_chunks = rhat_ref.shape[1] // t
    lane = lax.broadcasted_iota(jnp.int32, (RWKV_HEAD, LANES), 1)
    left = lane < RWKV_HEAD

    @pl.when(pl.program_id(0) == 0)
    def _():
        state[...] = jnp.zeros_like(state)

    s = [[state[b, j] for j in range(pairs)] for b in range(batch)]
    y_rows = []
    for b in range(batch):
        chunk_rows = []
        for c in range(n_chunks):
            rs = slice(c * t, (c + 1) * t)
            tiles = []
            for j in range(pairs):
                ls = slice(j * LANES, (j + 1) * LANES)
                s_bd = _pair_blockdiag(s[b][j], left)
                tiles.append(yhat_ref[b, rs, ls] + _dot_nt(rhat_ref[b, rs, ls].astype(BF16), s_bd))
                s[b][j] = _dot(s[b][j].astype(BF16), g_ref[b, c, j]) + h_ref[b, c, j]
            chunk_rows.append(jnp.concatenate(tiles, axis=1))
        y_rows.append(jnp.concatenate(chunk_rows, axis=0))
    for b in range(batch):
        for j in range(pairs):
            state[b, j] = s[b][j]

    ones = ones_ref[...]
    inv = 1.0 / RWKV_HEAD
    for b in range(batch):
        y = y_rows[b]
        cen = y - _dot(y.astype(BF16), ones) * inv
        var = _dot((cen * cen).astype(BF16), ones) * inv
        yn = cen * lax.rsqrt(var + RWKV_GN_EPS) * lnw_ref[...] + lnb_ref[...]
        y_ref[b] = (yn + bonus_ref[b]) * gate_ref[b]


def _rwkv_scan(rhat, yhat, g_mat, h_mat, gate, bonus, ln_w, ln_b, head_ones, batch, seq):
    n, width = rhat.shape
    t = RWKV_CHUNK
    cps = RWKV_CHUNKS_PER_STEP
    pairs = width // LANES
    nc = seq // t
    tok3 = lambda a: a.reshape(batch, seq, width)
    tok = pl.BlockSpec((batch, cps * t, width), lambda i: (0, i, 0))
    out = pl.pallas_call(
        _rwkv_scan_kernel,
        grid=(nc // cps,),
        in_specs=[tok, tok,
                  pl.BlockSpec((batch, cps, pairs, LANES, LANES), lambda i: (0, i, 0, 0, 0)),
                  pl.BlockSpec((batch, cps, pairs, RWKV_HEAD, LANES), lambda i: (0, i, 0, 0, 0)),
                  tok, tok, _full(ln_w.shape), _full(ln_b.shape), _full(head_ones.shape)],
        out_specs=tok,
        out_shape=jax.ShapeDtypeStruct((batch, seq, width), F32),
        scratch_shapes=[pltpu.VMEM((batch, pairs, RWKV_HEAD, LANES), F32)],
        compiler_params=_params("arbitrary"),
        name="rwkv_scan",
    )(tok3(rhat), tok3(yhat), g_mat.reshape(batch, nc, pairs, LANES, LANES),
      h_mat.reshape(batch, nc, pairs, RWKV_HEAD, LANES), tok3(gate), tok3(bonus), ln_w, ln_b, head_ones)
    return out.reshape(n, width)


def _post_mix_kernel(x_ref, yc_ref, u_ref, yr_ref, d_ref, wglu_ref, wout_ref, gain_ref,
                     rw_ref, rb_ref, x1_o, hm_o, gate_o, lpos_o, meta_o, tot_o, running):
    tm = x_ref.shape[0]
    width = yc_ref.shape[1]

    @pl.when(pl.program_id(0) == 0)
    def _():
        running[...] = jnp.zeros_like(running)

    y = yc_ref[...].astype(F32) + d_ref[...] * u_ref[...]
    y = 0.5 * y * (1.0 + jnp.tanh(0.7978845608028654 * (y + 0.044715 * (y * y * y))))
    y = y * _sigmoid(_dot(y.astype(BF16), wglu_ref[...]))
    x1 = (x_ref[...] + _dot(y.astype(BF16), wout_ref[:width, :])
          + _dot(yr_ref[...].astype(BF16), wout_ref[width:, :]))
    x1_o[...] = x1
    hm = _rms(x1, gain_ref[...])
    hm_o[...] = hm.astype(BF16)
    logits = _dot(hm.astype(BF16), rw_ref[...]) + rb_ref[...]

    lane = lax.broadcasted_iota(jnp.int32, logits.shape, 1)
    lanef = lane.astype(F32)
    sel = jnp.zeros(logits.shape, F32)
    idx_cols, val_cols = [], []
    work = logits
    for _ in range(TOP_K):
        m = jnp.max(work, axis=-1, keepdims=True)
        pick = jnp.min(jnp.where(work == m, lanef, float(LANES)), axis=-1, keepdims=True)
        hit = lanef == pick
        sel = jnp.where(hit, 1.0, sel)
        work = jnp.where(hit, -jnp.inf, work)
        idx_cols.append(pick)
        val_cols.append(m)
    exps = [jnp.exp(vv - val_cols[0]) for vv in val_cols]
    denom = exps[0] + exps[1] + exps[2] + exps[3]

    row = lax.broadcasted_iota(jnp.int32, (tm, tm), 0)
    col = lax.broadcasted_iota(jnp.int32, (tm, tm), 1)
    before = jnp.where(row > col, 1.0, 0.0).astype(BF16)
    local = _dot(before, sel.astype(BF16))
    cnt = jnp.sum(sel, axis=0, keepdims=True)
    cnt_al = jnp.floor((cnt + (RUN_ALIGN - 1)) * (1.0 / RUN_ALIGN)) * RUN_ALIGN
    erow = lax.broadcasted_iota(jnp.int32, (LANES, LANES), 0)
    ecol = lax.broadcasted_iota(jnp.int32, (LANES, LANES), 1)
    upper = jnp.where(erow < ecol, 1.0, 0.0).astype(BF16)
    toff = _dot(jnp.broadcast_to(cnt_al, (SUBLANES, LANES)).astype(BF16), upper)[0:1]
    tbase = running[...]
    gate_out = jnp.zeros(logits.shape, F32)
    lpos_out = jnp.zeros(logits.shape, F32)
    for j in range(TOP_K):
        lp = jnp.sum(jnp.where(lanef == idx_cols[j], local + toff, 0.0), axis=-1, keepdims=True)
        gate_out = jnp.where(lane == j, exps[j] / denom, gate_out)
        lpos_out = jnp.where(lane == j, lp, lpos_out)
    gate_o[...] = gate_out
    lpos_o[...] = lpos_out.astype(jnp.int32)
    srow = lax.broadcasted_iota(jnp.int32, (SUBLANES, LANES), 0)
    meta = jnp.where(srow == 0, cnt, jnp.where(srow == 1, tbase, jnp.where(srow == 2, toff, 0.0)))
    meta_o[...] = meta.astype(jnp.int32)
    running[...] = tbase + cnt_al
    tot_o[...] = jnp.broadcast_to(running[...], tot_o.shape).astype(jnp.int32)


def _post_mix(x2d, yconv, u, y_rwkv, s5_d, w_glu, w_out, gain, rw, rb):
    n, d = x2d.shape
    width = yconv.shape[1]
    tm = MOE_TILE
    row = lambda i: (i, 0)
    tok_d = pl.BlockSpec((tm, d), row)
    tok_w = pl.BlockSpec((tm, width), row)
    tok_l = pl.BlockSpec((tm, LANES), row)
    return pl.pallas_call(
        _post_mix_kernel,
        grid=(n // tm,),
        in_specs=[tok_d, tok_w, tok_w, tok_w, _full(s5_d.shape), _full(w_glu.shape),
                  _full(w_out.shape), _full(gain.shape), _full(rw.shape), _full(rb.shape)],
        out_specs=[tok_d, tok_d, tok_l, tok_l,
                   pl.BlockSpec((SUBLANES, LANES), row), _full((SUBLANES, LANES))],
        out_shape=[jax.ShapeDtypeStruct((n, d), F32), jax.ShapeDtypeStruct((n, d), BF16),
                   jax.ShapeDtypeStruct((n, LANES), F32), jax.ShapeDtypeStruct((n, LANES), jnp.int32),
                   jax.ShapeDtypeStruct((n // tm * SUBLANES, LANES), jnp.int32),
                   jax.ShapeDtypeStruct((SUBLANES, LANES), jnp.int32)],
        scratch_shapes=[pltpu.VMEM((1, LANES), F32)],
        compiler_params=_params("arbitrary"),
        name="post_mix",
    )(x2d, yconv, u, y_rwkv, s5_d, w_glu, w_out, gain, rw, rb)


def _sorted_rows(tm, n_experts, multiple):
    return -(-(tm * TOP_K + n_experts * (RUN_ALIGN - 1)) // multiple) * multiple


RUN_BIG = 8


def _groups(count):
    return lax.shift_right_logical(count + (RUN_ALIGN - 1), RUN_ALIGN.bit_length() - 1)


def _start_runs(n_experts, pstart_ref, meta_ref, make_copy):
    shift = RUN_BIG.bit_length() - 1
    for e in range(n_experts):
        groups = _groups(meta_ref[0, e])
        seg = pstart_ref[e] + meta_ref[1, e]
        loc = meta_ref[2, e]
        n_big = lax.shift_right_logical(groups, shift)
        rem = groups & (RUN_BIG - 1)

        def body(g, c, seg=seg, loc=loc):
            off = g * (RUN_BIG * RUN_ALIGN)
            make_copy(pl.multiple_of(loc + off, RUN_ALIGN), pl.multiple_of(seg + off, RUN_ALIGN),
                      RUN_BIG * RUN_ALIGN).start(priority=e % 2)
            return c

        lax.fori_loop(0, n_big, body, 0)
        size = RUN_BIG // 2
        while size >= 1:
            off = (n_big * RUN_BIG + (rem & (RUN_BIG - 2 * size))) * RUN_ALIGN

            @pl.when((rem & size) != 0)
            def _(off=off, size=size, seg=seg, loc=loc):
                make_copy(pl.multiple_of(loc + off, RUN_ALIGN), pl.multiple_of(seg + off, RUN_ALIGN),
                          size * RUN_ALIGN).start(priority=e % 2)

            size //= 2


def _wait_runs(n_experts, meta_ref, make_copy):
    groups = 0
    for e in range(n_experts):
        groups = groups + _groups(meta_ref[0, e])

    def wait_big(g, c):
        make_copy(0, 0, RUN_BIG * RUN_ALIGN).wait()
        return c

    def wait_small(g, c):
        make_copy(0, 0, RUN_ALIGN).wait()
        return c

    lax.fori_loop(0, lax.shift_right_logical(groups, RUN_BIG.bit_length() - 1), wait_big, 0)
    lax.fori_loop(0, groups & (RUN_BIG - 1), wait_small, 0)


def _dispatch_kernel(n_experts, pstart_ref, tot_ref, meta_ref, meta_prev_ref, lpos_ref, hm_ref, xs_ref,
                     srt, zero, sem):
    tm = hm_ref.shape[0]
    rows = srt.shape[1]
    i = pl.program_id(0)
    cur = i % 2
    pos_t = jnp.transpose(lpos_ref[...].astype(F32))
    hm = hm_ref[...]
    for r0 in range(0, rows, SORT_BLOCK):
        nr = min(SORT_BLOCK, rows - r0)
        rid = (lax.broadcasted_iota(jnp.int32, (nr, tm), 0) + r0).astype(F32)
        perm = jnp.zeros((nr, tm), F32)
        for j in range(TOP_K):
            perm = jnp.where(rid == pos_t[j:j + 1, :], 1.0, perm)
        srt[cur, r0:r0 + nr, :] = _dot(perm.astype(BF16), hm)

    def copies(slot):
        def make_copy(loc, dst, size):
            return pltpu.make_async_copy(srt.at[slot, pl.ds(loc, size)], xs_ref.at[pl.ds(dst, size)],
                                         sem.at[slot])
        return make_copy

    _start_runs(n_experts, pstart_ref, meta_ref, copies(cur))

    @pl.when(i > 0)
    def _():
        _wait_runs(n_experts, meta_prev_ref, copies(1 - cur))

    @pl.when(i == pl.num_programs(0) - 1)
    def _():
        _wait_runs(n_experts, meta_ref, copies(cur))

    @pl.when(pl.program_id(0) == pl.num_programs(0) - 1)
    def _():
        zero[...] = jnp.zeros_like(zero)
        for wait in (False, True):
            for e in range(n_experts):
                used = tot_ref[e]
                start = pstart_ref[e] + used
                groups = lax.shift_right_logical((-used) & (EXPERT_TILE - 1), RUN_ALIGN.bit_length() - 1)

                def body(g, c, start=start):
                    cp = pltpu.make_async_copy(
                        zero, xs_ref.at[pl.ds(pl.multiple_of(start + g * RUN_ALIGN, RUN_ALIGN), RUN_ALIGN)],
                        sem.at[0])
                    if wait:
                        cp.wait()
                    else:
                        cp.start()
                    return c

                lax.fori_loop(0, groups, body, 0)

        last = n_experts - 1
        end = pstart_ref[last] + tot_ref[last] + ((-tot_ref[last]) & (EXPERT_TILE - 1))
        srt[0, 0:EXPERT_TILE, :] = jnp.zeros((EXPERT_TILE, srt.shape[2]), srt.dtype)
        for wait in (False, True):
            def tail(b, c):
                cp = pltpu.make_async_copy(
                    srt.at[0, pl.ds(0, EXPERT_TILE)],
                    xs_ref.at[pl.ds(pl.multiple_of(end + b * EXPERT_TILE, EXPERT_TILE), EXPERT_TILE)], sem.at[0])
                if wait:
                    cp.wait()
                else:
                    cp.start()
                return c

            lax.fori_loop(0, lax.shift_right_logical(xs_ref.shape[0] - end, EXPERT_TILE.bit_length() - 1), tail, 0)


def _dispatch(hm, lpos, meta, pstarts, totals, m_pad, n_experts):
    n, dh = hm.shape
    tm = MOE_TILE
    rows = _sorted_rows(tm, n_experts, RUN_ALIGN)
    grid_spec = pltpu.PrefetchScalarGridSpec(
        num_scalar_prefetch=2,
        grid=(n // tm,),
        in_specs=[pl.BlockSpec((SUBLANES, LANES), lambda i, ps, tt: (i, 0), memory_space=pltpu.SMEM),
                  pl.BlockSpec((SUBLANES, LANES), lambda i, ps, tt: (jnp.maximum(i - 1, 0), 0),
                               memory_space=pltpu.SMEM),
                  pl.BlockSpec((tm, LANES), lambda i, ps, tt: (i, 0)),
                  pl.BlockSpec((tm, dh), lambda i, ps, tt: (i, 0))],
        out_specs=pl.BlockSpec(memory_space=pl.ANY),
        scratch_shapes=[pltpu.VMEM((2, rows, dh), F32), pltpu.VMEM((RUN_ALIGN, dh), F32),
                        pltpu.SemaphoreType.DMA((2,))],
    )
    return pl.pallas_call(
        functools.partial(_dispatch_kernel, n_experts),
        grid_spec=grid_spec,
        out_shape=jax.ShapeDtypeStruct((m_pad, dh), F32),
        compiler_params=_params("arbitrary"),
        name="dispatch",
    )(pstarts, totals, meta, meta, lpos, hm)


def _experts_kernel(be_ref, nu_ref, xs_ref, wg_ref, bg_ref, wu_ref, bu_ref, wd_ref, bd_ref, ys_ref,
                    wg_s, wu_s, wd_s):
    i = pl.program_id(0)
    changed = jnp.logical_or(i == 0, be_ref[i] != be_ref[jnp.maximum(i - 1, 0)])

    @pl.when(changed)
    def _():
        wg_s[...] = wg_ref[0].astype(BF16)
        wu_s[...] = wu_ref[0].astype(BF16)
        wd_s[...] = wd_ref[0].astype(BF16)

    @pl.when(i < nu_ref[0])
    def _():
        xb = xs_ref[...].astype(BF16)
        gt = jnp.minimum(_dot(xb, wg_s[...]) + bg_ref[0], SWIGLU_LIMIT)
        up = jnp.clip(_dot(xb, wu_s[...]) + bu_ref[0], -SWIGLU_LIMIT, SWIGLU_LIMIT)
        act = (up + 1.0) * gt * _sigmoid(SWIGLU_ALPHA * gt)
        ys_ref[...] = _dot(act.astype(BF16), wd_s[...]) + bd_ref[0]

    @pl.when(i >= nu_ref[0])
    def _():
        ys_ref[...] = jnp.zeros_like(ys_ref)


def _experts(xs, block_e, n_used, wg, bg, wu, bu, wd, bd):
    m_pad, dh = xs.shape
    tmb = EXPERT_TILE
    d, dff = wg.shape[1], wg.shape[2]
    wmap = lambda i, be, nu: (be[i], 0, 0)
    grid_spec = pltpu.PrefetchScalarGridSpec(
        num_scalar_prefetch=2,
        grid=(m_pad // tmb,),
        in_specs=[pl.BlockSpec((tmb, dh), lambda i, be, nu: (jnp.where(i < nu[0], i, 0), 0)),
                  pl.BlockSpec((1, d, dff), wmap), pl.BlockSpec((1, 1, dff), wmap),
                  pl.BlockSpec((1, d, dff), wmap), pl.BlockSpec((1, 1, dff), wmap),
                  pl.BlockSpec((1, dff, d), wmap), pl.BlockSpec((1, 1, d), wmap)],
        out_specs=pl.BlockSpec((tmb, dh), lambda i, be, nu: (i, 0)),
        scratch_shapes=[pltpu.VMEM((d, dff), BF16), pltpu.VMEM((d, dff), BF16), pltpu.VMEM((dff, d), BF16)],
    )
    return pl.pallas_call(
        _experts_kernel,
        grid_spec=grid_spec,
        out_shape=jax.ShapeDtypeStruct((m_pad, dh), F32),
        compiler_params=_params("arbitrary"),
        name="experts",
    )(block_e, n_used, xs, wg, bg, wu, bu, wd, bd)


def _final_kernel(n_experts, pstart_ref, meta_ref, meta_next_ref, lpos_ref, x1_ref, gate_ref, p_ref, ys_ref,
                  gple_ref, wproj_ref, wgate_ref, gfin_ref, out_ref, buf, sem):
    tm = x1_ref.shape[0]
    rows = buf.shape[1]
    i = pl.program_id(0)
    cur = i % 2

    def copies(slot):
        def make_copy(loc, src, size):
            return pltpu.make_async_copy(ys_ref.at[pl.ds(src, size)], buf.at[slot, pl.ds(loc, size)],
                                         sem.at[slot])
        return make_copy

    @pl.when(i == 0)
    def _():
        buf[...] = jnp.zeros_like(buf)
        _start_runs(n_experts, pstart_ref, meta_ref, copies(0))

    @pl.when(i + 1 < pl.num_programs(0))
    def _():
        _start_runs(n_experts, pstart_ref, meta_next_ref, copies(1 - cur))

    cid = lax.broadcasted_iota(jnp.int32, (tm, rows), 1)
    lpos = lpos_ref[...]
    gates = gate_ref[...]
    comb = jnp.zeros((tm, rows), F32)
    for j in range(TOP_K):
        comb = jnp.where(cid == lpos[:, j:j + 1], gates[:, j:j + 1], comb)

    _wait_runs(n_experts, meta_ref, copies(cur))
    x2 = x1_ref[...] + _dot(comb.astype(BF16), buf[cur].astype(BF16))
    gate = _sigmoid(_dot(_rms(x2, gple_ref[...]).astype(BF16), wgate_ref[...]))
    x3 = x2 + _dot(p_ref[...].astype(BF16), wproj_ref[...]) * gate
    out_ref[...] = _rms(x3, gfin_ref[...])


def _final(x1, gates, lpos, meta, pstarts, p2d, ys, g_ple, w_proj, w_gate, g_fin, n_experts):
    n, d = x1.shape
    tm = MOE_TILE
    steps = n // tm
    rows = _sorted_rows(tm, n_experts, LANES)
    row = lambda i, ps: (i, 0)
    const = lambda shape: pl.BlockSpec(shape, lambda i, ps: (0,) * len(shape))
    grid_spec = pltpu.PrefetchScalarGridSpec(
        num_scalar_prefetch=1,
        grid=(steps,),
        in_specs=[pl.BlockSpec((SUBLANES, LANES), row, memory_space=pltpu.SMEM),
                  pl.BlockSpec((SUBLANES, LANES), lambda i, ps: (jnp.minimum(i + 1, steps - 1), 0),
                               memory_space=pltpu.SMEM),
                  pl.BlockSpec((tm, LANES), row), pl.BlockSpec((tm, d), row), pl.BlockSpec((tm, LANES), row),
                  pl.BlockSpec((tm, p2d.shape[1]), row), pl.BlockSpec(memory_space=pl.ANY),
                  const(g_ple.shape), const(w_proj.shape), const(w_gate.shape), const(g_fin.shape)],
        out_specs=pl.BlockSpec((tm, d), row),
        scratch_shapes=[pltpu.VMEM((2, rows, ys.shape[1]), ys.dtype), pltpu.SemaphoreType.DMA((2,))],
    )
    return pl.pallas_call(
        functools.partial(_final_kernel, n_experts),
        grid_spec=grid_spec,
        out_shape=jax.ShapeDtypeStruct((n, d), F32),
        compiler_params=_params("arbitrary"),
        name="final",
    )(pstarts, meta, meta, lpos, x1, gates, p2d, ys, g_ple, w_proj, w_gate, g_fin)


def _head_ones(width):
    hd = jnp.arange(width) // RWKV_HEAD
    return (hd[:, None] == hd[None, :]).astype(BF16)


def _layer(x2d, p2d, batch, seq, norm_mix_g, w_in, s5_lam_re, s5_lam_im, s5_log_dt, s5_b_re, s5_b_im,
           s5_c_re, s5_c_im, s5_d, s5_w_glu, mu_rkv, mu_wag, w0, w1, w2, a0, a1, a2, g1, g2, k_k, k_a,
           r_k, ln_w, ln_b, w_out, norm_moe_g, router_w, router_b, wg, bg, wu, bu, wd, bd,
           norm_ple_g, ple_w_proj, ple_w_gate):
    n, d = x2d.shape
    width = w_in.shape[1] // 4
    n_experts = router_w.shape[1]
    row2 = lambda t: t.reshape(1, -1).astype(F32)
    head_ones = _head_ones(width)
    vecs = jnp.zeros((SUBLANES, width), F32)
    vecs = vecs.at[0].set(w0).at[1].set(a0).at[2].set(k_k).at[3].set(k_a).at[4].set(r_k.reshape(-1))

    w_ext, w2_cat = _mix_in_weights(w_in, w1, a1, g1, mu_wag, w2, a2, g2)
    u, r, lw, k, v, an, bb, gate, bonus, u_bf = _mix_in(
        x2d, seq, row2(norm_mix_g), w_ext, mu_rkv.astype(F32), vecs, w2_cat, head_ones)

    tables = _s5_tables(s5_lam_re, s5_lam_im, s5_log_dt, s5_b_re, s5_b_im, s5_c_re, s5_c_im,
                        seq // S5_CHUNK)
    yconv = _s5_conv(u_bf, batch, seq, tables)

    rhat, yhat, g_mat, h_mat = _rwkv_chunks(r, lw, k, v, an, bb)
    y_rwkv = _rwkv_scan(rhat, yhat, g_mat, h_mat, gate, bonus, row2(ln_w), row2(ln_b), head_ones,
                        batch, seq)

    rw = jnp.zeros((d, LANES), BF16).at[:, :n_experts].set(router_w.astype(BF16))
    rb = jnp.full((1, LANES), NEG_BIG, F32).at[0, :n_experts].set(router_b.astype(F32))
    x1, hm, gates, lpos, meta, totals = _post_mix(
        x2d, yconv, u, y_rwkv, row2(s5_d), s5_w_glu.astype(BF16), w_out.astype(BF16),
        row2(norm_moe_g), rw, rb)

    tmb = EXPERT_TILE
    n_tiles = n // MOE_TILE
    max_rows = n * TOP_K + n_tiles * n_experts * (RUN_ALIGN - 1)
    n_blocks = -(-max_rows // tmb) + n_experts
    seg = totals[0, :n_experts].astype(jnp.int32)
    padded = ((seg + tmb - 1) // tmb) * tmb
    pends = jnp.cumsum(padded)
    pstarts = (pends - padded).astype(jnp.int32)
    block_start = jnp.arange(n_blocks, dtype=jnp.int32) * tmb
    block_e = jnp.minimum(jnp.sum((pends[None, :] <= block_start[:, None]).astype(jnp.int32), axis=1),
                          n_experts - 1).astype(jnp.int32)
    n_used = (pends[-1] // tmb).astype(jnp.int32).reshape(1)

    xs = _dispatch(hm, lpos, meta, pstarts, seg, n_blocks * tmb, n_experts)
    ys = _experts(xs, block_e, n_used, wg, bg.reshape(n_experts, 1, -1).astype(F32),
                  wu, bu.reshape(n_experts, 1, -1).astype(F32), wd,
                  bd.reshape(n_experts, 1, -1).astype(F32))
    return (x1, gates, lpos, meta, pstarts, ys, p2d, row2(norm_ple_g), ple_w_proj.astype(BF16),
            ple_w_gate.astype(BF16), n_experts)


def kernel(x, p, norm_mix_g, w_in, s5_lam_re, s5_lam_im, s5_log_dt, s5_b_re, s5_b_im, s5_c_re, s5_c_im, s5_d, s5_w_glu, rwkv_mu_rkv, rwkv_mu_wag, rwkv_w0, rwkv_w1, rwkv_w2, rwkv_a0, rwkv_a1, rwkv_a2, rwkv_g1, rwkv_g2, rwkv_k_k, rwkv_k_a, rwkv_r_k, rwkv_ln_w, rwkv_ln_b, w_out, norm_moe_g, router_w, router_b, exp_w_gate, exp_b_gate, exp_w_up, exp_b_up, exp_w_down, exp_b_down, norm_ple_g, ple_w_proj, ple_w_gate, final_norm_g):
    batch, seq, d = x.shape
    assert w_in.shape[0] == 1, "the final kernel fuses the last RMSNorm: single-layer stacks only"
    i = 0
    x1, gates, lpos, meta, pstarts, ys, p2d, g_ple, w_proj, w_gate, n_experts = _layer(
        x.reshape(batch * seq, d), p[i].reshape(batch * seq, -1), batch, seq, norm_mix_g[i], w_in[i],
        s5_lam_re[i], s5_lam_im[i], s5_log_dt[i], s5_b_re[i], s5_b_im[i], s5_c_re[i], s5_c_im[i],
        s5_d[i], s5_w_glu[i], rwkv_mu_rkv[i], rwkv_mu_wag[i], rwkv_w0[i], rwkv_w1[i], rwkv_w2[i],
        rwkv_a0[i], rwkv_a1[i], rwkv_a2[i], rwkv_g1[i], rwkv_g2[i], rwkv_k_k[i], rwkv_k_a[i],
        rwkv_r_k[i], rwkv_ln_w[i], rwkv_ln_b[i], w_out[i], norm_moe_g[i], router_w[i],
        router_b[i], exp_w_gate[i], exp_b_gate[i], exp_w_up[i], exp_b_up[i], exp_w_down[i],
        exp_b_down[i], norm_ple_g[i], ple_w_proj[i], ple_w_gate[i])
    out = _final(x1, gates, lpos, meta, pstarts, p2d, ys, g_ple, w_proj, w_gate,
                 final_norm_g.reshape(1, -1).astype(F32), n_experts)
    return out.reshape(batch, seq, d)
```

```python
import functools

import jax
import jax.numpy as jnp
from jax import lax
from jax.experimental import pallas as pl
from jax.experimental.pallas import tpu as pltpu

F32 = jnp.float32
BF16 = jnp.bfloat16

S5_GROUP = 16
S5_STATE = 64
RWKV_HEAD = 64
DECAY_LORA = 64
ICLR_LORA = 64
DECAY_SCALE = 0.6065306597126334
TOP_K = 4
RMS_EPS = 1e-6
RWKV_GN_EPS = 64e-5
LAMBDA_RE_MAX = -1e-4
SWIGLU_LIMIT = 7.0
SWIGLU_ALPHA = 1.702

LANES = 128
SUBLANES = 8
MXU_WIDTH = 256
VMEM_LIMIT = 56 * 1024 * 1024

S5_CHUNK = 64
RWKV_CHUNK = 64
RWKV_SUB = 16
TOKEN_TILE = 512
EXPERT_TILE = 512
MOE_TILE = 512
RUN_ALIGN = 8
SORT_BLOCK = 256
NEG_BIG = -1e30


def _dot(a, b):
    return jnp.dot(a, b, preferred_element_type=F32)


def _dot_nt(a, b):
    return lax.dot_general(a, b, (((1,), (1,)), ((), ())), preferred_element_type=F32)


def _dot_tn(a, b):
    return lax.dot_general(a, b, (((0,), (0,)), ((), ())), preferred_element_type=F32)


def _rms(t, gain):
    return t * lax.rsqrt(jnp.mean(t * t, axis=-1, keepdims=True) + RMS_EPS) * gain


def _sigmoid(t):
    return 1.0 / (1.0 + jnp.exp(-t))


def _head_sums(x, ones_ref):
    w = ones_ref.shape[0]
    ones = ones_ref[...]
    xb = x.astype(BF16)
    return jnp.concatenate([_dot(xb[:, i:i + w], ones) for i in range(0, x.shape[1], w)], axis=1)


def _params(*sem):
    return pltpu.CompilerParams(dimension_semantics=sem, vmem_limit_bytes=VMEM_LIMIT)


def _full(shape):
    return pl.BlockSpec(shape, lambda *_: (0,) * len(shape))


def _mix_in_kernel(seq, x_ref, xp_ref, gain_ref, win_ref, murkv_ref, vec_ref, w2_ref, ones_ref,
                   u_o, r_o, lw_o, k_o, v_o, an_o, bb_o, g_o, bonus_o, ub_o):
    tm = x_ref.shape[0]
    width = u_o.shape[1]
    lora = w2_ref.shape[0]
    gain = gain_ref[...]
    h = _rms(x_ref[...], gain)
    keep = jnp.where((pl.program_id(0) * tm) % seq == 0, 0.0, 1.0)
    hp = _rms(xp_ref[...], gain) * keep
    h_ext = jnp.concatenate([hp, h], axis=0).astype(BF16)
    proj = _dot(h_ext, win_ref[...])
    shifted = pltpu.roll(proj[:, width:4 * width + lora], 1, 0)[SUBLANES:]
    cur = proj[SUBLANES:]
    u_o[...] = cur[:, :width]
    ub_o[...] = cur[:, :width].astype(BF16)

    l1 = cur[:, 4 * width + lora:] + shifted[:, 3 * width:]
    lane = lax.broadcasted_iota(jnp.int32, l1.shape, 1)
    hidden = jnp.where(lane < DECAY_LORA, jnp.tanh(l1),
                       jnp.where(lane < DECAY_LORA + ICLR_LORA, l1, _sigmoid(l1))).astype(BF16)
    z3 = _dot(hidden, w2_ref[...])

    w0 = vec_ref[0:1, :]
    a0 = vec_ref[1:2, :]
    k_k = vec_ref[2:3, :]
    k_a = vec_ref[3:4, :]
    r_k = vec_ref[4:5, :]
    lw_o[...] = (-DECAY_SCALE) * _sigmoid(w0 + z3[:, :width])
    a = _sigmoid(a0 + z3[:, width:2 * width])
    g_o[...] = z3[:, 2 * width:]

    r_cur = cur[:, width:2 * width]
    k_cur = cur[:, 2 * width:3 * width]
    v_cur = cur[:, 3 * width:4 * width]
    r = r_cur + (shifted[:, :width] - r_cur) * murkv_ref[0:1, :]
    k = k_cur + (shifted[:, width:2 * width] - k_cur) * murkv_ref[1:2, :]
    v = v_cur + (shifted[:, 2 * width:3 * width] - v_cur) * murkv_ref[2:3, :]

    kk = k * k_k
    ssq = _head_sums(kk * kk, ones_ref)
    kkn = kk * jnp.minimum(lax.rsqrt(ssq), 1e12)
    k2 = k * (1.0 + (a - 1.0) * k_a)
    r_o[...] = r
    k_o[...] = k2
    v_o[...] = v
    an_o[...] = -kkn
    bb_o[...] = kkn * a
    bonus_o[...] = _head_sums(r * k2 * r_k, ones_ref) * v


def _mix_in(x2d, seq, gain, w_ext, mu_rkv, vecs, w2_cat, head_ones):
    n, d = x2d.shape
    width = w2_cat.shape[1] // 3
    tm = TOKEN_TILE
    prev_blocks = tm // SUBLANES
    row = lambda i: (i, 0)
    out = jax.ShapeDtypeStruct((n, width), F32)
    return pl.pallas_call(
        functools.partial(_mix_in_kernel, seq),
        grid=(n // tm,),
        in_specs=[
            pl.BlockSpec((tm, d), row),
            pl.BlockSpec((SUBLANES, d), lambda i: (jnp.maximum(i * prev_blocks - 1, 0), 0)),
            _full(gain.shape), _full(w_ext.shape), _full(mu_rkv.shape), _full(vecs.shape),
            _full(w2_cat.shape), _full(head_ones.shape),
        ],
        out_specs=[pl.BlockSpec((tm, width), row)] * 10,
        out_shape=[out] * 9 + [jax.ShapeDtypeStruct((n, width), BF16)],
        compiler_params=_params("parallel"),
        name="mix_in",
    )(x2d, x2d, gain, w_ext, mu_rkv, vecs, w2_cat, head_ones)


def _mix_in_weights(w_in, w1, a1, g1, mu_wag, w2, a2, g2):
    l1 = jnp.concatenate([w1, a1, g1], axis=1).astype(F32)
    mu = jnp.concatenate([jnp.broadcast_to(mu_wag[j][:, None], (w.shape[0], w.shape[1]))
                          for j, w in enumerate((w1, a1, g1))], axis=1).astype(F32)
    w_ext = jnp.concatenate([w_in.astype(F32), mu * l1, (1.0 - mu) * l1], axis=1).astype(BF16)
    width = w2.shape[1]
    z = lambda rows: jnp.zeros((rows, width), F32)
    w2_cat = jnp.concatenate([
        jnp.concatenate([w2.astype(F32), z(w2.shape[0]), z(w2.shape[0])], axis=1),
        jnp.concatenate([z(a2.shape[0]), a2.astype(F32), z(a2.shape[0])], axis=1),
        jnp.concatenate([z(g2.shape[0]), z(g2.shape[0]), g2.astype(F32)], axis=1)], axis=0).astype(BF16)
    return w_ext, w2_cat


def _s5_tables(lam_re, lam_im, log_dt, b_re, b_im, c_re, c_im, n_chunks):
    t = S5_CHUNK
    hi = lax.Precision.HIGHEST
    lre = jnp.minimum(lam_re.astype(F32), LAMBDA_RE_MAX)
    lim = lam_im.astype(F32)
    dt = jnp.exp(log_dt.astype(F32))[:, None]
    mag = jnp.exp(lre * dt)
    lb_re = mag * jnp.cos(lim * dt)
    lb_im = mag * jnp.sin(lim * dt)
    den = lre * lre + lim * lim
    z_re = lb_re - 1.0
    coef_re = ((z_re * lre + lb_im * lim) / den)[..., None]
    coef_im = ((lb_im * lre - z_re * lim) / den)[..., None]
    b_re = b_re.astype(F32)
    b_im = b_im.astype(F32)
    bb_re = coef_re * b_re - coef_im * b_im
    bb_im = coef_re * b_im + coef_im * b_re
    c_re = c_re.astype(F32)
    c_im = c_im.astype(F32)

    def power(e):
        e = e.astype(F32)[:, None, None]
        m = jnp.exp(e * (lre * dt))
        ang = e * (lim * dt)
        return m * jnp.cos(ang), m * jnp.sin(ang)

    g = lre.shape[0]
    c = S5_GROUP
    p_re, p_im = power(jnp.arange(t + 1))
    cb_re = (c_re[:, None, :, :] * bb_re.transpose(0, 2, 1)[:, :, None, :]
             - c_im[:, None, :, :] * bb_im.transpose(0, 2, 1)[:, :, None, :])
    cb_im = (c_re[:, None, :, :] * bb_im.transpose(0, 2, 1)[:, :, None, :]
             + c_im[:, None, :, :] * bb_re.transpose(0, 2, 1)[:, :, None, :])
    cb = jnp.concatenate([cb_re, -cb_im], axis=-1).reshape(g, c * c, 2 * S5_STATE)
    pw = jnp.concatenate([p_re[:t], p_im[:t]], axis=-1).transpose(1, 2, 0)
    kvec = jnp.einsum('gkp,gpt->gkt', cb, pw, precision=hi).reshape(g, c, c // 2, 2 * t)
    r_re, r_im = power(t - 1 - jnp.arange(t))
    st_re = r_re[:, :, :, None] * bb_re - r_im[:, :, :, None] * bb_im
    st_im = r_re[:, :, :, None] * bb_im + r_im[:, :, :, None] * bb_re
    state_map = jnp.concatenate([st_re.transpose(1, 3, 0, 2).reshape(g, c * t, S5_STATE),
                                 st_im.transpose(1, 3, 0, 2).reshape(g, c * t, S5_STATE)],
                                axis=2).astype(BF16)
    q_re, q_im = p_re[1:], p_im[1:]
    ca_re = c_re[:, :, None, :] * q_re.transpose(1, 0, 2)[:, None] - c_im[:, :, None, :] * q_im.transpose(1, 0, 2)[:, None]
    ca_im = c_re[:, :, None, :] * q_im.transpose(1, 0, 2)[:, None] + c_im[:, :, None, :] * q_re.transpose(1, 0, 2)[:, None]
    cross = jnp.concatenate([ca_re.transpose(0, 3, 1, 2).reshape(g, S5_STATE, c * t),
                             -ca_im.transpose(0, 3, 1, 2).reshape(g, S5_STATE, c * t)], axis=1).astype(BF16)
    n_steps = max(1, (n_chunks - 1).bit_length())
    s_re, s_im = power(t * (2 ** jnp.arange(n_steps)))
    tab = jnp.stack([jnp.concatenate([s_re, s_re], axis=-1),
                     jnp.concatenate([-s_im, s_im], axis=-1)], axis=1)
    tab = tab.transpose(2, 0, 1, 3).reshape(g, 2 * n_steps, 2 * S5_STATE)
    return kvec, state_map, cross, tab, n_steps


def _s5_conv_kernel(n_chunks, n_steps, u_ref, kvec_ref, st_ref, cross_ref, tab_ref, y_ref, conv):
    t = S5_CHUNK
    row = lax.broadcasted_iota(jnp.int32, (t, LANES), 0)
    lane = lax.broadcasted_iota(jnp.int32, (t, LANES), 1)
    causal = jnp.bitwise_and(lane, t - 1) >= row
    for cin in range(S5_GROUP):
        for cp in range(S5_GROUP // 2):
            base = jnp.broadcast_to(kvec_ref[0, cin, cp:cp + 1, :], (t, LANES))
            blk = pltpu.roll(base, 0, 1, stride=1, stride_axis=0)
            conv[cin * t:(cin + 1) * t, cp * LANES:(cp + 1) * LANES] = jnp.where(causal, blk, 0.0).astype(BF16)
    u = jnp.concatenate([u_ref[0, cin] for cin in range(S5_GROUP)], axis=1)
    y = _dot(u, conv[...])
    x = _dot(u, st_ref[0])
    chunk = lax.broadcasted_iota(jnp.int32, x.shape, 0) % n_chunks
    half = x.shape[1] // 2
    for j in range(n_steps):
        sh = 1 << j
        xs = jnp.where(chunk >= sh, pltpu.roll(x, sh, 0), 0.0)
        x = x + tab_ref[0, 2 * j:2 * j + 1, :] * xs + tab_ref[0, 2 * j + 1:2 * j + 2, :] * pltpu.roll(xs, half, 1)
    x_in = jnp.where(chunk >= 1, pltpu.roll(x, 1, 0), 0.0)
    y = (y + _dot(x_in.astype(BF16), cross_ref[0])).astype(y_ref.dtype)
    for cout in range(S5_GROUP):
        y_ref[0, cout] = y[:, cout * t:(cout + 1) * t]


def _s5_conv(u2d, batch, seq, tables):
    kvec, state_map, cross, tab, n_steps = tables
    n, width = u2d.shape
    g = width // S5_GROUP
    t = S5_CHUNK
    n_chunks = seq // t
    ct = S5_GROUP * t
    ut = u2d.T.reshape(g, S5_GROUP, batch * n_chunks, t)
    rows = batch * n_chunks
    slab = pl.BlockSpec((1, S5_GROUP, rows, t), lambda i: (i, 0, 0, 0))
    y = pl.pallas_call(
        functools.partial(_s5_conv_kernel, n_chunks, n_steps),
        grid=(g,),
        in_specs=[
            slab,
            pl.BlockSpec((1,) + kvec.shape[1:], lambda i: (i, 0, 0, 0)),
            pl.BlockSpec((1,) + state_map.shape[1:], lambda i: (i, 0, 0)),
            pl.BlockSpec((1,) + cross.shape[1:], lambda i: (i, 0, 0)),
            pl.BlockSpec((1,) + tab.shape[1:], lambda i: (i, 0, 0)),
        ],
        out_specs=slab,
        out_shape=jax.ShapeDtypeStruct((g, S5_GROUP, rows, t), BF16),
        scratch_shapes=[pltpu.VMEM((ct, ct), BF16)],
        compiler_params=_params("parallel"),
        name="s5_conv",
    )(ut, kvec, state_map, cross, tab)
    return y.reshape(width, n).T


def _pair_blockdiag(y, left):
    return jnp.concatenate([jnp.where(left, y, 0.0), jnp.where(left, 0.0, y)], axis=0).astype(BF16)


def _rwkv_chunk_kernel(r_ref, lw_ref, k_ref, v_ref, an_ref, bb_ref, rhat_o, yhat_o, g_o, h_o):
    t = RWKV_CHUNK
    rows, width = r_ref.shape
    n_chunks = rows // t
    pairs = width // LANES
    row = lax.broadcasted_iota(jnp.int32, (t, LANES), 0)
    lane = lax.broadcasted_iota(jnp.int32, (t, LANES), 1)
    col = jnp.bitwise_and(lane, RWKV_HEAD - 1)
    left = lane < RWKV_HEAD
    incl = row >= col
    strict = row > col
    same_blk = (row // RWKV_SUB) == (col // RWKV_SUB)
    eye = jnp.where(row == col, 1.0, 0.0).astype(F32)
    brow = lax.broadcasted_iota(jnp.int32, (LANES, LANES), 0)
    bcol = lax.broadcasted_iota(jnp.int32, (LANES, LANES), 1)
    same_head = (brow // RWKV_HEAD) == (bcol // RWKV_HEAD)
    eye_full = brow == bcol
    crow = lax.broadcasted_iota(jnp.int32, (rows, rows), 0)
    ccol = lax.broadcasted_iota(jnp.int32, (rows, rows), 1)
    tril = jnp.where(crow >= ccol, jnp.where((crow // t) == (ccol // t), 1.0, 0.0), 0.0).astype(BF16)

    lw = lw_ref[...]
    p1 = lw.astype(BF16)
    p2 = (lw - p1.astype(F32)).astype(BF16)
    cs = _dot(tril, p1) + _dot(tril, p2)
    a_t = an_ref[...] * jnp.exp(cs - lw)
    r_t = r_ref[...] * jnp.exp(cs)
    p_inv = jnp.exp(-cs)
    b_t = bb_ref[...] * p_inv
    k_t = k_ref[...] * p_inv

    def bd(y):
        return _pair_blockdiag(y, left)

    def pmm(x, y_bd):
        return _dot(x.astype(BF16), y_bd)

    units = [(c, j) for c in range(n_chunks) for j in range(pairs)]

    def tile(arr, c, j):
        return arr[c * t:(c + 1) * t, j * LANES:(j + 1) * LANES]

    a2 = [tile(a_t, c, j) for c, j in units]
    r2 = [tile(r_t, c, j) for c, j in units]
    v2 = [tile(v_ref[...], c, j) for c, j in units]
    ar = [jnp.concatenate([a, r], axis=0).astype(BF16) for a, r in zip(a2, r2)]
    prod = [_dot_nt(x, jnp.concatenate([bd(tile(b_t, c, j)), bd(tile(k_t, c, j))], axis=0))
            for x, (c, j) in zip(ar, units)]
    l_ab = [jnp.where(strict, p[:t, :LANES], 0.0) for p in prod]
    l_rb = [jnp.where(incl, p[t:, :LANES], 0.0) for p in prod]
    l_ak = [jnp.where(strict, p[:t, LANES:], 0.0) for p in prod]
    l_rk = [jnp.where(incl, p[t:, LANES:], 0.0) for p in prod]
    l_d = [jnp.where(same_blk, m, 0.0) for m in l_ab]
    l_o = [m - d for m, d in zip(l_ab, l_d)]
    l2 = [pmm(m, bd(m)) for m in l_d]
    q0 = [eye + m for m in l_d]
    s1 = [pmm(m, jnp.concatenate([bd(m), bd(q)], axis=1)) for m, q in zip(l2, q0)]
    l4 = [z[:, :LANES] for z in s1]
    q1 = [q + z[:, LANES:] for q, z in zip(q0, s1)]
    s2 = [pmm(m, jnp.concatenate([bd(m), bd(q)], axis=1)) for m, q in zip(l4, q1)]
    l8 = [z[:, :LANES] for z in s2]
    q2 = [q + z[:, LANES:] for q, z in zip(q1, s2)]
    d_inv = [q + pmm(m, bd(q)) for m, q in zip(l8, q2)]
    akv = [pmm(jnp.concatenate([m, n], axis=0), bd(v)) for m, n, v in zip(l_ak, l_rk, v2)]
    x1 = [z[:t] for z in akv]

    def bd2(z1, z2):
        return jnp.concatenate([bd(z1), bd(z2)], axis=1)

    dz = [pmm(d, bd2(a, x)) for d, a, x in zip(d_inv, a2, x1)]
    n1 = [pmm(d, bd(o)) for d, o in zip(d_inv, l_o)]
    n2 = [pmm(m, bd(m)) for m in n1]
    t1 = [z + pmm(m, bd2(z[:, :LANES], z[:, LANES:])) for z, m in zip(dz, n2)]
    wu = [z + pmm(m, bd2(z[:, :LANES], z[:, LANES:])) for z, m in zip(t1, n1)]
    ry = [pmm(m, bd2(z[:, :LANES], z[:, LANES:])) for m, z in zip(l_rb, wu)]
    rk_v = [z[t:] for z in akv]

    for i, (c, j) in enumerate(units):
        rs = slice(c * t, (c + 1) * t)
        ls = slice(j * LANES, (j + 1) * LANES)
        rhat_o[rs, ls] = r2[i] + ry[i][:, :LANES]
        yhat_o[rs, ls] = ry[i][:, LANES:] + rk_v[i]
        cs_c = cs[rs, ls]
        cs_end = cs_c[t - 1:t, :]
        p_end = jnp.exp(cs_end - cs_c)
        b_h = (bb_ref[rs, ls] * p_end).astype(BF16)
        k_h = (k_ref[rs, ls] * p_end).astype(BF16)
        g_full = _dot_tn(wu[i][:, :LANES].astype(BF16), b_h)
        g_bd = jnp.where(same_head, g_full, 0.0) + jnp.where(eye_full, jnp.exp(cs_end), 0.0)
        g_o[c, j] = g_bd.astype(BF16)
        uv = jnp.concatenate([wu[i][:, LANES:], v2[i]], axis=0).astype(BF16)
        h_full = _dot_tn(uv, jnp.concatenate([b_h, k_h], axis=0))
        h_o[c, j] = jnp.where(left, h_full[:RWKV_HEAD], h_full[RWKV_HEAD:])


RWKV_CHUNKS_PER_STEP = 4


def _rwkv_chunks(r, lw, k, v, an, bb):
    n, width = r.shape
    t = RWKV_CHUNK
    cps = RWKV_CHUNKS_PER_STEP
    pairs = width // LANES
    nck = n // t
    tok = pl.BlockSpec((cps * t, width), lambda i: (i, 0))
    return pl.pallas_call(
        _rwkv_chunk_kernel,
        grid=(nck // cps,),
        in_specs=[tok] * 6,
        out_specs=[tok, tok,
                   pl.BlockSpec((cps, pairs, LANES, LANES), lambda i: (i, 0, 0, 0)),
                   pl.BlockSpec((cps, pairs, RWKV_HEAD, LANES), lambda i: (i, 0, 0, 0))],
        out_shape=[jax.ShapeDtypeStruct((n, width), F32)] * 2
        + [jax.ShapeDtypeStruct((nck, pairs, LANES, LANES), BF16),
           jax.ShapeDtypeStruct((nck, pairs, RWKV_HEAD, LANES), F32)],
        compiler_params=_params("parallel"),
        name="rwkv_chunk",
    )(r, lw, k, v, an, bb)


def _rwkv_scan_kernel(rhat_ref, yhat_ref, g_ref, h_ref, gate_ref, bonus_ref, lnw_ref, lnb_ref,
                      ones_ref, y_ref, state):
    t = RWKV_CHUNK
    batch, pairs = state.shape[0], state.shape[1]
    n_chunks = rhat_ref.shape[1] // t
    lane = lax.broadcasted_iota(jnp.int32, (RWKV_HEAD, LANES), 1)
    left = lane < RWKV_HEAD

    @pl.when(pl.program_id(0) == 0)
    def _():
        state[...] = jnp.zeros_like(state)

    s = [[state[b, j] for j in range(pairs)] for b in range(batch)]
    y_rows = []
    for b in range(batch):
        chunk_rows = []
        for c in range(n_chunks):
            rs = slice(c * t, (c + 1) * t)
            tiles = []
            for j in range(pairs):
                ls = slice(j * LANES, (j + 1) * LANES)
                s_bd = _pair_blockdiag(s[b][j], left)
                tiles.append(yhat_ref[b, rs, ls] + _dot_nt(rhat_ref[b, rs, ls].astype(BF16), s_bd))
                s[b][j] = _dot(s[b][j].astype(BF16), g_ref[b, c, j]) + h_ref[b, c, j]
            chunk_rows.append(jnp.concatenate(tiles, axis=1))
        y_rows.append(jnp.concatenate(chunk_rows, axis=0))
    for b in range(batch):
        for j in range(pairs):
            state[b, j] = s[b][j]

    inv = 1.0 / RWKV_HEAD
    for b in range(batch):
        y = y_rows[b]
        cen = y - _head_sums(y, ones_ref) * inv
        var = _head_sums(cen * cen, ones_ref) * inv
        yn = cen * lax.rsqrt(var + RWKV_GN_EPS) * lnw_ref[...] + lnb_ref[...]
        y_ref[b] = (yn + bonus_ref[b]) * gate_ref[b]


def _rwkv_scan(rhat, yhat, g_mat, h_mat, gate, bonus, ln_w, ln_b, head_ones, batch, seq):
    n, width = rhat.shape
    t = RWKV_CHUNK
    cps = RWKV_CHUNKS_PER_STEP
    pairs = width // LANES
    nc = seq // t
    tok3 = lambda a: a.reshape(batch, seq, width)
    tok = pl.BlockSpec((batch, cps * t, width), lambda i: (0, i, 0))
    out = pl.pallas_call(
        _rwkv_scan_kernel,
        grid=(nc // cps,),
        in_specs=[tok, tok,
                  pl.BlockSpec((batch, cps, pairs, LANES, LANES), lambda i: (0, i, 0, 0, 0)),
                  pl.BlockSpec((batch, cps, pairs, RWKV_HEAD, LANES), lambda i: (0, i, 0, 0, 0)),
                  tok, tok, _full(ln_w.shape), _full(ln_b.shape), _full(head_ones.shape)],
        out_specs=tok,
        out_shape=jax.ShapeDtypeStruct((batch, seq, width), F32),
        scratch_shapes=[pltpu.VMEM((batch, pairs, RWKV_HEAD, LANES), F32)],
        compiler_params=_params("arbitrary"),
        name="rwkv_scan",
    )(tok3(rhat), tok3(yhat), g_mat.reshape(batch, nc, pairs, LANES, LANES),
      h_mat.reshape(batch, nc, pairs, RWKV_HEAD, LANES), tok3(gate), tok3(bonus), ln_w, ln_b, head_ones)
    return out.reshape(n, width)


def _post_mix_kernel(x_ref, yc_ref, u_ref, yr_ref, d_ref, wglu_ref, wout_ref, gain_ref,
                     rw_ref, rb_ref, x1_o, hm_o, gate_o, lpos_o, meta_o, tot_o, running):
    tm = x_ref.shape[0]
    width = yc_ref.shape[1]

    @pl.when(pl.program_id(0) == 0)
    def _():
        running[...] = jnp.zeros_like(running)

    y = yc_ref[...].astype(F32) + d_ref[...] * u_ref[...]
    y = 0.5 * y * (1.0 + jnp.tanh(0.7978845608028654 * (y + 0.044715 * (y * y * y))))
    y = y * _sigmoid(_dot(y.astype(BF16), wglu_ref[...]))
    x1 = (x_ref[...] + _dot(y.astype(BF16), wout_ref[:width, :])
          + _dot(yr_ref[...].astype(BF16), wout_ref[width:, :]))
    x1_o[...] = x1
    hm = _rms(x1, gain_ref[...])
    hm_o[...] = hm.astype(BF16)
    logits = _dot(hm.astype(BF16), rw_ref[...]) + rb_ref[...]

    lane = lax.broadcasted_iota(jnp.int32, logits.shape, 1)
    lanef = lane.astype(F32)
    sel = jnp.zeros(logits.shape, F32)
    idx_cols, val_cols = [], []
    work = logits
    for _ in range(TOP_K):
        m = jnp.max(work, axis=-1, keepdims=True)
        pick = jnp.min(jnp.where(work == m, lanef, float(LANES)), axis=-1, keepdims=True)
        hit = lanef == pick
        sel = jnp.where(hit, 1.0, sel)
        work = jnp.where(hit, -jnp.inf, work)
        idx_cols.append(pick)
        val_cols.append(m)
    exps = [jnp.exp(vv - val_cols[0]) for vv in val_cols]
    denom = exps[0] + exps[1] + exps[2] + exps[3]

    row = lax.broadcasted_iota(jnp.int32, (tm, tm), 0)
    col = lax.broadcasted_iota(jnp.int32, (tm, tm), 1)
    before = jnp.where(row > col, 1.0, 0.0).astype(BF16)
    local = _dot(before, sel.astype(BF16))
    cnt = jnp.sum(sel, axis=0, keepdims=True)
    cnt_al = jnp.floor((cnt + (RUN_ALIGN - 1)) * (1.0 / RUN_ALIGN)) * RUN_ALIGN
    erow = lax.broadcasted_iota(jnp.int32, (LANES, LANES), 0)
    ecol = lax.broadcasted_iota(jnp.int32, (LANES, LANES), 1)
    upper = jnp.where(erow < ecol, 1.0, 0.0).astype(BF16)
    toff = _dot(jnp.broadcast_to(cnt_al, (SUBLANES, LANES)).astype(BF16), upper)[0:1]
    tbase = running[...]
    gate_out = jnp.zeros(logits.shape, F32)
    lpos_out = jnp.zeros(logits.shape, F32)
    for j in range(TOP_K):
        lp = jnp.sum(jnp.where(lanef == idx_cols[j], local + toff, 0.0), axis=-1, keepdims=True)
        gate_out = jnp.where(lane == j, exps[j] / denom, gate_out)
        lpos_out = jnp.where(lane == j, lp, lpos_out)
    gate_o[...] = gate_out
    lpos_o[...] = lpos_out.astype(jnp.int32)
    srow = lax.broadcasted_iota(jnp.int32, (SUBLANES, LANES), 0)
    meta = jnp.where(srow == 0, cnt, jnp.where(srow == 1, tbase, jnp.where(srow == 2, toff, 0.0)))
    meta_o[...] = meta.astype(jnp.int32)
    running[...] = tbase + cnt_al
    tot_o[...] = jnp.broadcast_to(running[...], tot_o.shape).astype(jnp.int32)


def _post_mix(x2d, yconv, u, y_rwkv, s5_d, w_glu, w_out, gain, rw, rb):
    n, d = x2d.shape
    width = yconv.shape[1]
    tm = MOE_TILE
    row = lambda i: (i, 0)
    tok_d = pl.BlockSpec((tm, d), row)
    tok_w = pl.BlockSpec((tm, width), row)
    tok_l = pl.BlockSpec((tm, LANES), row)
    return pl.pallas_call(
        _post_mix_kernel,
        grid=(n // tm,),
        in_specs=[tok_d, tok_w, tok_w, tok_w, _full(s5_d.shape), _full(w_glu.shape),
                  _full(w_out.shape), _full(gain.shape), _full(rw.shape), _full(rb.shape)],
        out_specs=[tok_d, tok_d, tok_l, tok_l,
                   pl.BlockSpec((SUBLANES, LANES), row), _full((SUBLANES, LANES))],
        out_shape=[jax.ShapeDtypeStruct((n, d), F32), jax.ShapeDtypeStruct((n, d), BF16),
                   jax.ShapeDtypeStruct((n, LANES), F32), jax.ShapeDtypeStruct((n, LANES), jnp.int32),
                   jax.ShapeDtypeStruct((n // tm * SUBLANES, LANES), jnp.int32),
                   jax.ShapeDtypeStruct((SUBLANES, LANES), jnp.int32)],
        scratch_shapes=[pltpu.VMEM((1, LANES), F32)],
        compiler_params=_params("arbitrary"),
        name="post_mix",
    )(x2d, yconv, u, y_rwkv, s5_d, w_glu, w_out, gain, rw, rb)


def _sorted_rows(tm, n_experts, multiple):
    return -(-(tm * TOP_K + n_experts * (RUN_ALIGN - 1)) // multiple) * multiple


RUN_BIG = 8


def _groups(count):
    return lax.shift_right_logical(count + (RUN_ALIGN - 1), RUN_ALIGN.bit_length() - 1)


def _start_runs(n_experts, pstart_ref, meta_ref, make_copy):
    shift = RUN_BIG.bit_length() - 1
    for e in range(n_experts):
        groups = _groups(meta_ref[0, e])
        seg = pstart_ref[e] + meta_ref[1, e]
        loc = meta_ref[2, e]
        n_big = lax.shift_right_logical(groups, shift)
        rem = groups & (RUN_BIG - 1)

        def body(g, c, seg=seg, loc=loc):
            off = g * (RUN_BIG * RUN_ALIGN)
            make_copy(pl.multiple_of(loc + off, RUN_ALIGN), pl.multiple_of(seg + off, RUN_ALIGN),
                      RUN_BIG * RUN_ALIGN).start(priority=e % 2)
            return c

        lax.fori_loop(0, n_big, body, 0)
        size = RUN_BIG // 2
        while size >= 1:
            off = (n_big * RUN_BIG + (rem & (RUN_BIG - 2 * size))) * RUN_ALIGN

            @pl.when((rem & size) != 0)
            def _(off=off, size=size, seg=seg, loc=loc):
                make_copy(pl.multiple_of(loc + off, RUN_ALIGN), pl.multiple_of(seg + off, RUN_ALIGN),
                          size * RUN_ALIGN).start(priority=e % 2)

            size //= 2


def _wait_runs(n_experts, meta_ref, make_copy):
    groups = 0
    for e in range(n_experts):
        groups = groups + _groups(meta_ref[0, e])

    def wait_big(g, c):
        make_copy(0, 0, RUN_BIG * RUN_ALIGN).wait()
        return c

    def wait_small(g, c):
        make_copy(0, 0, RUN_ALIGN).wait()
        return c

    lax.fori_loop(0, lax.shift_right_logical(groups, RUN_BIG.bit_length() - 1), wait_big, 0)
    lax.fori_loop(0, groups & (RUN_BIG - 1), wait_small, 0)


def _dispatch_kernel(n_experts, pstart_ref, tot_ref, meta_ref, meta_prev_ref, lpos_ref, hm_ref, xs_ref,
                     srt, zero, sem):
    tm = hm_ref.shape[0]
    rows = srt.shape[1]
    i = pl.program_id(0)
    cur = i % 2
    pos_t = jnp.transpose(lpos_ref[...].astype(F32))
    hm = hm_ref[...]
    for r0 in range(0, rows, SORT_BLOCK):
        nr = min(SORT_BLOCK, rows - r0)
        rid = (lax.broadcasted_iota(jnp.int32, (nr, tm), 0) + r0).astype(F32)
        perm = jnp.zeros((nr, tm), F32)
        for j in range(TOP_K):
            perm = jnp.where(rid == pos_t[j:j + 1, :], 1.0, perm)
        srt[cur, r0:r0 + nr, :] = _dot(perm.astype(BF16), hm)

    def copies(slot):
        def make_copy(loc, dst, size):
            return pltpu.make_async_copy(srt.at[slot, pl.ds(loc, size)], xs_ref.at[pl.ds(dst, size)],
                                         sem.at[slot])
        return make_copy

    _start_runs(n_experts, pstart_ref, meta_ref, copies(cur))

    @pl.when(i > 0)
    def _():
        _wait_runs(n_experts, meta_prev_ref, copies(1 - cur))

    @pl.when(i == pl.num_programs(0) - 1)
    def _():
        _wait_runs(n_experts, meta_ref, copies(cur))

    @pl.when(pl.program_id(0) == pl.num_programs(0) - 1)
    def _():
        zero[...] = jnp.zeros_like(zero)
        for wait in (False, True):
            for e in range(n_experts):
                used = tot_ref[e]
                start = pstart_ref[e] + used
                groups = lax.shift_right_logical((-used) & (EXPERT_TILE - 1), RUN_ALIGN.bit_length() - 1)

                def body(g, c, start=start):
                    cp = pltpu.make_async_copy(
                        zero, xs_ref.at[pl.ds(pl.multiple_of(start + g * RUN_ALIGN, RUN_ALIGN), RUN_ALIGN)],
                        sem.at[0])
                    if wait:
                        cp.wait()
                    else:
                        cp.start()
                    return c

                lax.fori_loop(0, groups, body, 0)

        last = n_experts - 1
        end = pstart_ref[last] + tot_ref[last] + ((-tot_ref[last]) & (EXPERT_TILE - 1))
        srt[0, 0:EXPERT_TILE, :] = jnp.zeros((EXPERT_TILE, srt.shape[2]), srt.dtype)
        for wait in (False, True):
            def tail(b, c):
                cp = pltpu.make_async_copy(
                    srt.at[0, pl.ds(0, EXPERT_TILE)],
                    xs_ref.at[pl.ds(pl.multiple_of(end + b * EXPERT_TILE, EXPERT_TILE), EXPERT_TILE)], sem.at[0])
                if wait:
                    cp.wait()
                else:
                    cp.start()
                return c

            lax.fori_loop(0, lax.shift_right_logical(xs_ref.shape[0] - end, EXPERT_TILE.bit_length() - 1), tail, 0)


def _dispatch(hm, lpos, meta, pstarts, totals, m_pad, n_experts):
    n, dh = hm.shape
    tm = MOE_TILE
    rows = _sorted_rows(tm, n_experts, RUN_ALIGN)
    grid_spec = pltpu.PrefetchScalarGridSpec(
        num_scalar_prefetch=2,
        grid=(n // tm,),
        in_specs=[pl.BlockSpec((SUBLANES, LANES), lambda i, ps, tt: (i, 0), memory_space=pltpu.SMEM),
                  pl.BlockSpec((SUBLANES, LANES), lambda i, ps, tt: (jnp.maximum(i - 1, 0), 0),
                               memory_space=pltpu.SMEM),
                  pl.BlockSpec((tm, LANES), lambda i, ps, tt: (i, 0)),
                  pl.BlockSpec((tm, dh), lambda i, ps, tt: (i, 0))],
        out_specs=pl.BlockSpec(memory_space=pl.ANY),
        scratch_shapes=[pltpu.VMEM((2, rows, dh), F32), pltpu.VMEM((RUN_ALIGN, dh), F32),
                        pltpu.SemaphoreType.DMA((2,))],
    )
    return pl.pallas_call(
        functools.partial(_dispatch_kernel, n_experts),
        grid_spec=grid_spec,
        out_shape=jax.ShapeDtypeStruct((m_pad, dh), F32),
        compiler_params=_params("arbitrary"),
        name="dispatch",
    )(pstarts, totals, meta, meta, lpos, hm)


def _experts_kernel(be_ref, nu_ref, xs_ref, wg_ref, bg_ref, wu_ref, bu_ref, wd_ref, bd_ref, ys_ref,
                    wg_s, wu_s, wd_s):
    i = pl.program_id(0)
    changed = jnp.logical_or(i == 0, be_ref[i] != be_ref[jnp.maximum(i - 1, 0)])

    @pl.when(changed)
    def _():
        wg_s[...] = wg_ref[0].astype(BF16)
        wu_s[...] = wu_ref[0].astype(BF16)
        wd_s[...] = wd_ref[0].astype(BF16)

    @pl.when(i < nu_ref[0])
    def _():
        xb = xs_ref[...].astype(BF16)
        gt = jnp.minimum(_dot(xb, wg_s[...]) + bg_ref[0], SWIGLU_LIMIT)
        up = jnp.clip(_dot(xb, wu_s[...]) + bu_ref[0], -SWIGLU_LIMIT, SWIGLU_LIMIT)
        act = (up + 1.0) * gt * _sigmoid(SWIGLU_ALPHA * gt)
        ys_ref[...] = _dot(act.astype(BF16), wd_s[...]) + bd_ref[0]

    @pl.when(i >= nu_ref[0])
    def _():
        ys_ref[...] = jnp.zeros_like(ys_ref)


def _experts(xs, block_e, n_used, wg, bg, wu, bu, wd, bd):
    m_pad, dh = xs.shape
    tmb = EXPERT_TILE
    d, dff = wg.shape[1], wg.shape[2]
    wmap = lambda i, be, nu: (be[i], 0, 0)
    grid_spec = pltpu.PrefetchScalarGridSpec(
        num_scalar_prefetch=2,
        grid=(m_pad // tmb,),
        in_specs=[pl.BlockSpec((tmb, dh), lambda i, be, nu: (jnp.where(i < nu[0], i, 0), 0)),
                  pl.BlockSpec((1, d, dff), wmap), pl.BlockSpec((1, 1, dff), wmap),
                  pl.BlockSpec((1, d, dff), wmap), pl.BlockSpec((1, 1, dff), wmap),
                  pl.BlockSpec((1, dff, d), wmap), pl.BlockSpec((1, 1, d), wmap)],
        out_specs=pl.BlockSpec((tmb, dh), lambda i, be, nu: (i, 0)),
        scratch_shapes=[pltpu.VMEM((d, dff), BF16), pltpu.VMEM((d, dff), BF16), pltpu.VMEM((dff, d), BF16)],
    )
    return pl.pallas_call(
        _experts_kernel,
        grid_spec=grid_spec,
        out_shape=jax.ShapeDtypeStruct((m_pad, dh), F32),
        compiler_params=_params("arbitrary"),
        name="experts",
    )(block_e, n_used, xs, wg, bg, wu, bu, wd, bd)


def _final_kernel(n_experts, pstart_ref, meta_ref, meta_next_ref, lpos_ref, x1_ref, gate_ref, p_ref, ys_ref,
                  gple_ref, wproj_ref, wgate_ref, gfin_ref, out_ref, buf, sem):
    tm = x1_ref.shape[0]
    rows = buf.shape[1]
    i = pl.program_id(0)
    cur = i % 2

    def copies(slot):
        def make_copy(loc, src, size):
            return pltpu.make_async_copy(ys_ref.at[pl.ds(src, size)], buf.at[slot, pl.ds(loc, size)],
                                         sem.at[slot])
        return make_copy

    @pl.when(i == 0)
    def _():
        buf[...] = jnp.zeros_like(buf)
        _start_runs(n_experts, pstart_ref, meta_ref, copies(0))

    @pl.when(i + 1 < pl.num_programs(0))
    def _():
        _start_runs(n_experts, pstart_ref, meta_next_ref, copies(1 - cur))

    cid = lax.broadcasted_iota(jnp.int32, (tm, rows), 1)
    lpos = lpos_ref[...]
    gates = gate_ref[...]
    comb = jnp.zeros((tm, rows), F32)
    for j in range(TOP_K):
        comb = jnp.where(cid == lpos[:, j:j + 1], gates[:, j:j + 1], comb)

    _wait_runs(n_experts, meta_ref, copies(cur))
    x2 = x1_ref[...] + _dot(comb.astype(BF16), buf[cur].astype(BF16))
    gate = _sigmoid(_dot(_rms(x2, gple_ref[...]).astype(BF16), wgate_ref[...]))
    x3 = x2 + _dot(p_ref[...].astype(BF16), wproj_ref[...]) * gate
    out_ref[...] = _rms(x3, gfin_ref[...])


def _final(x1, gates, lpos, meta, pstarts, p2d, ys, g_ple, w_proj, w_gate, g_fin, n_experts):
    n, d = x1.shape
    tm = MOE_TILE
    steps = n // tm
    rows = _sorted_rows(tm, n_experts, LANES)
    row = lambda i, ps: (i, 0)
    const = lambda shape: pl.BlockSpec(shape, lambda i, ps: (0,) * len(shape))
    grid_spec = pltpu.PrefetchScalarGridSpec(
        num_scalar_prefetch=1,
        grid=(steps,),
        in_specs=[pl.BlockSpec((SUBLANES, LANES), row, memory_space=pltpu.SMEM),
                  pl.BlockSpec((SUBLANES, LANES), lambda i, ps: (jnp.minimum(i + 1, steps - 1), 0),
                               memory_space=pltpu.SMEM),
                  pl.BlockSpec((tm, LANES), row), pl.BlockSpec((tm, d), row), pl.BlockSpec((tm, LANES), row),
                  pl.BlockSpec((tm, p2d.shape[1]), row), pl.BlockSpec(memory_space=pl.ANY),
                  const(g_ple.shape), const(w_proj.shape), const(w_gate.shape), const(g_fin.shape)],
        out_specs=pl.BlockSpec((tm, d), row),
        scratch_shapes=[pltpu.VMEM((2, rows, ys.shape[1]), ys.dtype), pltpu.SemaphoreType.DMA((2,))],
    )
    return pl.pallas_call(
        functools.partial(_final_kernel, n_experts),
        grid_spec=grid_spec,
        out_shape=jax.ShapeDtypeStruct((n, d), F32),
        compiler_params=_params("arbitrary"),
        name="final",
    )(pstarts, meta, meta, lpos, x1, gates, p2d, ys, g_ple, w_proj, w_gate, g_fin)


def _head_ones():
    hd = jnp.arange(MXU_WIDTH) // RWKV_HEAD
    return (hd[:, None] == hd[None, :]).astype(BF16)


def _layer(x2d, p2d, batch, seq, norm_mix_g, w_in, s5_lam_re, s5_lam_im, s5_log_dt, s5_b_re, s5_b_im,
           s5_c_re, s5_c_im, s5_d, s5_w_glu, mu_rkv, mu_wag, w0, w1, w2, a0, a1, a2, g1, g2, k_k, k_a,
           r_k, ln_w, ln_b, w_out, norm_moe_g, router_w, router_b, wg, bg, wu, bu, wd, bd,
           norm_ple_g, ple_w_proj, ple_w_gate):
    n, d = x2d.shape
    width = w_in.shape[1] // 4
    n_experts = router_w.shape[1]
    row2 = lambda t: t.reshape(1, -1).astype(F32)
    head_ones = _head_ones()
    vecs = jnp.zeros((SUBLANES, width), F32)
    vecs = vecs.at[0].set(w0).at[1].set(a0).at[2].set(k_k).at[3].set(k_a).at[4].set(r_k.reshape(-1))

    w_ext, w2_cat = _mix_in_weights(w_in, w1, a1, g1, mu_wag, w2, a2, g2)
    u, r, lw, k, v, an, bb, gate, bonus, u_bf = _mix_in(
        x2d, seq, row2(norm_mix_g), w_ext, mu_rkv.astype(F32), vecs, w2_cat, head_ones)

    tables = _s5_tables(s5_lam_re, s5_lam_im, s5_log_dt, s5_b_re, s5_b_im, s5_c_re, s5_c_im,
                        seq // S5_CHUNK)
    yconv = _s5_conv(u_bf, batch, seq, tables)

    rhat, yhat, g_mat, h_mat = _rwkv_chunks(r, lw, k, v, an, bb)
    y_rwkv = _rwkv_scan(rhat, yhat, g_mat, h_mat, gate, bonus, row2(ln_w), row2(ln_b), head_ones,
                        batch, seq)

    rw = jnp.zeros((d, LANES), BF16).at[:, :n_experts].set(router_w.astype(BF16))
    rb = jnp.full((1, LANES), NEG_BIG, F32).at[0, :n_experts].set(router_b.astype(F32))
    x1, hm, gates, lpos, meta, totals = _post_mix(
        x2d, yconv, u, y_rwkv, row2(s5_d), s5_w_glu.astype(BF16), w_out.astype(BF16),
        row2(norm_moe_g), rw, rb)

    tmb = EXPERT_TILE
    n_tiles = n // MOE_TILE
    max_rows = n * TOP_K + n_tiles * n_experts * (RUN_ALIGN - 1)
    n_blocks = -(-max_rows // tmb) + n_experts
    seg = totals[0, :n_experts].astype(jnp.int32)
    padded = ((seg + tmb - 1) // tmb) * tmb
    pends = jnp.cumsum(padded)
    pstarts = (pends - padded).astype(jnp.int32)
    block_start = jnp.arange(n_blocks, dtype=jnp.int32) * tmb
    block_e = jnp.minimum(jnp.sum((pends[None, :] <= block_start[:, None]).astype(jnp.int32), axis=1),
                          n_experts - 1).astype(jnp.int32)
    n_used = (pends[-1] // tmb).astype(jnp.int32).reshape(1)

    xs = _dispatch(hm, lpos, meta, pstarts, seg, n_blocks * tmb, n_experts)
    ys = _experts(xs, block_e, n_used, wg, bg.reshape(n_experts, 1, -1).astype(F32),
                  wu, bu.reshape(n_experts, 1, -1).astype(F32), wd,
                  bd.reshape(n_experts, 1, -1).astype(F32))
    return (x1, gates, lpos, meta, pstarts, ys, p2d, row2(norm_ple_g), ple_w_proj.astype(BF16),
            ple_w_gate.astype(BF16), n_experts)


def kernel(x, p, norm_mix_g, w_in, s5_lam_re, s5_lam_im, s5_log_dt, s5_b_re, s5_b_im, s5_c_re, s5_c_im, s5_d, s5_w_glu, rwkv_mu_rkv, rwkv_mu_wag, rwkv_w0, rwkv_w1, rwkv_w2, rwkv_a0, rwkv_a1, rwkv_a2, rwkv_g1, rwkv_g2, rwkv_k_k, rwkv_k_a, rwkv_r_k, rwkv_ln_w, rwkv_ln_b, w_out, norm_moe_g, router_w, router_b, exp_w_gate, exp_b_gate, exp_w_up, exp_b_up, exp_w_down, exp_b_down, norm_ple_g, ple_w_proj, ple_w_gate, final_norm_g):
    batch, seq, d = x.shape
    assert w_in.shape[0] == 1, "the final kernel fuses the last RMSNorm: single-layer stacks only"
    i = 0
    x1, gates, lpos, meta, pstarts, ys, p2d, g_ple, w_proj, w_gate, n_experts = _layer(
        x.reshape(batch * seq, d), p[i].reshape(batch * seq, -1), batch, seq, norm_mix_g[i], w_in[i],
        s5_lam_re[i], s5_lam_im[i], s5_log_dt[i], s5_b_re[i], s5_b_im[i], s5_c_re[i], s5_c_im[i],
        s5_d[i], s5_w_glu[i], rwkv_mu_rkv[i], rwkv_mu_wag[i], rwkv_w0[i], rwkv_w1[i], rwkv_w2[i],
        rwkv_a0[i], rwkv_a1[i], rwkv_a2[i], rwkv_g1[i], rwkv_g2[i], rwkv_k_k[i], rwkv_k_a[i],
        rwkv_r_k[i], rwkv_ln_w[i], rwkv_ln_b[i], w_out[i], norm_moe_g[i], router_w[i],
        router_b[i], exp_w_gate[i], exp_b_gate[i], exp_w_up[i], exp_b_up[i], exp_w_down[i],
        exp_b_down[i], norm_ple_g[i], ple_w_proj[i], ple_w_gate[i])
    out = _final(x1, gates, lpos, meta, pstarts, p2d, ys, g_ple, w_proj, w_gate,
                 final_norm_g.reshape(1, -1).astype(F32), n_experts)
    return out.reshape(batch, seq, d)
```

```python
import functools

import jax
import jax.numpy as jnp
from jax import lax
from jax.experimental import pallas as pl
from jax.experimental.pallas import tpu as pltpu

F32 = jnp.float32
BF16 = jnp.bfloat16

S5_GROUP = 16
S5_STATE = 64
RWKV_HEAD = 64
DECAY_LORA = 64
ICLR_LORA = 64
DECAY_SCALE = 0.6065306597126334
TOP_K = 4
RMS_EPS = 1e-6
RWKV_GN_EPS = 64e-5
LAMBDA_RE_MAX = -1e-4
SWIGLU_LIMIT = 7.0
SWIGLU_ALPHA = 1.702

LANES = 128
SUBLANES = 8
MXU_WIDTH = 256
VMEM_LIMIT = 56 * 1024 * 1024

S5_CHUNK = 64
RWKV_CHUNK = 64
RWKV_SUB = 16
TOKEN_TILE = 512
EXPERT_TILE = 512
MOE_TILE = 512
RUN_ALIGN = 8
SORT_BLOCK = 256
FINAL_BLOCK = 256
NEG_BIG = -1e30


def _dot(a, b):
    return jnp.dot(a, b, preferred_element_type=F32)


def _dot_nt(a, b):
    return lax.dot_general(a, b, (((1,), (1,)), ((), ())), preferred_element_type=F32)


def _dot_tn(a, b):
    return lax.dot_general(a, b, (((0,), (0,)), ((), ())), preferred_element_type=F32)


def _rms(t, gain):
    return t * lax.rsqrt(jnp.mean(t * t, axis=-1, keepdims=True) + RMS_EPS) * gain


def _sigmoid(t):
    return 1.0 / (1.0 + jnp.exp(-t))


def _head_sums(x, ones_ref):
    w = ones_ref.shape[0]
    ones = ones_ref[...]
    xb = x.astype(BF16)
    return jnp.concatenate([_dot(xb[:, i:i + w], ones) for i in range(0, x.shape[1], w)], axis=1)


def _params(*sem):
    return pltpu.CompilerParams(dimension_semantics=sem, vmem_limit_bytes=VMEM_LIMIT)


def _full(shape):
    return pl.BlockSpec(shape, lambda *_: (0,) * len(shape))


def _mix_in_kernel(seq, x_ref, xp_ref, gain_ref, win_ref, murkv_ref, vec_ref, w2_ref, ones_ref,
                   u_o, r_o, lw_o, k_o, v_o, an_o, bb_o, g_o, bonus_o, ub_o):
    tm = x_ref.shape[0]
    width = u_o.shape[1]
    lora = w2_ref.shape[0]
    gain = gain_ref[...]
    h = _rms(x_ref[...], gain)
    keep = jnp.where((pl.program_id(0) * tm) % seq == 0, 0.0, 1.0)
    hp = _rms(xp_ref[...], gain) * keep
    h_ext = jnp.concatenate([hp, h], axis=0).astype(BF16)
    proj = _dot(h_ext, win_ref[...])
    shifted = pltpu.roll(proj[:, width:4 * width + lora], 1, 0)[SUBLANES:]
    cur = proj[SUBLANES:]
    u_o[...] = cur[:, :width]
    ub_o[...] = cur[:, :width].astype(BF16)

    l1 = cur[:, 4 * width + lora:] + shifted[:, 3 * width:]
    lane = lax.broadcasted_iota(jnp.int32, l1.shape, 1)
    hidden = jnp.where(lane < DECAY_LORA, jnp.tanh(l1),
                       jnp.where(lane < DECAY_LORA + ICLR_LORA, l1, _sigmoid(l1))).astype(BF16)
    z3 = _dot(hidden, w2_ref[...])

    w0 = vec_ref[0:1, :]
    a0 = vec_ref[1:2, :]
    k_k = vec_ref[2:3, :]
    k_a = vec_ref[3:4, :]
    r_k = vec_ref[4:5, :]
    lw_o[...] = (-DECAY_SCALE) * _sigmoid(w0 + z3[:, :width])
    a = _sigmoid(a0 + z3[:, width:2 * width])
    g_o[...] = z3[:, 2 * width:]

    r_cur = cur[:, width:2 * width]
    k_cur = cur[:, 2 * width:3 * width]
    v_cur = cur[:, 3 * width:4 * width]
    r = r_cur + (shifted[:, :width] - r_cur) * murkv_ref[0:1, :]
    k = k_cur + (shifted[:, width:2 * width] - k_cur) * murkv_ref[1:2, :]
    v = v_cur + (shifted[:, 2 * width:3 * width] - v_cur) * murkv_ref[2:3, :]

    kk = k * k_k
    ssq = _head_sums(kk * kk, ones_ref)
    kkn = kk * jnp.minimum(lax.rsqrt(ssq), 1e12)
    k2 = k * (1.0 + (a - 1.0) * k_a)
    r_o[...] = r
    k_o[...] = k2
    v_o[...] = v
    an_o[...] = -kkn
    bb_o[...] = kkn * a
    bonus_o[...] = _head_sums(r * k2 * r_k, ones_ref) * v


def _mix_in(x2d, seq, gain, w_ext, mu_rkv, vecs, w2_cat, head_ones):
    n, d = x2d.shape
    width = w2_cat.shape[1] // 3
    tm = TOKEN_TILE
    prev_blocks = tm // SUBLANES
    row = lambda i: (i, 0)
    out = jax.ShapeDtypeStruct((n, width), F32)
    return pl.pallas_call(
        functools.partial(_mix_in_kernel, seq),
        grid=(n // tm,),
        in_specs=[
            pl.BlockSpec((tm, d), row),
            pl.BlockSpec((SUBLANES, d), lambda i: (jnp.maximum(i * prev_blocks - 1, 0), 0)),
            _full(gain.shape), _full(w_ext.shape), _full(mu_rkv.shape), _full(vecs.shape),
            _full(w2_cat.shape), _full(head_ones.shape),
        ],
        out_specs=[pl.BlockSpec((tm, width), row)] * 10,
        out_shape=[out] * 9 + [jax.ShapeDtypeStruct((n, width), BF16)],
        compiler_params=_params("parallel"),
        name="mix_in",
    )(x2d, x2d, gain, w_ext, mu_rkv, vecs, w2_cat, head_ones)


def _mix_in_weights(w_in, w1, a1, g1, mu_wag, w2, a2, g2):
    l1 = jnp.concatenate([w1, a1, g1], axis=1).astype(F32)
    mu = jnp.concatenate([jnp.broadcast_to(mu_wag[j][:, None], (w.shape[0], w.shape[1]))
                          for j, w in enumerate((w1, a1, g1))], axis=1).astype(F32)
    w_ext = jnp.concatenate([w_in.astype(F32), mu * l1, (1.0 - mu) * l1], axis=1).astype(BF16)
    width = w2.shape[1]
    z = lambda rows: jnp.zeros((rows, width), F32)
    w2_cat = jnp.concatenate([
        jnp.concatenate([w2.astype(F32), z(w2.shape[0]), z(w2.shape[0])], axis=1),
        jnp.concatenate([z(a2.shape[0]), a2.astype(F32), z(a2.shape[0])], axis=1),
        jnp.concatenate([z(g2.shape[0]), z(g2.shape[0]), g2.astype(F32)], axis=1)], axis=0).astype(BF16)
    return w_ext, w2_cat


def _s5_tables(lam_re, lam_im, log_dt, b_re, b_im, c_re, c_im, n_chunks):
    t = S5_CHUNK
    hi = lax.Precision.HIGHEST
    lre = jnp.minimum(lam_re.astype(F32), LAMBDA_RE_MAX)
    lim = lam_im.astype(F32)
    dt = jnp.exp(log_dt.astype(F32))[:, None]
    mag = jnp.exp(lre * dt)
    lb_re = mag * jnp.cos(lim * dt)
    lb_im = mag * jnp.sin(lim * dt)
    den = lre * lre + lim * lim
    z_re = lb_re - 1.0
    coef_re = ((z_re * lre + lb_im * lim) / den)[..., None]
    coef_im = ((lb_im * lre - z_re * lim) / den)[..., None]
    b_re = b_re.astype(F32)
    b_im = b_im.astype(F32)
    bb_re = coef_re * b_re - coef_im * b_im
    bb_im = coef_re * b_im + coef_im * b_re
    c_re = c_re.astype(F32)
    c_im = c_im.astype(F32)

    def power(e):
        e = e.astype(F32)[:, None, None]
        m = jnp.exp(e * (lre * dt))
        ang = e * (lim * dt)
        return m * jnp.cos(ang), m * jnp.sin(ang)

    g = lre.shape[0]
    c = S5_GROUP
    p_re, p_im = power(jnp.arange(t + 1))
    cb_re = (c_re[:, None, :, :] * bb_re.transpose(0, 2, 1)[:, :, None, :]
             - c_im[:, None, :, :] * bb_im.transpose(0, 2, 1)[:, :, None, :])
    cb_im = (c_re[:, None, :, :] * bb_im.transpose(0, 2, 1)[:, :, None, :]
             + c_im[:, None, :, :] * bb_re.transpose(0, 2, 1)[:, :, None, :])
    cb = jnp.concatenate([cb_re, -cb_im], axis=-1).reshape(g, c * c, 2 * S5_STATE)
    pw = jnp.concatenate([p_re[:t], p_im[:t]], axis=-1).transpose(1, 2, 0)
    kvec = jnp.einsum('gkp,gpt->gkt', cb, pw, precision=hi).reshape(g, c, c // 2, 2 * t)
    r_re, r_im = power(t - 1 - jnp.arange(t))
    st_re = r_re[:, :, :, None] * bb_re - r_im[:, :, :, None] * bb_im
    st_im = r_re[:, :, :, None] * bb_im + r_im[:, :, :, None] * bb_re
    state_map = jnp.concatenate([st_re.transpose(1, 3, 0, 2).reshape(g, c * t, S5_STATE),
                                 st_im.transpose(1, 3, 0, 2).reshape(g, c * t, S5_STATE)],
                                axis=2).astype(BF16)
    q_re, q_im = p_re[1:], p_im[1:]
    ca_re = c_re[:, :, None, :] * q_re.transpose(1, 0, 2)[:, None] - c_im[:, :, None, :] * q_im.transpose(1, 0, 2)[:, None]
    ca_im = c_re[:, :, None, :] * q_im.transpose(1, 0, 2)[:, None] + c_im[:, :, None, :] * q_re.transpose(1, 0, 2)[:, None]
    cross = jnp.concatenate([ca_re.transpose(0, 3, 1, 2).reshape(g, S5_STATE, c * t),
                             -ca_im.transpose(0, 3, 1, 2).reshape(g, S5_STATE, c * t)], axis=1).astype(BF16)
    n_steps = max(1, (n_chunks - 1).bit_length())
    s_re, s_im = power(t * (2 ** jnp.arange(n_steps)))
    tab = jnp.stack([jnp.concatenate([s_re, s_re], axis=-1),
                     jnp.concatenate([-s_im, s_im], axis=-1)], axis=1)
    tab = tab.transpose(2, 0, 1, 3).reshape(g, 2 * n_steps, 2 * S5_STATE)
    return kvec, state_map, cross, tab, n_steps


def _s5_conv_kernel(n_chunks, n_steps, u_ref, kvec_ref, st_ref, cross_ref, tab_ref, y_ref, conv):
    t = S5_CHUNK
    row = lax.broadcasted_iota(jnp.int32, (t, LANES), 0)
    lane = lax.broadcasted_iota(jnp.int32, (t, LANES), 1)
    causal = jnp.bitwise_and(lane, t - 1) >= row
    for cin in range(S5_GROUP):
        for cp in range(S5_GROUP // 2):
            base = jnp.broadcast_to(kvec_ref[0, cin, cp:cp + 1, :], (t, LANES))
            blk = pltpu.roll(base, 0, 1, stride=1, stride_axis=0)
            conv[cin * t:(cin + 1) * t, cp * LANES:(cp + 1) * LANES] = jnp.where(causal, blk, 0.0).astype(BF16)
    u = jnp.concatenate([u_ref[0, cin] for cin in range(S5_GROUP)], axis=1)
    y = _dot(u, conv[...])
    x = _dot(u, st_ref[0])
    chunk = lax.broadcasted_iota(jnp.int32, x.shape, 0) % n_chunks
    half = x.shape[1] // 2
    for j in range(n_steps):
        sh = 1 << j
        xs = jnp.where(chunk >= sh, pltpu.roll(x, sh, 0), 0.0)
        x = x + tab_ref[0, 2 * j:2 * j + 1, :] * xs + tab_ref[0, 2 * j + 1:2 * j + 2, :] * pltpu.roll(xs, half, 1)
    x_in = jnp.where(chunk >= 1, pltpu.roll(x, 1, 0), 0.0)
    y = (y + _dot(x_in.astype(BF16), cross_ref[0])).astype(y_ref.dtype)
    for cout in range(S5_GROUP):
        y_ref[0, cout] = y[:, cout * t:(cout + 1) * t]


def _s5_conv(u2d, batch, seq, tables):
    kvec, state_map, cross, tab, n_steps = tables
    n, width = u2d.shape
    g = width // S5_GROUP
    t = S5_CHUNK
    n_chunks = seq // t
    ct = S5_GROUP * t
    ut = u2d.T.reshape(g, S5_GROUP, batch * n_chunks, t)
    rows = batch * n_chunks
    slab = pl.BlockSpec((1, S5_GROUP, rows, t), lambda i: (i, 0, 0, 0))
    y = pl.pallas_call(
        functools.partial(_s5_conv_kernel, n_chunks, n_steps),
        grid=(g,),
        in_specs=[
            slab,
            pl.BlockSpec((1,) + kvec.shape[1:], lambda i: (i, 0, 0, 0)),
            pl.BlockSpec((1,) + state_map.shape[1:], lambda i: (i, 0, 0)),
            pl.BlockSpec((1,) + cross.shape[1:], lambda i: (i, 0, 0)),
            pl.BlockSpec((1,) + tab.shape[1:], lambda i: (i, 0, 0)),
        ],
        out_specs=slab,
        out_shape=jax.ShapeDtypeStruct((g, S5_GROUP, rows, t), BF16),
        scratch_shapes=[pltpu.VMEM((ct, ct), BF16)],
        compiler_params=_params("parallel"),
        name="s5_conv",
    )(ut, kvec, state_map, cross, tab)
    return y.reshape(width, n).T


def _pair_blockdiag(y, left):
    return jnp.concatenate([jnp.where(left, y, 0.0), jnp.where(left, 0.0, y)], axis=0).astype(BF16)


def _rwkv_chunk_kernel(r_ref, lw_ref, k_ref, v_ref, an_ref, bb_ref, rhat_o, yhat_o, g_o, h_o):
    t = RWKV_CHUNK
    rows, width = r_ref.shape
    n_chunks = rows // t
    pairs = width // LANES
    row = lax.broadcasted_iota(jnp.int32, (t, LANES), 0)
    lane = lax.broadcasted_iota(jnp.int32, (t, LANES), 1)
    col = jnp.bitwise_and(lane, RWKV_HEAD - 1)
    left = lane < RWKV_HEAD
    incl = row >= col
    strict = row > col
    same_blk = (row // RWKV_SUB) == (col // RWKV_SUB)
    eye = jnp.where(row == col, 1.0, 0.0).astype(F32)
    brow = lax.broadcasted_iota(jnp.int32, (LANES, LANES), 0)
    bcol = lax.broadcasted_iota(jnp.int32, (LANES, LANES), 1)
    same_head = (brow // RWKV_HEAD) == (bcol // RWKV_HEAD)
    eye_full = brow == bcol
    crow = lax.broadcasted_iota(jnp.int32, (rows, rows), 0)
    ccol = lax.broadcasted_iota(jnp.int32, (rows, rows), 1)
    tril = jnp.where(crow >= ccol, jnp.where((crow // t) == (ccol // t), 1.0, 0.0), 0.0).astype(BF16)

    lw = lw_ref[...]
    p1 = lw.astype(BF16)
    p2 = (lw - p1.astype(F32)).astype(BF16)
    cs = _dot(tril, p1) + _dot(tril, p2)
    a_t = an_ref[...] * jnp.exp(cs - lw)
    r_t = r_ref[...] * jnp.exp(cs)
    p_inv = jnp.exp(-cs)
    b_t = bb_ref[...] * p_inv
    k_t = k_ref[...] * p_inv

    def bd(y):
        return _pair_blockdiag(y, left)

    def pmm(x, y_bd):
        return _dot(x.astype(BF16), y_bd)

    units = [(c, j) for c in range(n_chunks) for j in range(pairs)]

    def tile(arr, c, j):
        return arr[c * t:(c + 1) * t, j * LANES:(j + 1) * LANES]

    a2 = [tile(a_t, c, j) for c, j in units]
    r2 = [tile(r_t, c, j) for c, j in units]
    v2 = [tile(v_ref[...], c, j) for c, j in units]
    ar = [jnp.concatenate([a, r], axis=0).astype(BF16) for a, r in zip(a2, r2)]
    prod = [_dot_nt(x, jnp.concatenate([bd(tile(b_t, c, j)), bd(tile(k_t, c, j))], axis=0))
            for x, (c, j) in zip(ar, units)]
    l_ab = [jnp.where(strict, p[:t, :LANES], 0.0) for p in prod]
    l_rb = [jnp.where(incl, p[t:, :LANES], 0.0) for p in prod]
    l_ak = [jnp.where(strict, p[:t, LANES:], 0.0) for p in prod]
    l_rk = [jnp.where(incl, p[t:, LANES:], 0.0) for p in prod]
    l_d = [jnp.where(same_blk, m, 0.0) for m in l_ab]
    l_o = [m - d for m, d in zip(l_ab, l_d)]
    l2 = [pmm(m, bd(m)) for m in l_d]
    q0 = [eye + m for m in l_d]
    s1 = [pmm(m, jnp.concatenate([bd(m), bd(q)], axis=1)) for m, q in zip(l2, q0)]
    l4 = [z[:, :LANES] for z in s1]
    q1 = [q + z[:, LANES:] for q, z in zip(q0, s1)]
    s2 = [pmm(m, jnp.concatenate([bd(m), bd(q)], axis=1)) for m, q in zip(l4, q1)]
    l8 = [z[:, :LANES] for z in s2]
    q2 = [q + z[:, LANES:] for q, z in zip(q1, s2)]
    d_inv = [q + pmm(m, bd(q)) for m, q in zip(l8, q2)]
    akv = [pmm(jnp.concatenate([m, n], axis=0), bd(v)) for m, n, v in zip(l_ak, l_rk, v2)]
    x1 = [z[:t] for z in akv]

    def bd2(z1, z2):
        return jnp.concatenate([bd(z1), bd(z2)], axis=1)

    dz = [pmm(d, bd2(a, x)) for d, a, x in zip(d_inv, a2, x1)]
    n1 = [pmm(d, bd(o)) for d, o in zip(d_inv, l_o)]
    n2 = [pmm(m, bd(m)) for m in n1]
    t1 = [z + pmm(m, bd2(z[:, :LANES], z[:, LANES:])) for z, m in zip(dz, n2)]
    wu = [z + pmm(m, bd2(z[:, :LANES], z[:, LANES:])) for z, m in zip(t1, n1)]
    ry = [pmm(m, bd2(z[:, :LANES], z[:, LANES:])) for m, z in zip(l_rb, wu)]
    rk_v = [z[t:] for z in akv]

    for i, (c, j) in enumerate(units):
        rs = slice(c * t, (c + 1) * t)
        ls = slice(j * LANES, (j + 1) * LANES)
        rhat_o[rs, ls] = r2[i] + ry[i][:, :LANES]
        yhat_o[rs, ls] = ry[i][:, LANES:] + rk_v[i]
        cs_c = cs[rs, ls]
        cs_end = cs_c[t - 1:t, :]
        p_end = jnp.exp(cs_end - cs_c)
        b_h = (bb_ref[rs, ls] * p_end).astype(BF16)
        k_h = (k_ref[rs, ls] * p_end).astype(BF16)
        g_full = _dot_tn(wu[i][:, :LANES].astype(BF16), b_h)
        g_bd = jnp.where(same_head, g_full, 0.0) + jnp.where(eye_full, jnp.exp(cs_end), 0.0)
        g_o[c, j] = g_bd.astype(BF16)
        uv = jnp.concatenate([wu[i][:, LANES:], v2[i]], axis=0).astype(BF16)
        h_full = _dot_tn(uv, jnp.concatenate([b_h, k_h], axis=0))
        h_o[c, j] = jnp.where(left, h_full[:RWKV_HEAD], h_full[RWKV_HEAD:])


RWKV_CHUNKS_PER_STEP = 4


def _rwkv_chunks(r, lw, k, v, an, bb):
    n, width = r.shape
    t = RWKV_CHUNK
    cps = RWKV_CHUNKS_PER_STEP
    pairs = width // LANES
    nck = n // t
    tok = pl.BlockSpec((cps * t, width), lambda i: (i, 0))
    return pl.pallas_call(
        _rwkv_chunk_kernel,
        grid=(nck // cps,),
        in_specs=[tok] * 6,
        out_specs=[tok, tok,
                   pl.BlockSpec((cps, pairs, LANES, LANES), lambda i: (i, 0, 0, 0)),
                   pl.BlockSpec((cps, pairs, RWKV_HEAD, LANES), lambda i: (i, 0, 0, 0))],
        out_shape=[jax.ShapeDtypeStruct((n, width), F32)] * 2
        + [jax.ShapeDtypeStruct((nck, pairs, LANES, LANES), BF16),
           jax.ShapeDtypeStruct((nck, pairs, RWKV_HEAD, LANES), F32)],
        compiler_params=_params("parallel"),
        name="rwkv_chunk",
    )(r, lw, k, v, an, bb)


def _rwkv_scan_kernel(rhat_ref, yhat_ref, g_ref, h_ref, gate_ref, bonus_ref, lnw_ref, lnb_ref,
                      ones_ref, y_ref, state):
    t = RWKV_CHUNK
    batch, pairs = state.shape[0], state.shape[1]
    n_chunks = rhat_ref.shape[1] // t
    lane = lax.broadcasted_iota(jnp.int32, (RWKV_HEAD, LANES), 1)
    left = lane < RWKV_HEAD

    @pl.when(pl.program_id(0) == 0)
    def _():
        state[...] = jnp.zeros_like(state)

    s = [[state[b, j] for j in range(pairs)] for b in range(batch)]
    y_rows = []
    for b in range(batch):
        chunk_rows = []
        for c in range(n_chunks):
            rs = slice(c * t, (c + 1) * t)
            tiles = []
            for j in range(pairs):
                ls = slice(j * LANES, (j + 1) * LANES)
                s_bd = _pair_blockdiag(s[b][j], left)
                tiles.append(yhat_ref[b, rs, ls] + _dot_nt(rhat_ref[b, rs, ls].astype(BF16), s_bd))
                s[b][j] = _dot(s[b][j].astype(BF16), g_ref[b, c, j]) + h_ref[b, c, j]
            chunk_rows.append(jnp.concatenate(tiles, axis=1))
        y_rows.append(jnp.concatenate(chunk_rows, axis=0))
    for b in range(batch):
        for j in range(pairs):
            state[b, j] = s[b][j]

    inv = 1.0 / RWKV_HEAD
    for b in range(batch):
        y = y_rows[b]
        cen = y - _head_sums(y, ones_ref) * inv
        var = _head_sums(cen * cen, ones_ref) * inv
        yn = cen * lax.rsqrt(var + RWKV_GN_EPS) * lnw_ref[...] + lnb_ref[...]
        y_ref[b] = (yn + bonus_ref[b]) * gate_ref[b]


def _rwkv_scan(rhat, yhat, g_mat, h_mat, gate, bonus, ln_w, ln_b, head_ones, batch, seq):
    n, width = rhat.shape
    t = RWKV_CHUNK
    cps = RWKV_CHUNKS_PER_STEP
    pairs = width // LANES
    nc = seq // t
    tok3 = lambda a: a.reshape(batch, seq, width)
    tok = pl.BlockSpec((batch, cps * t, width), lambda i: (0, i, 0))
    out = pl.pallas_call(
        _rwkv_scan_kernel,
        grid=(nc // cps,),
        in_specs=[tok, tok,
                  pl.BlockSpec((batch, cps, pairs, LANES, LANES), lambda i: (0, i, 0, 0, 0)),
                  pl.BlockSpec((batch, cps, pairs, RWKV_HEAD, LANES), lambda i: (0, i, 0, 0, 0)),
                  tok, tok, _full(ln_w.shape), _full(ln_b.shape), _full(head_ones.shape)],
        out_specs=tok,
        out_shape=jax.ShapeDtypeStruct((batch, seq, width), F32),
        scratch_shapes=[pltpu.VMEM((batch, pairs, RWKV_HEAD, LANES), F32)],
        compiler_params=_params("arbitrary"),
        name="rwkv_scan",
    )(tok3(rhat), tok3(yhat), g_mat.reshape(batch, nc, pairs, LANES, LANES),
      h_mat.reshape(batch, nc, pairs, RWKV_HEAD, LANES), tok3(gate), tok3(bonus), ln_w, ln_b, head_ones)
    return out.reshape(n, width)


def _post_mix_kernel(x_ref, yc_ref, u_ref, yr_ref, d_ref, wglu_ref, wout_ref, gain_ref,
                     rw_ref, rb_ref, x1_o, hm_o, gate_o, lpos_o, meta_o, tot_o, running):
    tm = x_ref.shape[0]
    width = yc_ref.shape[1]

    @pl.when(pl.program_id(0) == 0)
    def _():
        running[...] = jnp.zeros_like(running)

    y = yc_ref[...].astype(F32) + d_ref[...] * u_ref[...]
    y = 0.5 * y * (1.0 + jnp.tanh(0.7978845608028654 * (y + 0.044715 * (y * y * y))))
    y = y * _sigmoid(_dot(y.astype(BF16), wglu_ref[...]))
    x1 = (x_ref[...] + _dot(y.astype(BF16), wout_ref[:width, :])
          + _dot(yr_ref[...].astype(BF16), wout_ref[width:, :]))
    x1_o[...] = x1
    hm = _rms(x1, gain_ref[...])
    hm_o[...] = hm.astype(BF16)
    logits = _dot(hm.astype(BF16), rw_ref[...]) + rb_ref[...]

    lane = lax.broadcasted_iota(jnp.int32, logits.shape, 1)
    lanef = lane.astype(F32)
    sel = jnp.zeros(logits.shape, F32)
    idx_cols, val_cols = [], []
    work = logits
    for _ in range(TOP_K):
        m = jnp.max(work, axis=-1, keepdims=True)
        pick = jnp.min(jnp.where(work == m, lanef, float(LANES)), axis=-1, keepdims=True)
        hit = lanef == pick
        sel = jnp.where(hit, 1.0, sel)
        work = jnp.where(hit, -jnp.inf, work)
        idx_cols.append(pick)
        val_cols.append(m)
    exps = [jnp.exp(vv - val_cols[0]) for vv in val_cols]
    denom = exps[0] + exps[1] + exps[2] + exps[3]

    row = lax.broadcasted_iota(jnp.int32, (tm, tm), 0)
    col = lax.broadcasted_iota(jnp.int32, (tm, tm), 1)
    before = jnp.where(row > col, 1.0, 0.0).astype(BF16)
    local = _dot(before, sel.astype(BF16))
    cnt = jnp.sum(sel, axis=0, keepdims=True)
    cnt_al = jnp.floor((cnt + (RUN_ALIGN - 1)) * (1.0 / RUN_ALIGN)) * RUN_ALIGN
    erow = lax.broadcasted_iota(jnp.int32, (LANES, LANES), 0)
    ecol = lax.broadcasted_iota(jnp.int32, (LANES, LANES), 1)
    upper = jnp.where(erow < ecol, 1.0, 0.0).astype(BF16)
    toff = _dot(jnp.broadcast_to(cnt_al, (SUBLANES, LANES)).astype(BF16), upper)[0:1]
    tbase = running[...]
    gate_out = jnp.zeros(logits.shape, F32)
    lpos_out = jnp.zeros(logits.shape, F32)
    for j in range(TOP_K):
        lp = jnp.sum(jnp.where(lanef == idx_cols[j], local + toff, 0.0), axis=-1, keepdims=True)
        gate_out = jnp.where(lane == j, exps[j] / denom, gate_out)
        lpos_out = jnp.where(lane == j, lp, lpos_out)
    gate_o[...] = gate_out
    lpos_o[...] = lpos_out.astype(jnp.int32)
    srow = lax.broadcasted_iota(jnp.int32, (SUBLANES, LANES), 0)
    meta = jnp.where(srow == 0, cnt, jnp.where(srow == 1, tbase, jnp.where(srow == 2, toff, 0.0)))
    meta_o[...] = meta.astype(jnp.int32)
    running[...] = tbase + cnt_al
    tot_o[...] = jnp.broadcast_to(running[...], tot_o.shape).astype(jnp.int32)


def _post_mix(x2d, yconv, u, y_rwkv, s5_d, w_glu, w_out, gain, rw, rb):
    n, d = x2d.shape
    width = yconv.shape[1]
    tm = MOE_TILE
    row = lambda i: (i, 0)
    tok_d = pl.BlockSpec((tm, d), row)
    tok_w = pl.BlockSpec((tm, width), row)
    tok_l = pl.BlockSpec((tm, LANES), row)
    return pl.pallas_call(
        _post_mix_kernel,
        grid=(n // tm,),
        in_specs=[tok_d, tok_w, tok_w, tok_w, _full(s5_d.shape), _full(w_glu.shape),
                  _full(w_out.shape), _full(gain.shape), _full(rw.shape), _full(rb.shape)],
        out_specs=[tok_d, tok_d, tok_l, tok_l,
                   pl.BlockSpec((SUBLANES, LANES), row), _full((SUBLANES, LANES))],
        out_shape=[jax.ShapeDtypeStruct((n, d), F32), jax.ShapeDtypeStruct((n, d), BF16),
                   jax.ShapeDtypeStruct((n, LANES), F32), jax.ShapeDtypeStruct((n, LANES), jnp.int32),
                   jax.ShapeDtypeStruct((n // tm * SUBLANES, LANES), jnp.int32),
                   jax.ShapeDtypeStruct((SUBLANES, LANES), jnp.int32)],
        scratch_shapes=[pltpu.VMEM((1, LANES), F32)],
        compiler_params=_params("arbitrary"),
        name="post_mix",
    )(x2d, yconv, u, y_rwkv, s5_d, w_glu, w_out, gain, rw, rb)


def _sorted_rows(tm, n_experts, multiple):
    return -(-(tm * TOP_K + n_experts * (RUN_ALIGN - 1)) // multiple) * multiple


RUN_BIG = 8


def _groups(count):
    return lax.shift_right_logical(count + (RUN_ALIGN - 1), RUN_ALIGN.bit_length() - 1)


def _start_runs(n_experts, pstart_ref, meta_ref, make_copy):
    shift = RUN_BIG.bit_length() - 1
    for e in range(n_experts):
        groups = _groups(meta_ref[0, e])
        seg = pstart_ref[e] + meta_ref[1, e]
        loc = meta_ref[2, e]
        n_big = lax.shift_right_logical(groups, shift)
        rem = groups & (RUN_BIG - 1)

        def body(g, c, seg=seg, loc=loc):
            off = g * (RUN_BIG * RUN_ALIGN)
            make_copy(pl.multiple_of(loc + off, RUN_ALIGN), pl.multiple_of(seg + off, RUN_ALIGN),
                      RUN_BIG * RUN_ALIGN).start(priority=e % 2)
            return c

        lax.fori_loop(0, n_big, body, 0)
        size = RUN_BIG // 2
        while size >= 1:
            off = (n_big * RUN_BIG + (rem & (RUN_BIG - 2 * size))) * RUN_ALIGN

            @pl.when((rem & size) != 0)
            def _(off=off, size=size, seg=seg, loc=loc):
                make_copy(pl.multiple_of(loc + off, RUN_ALIGN), pl.multiple_of(seg + off, RUN_ALIGN),
                          size * RUN_ALIGN).start(priority=e % 2)

            size //= 2


def _wait_runs(n_experts, meta_ref, make_copy):
    groups = 0
    for e in range(n_experts):
        groups = groups + _groups(meta_ref[0, e])

    def wait_big(g, c):
        make_copy(0, 0, RUN_BIG * RUN_ALIGN).wait()
        return c

    def wait_small(g, c):
        make_copy(0, 0, RUN_ALIGN).wait()
        return c

    lax.fori_loop(0, lax.shift_right_logical(groups, RUN_BIG.bit_length() - 1), wait_big, 0)
    lax.fori_loop(0, groups & (RUN_BIG - 1), wait_small, 0)


def _dispatch_kernel(n_experts, pstart_ref, tot_ref, meta_ref, meta_prev_ref, lpos_ref, hm_ref, xs_ref,
                     srt, zero, sem):
    tm = hm_ref.shape[0]
    rows = srt.shape[1]
    i = pl.program_id(0)
    cur = i % 2
    pos_t = jnp.transpose(lpos_ref[...].astype(F32))
    hm = hm_ref[...]
    for r0 in range(0, rows, SORT_BLOCK):
        nr = min(SORT_BLOCK, rows - r0)
        rid = (lax.broadcasted_iota(jnp.int32, (nr, tm), 0) + r0).astype(F32)
        perm = jnp.zeros((nr, tm), F32)
        for j in range(TOP_K):
            perm = jnp.where(rid == pos_t[j:j + 1, :], 1.0, perm)
        srt[cur, r0:r0 + nr, :] = _dot(perm.astype(BF16), hm)

    def copies(slot):
        def make_copy(loc, dst, size):
            return pltpu.make_async_copy(srt.at[slot, pl.ds(loc, size)], xs_ref.at[pl.ds(dst, size)],
                                         sem.at[slot])
        return make_copy

    _start_runs(n_experts, pstart_ref, meta_ref, copies(cur))

    @pl.when(i > 0)
    def _():
        _wait_runs(n_experts, meta_prev_ref, copies(1 - cur))

    @pl.when(i == pl.num_programs(0) - 1)
    def _():
        _wait_runs(n_experts, meta_ref, copies(cur))

    @pl.when(pl.program_id(0) == pl.num_programs(0) - 1)
    def _():
        zero[...] = jnp.zeros_like(zero)
        for wait in (False, True):
            for e in range(n_experts):
                used = tot_ref[e]
                start = pstart_ref[e] + used
                groups = lax.shift_right_logical((-used) & (EXPERT_TILE - 1), RUN_ALIGN.bit_length() - 1)

                def body(g, c, start=start):
                    cp = pltpu.make_async_copy(
                        zero, xs_ref.at[pl.ds(pl.multiple_of(start + g * RUN_ALIGN, RUN_ALIGN), RUN_ALIGN)],
                        sem.at[0])
                    if wait:
                        cp.wait()
                    else:
                        cp.start()
                    return c

                lax.fori_loop(0, groups, body, 0)

        last = n_experts - 1
        end = pstart_ref[last] + tot_ref[last] + ((-tot_ref[last]) & (EXPERT_TILE - 1))
        srt[0, 0:EXPERT_TILE, :] = jnp.zeros((EXPERT_TILE, srt.shape[2]), srt.dtype)
        for wait in (False, True):
            def tail(b, c):
                cp = pltpu.make_async_copy(
                    srt.at[0, pl.ds(0, EXPERT_TILE)],
                    xs_ref.at[pl.ds(pl.multiple_of(end + b * EXPERT_TILE, EXPERT_TILE), EXPERT_TILE)], sem.at[0])
                if wait:
                    cp.wait()
                else:
                    cp.start()
                return c

            lax.fori_loop(0, lax.shift_right_logical(xs_ref.shape[0] - end, EXPERT_TILE.bit_length() - 1), tail, 0)


def _dispatch(hm, lpos, meta, pstarts, totals, m_pad, n_experts):
    n, dh = hm.shape
    tm = MOE_TILE
    rows = _sorted_rows(tm, n_experts, RUN_ALIGN)
    grid_spec = pltpu.PrefetchScalarGridSpec(
        num_scalar_prefetch=2,
        grid=(n // tm,),
        in_specs=[pl.BlockSpec((SUBLANES, LANES), lambda i, ps, tt: (i, 0), memory_space=pltpu.SMEM),
                  pl.BlockSpec((SUBLANES, LANES), lambda i, ps, tt: (jnp.maximum(i - 1, 0), 0),
                               memory_space=pltpu.SMEM),
                  pl.BlockSpec((tm, LANES), lambda i, ps, tt: (i, 0)),
                  pl.BlockSpec((tm, dh), lambda i, ps, tt: (i, 0))],
        out_specs=pl.BlockSpec(memory_space=pl.ANY),
        scratch_shapes=[pltpu.VMEM((2, rows, dh), F32), pltpu.VMEM((RUN_ALIGN, dh), F32),
                        pltpu.SemaphoreType.DMA((2,))],
    )
    return pl.pallas_call(
        functools.partial(_dispatch_kernel, n_experts),
        grid_spec=grid_spec,
        out_shape=jax.ShapeDtypeStruct((m_pad, dh), F32),
        compiler_params=_params("arbitrary"),
        name="dispatch",
    )(pstarts, totals, meta, meta, lpos, hm)


def _experts_kernel(be_ref, nu_ref, xs_ref, wg_ref, bg_ref, wu_ref, bu_ref, wd_ref, bd_ref, ys_ref,
                    wg_s, wu_s, wd_s):
    i = pl.program_id(0)
    changed = jnp.logical_or(i == 0, be_ref[i] != be_ref[jnp.maximum(i - 1, 0)])

    @pl.when(changed)
    def _():
        wg_s[...] = wg_ref[0].astype(BF16)
        wu_s[...] = wu_ref[0].astype(BF16)
        wd_s[...] = wd_ref[0].astype(BF16)

    @pl.when(i < nu_ref[0])
    def _():
        xb = xs_ref[...].astype(BF16)
        gt = jnp.minimum(_dot(xb, wg_s[...]) + bg_ref[0], SWIGLU_LIMIT)
        up = jnp.clip(_dot(xb, wu_s[...]) + bu_ref[0], -SWIGLU_LIMIT, SWIGLU_LIMIT)
        act = (up + 1.0) * gt * _sigmoid(SWIGLU_ALPHA * gt)
        ys_ref[...] = _dot(act.astype(BF16), wd_s[...]) + bd_ref[0]

    @pl.when(i >= nu_ref[0])
    def _():
        ys_ref[...] = jnp.zeros_like(ys_ref)


def _experts(xs, block_e, n_used, wg, bg, wu, bu, wd, bd):
    m_pad, dh = xs.shape
    tmb = EXPERT_TILE
    d, dff = wg.shape[1], wg.shape[2]
    wmap = lambda i, be, nu: (be[i], 0, 0)
    grid_spec = pltpu.PrefetchScalarGridSpec(
        num_scalar_prefetch=2,
        grid=(m_pad // tmb,),
        in_specs=[pl.BlockSpec((tmb, dh), lambda i, be, nu: (jnp.where(i < nu[0], i, 0), 0)),
                  pl.BlockSpec((1, d, dff), wmap), pl.BlockSpec((1, 1, dff), wmap),
                  pl.BlockSpec((1, d, dff), wmap), pl.BlockSpec((1, 1, dff), wmap),
                  pl.BlockSpec((1, dff, d), wmap), pl.BlockSpec((1, 1, d), wmap)],
        out_specs=pl.BlockSpec((tmb, dh), lambda i, be, nu: (i, 0)),
        scratch_shapes=[pltpu.VMEM((d, dff), BF16), pltpu.VMEM((d, dff), BF16), pltpu.VMEM((dff, d), BF16)],
    )
    return pl.pallas_call(
        _experts_kernel,
        grid_spec=grid_spec,
        out_shape=jax.ShapeDtypeStruct((m_pad, dh), F32),
        compiler_params=_params("arbitrary"),
        name="experts",
    )(block_e, n_used, xs, wg, bg, wu, bu, wd, bd)


def _final_kernel(n_experts, pstart_ref, meta_ref, meta_next_ref, lpos_ref, x1_ref, gate_ref, p_ref, ys_ref,
                  gple_ref, wproj_ref, wgate_ref, gfin_ref, out_ref, buf, sem):
    tm = x1_ref.shape[0]
    rows = buf.shape[1]
    i = pl.program_id(0)
    cur = i % 2

    def copies(slot):
        def make_copy(loc, src, size):
            return pltpu.make_async_copy(ys_ref.at[pl.ds(src, size)], buf.at[slot, pl.ds(loc, size)],
                                         sem.at[slot])
        return make_copy

    @pl.when(i == 0)
    def _():
        buf[...] = jnp.zeros_like(buf)
        _start_runs(n_experts, pstart_ref, meta_ref, copies(0))

    @pl.when(i + 1 < pl.num_programs(0))
    def _():
        _start_runs(n_experts, pstart_ref, meta_next_ref, copies(1 - cur))

    _wait_runs(n_experts, meta_ref, copies(cur))
    ysb = buf[cur].astype(BF16)
    for r0 in range(0, tm, FINAL_BLOCK):
        rs = slice(r0, r0 + FINAL_BLOCK)
        cid = lax.broadcasted_iota(jnp.int32, (FINAL_BLOCK, rows), 1)
        lpos = lpos_ref[rs, :]
        gates = gate_ref[rs, :]
        comb = jnp.zeros((FINAL_BLOCK, rows), F32)
        for j in range(TOP_K):
            comb = jnp.where(cid == lpos[:, j:j + 1], gates[:, j:j + 1], comb)
        x2 = x1_ref[rs, :] + _dot(comb.astype(BF16), ysb)
        gate = _sigmoid(_dot(_rms(x2, gple_ref[...]).astype(BF16), wgate_ref[...]))
        x3 = x2 + _dot(p_ref[rs, :].astype(BF16), wproj_ref[...]) * gate
        out_ref[rs, :] = _rms(x3, gfin_ref[...])


def _final(x1, gates, lpos, meta, pstarts, p2d, ys, g_ple, w_proj, w_gate, g_fin, n_experts):
    n, d = x1.shape
    tm = MOE_TILE
    steps = n // tm
    rows = _sorted_rows(tm, n_experts, LANES)
    row = lambda i, ps: (i, 0)
    const = lambda shape: pl.BlockSpec(shape, lambda i, ps: (0,) * len(shape))
    grid_spec = pltpu.PrefetchScalarGridSpec(
        num_scalar_prefetch=1,
        grid=(steps,),
        in_specs=[pl.BlockSpec((SUBLANES, LANES), row, memory_space=pltpu.SMEM),
                  pl.BlockSpec((SUBLANES, LANES), lambda i, ps: (jnp.minimum(i + 1, steps - 1), 0),
                               memory_space=pltpu.SMEM),
                  pl.BlockSpec((tm, LANES), row), pl.BlockSpec((tm, d), row), pl.BlockSpec((tm, LANES), row),
                  pl.BlockSpec((tm, p2d.shape[1]), row), pl.BlockSpec(memory_space=pl.ANY),
                  const(g_ple.shape), const(w_proj.shape), const(w_gate.shape), const(g_fin.shape)],
        out_specs=pl.BlockSpec((tm, d), row),
        scratch_shapes=[pltpu.VMEM((2, rows, ys.shape[1]), ys.dtype), pltpu.SemaphoreType.DMA((2,))],
    )
    return pl.pallas_call(
        functools.partial(_final_kernel, n_experts),
        grid_spec=grid_spec,
        out_shape=jax.ShapeDtypeStruct((n, d), F32),
        compiler_params=_params("arbitrary"),
        name="final",
    )(pstarts, meta, meta, lpos, x1, gates, p2d, ys, g_ple, w_proj, w_gate, g_fin)


def _head_ones():
    hd = jnp.arange(MXU_WIDTH) // RWKV_HEAD
    return (hd[:, None] == hd[None, :]).astype(BF16)


def _layer(x2d, p2d, batch, seq, norm_mix_g, w_in, s5_lam_re, s5_lam_im, s5_log_dt, s5_b_re, s5_b_im,
           s5_c_re, s5_c_im, s5_d, s5_w_glu, mu_rkv, mu_wag, w0, w1, w2, a0, a1, a2, g1, g2, k_k, k_a,
           r_k, ln_w, ln_b, w_out, norm_moe_g, router_w, router_b, wg, bg, wu, bu, wd, bd,
           norm_ple_g, ple_w_proj, ple_w_gate):
    n, d = x2d.shape
    width = w_in.shape[1] // 4
    n_experts = router_w.shape[1]
    row2 = lambda t: t.reshape(1, -1).astype(F32)
    head_ones = _head_ones()
    vecs = jnp.zeros((SUBLANES, width), F32)
    vecs = vecs.at[0].set(w0).at[1].set(a0).at[2].set(k_k).at[3].set(k_a).at[4].set(r_k.reshape(-1))

    w_ext, w2_cat = _mix_in_weights(w_in, w1, a1, g1, mu_wag, w2, a2, g2)
    u, r, lw, k, v, an, bb, gate, bonus, u_bf = _mix_in(
        x2d, seq, row2(norm_mix_g), w_ext, mu_rkv.astype(F32), vecs, w2_cat, head_ones)

    tables = _s5_tables(s5_lam_re, s5_lam_im, s5_log_dt, s5_b_re, s5_b_im, s5_c_re, s5_c_im,
                        seq // S5_CHUNK)
    yconv = _s5_conv(u_bf, batch, seq, tables)

    rhat, yhat, g_mat, h_mat = _rwkv_chunks(r, lw, k, v, an, bb)
    y_rwkv = _rwkv_scan(rhat, yhat, g_mat, h_mat, gate, bonus, row2(ln_w), row2(ln_b), head_ones,
                        batch, seq)

    rw = jnp.zeros((d, LANES), BF16).at[:, :n_experts].set(router_w.astype(BF16))
    rb = jnp.full((1, LANES), NEG_BIG, F32).at[0, :n_experts].set(router_b.astype(F32))
    x1, hm, gates, lpos, meta, totals = _post_mix(
        x2d, yconv, u, y_rwkv, row2(s5_d), s5_w_glu.astype(BF16), w_out.astype(BF16),
        row2(norm_moe_g), rw, rb)

    tmb = EXPERT_TILE
    n_tiles = n // MOE_TILE
    max_rows = n * TOP_K + n_tiles * n_experts * (RUN_ALIGN - 1)
    n_blocks = -(-max_rows // tmb) + n_experts
    seg = totals[0, :n_experts].astype(jnp.int32)
    padded = ((seg + tmb - 1) // tmb) * tmb
    pends = jnp.cumsum(padded)
    pstarts = (pends - padded).astype(jnp.int32)
    block_start = jnp.arange(n_blocks, dtype=jnp.int32) * tmb
    block_e = jnp.minimum(jnp.sum((pends[None, :] <= block_start[:, None]).astype(jnp.int32), axis=1),
                          n_experts - 1).astype(jnp.int32)
    n_used = (pends[-1] // tmb).astype(jnp.int32).reshape(1)

    xs = _dispatch(hm, lpos, meta, pstarts, seg, n_blocks * tmb, n_experts)
    ys = _experts(xs, block_e, n_used, wg, bg.reshape(n_experts, 1, -1).astype(F32),
                  wu, bu.reshape(n_experts, 1, -1).astype(F32), wd,
                  bd.reshape(n_experts, 1, -1).astype(F32))
    return (x1, gates, lpos, meta, pstarts, ys, p2d, row2(norm_ple_g), ple_w_proj.astype(BF16),
            ple_w_gate.astype(BF16), n_experts)


def kernel(x, p, norm_mix_g, w_in, s5_lam_re, s5_lam_im, s5_log_dt, s5_b_re, s5_b_im, s5_c_re, s5_c_im, s5_d, s5_w_glu, rwkv_mu_rkv, rwkv_mu_wag, rwkv_w0, rwkv_w1, rwkv_w2, rwkv_a0, rwkv_a1, rwkv_a2, rwkv_g1, rwkv_g2, rwkv_k_k, rwkv_k_a, rwkv_r_k, rwkv_ln_w, rwkv_ln_b, w_out, norm_moe_g, router_w, router_b, exp_w_gate, exp_b_gate, exp_w_up, exp_b_up, exp_w_down, exp_b_down, norm_ple_g, ple_w_proj, ple_w_gate, final_norm_g):
    batch, seq, d = x.shape
    assert w_in.shape[0] == 1, "the final kernel fuses the last RMSNorm: single-layer stacks only"
    i = 0
    x1, gates, lpos, meta, pstarts, ys, p2d, g_ple, w_proj, w_gate, n_experts = _layer(
        x.reshape(batch * seq, d), p[i].reshape(batch * seq, -1), batch, seq, norm_mix_g[i], w_in[i],
        s5_lam_re[i], s5_lam_im[i], s5_log_dt[i], s5_b_re[i], s5_b_im[i], s5_c_re[i], s5_c_im[i],
        s5_d[i], s5_w_glu[i], rwkv_mu_rkv[i], rwkv_mu_wag[i], rwkv_w0[i], rwkv_w1[i], rwkv_w2[i],
        rwkv_a0[i], rwkv_a1[i], rwkv_a2[i], rwkv_g1[i], rwkv_g2[i], rwkv_k_k[i], rwkv_k_a[i],
        rwkv_r_k[i], rwkv_ln_w[i], rwkv_ln_b[i], w_out[i], norm_moe_g[i], router_w[i],
        router_b[i], exp_w_gate[i], exp_b_gate[i], exp_w_up[i], exp_b_up[i], exp_w_down[i],
        exp_b_down[i], norm_ple_g[i], ple_w_proj[i], ple_w_gate[i])
    out = _final(x1, gates, lpos, meta, pstarts, p2d, ys, g_ple, w_proj, w_gate,
                 final_norm_g.reshape(1, -1).astype(F32), n_experts)
    return out.reshape(batch, seq, d)
```
